```python
import jax
import jax.numpy as jnp
from jax import lax
import numpy as np

D_MODEL = 1024
BATCH = 4
SEQ = 8192
DEPTH = 1

CHUNK = 64
D_MIX = D_MODEL
HEAD_DIM = 64
D_A = D_MIX // 2
D_B = D_MIX - D_A
N_HEADS_A = D_A // HEAD_DIM
N_HEADS_B = D_B // HEAD_DIM
N_IDX_HEADS = 16
IDX_DIM = 64
TOPK_MAX = 256
ROPE_THETA = 500000.0
ROT_DIM = HEAD_DIM // 4
IDX_ROT_DIM = IDX_DIM // 4
RET_THETA = 10000.0
Q_BLOCK = 128
N_GROUPS = 4
EXPERTS_PER_GROUP = 8
N_EXPERTS = N_GROUPS * EXPERTS_PER_GROUP
TOP_K_FINE = 2
D_EXPERT = D_MODEL // 2
MOE_BLOCK = 128
EPS = 1e-6
SPLITS = (D_A, D_A, D_A, N_IDX_HEADS * IDX_DIM, IDX_DIM, N_IDX_HEADS, D_B, D_B, D_B, D_B)
D_IN = sum(SPLITS)

kernel_name = 'hybrid_dsa_retention_hmoe_block'


def rmsnorm(x, g):
    xf = x.astype(jnp.float32)
    xf = xf * lax.rsqrt(jnp.mean(xf * xf, axis=-1, keepdims=True) + EPS)
    return xf.astype(x.dtype) * g


def rope(x, pos, rot_dim, theta):
    half = rot_dim // 2
    inv_freq = theta ** (-jnp.arange(half, dtype=jnp.float32) / half)
    ang = pos.astype(jnp.float32)[:, :, None, None] * inv_freq
    cos, sin = jnp.cos(ang), jnp.sin(ang)
    xr = x[..., :rot_dim].astype(jnp.float32)
    x1, x2 = xr[..., :half], xr[..., half:]
    rot = jnp.concatenate([x1 * cos - x2 * sin, x2 * cos + x1 * sin], axis=-1).astype(x.dtype)
    return jnp.concatenate([rot, x[..., rot_dim:]], axis=-1)


def dsa_attention(q, k, v, q_idx, k_idx, w_idx):
    B, S = q.shape[0], q.shape[1]
    k_top = min(TOPK_MAX, S // 4)
    n_blk = S // Q_BLOCK
    key_chunk = jnp.arange(S) // CHUNK

    def to_blocks(t):
        return t.reshape((B, n_blk, Q_BLOCK) + t.shape[2:]).swapaxes(0, 1)

    def block(xs):
        i, qb, qib, wb = xs
        q_chunk = (i * Q_BLOCK + jnp.arange(Q_BLOCK)) // CHUNK
        dots = jnp.einsum('bqhe,bse->bqhs', qib, k_idx, preferred_element_type=jnp.float32)
        score = jnp.einsum('bqh,bqhs->bqs', wb.astype(jnp.float32), jax.nn.relu(dots * IDX_DIM ** -0.5))
        admissible = key_chunk[None, :] <= q_chunk[:, None]
        score = jnp.where(admissible[None], score, -jnp.inf)
        _, sel = lax.top_k(score, k_top)
        k_sel = jax.vmap(lambda kk, ii: kk[ii])(k, sel)
        v_sel = jax.vmap(lambda vv, ii: vv[ii])(v, sel)
        valid = key_chunk[sel] <= q_chunk[None, :, None]
        logits = jnp.einsum('bqhd,bqkhd->bqhk', qb, k_sel, preferred_element_type=jnp.float32) * HEAD_DIM ** -0.5
        logits = jnp.where(valid[:, :, None, :], logits, -1e30)
        p = jax.nn.softmax(logits, axis=-1).astype(v.dtype)
        return jnp.einsum('bqhk,bqkhd->bqhd', p, v_sel)

    out = lax.map(block, (jnp.arange(n_blk), to_blocks(q), to_blocks(q_idx), to_blocks(w_idx)))
    return out.swapaxes(0, 1).reshape(B, S, -1)


def retention(q, k, v, g, pos):
    B, S, H, d = q.shape
    n_c = S // CHUNK
    dt = v.dtype
    log_gamma = jnp.log1p(-jnp.exp2(-5.0 - jnp.arange(H, dtype=jnp.float32)))
    q = rope(q, pos, d, RET_THETA).astype(jnp.float32).reshape(B, n_c, CHUNK, H, d)
    k = (rope(k, pos, d, RET_THETA).astype(jnp.float32) * d ** -0.5).reshape(B, n_c, CHUNK, H, d)
    v = v.astype(jnp.float32).reshape(B, n_c, CHUNK, H, d)
    n = jnp.arange(CHUNK, dtype=jnp.float32)
    diff = n[:, None] - n[None, :]
    decay = jnp.where(diff[None] >= 0, jnp.exp(jnp.maximum(diff, 0.0)[None] * log_gamma[:, None, None]), 0.0)
    scores = jnp.einsum('bcnhd,bcmhd->bchnm', q, k) * decay
    inner = jnp.einsum('bchnm,bcmhe->bcnhe', scores, v)
    zeta = jnp.exp((CHUNK - 1.0 - n)[:, None] * log_gamma[None, :])
    kv = jnp.einsum('bcmhd,bcmhe->bchde', k * zeta[None, None, :, :, None], v)
    g_chunk = jnp.exp(CHUNK * log_gamma)[None, :, None, None]

    def step(state, kv_c):
        return state * g_chunk + kv_c, state

    _, prev = lax.scan(step, jnp.zeros((B, H, d, d), jnp.float32), kv.swapaxes(0, 1))
    prev = prev.swapaxes(0, 1)
    xi = jnp.exp((n + 1.0)[:, None] * log_gamma[None, :])
    cross = jnp.einsum('bcnhd,bchde->bcnhe', q, prev) * xi[None, None, :, :, None]
    y = (inner + cross).reshape(B, S, H, d)
    mu = jnp.mean(y, axis=-1, keepdims=True)
    var = jnp.mean(jnp.square(y - mu), axis=-1, keepdims=True)
    y = (y - mu) * lax.rsqrt(var + EPS)
    return (jax.nn.silu(g.astype(jnp.float32)) * y).astype(dt).reshape(B, S, H * d)


def hybrid_mixer(h, positions, w_in, w_o):
    B, S, _ = h.shape
    proj = jnp.einsum('bsd,de->bse', h, w_in)
    cuts = [int(i) for i in np.cumsum(SPLITS)[:-1]]
    qa, ka, va, qi, ki, wi, qb, kb, vb, gb = jnp.split(proj, cuts, axis=-1)
    qa = rope(qa.reshape(B, S, N_HEADS_A, HEAD_DIM), positions, ROT_DIM, ROPE_THETA)
    ka = rope(ka.reshape(B, S, N_HEADS_A, HEAD_DIM), positions, ROT_DIM, ROPE_THETA)
    va = va.reshape(B, S, N_HEADS_A, HEAD_DIM)
    qi = rope(qi.reshape(B, S, N_IDX_HEADS, IDX_DIM), positions, IDX_ROT_DIM, ROPE_THETA)
    ki = rope(ki.reshape(B, S, 1, IDX_DIM), positions, IDX_ROT_DIM, ROPE_THETA)[:, :, 0]
    wi = wi * N_IDX_HEADS ** -0.5
    y_a = dsa_attention(qa, ka, va, qi, ki, wi)
    shp = (B, S, N_HEADS_B, HEAD_DIM)
    y_b = retention(qb.reshape(shp), kb.reshape(shp), vb.reshape(shp), gb.reshape(shp), positions)
    y = jnp.concatenate([y_a, y_b], axis=-1)
    return jnp.einsum('bse,ed->bsd', y, w_o)


def hier_moe(h, w_router_group, w_router_expert, w_up, w_gate, w_down):
    B, S, D = h.shape
    tok = h.reshape(-1, D)
    N = tok.shape[0]
    pg = jax.nn.softmax(jnp.einsum('nd,dg->ng', tok, w_router_group).astype(jnp.float32), axis=-1)
    grp = jnp.argmax(pg, axis=-1)
    p_grp = jnp.max(pg, axis=-1)
    fine = jnp.einsum('nd,gde->nge', tok, w_router_expert).astype(jnp.float32)
    fine = jnp.take_along_axis(fine, grp[:, None, None], axis=1)[:, 0]
    pe = jax.nn.softmax(fine, axis=-1)
    top_p, top_i = lax.top_k(pe, TOP_K_FINE)
    weights = p_grp[:, None] * top_p / jnp.sum(top_p, axis=-1, keepdims=True)
    experts = grp[:, None] * EXPERTS_PER_GROUP + top_i
    A = N * TOP_K_FINE
    e_flat = experts.reshape(-1)
    w_flat = weights.reshape(-1)
    t_flat = jnp.repeat(jnp.arange(N), TOP_K_FINE)
    order = jnp.argsort(e_flat)
    e_s, t_s, w_s = e_flat[order], t_flat[order], w_flat[order]
    counts = jnp.bincount(e_flat, length=N_EXPERTS)
    starts = jnp.cumsum(counts) - counts
    padded = (counts + MOE_BLOCK - 1) // MOE_BLOCK * MOE_BLOCK
    pends = jnp.cumsum(padded)
    pstarts = pends - padded
    dest = pstarts[e_s] + (jnp.arange(A) - starts[e_s])
    P = A + N_EXPERTS * MOE_BLOCK
    n_blocks = P // MOE_BLOCK
    x_buf = jnp.zeros((P, D), h.dtype).at[dest].set(tok[t_s])
    w_buf = jnp.zeros((P,), jnp.float32).at[dest].set(w_s)
    t_buf = jnp.zeros((P,), jnp.int32).at[dest].set(t_s.astype(jnp.int32))
    blk_expert = jnp.minimum(jnp.searchsorted(pends, jnp.arange(n_blocks) * MOE_BLOCK, side='right'), N_EXPERTS - 1)

    def run(xs):
        xb, e = xs
        return (jax.nn.silu(xb @ w_gate[e]) * (xb @ w_up[e])) @ w_down[e]

    y_buf = lax.map(run, (x_buf.reshape(n_blocks, MOE_BLOCK, D), blk_expert)).reshape(P, D)
    out = jnp.zeros((N, D), h.dtype).at[t_buf].add(y_buf * w_buf[:, None].astype(h.dtype))
    return out.reshape(B, S, D)


def setup_inputs(seed: int = 0) -> dict:
    key = jax.random.key(seed)
    ks = jax.random.split(key, 16)
    nrm = jax.random.normal
    D = D_MODEL
    x = nrm(ks[0], (BATCH, SEQ, D), jnp.float32)
    c = nrm(ks[1], (BATCH, D), jnp.float32)
    offs = jax.random.randint(ks[2], (BATCH, 1), 0, 4096)
    positions = (offs + jnp.arange(SEQ)[None, :]).astype(jnp.int32)
    w_ada = nrm(ks[3], (DEPTH, D, 6 * D), jnp.float32) * (0.5 * D ** -0.5)
    b_ada = 0.01 * nrm(ks[4], (DEPTH, 6 * D), jnp.float32)
    g_norm_mix = 1.0 + 0.01 * nrm(ks[5], (DEPTH, D), jnp.float32)
    w_in = nrm(ks[6], (DEPTH, D, D_IN), jnp.float32) * D ** -0.5
    w_o = nrm(ks[7], (DEPTH, D_MIX, D), jnp.float32) * D_MIX ** -0.5
    g_norm_ffn = 1.0 + 0.01 * nrm(ks[8], (DEPTH, D), jnp.float32)
    w_router_group = nrm(ks[9], (DEPTH, D, N_GROUPS), jnp.float32) * D ** -0.5
    w_router_expert = nrm(ks[10], (DEPTH, N_GROUPS, D, EXPERTS_PER_GROUP), jnp.float32) * D ** -0.5
    w_up = nrm(ks[11], (DEPTH, N_EXPERTS, D, D_EXPERT), jnp.float32) * D ** -0.5
    w_gate = nrm(ks[12], (DEPTH, N_EXPERTS, D, D_EXPERT), jnp.float32) * D ** -0.5
    w_down = nrm(ks[13], (DEPTH, N_EXPERTS, D_EXPERT, D), jnp.float32) * D_EXPERT ** -0.5
    g_norm_final = 1.0 + 0.01 * nrm(ks[14], (D,), jnp.float32)
    return {'x': x, 'c': c, 'positions': positions, 'w_ada': w_ada, 'b_ada': b_ada,
            'g_norm_mix': g_norm_mix, 'w_in': w_in, 'w_o': w_o, 'g_norm_ffn': g_norm_ffn,
            'w_router_group': w_router_group, 'w_router_expert': w_router_expert,
            'w_up': w_up, 'w_gate': w_gate, 'w_down': w_down, 'g_norm_final': g_norm_final}


def reference(x, c, positions, w_ada, b_ada, g_norm_mix, w_in, w_o, g_norm_ffn,
              w_router_group, w_router_expert, w_up, w_gate, w_down, g_norm_final):
    for l in range(DEPTH):
        mod = jnp.einsum('bd,de->be', c, w_ada[l]) + b_ada[l]
        sh1, sc1, gt1, sh2, sc2, gt2 = jnp.split(mod, 6, axis=-1)
        h = rmsnorm(x, g_norm_mix[l]) * (1.0 + sc1[:, None, :]) + sh1[:, None, :]
        x = x + gt1[:, None, :] * hybrid_mixer(h, positions, w_in[l], w_o[l])
        h = rmsnorm(x, g_norm_ffn[l]) * (1.0 + sc2[:, None, :]) + sh2[:, None, :]
        x = x + gt2[:, None, :] * hier_moe(h, w_router_group[l], w_router_expert[l], w_up[l], w_gate[l], w_down[l])
    return rmsnorm(x, g_norm_final)
```

```python
import functools

import jax
import jax.numpy as jnp
import numpy as np
from jax import lax
from jax.experimental import pallas as pl
from jax.experimental.pallas import tpu as pltpu

CHUNK = 64
HEAD_DIM = 64
N_IDX_HEADS = 16
IDX_DIM = 64
TOPK_MAX = 256
ROPE_THETA = 500000.0
ROT_DIM = HEAD_DIM // 4
RET_THETA = 10000.0
N_GROUPS = 4
EXPERTS_PER_GROUP = 8
N_EXPERTS = N_GROUPS * EXPERTS_PER_GROUP
EPS = 1e-6

LANES = 128
VMEM_LIMIT = 56 * 1024 * 1024

TM = 256
QB = 128
RET_C = 256
BM = 256
N_BISECT = 32
GAP = 2 * BM

NEG_BIG = -1e30


def _cparams(sem):
    return pltpu.CompilerParams(dimension_semantics=sem, vmem_limit_bytes=VMEM_LIMIT)


def _ada_kernel(c_ref, w_ref, b_ref, o_ref):
    o_ref[...] = jnp.dot(c_ref[...], w_ref[...], preferred_element_type=jnp.float32) + b_ref[...]


def _ada(c, w_ada, b_ada):
    bsz, d = c.shape
    n_out = w_ada.shape[1]
    return pl.pallas_call(
        _ada_kernel,
        grid=(n_out // d,),
        in_specs=[pl.BlockSpec((bsz, d), lambda j: (0, 0)),
                  pl.BlockSpec((d, d), lambda j: (0, j)),
                  pl.BlockSpec((1, d), lambda j: (0, j))],
        out_specs=pl.BlockSpec((bsz, d), lambda j: (0, j)),
        out_shape=jax.ShapeDtypeStruct((bsz, n_out), jnp.float32),
        compiler_params=_cparams(("arbitrary",)),
        name="ada",
    )(c, w_ada, b_ada.reshape(1, n_out))


def _rmsnorm_mod(x, g, sc, sh):
    xn = x * lax.rsqrt(jnp.mean(x * x, axis=-1, keepdims=True) + EPS)
    return xn * g * (1.0 + sc) + sh


def _rope_lanes(x, cos, sin_lo, sin_hi, half):
    cols = []
    for k in range(x.shape[1] // LANES):
        xb = x[:, k * LANES:(k + 1) * LANES]
        cols.append(xb * cos + pltpu.roll(xb, LANES - half, 1) * sin_lo + pltpu.roll(xb, half, 1) * sin_hi)
    return cols[0] if len(cols) == 1 else jnp.concatenate(cols, axis=1)


def _proj_kernel(x_ref, pos_ref, sc_ref, sh_ref, g_ref, w_ref, tab_ref,
                 qa_ref, kat_ref, va_ref, qi_ref, kit_ref, wi_ref, qb_ref, kb_ref, vb_ref, gb_ref,
                 *, d_a, d_i, d_b):
    h = _rmsnorm_mod(x_ref[...], g_ref[...], sc_ref[0], sh_ref[0]).astype(jnp.bfloat16)
    pos = pos_ref[...]
    ang_a = pos * tab_ref[0:1, :]
    cos_a, sin_a = jnp.cos(ang_a), jnp.sin(ang_a)
    sa_lo, sa_hi = sin_a * tab_ref[1:2, :], sin_a * tab_ref[2:3, :]
    ang_b = pos * tab_ref[3:4, :]
    cos_b, sin_b = jnp.cos(ang_b), jnp.sin(ang_b)
    sb_lo, sb_hi = sin_b * tab_ref[4:5, :], sin_b * tab_ref[5:6, :]
    half_a, half_b = ROT_DIM // 2, HEAD_DIM // 2

    def seg(lo, width):
        return jnp.dot(h, w_ref[:, lo:lo + width], preferred_element_type=jnp.float32)

    o = 0
    qa = seg(o, d_a); o += d_a
    qa_ref[...] = (_rope_lanes(qa, cos_a, sa_lo, sa_hi, half_a) * (HEAD_DIM ** -0.5)).astype(qa_ref.dtype)
    ka = seg(o, d_a); o += d_a
    kat_ref[0, 0] = _rope_lanes(ka, cos_a, sa_lo, sa_hi, half_a).T.astype(kat_ref.dtype)
    va_ref[...] = seg(o, d_a).astype(va_ref.dtype); o += d_a
    qi = seg(o, d_i); o += d_i
    qi_ref[...] = (_rope_lanes(qi, cos_a, sa_lo, sa_hi, half_a) * (IDX_DIM ** -0.5)).astype(qi_ref.dtype)
    kw = seg(o, LANES); o += LANES
    kit_ref[0, 0] = _rope_lanes(kw, cos_a, sa_lo, sa_hi, half_a).T[0:IDX_DIM, :].astype(kit_ref.dtype)
    wi_ref[...] = kw[:, IDX_DIM:IDX_DIM + N_IDX_HEADS] * (N_IDX_HEADS ** -0.5)
    qb = seg(o, d_b); o += d_b
    qb_ref[...] = _rope_lanes(qb, cos_b, sb_lo, sb_hi, half_b).astype(qb_ref.dtype)
    kb = seg(o, d_b); o += d_b
    kb_ref[...] = (_rope_lanes(kb, cos_b, sb_lo, sb_hi, half_b) * (HEAD_DIM ** -0.5)).astype(kb_ref.dtype)
    vb_ref[...] = seg(o, d_b).astype(vb_ref.dtype); o += d_b
    gb_ref[...] = seg(o, d_b)


def _rope_tables():
    lane = jnp.arange(LANES) % HEAD_DIM
    rows = []
    for rot, theta in ((ROT_DIM, ROPE_THETA), (HEAD_DIM, RET_THETA)):
        half = rot // 2
        inv_freq = theta ** (-jnp.arange(half, dtype=jnp.float32) / half)
        rows.append(jnp.where(lane < rot, inv_freq[lane % half], 0.0))
        rows.append(jnp.where(lane < half, -1.0, 0.0))
        rows.append(jnp.where((lane >= half) & (lane < rot), 1.0, 0.0))
    rows += [jnp.zeros((LANES,), jnp.float32)] * 2
    return jnp.stack(rows).astype(jnp.float32)


def _proj(x2, pos2, mod3, g_mix, w_cat, seq):
    n, d = x2.shape
    bsz = n // seq
    d_a = d // 2
    d_b = d // 2
    d_i = N_IDX_HEADS * IDX_DIM
    tiles_per_seq = seq // TM
    tab = _rope_tables()
    row = lambda i: (i, 0)
    const = lambda i: (0, 0)
    tile4 = lambda i: (i // tiles_per_seq, i % tiles_per_seq, 0, 0)
    bf = jnp.bfloat16
    out_shape = (
        jax.ShapeDtypeStruct((n, d_a), bf),
        jax.ShapeDtypeStruct((bsz, tiles_per_seq, d_a, TM), bf),
        jax.ShapeDtypeStruct((n, d_a), bf),
        jax.ShapeDtypeStruct((n, d_i), bf),
        jax.ShapeDtypeStruct((bsz, tiles_per_seq, IDX_DIM, TM), bf),
        jax.ShapeDtypeStruct((n, N_IDX_HEADS), jnp.float32),
        jax.ShapeDtypeStruct((n, d_b), bf),
        jax.ShapeDtypeStruct((n, d_b), bf),
        jax.ShapeDtypeStruct((n, d_b), bf),
        jax.ShapeDtypeStruct((n, d_b), jnp.float32),
    )
    out_specs = (
        pl.BlockSpec((TM, d_a), row),
        pl.BlockSpec((1, 1, d_a, TM), tile4),
        pl.BlockSpec((TM, d_a), row),
        pl.BlockSpec((TM, d_i), row),
        pl.BlockSpec((1, 1, IDX_DIM, TM), tile4),
        pl.BlockSpec((TM, N_IDX_HEADS), row),
        pl.BlockSpec((TM, d_b), row),
        pl.BlockSpec((TM, d_b), row),
        pl.BlockSpec((TM, d_b), row),
        pl.BlockSpec((TM, d_b), row),
    )
    return pl.pallas_call(
        functools.partial(_proj_kernel, d_a=d_a, d_i=d_i, d_b=d_b),
        grid=(n // TM,),
        in_specs=[pl.BlockSpec((TM, d), row),
                  pl.BlockSpec((TM, 1), row),
                  pl.BlockSpec((1, 1, d), lambda i: ((i // tiles_per_seq) * 6 + 1, 0, 0)),
                  pl.BlockSpec((1, 1, d), lambda i: ((i // tiles_per_seq) * 6 + 0, 0, 0)),
                  pl.BlockSpec((1, d), const),
                  pl.BlockSpec(w_cat.shape, const),
                  pl.BlockSpec(tab.shape, const)],
        out_specs=out_specs,
        out_shape=out_shape,
        compiler_params=_cparams(("arbitrary",)),
        name="proj",
    )(x2, pos2, mod3, mod3, g_mix, w_cat, tab)


def _dsa_kernel(qa_ref, qi_ref, wi_ref, kit_ref, kat_ref, va_ref, o_ref,
                qih_ref, wb_ref, sc_ref, lg_ref, *, k_top, n_heads):
    i = pl.program_id(1)
    n_kt = (i * QB + QB + TM - 1) // TM

    for h in range(N_IDX_HEADS):
        qih_ref[h] = qi_ref[:, h * IDX_DIM:(h + 1) * IDX_DIM]
        wb_ref[h] = jnp.broadcast_to(wi_ref[:, h:h + 1], (QB, TM))

    q_chunk = (i * QB + lax.broadcasted_iota(jnp.int32, (QB, TM), 0)) // CHUNK
    key_in_tile = lax.broadcasted_iota(jnp.int32, (QB, TM), 1)

    def idx_tile(j, carry):
        lo, hi = carry
        kt = kit_ref[0, j]
        acc = jnp.zeros((QB, TM), jnp.float32)
        for h in range(N_IDX_HEADS):
            d = jnp.dot(qih_ref[h], kt, preferred_element_type=jnp.float32)
            acc = acc + wb_ref[h] * jnp.maximum(d, 0.0)
        adm = (j * TM + key_in_tile) // CHUNK <= q_chunk
        sc_ref[j] = jnp.where(adm, acc, -jnp.inf)
        lo = jnp.minimum(lo, jnp.where(adm, acc, jnp.inf))
        hi = jnp.maximum(hi, jnp.where(adm, acc, -jnp.inf))
        return lo, hi

    lo, hi = lax.fori_loop(0, n_kt, idx_tile,
                           (jnp.full((QB, TM), jnp.inf, jnp.float32), jnp.full((QB, TM), -jnp.inf, jnp.float32)))
    lo = jnp.min(lo, axis=1, keepdims=True)
    hi = jnp.max(hi, axis=1, keepdims=True)

    def bisect(_, carry):
        lo, hi = carry
        mid = lo + (hi - lo) * 0.5

        def count_tile(j, cnt):
            return cnt + jnp.where(sc_ref[j] >= mid, 1.0, 0.0)

        cnt = lax.fori_loop(0, n_kt, count_tile, jnp.zeros((QB, TM), jnp.float32))
        ge = jnp.sum(cnt, axis=1, keepdims=True) >= k_top
        return jnp.where(ge, mid, lo), jnp.where(ge, hi, mid)

    thr, _ = lax.fori_loop(0, N_BISECT, bisect, (lo, hi))

    def bias_tile(j, carry):
        sc_ref[j] = jnp.where(sc_ref[j] >= thr, 0.0, NEG_BIG)
        return carry

    lax.fori_loop(0, n_kt, bias_tile, 0)

    for h in range(n_heads):
        q_h = qa_ref[:, h * HEAD_DIM:(h + 1) * HEAD_DIM]

        def logit_tile(j, m):
            s = jnp.dot(q_h, kat_ref[0, j, h * HEAD_DIM:(h + 1) * HEAD_DIM, :],
                        preferred_element_type=jnp.float32) + sc_ref[j]
            lg_ref[j] = s
            return jnp.maximum(m, s)

        m = lax.fori_loop(0, n_kt, logit_tile, jnp.full((QB, TM), NEG_BIG, jnp.float32))
        m = jnp.max(m, axis=1, keepdims=True)

        def pv_tile(j, carry):
            l, acc = carry
            p = jnp.exp(lg_ref[j] - m)
            acc = acc + jnp.dot(p.astype(va_ref.dtype), va_ref[0, j, :, h * HEAD_DIM:(h + 1) * HEAD_DIM],
                                preferred_element_type=jnp.float32)
            return l + p, acc

        l, acc = lax.fori_loop(0, n_kt, pv_tile, (jnp.zeros((QB, TM), jnp.float32),
                                                  jnp.zeros((QB, HEAD_DIM), jnp.float32)))
        o_ref[:, h * HEAD_DIM:(h + 1) * HEAD_DIM] = (acc / jnp.sum(l, axis=1, keepdims=True)).astype(o_ref.dtype)


def _dsa(qa, qi, wi, kit, kat, va4, seq):
    n, d_a = qa.shape
    bsz = n // seq
    n_heads = d_a // HEAD_DIM
    n_qb = seq // QB
    n_kt = seq // TM
    k_top = min(TOPK_MAX, seq // 4)
    qrow = lambda b, i: (b * n_qb + i, 0)
    per_b = lambda b, i: (b, 0, 0, 0)
    return pl.pallas_call(
        functools.partial(_dsa_kernel, k_top=float(k_top), n_heads=n_heads),
        grid=(bsz, n_qb),
        in_specs=[pl.BlockSpec((QB, d_a), qrow),
                  pl.BlockSpec((QB, qi.shape[1]), qrow),
                  pl.BlockSpec((QB, N_IDX_HEADS), qrow),
                  pl.BlockSpec((1, n_kt, IDX_DIM, TM), per_b, pipeline_mode=pl.Buffered(1)),
                  pl.BlockSpec((1, n_kt, d_a, TM), per_b, pipeline_mode=pl.Buffered(1)),
                  pl.BlockSpec((1, n_kt, TM, d_a), per_b, pipeline_mode=pl.Buffered(1))],
        out_specs=pl.BlockSpec((QB, d_a), qrow),
        out_shape=jax.ShapeDtypeStruct((n, d_a), jnp.bfloat16),
        scratch_shapes=[pltpu.VMEM((N_IDX_HEADS, QB, IDX_DIM), jnp.bfloat16),
                        pltpu.VMEM((N_IDX_HEADS, QB, TM), jnp.float32),
                        pltpu.VMEM((n_kt, QB, TM), jnp.float32),
                        pltpu.VMEM((n_kt, QB, TM), jnp.float32)],
        compiler_params=_cparams(("arbitrary", "arbitrary")),
        name="dsa",
    )(qa, qi, wi, kit, kat, va4)


def _ret_kernel(q_ref, k_ref, v_ref, g_ref, dec_ref, zeta_ref, xi_ref, gc_ref, o_ref, st_ref, *, n_heads):
    @pl.when(pl.program_id(1) == 0)
    def _():
        st_ref[...] = jnp.zeros_like(st_ref)

    for h in range(n_heads):
        sl = slice(h * HEAD_DIM, (h + 1) * HEAD_DIM)
        q, k, v = q_ref[:, sl], k_ref[:, sl], v_ref[:, sl]
        s = lax.dot_general(q, k, (((1,), (1,)), ((), ())), preferred_element_type=jnp.float32) * dec_ref[h]
        inner = jnp.dot(s.astype(v.dtype), v, preferred_element_type=jnp.float32)
        state = st_ref[h]
        cross = jnp.dot(q, state.astype(q.dtype), preferred_element_type=jnp.float32) * xi_ref[h]
        y = inner + cross
        mu = jnp.mean(y, axis=-1, keepdims=True)
        yc = y - mu
        yn = yc * lax.rsqrt(jnp.mean(yc * yc, axis=-1, keepdims=True) + EPS)
        g = g_ref[:, sl]
        o_ref[:, sl] = (g / (1.0 + jnp.exp(-g)) * yn).astype(o_ref.dtype)
        kz = (k.astype(jnp.float32) * zeta_ref[h]).astype(k.dtype)
        kv = lax.dot_general(kz, v, (((0,), (0,)), ((), ())), preferred_element_type=jnp.float32)
        st_ref[h] = state * gc_ref[h] + kv


def _ret_consts(n_heads):
    log_gamma = jnp.log1p(-jnp.exp2(-5.0 - jnp.arange(n_heads, dtype=jnp.float32)))
    pos = jnp.arange(RET_C, dtype=jnp.float32)
    diff = pos[:, None] - pos[None, :]
    dec = jnp.where(diff[None] >= 0, jnp.exp(jnp.maximum(diff, 0.0)[None] * log_gamma[:, None, None]), 0.0)
    zeta = jnp.exp((RET_C - 1.0 - pos)[None, :] * log_gamma[:, None])
    xi = jnp.exp((pos + 1.0)[None, :] * log_gamma[:, None])
    gc = jnp.exp(RET_C * log_gamma)
    bc = lambda a, shape: jnp.broadcast_to(a, shape).astype(jnp.float32)
    return (dec, bc(zeta[:, :, None], (n_heads, RET_C, HEAD_DIM)), bc(xi[:, :, None], (n_heads, RET_C, HEAD_DIM)),
            bc(gc[:, None, None], (n_heads, HEAD_DIM, HEAD_DIM)))


def _ret(qb, kb, vb, gb, seq):
    n, d_b = qb.shape
    bsz = n // seq
    n_heads = d_b // HEAD_DIM
    n_c = seq // RET_C
    dec, zeta, xi, gc = _ret_consts(n_heads)
    row = lambda b, c: (b * n_c + c, 0)
    const3 = lambda b, c: (0, 0, 0)
    return pl.pallas_call(
        functools.partial(_ret_kernel, n_heads=n_heads),
        grid=(bsz, n_c),
        in_specs=[pl.BlockSpec((RET_C, d_b), row)] * 4 + [
            pl.BlockSpec(dec.shape, const3), pl.BlockSpec(zeta.shape, const3),
            pl.BlockSpec(xi.shape, const3), pl.BlockSpec(gc.shape, const3)],
        out_specs=pl.BlockSpec((RET_C, d_b), row),
        out_shape=jax.ShapeDtypeStruct((n, d_b), jnp.bfloat16),
        scratch_shapes=[pltpu.VMEM((n_heads, HEAD_DIM, HEAD_DIM), jnp.float32)],
        compiler_params=_cparams(("arbitrary", "arbitrary")),
        name="ret",
    )(qb, kb, vb, gb, dec, zeta, xi, gc)


def _lane_first_eq(x, m, lane):
    return jnp.min(jnp.where(x == m, lane, float(LANES)), axis=1, keepdims=True)


def _mix_out_kernel(x_ref, ya_ref, yb_ref, woa_ref, wob_ref, gt_ref, sc_ref, sh_ref, g_ref, wr_ref, tri_ref,
                    x1_ref, h2_ref, route_ref, cnt_ref, carry_ref):
    @pl.when(pl.program_id(0) == 0)
    def _():
        carry_ref[...] = jnp.zeros_like(carry_ref)

    mix = (jnp.dot(ya_ref[...], woa_ref[...], preferred_element_type=jnp.float32)
           + jnp.dot(yb_ref[...], wob_ref[...], preferred_element_type=jnp.float32))
    x1 = x_ref[...] + gt_ref[0] * mix
    x1_ref[...] = x1
    h2 = _rmsnorm_mod(x1, g_ref[...], sc_ref[0], sh_ref[0])
    h2_ref[...] = h2

    lg = jnp.dot(h2.astype(jnp.bfloat16), wr_ref[...], preferred_element_type=jnp.float32)
    lane = lax.broadcasted_iota(jnp.int32, lg.shape, 1).astype(jnp.float32)
    is_grp = (lane >= N_EXPERTS) & (lane < N_EXPERTS + N_GROUPS)
    gl = jnp.where(is_grp, lg, -jnp.inf)
    gmax = jnp.max(gl, axis=1, keepdims=True)
    grp = _lane_first_eq(gl, gmax, lane) - N_EXPERTS
    p_grp = 1.0 / jnp.sum(jnp.exp(gl - gmax), axis=1, keepdims=True)
    in_grp = jnp.floor(lane * (1.0 / EXPERTS_PER_GROUP)) == grp
    f = jnp.where(in_grp & (lane < N_EXPERTS), lg, -jnp.inf)
    f1 = jnp.max(f, axis=1, keepdims=True)
    e1 = _lane_first_eq(f, f1, lane)
    f = jnp.where(lane == e1, -jnp.inf, f)
    f2 = jnp.max(f, axis=1, keepdims=True)
    e2 = _lane_first_eq(f, f2, lane)
    a2 = jnp.exp(f2 - f1)
    w1 = p_grp / (1.0 + a2)
    w2 = p_grp * a2 / (1.0 + a2)

    oh1 = jnp.where(lane == e1, 1.0, 0.0)
    oh2 = jnp.where(lane == e2, 1.0, 0.0)
    both = oh1 + oh2
    before = jnp.dot(tri_ref[...], both.astype(jnp.bfloat16), preferred_element_type=jnp.float32) + carry_ref[...]
    r1 = jnp.sum(before * oh1, axis=1, keepdims=True)
    r2 = jnp.sum(before * oh2, axis=1, keepdims=True)
    carry = carry_ref[...] + jnp.sum(both, axis=0, keepdims=True)
    carry_ref[...] = carry
    cnt_ref[...] = carry

    out = jnp.zeros(lg.shape, jnp.float32)
    for col, val in enumerate((e1, e2, w1, w2, r1, r2)):
        out = jnp.where(lane == col, val, out)
    route_ref[...] = out


def _mix_out(x2, ya, yb, wo_a, wo_b, mod3, g_ffn, w_route, seq):
    n, d = x2.shape
    tiles_per_seq = seq // TM
    tri = jnp.asarray(np.tril(np.ones((TM, TM), np.float32), -1), jnp.bfloat16)
    row = lambda i: (i, 0)
    const = lambda i: (0, 0)
    modk = lambda k: pl.BlockSpec((1, 1, d), lambda i: ((i // tiles_per_seq) * 6 + k, 0, 0))
    return pl.pallas_call(
        _mix_out_kernel,
        grid=(n // TM,),
        in_specs=[pl.BlockSpec((TM, d), row),
                  pl.BlockSpec((TM, ya.shape[1]), row),
                  pl.BlockSpec((TM, yb.shape[1]), row),
                  pl.BlockSpec(wo_a.shape, const),
                  pl.BlockSpec(wo_b.shape, const),
                  modk(2), modk(4), modk(3),
                  pl.BlockSpec((1, d), const),
                  pl.BlockSpec(w_route.shape, const),
                  pl.BlockSpec(tri.shape, const)],
        out_specs=(pl.BlockSpec((TM, d), row), pl.BlockSpec((TM, d), row),
                   pl.BlockSpec((TM, LANES), row), pl.BlockSpec((1, LANES), const)),
        out_shape=(jax.ShapeDtypeStruct((n, d), jnp.float32), jax.ShapeDtypeStruct((n, d), jnp.float32),
                   jax.ShapeDtypeStruct((n, LANES), jnp.float32), jax.ShapeDtypeStruct((1, LANES), jnp.float32)),
        scratch_shapes=[pltpu.VMEM((1, LANES), jnp.float32)],
        compiler_params=_cparams(("arbitrary",)),
        name="mix_out",
    )(x2, ya, yb, wo_a, wo_b, mod3, mod3, mod3, g_ffn, w_route, tri)


def _experts_kernel(blk_e_ref, n_used_ref, src_ref, src_next_ref, dst_ref, h2_hbm, wg_ref, wu_ref, wd_ref, y_hbm,
                    xbuf, ybuf, gsem, ssem, *, n_tok):
    j = pl.program_id(0)
    n_used = n_used_ref[0]
    slot = j % 2

    def gather(idx_ref, s):
        def issue(r, c):
            pltpu.make_async_copy(h2_hbm.at[pl.ds(idx_ref[0, 0, r], 1)], xbuf.at[s, pl.ds(r, 1)], gsem.at[s]).start()
            return c
        lax.fori_loop(0, BM, issue, 0)

    def wait_gather(s):
        pltpu.make_async_copy(h2_hbm.at[pl.ds(0, BM)], xbuf.at[s], gsem.at[s]).wait()

    def wait_scatter(s):
        pltpu.make_async_copy(ybuf.at[s], y_hbm.at[pl.ds(0, BM)], ssem.at[s]).wait()

    @pl.when(j == 0)
    def _():
        ybuf[1] = jnp.zeros(ybuf.shape[1:], ybuf.dtype)
        for g in range(GAP // BM):
            fill = pltpu.make_async_copy(ybuf.at[1], y_hbm.at[pl.ds(n_tok + g * BM, BM)], ssem.at[1])
            fill.start()
            fill.wait()
        gather(src_ref, 0)

    @pl.when(j + 1 < n_used)
    def _():
        gather(src_next_ref, 1 - slot)

    @pl.when(j < n_used)
    def _():
        wait_gather(slot)

        @pl.when(j >= 2)
        def _():
            wait_scatter(slot)

        x = xbuf[slot].astype(wg_ref.dtype)
        a = jnp.dot(x, wg_ref[0], preferred_element_type=jnp.float32)
        b = jnp.dot(x, wu_ref[0], preferred_element_type=jnp.float32)
        hmid = (a / (1.0 + jnp.exp(-a)) * b).astype(x.dtype)
        ybuf[slot] = jnp.dot(hmid, wd_ref[0], preferred_element_type=jnp.float32)

        def issue(r, c):
            pltpu.make_async_copy(ybuf.at[slot, pl.ds(r, 1)], y_hbm.at[pl.ds(dst_ref[0, 0, r], 1)],
                                  ssem.at[slot]).start()
            return c
        lax.fori_loop(0, BM, issue, 0)

    @pl.when(j == n_used - 1)
    def _():
        wait_scatter(slot)

        @pl.when(j >= 1)
        def _():
            wait_scatter(1 - slot)


def _experts(h2, wg, wu, wd, blk_e, n_used, src, dst, n_rows_out):
    n, d = h2.shape
    n_blk = src.shape[0]
    d_e = wg.shape[2]
    smem_row = lambda off: pl.BlockSpec((1, 1, BM), lambda j, be, nu: (jnp.minimum(j + off, n_blk - 1), 0, 0),
                                        memory_space=pltpu.SMEM)
    grid_spec = pltpu.PrefetchScalarGridSpec(
        num_scalar_prefetch=2,
        grid=(n_blk,),
        in_specs=[smem_row(0), smem_row(1), smem_row(0),
                  pl.BlockSpec(memory_space=pl.ANY),
                  pl.BlockSpec((1, d, d_e), lambda j, be, nu: (be[j], 0, 0)),
                  pl.BlockSpec((1, d, d_e), lambda j, be, nu: (be[j], 0, 0)),
                  pl.BlockSpec((1, d_e, d), lambda j, be, nu: (be[j], 0, 0))],
        out_specs=pl.BlockSpec(memory_space=pl.ANY),
        scratch_shapes=[pltpu.VMEM((2, BM, d), h2.dtype),
                        pltpu.VMEM((2, BM, d), jnp.float32),
                        pltpu.SemaphoreType.DMA((2,)),
                        pltpu.SemaphoreType.DMA((2,))],
    )
    return pl.pallas_call(
        functools.partial(_experts_kernel, n_tok=n),
        grid_spec=grid_spec,
        out_shape=jax.ShapeDtypeStruct((n_rows_out, d), jnp.float32),
        compiler_params=_cparams(("arbitrary",)),
        name="experts",
    )(blk_e, n_used, src, src, dst, h2, wg, wu, wd)


def _final_kernel(x1_ref, y0_ref, y1_ref, route_ref, gt_ref, g_ref, o_ref):
    w1 = route_ref[:, 2:3]
    w2 = route_ref[:, 3:4]
    x2 = x1_ref[...] + gt_ref[0] * (w1 * y0_ref[...] + w2 * y1_ref[...])
    o_ref[...] = x2 * lax.rsqrt(jnp.mean(x2 * x2, axis=-1, keepdims=True) + EPS) * g_ref[...]


def _final(x1, y, route, mod3, g_final, seq):
    n, d = x1.shape
    tiles_per_seq = seq // TM
    slot1 = (n + GAP) // TM
    row = lambda i: (i, 0)
    return pl.pallas_call(
        _final_kernel,
        grid=(n // TM,),
        in_specs=[pl.BlockSpec((TM, d), row),
                  pl.BlockSpec((TM, d), row),
                  pl.BlockSpec((TM, d), lambda i: (slot1 + i, 0)),
                  pl.BlockSpec((TM, LANES), row),
                  pl.BlockSpec((1, 1, d), lambda i: ((i // tiles_per_seq) * 6 + 5, 0, 0)),
                  pl.BlockSpec((1, d), lambda i: (0, 0))],
        out_specs=pl.BlockSpec((TM, d), row),
        out_shape=jax.ShapeDtypeStruct((n, d), jnp.float32),
        compiler_params=_cparams(("arbitrary",)),
        name="final",
    )(x1, y, y, route, mod3, g_final)


def _dispatch_plan(route, counts, n):
    e1 = route[:, 0].astype(jnp.int32)
    e2 = route[:, 1].astype(jnp.int32)
    r1 = route[:, 4].astype(jnp.int32)
    r2 = route[:, 5].astype(jnp.int32)
    cnt = counts[0, :N_EXPERTS].astype(jnp.int32)
    blocks = (cnt + BM - 1) // BM
    bends = jnp.cumsum(blocks)
    pstarts = (bends - blocks) * BM
    n_blk = (2 * n) // BM + N_EXPERTS
    tok = jnp.arange(n, dtype=jnp.int32)
    d1 = pstarts[e1] + r1
    d2 = pstarts[e2] + r2
    trash = n + jnp.arange(n_blk * BM, dtype=jnp.int32) % GAP
    src = jnp.zeros((n_blk * BM,), jnp.int32).at[d1].set(tok).at[d2].set(tok)
    dst = trash.at[d1].set(tok).at[d2].set(tok + n + GAP)
    n_used = bends[-1:].astype(jnp.int32)
    blk_e = jnp.minimum(jnp.searchsorted(bends, jnp.arange(n_blk, dtype=jnp.int32), side="right"),
                        N_EXPERTS - 1).astype(jnp.int32)
    last_e = blk_e[jnp.maximum(n_used[0] - 1, 0)]
    blk_e = jnp.where(jnp.arange(n_blk) < n_used[0], blk_e, last_e)
    return blk_e, n_used, src.reshape(n_blk, 1, BM), dst.reshape(n_blk, 1, BM)


def _layer(x2, c, pos2, w_ada, b_ada, g_mix, w_in, w_o, g_ffn, w_rg, w_re, w_up, w_gate, w_down, seq):
    n, d = x2.shape
    bsz = n // seq
    bf = jnp.bfloat16
    d_a = d // 2
    d_i = N_IDX_HEADS * IDX_DIM
    mod3 = _ada(c, w_ada, b_ada).reshape(bsz * 6, 1, d)

    c0 = 3 * d_a + d_i
    c1 = c0 + IDX_DIM + N_IDX_HEADS
    w_cat = jnp.concatenate([w_in[:, :c0], w_in[:, c0:c1],
                             jnp.zeros((d, LANES - (c1 - c0)), w_in.dtype), w_in[:, c1:]], axis=1).astype(bf)
    qa, kat, va, qi, kit, wi, qb, kb, vb, gb = _proj(x2, pos2, mod3, g_mix.reshape(1, d), w_cat, seq)

    ya = _dsa(qa, qi, wi, kit, kat, va.reshape(bsz, seq // TM, TM, d_a), seq)
    yb = _ret(qb, kb, vb, gb, seq)

    w_route = jnp.concatenate([jnp.transpose(w_re, (1, 0, 2)).reshape(d, N_EXPERTS), w_rg,
                               jnp.zeros((d, LANES - N_EXPERTS - N_GROUPS), w_rg.dtype)], axis=1).astype(bf)
    x1, h2, route, counts = _mix_out(x2, ya, yb, w_o[:d_a].astype(bf), w_o[d_a:].astype(bf), mod3,
                                     g_ffn.reshape(1, d), w_route, seq)

    blk_e, n_used, src, dst = _dispatch_plan(route, counts, n)
    y = _experts(h2, w_gate.astype(bf), w_up.astype(bf), w_down.astype(bf), blk_e, n_used, src, dst, 2 * n + GAP)
    return x1, y, route, mod3


def kernel(x, c, positions, w_ada, b_ada, g_norm_mix, w_in, w_o, g_norm_ffn, w_router_group, w_router_expert,
           w_up, w_gate, w_down, g_norm_final):
    bsz, seq, d = x.shape
    depth = w_ada.shape[0]
    assert depth == 1, "the final norm is fused into the last layer's combine kernel"
    assert seq % TM == 0 and seq % RET_C == 0 and TM % QB == 0 and (2 * bsz * seq) % BM == 0 and GAP % TM == 0
    x2 = x.reshape(bsz * seq, d)
    pos2 = positions.astype(jnp.float32).reshape(bsz * seq, 1)
    x1, y, route, mod3 = _layer(x2, c, pos2, w_ada[0], b_ada[0], g_norm_mix[0], w_in[0], w_o[0], g_norm_ffn[0],
                                w_router_group[0], w_router_expert[0], w_up[0], w_gate[0], w_down[0], seq)
    out = _final(x1, y, route, mod3, g_norm_final.reshape(1, d), seq)
    return out.reshape(bsz, seq, d)
```

```python
import functools

import jax
import jax.numpy as jnp
import numpy as np
from jax import lax
from jax.experimental import pallas as pl
from jax.experimental.pallas import tpu as pltpu

CHUNK = 64
HEAD_DIM = 64
N_IDX_HEADS = 16
IDX_DIM = 64
TOPK_MAX = 256
ROPE_THETA = 500000.0
ROT_DIM = HEAD_DIM // 4
RET_THETA = 10000.0
N_GROUPS = 4
EXPERTS_PER_GROUP = 8
N_EXPERTS = N_GROUPS * EXPERTS_PER_GROUP
EPS = 1e-6

LANES = 128
VMEM_LIMIT = 56 * 1024 * 1024

TM = 256
QB = 128
RET_C = 256
BM = 256
KG = 2
MAX_BISECT = 40
GAP = 2 * BM

NEG_BIG = -1e30
LOG2E = 1.4426950408889634


def _cparams(sem):
    return pltpu.CompilerParams(dimension_semantics=sem, vmem_limit_bytes=VMEM_LIMIT)


def _ada_kernel(c_ref, w_ref, b_ref, o_ref):
    o_ref[...] = jnp.dot(c_ref[...], w_ref[...], preferred_element_type=jnp.float32) + b_ref[...]


def _ada(c, w_ada, b_ada):
    bsz, d = c.shape
    n_out = w_ada.shape[1]
    return pl.pallas_call(
        _ada_kernel,
        grid=(n_out // d,),
        in_specs=[pl.BlockSpec((bsz, d), lambda j: (0, 0)),
                  pl.BlockSpec((d, d), lambda j: (0, j)),
                  pl.BlockSpec((1, d), lambda j: (0, j))],
        out_specs=pl.BlockSpec((bsz, d), lambda j: (0, j)),
        out_shape=jax.ShapeDtypeStruct((bsz, n_out), jnp.float32),
        compiler_params=_cparams(("arbitrary",)),
        name="ada",
    )(c, w_ada, b_ada.reshape(1, n_out))


def _rmsnorm_mod(x, g, sc, sh):
    xn = x * lax.rsqrt(jnp.mean(x * x, axis=-1, keepdims=True) + EPS)
    return xn * g * (1.0 + sc) + sh


def _rope_lanes(x, cos, sin_lo, sin_hi, half):
    cols = []
    for k in range(x.shape[1] // LANES):
        xb = x[:, k * LANES:(k + 1) * LANES]
        cols.append(xb * cos + pltpu.roll(xb, LANES - half, 1) * sin_lo + pltpu.roll(xb, half, 1) * sin_hi)
    return cols[0] if len(cols) == 1 else jnp.concatenate(cols, axis=1)


def _proj_kernel(x_ref, pos_ref, sc_ref, sh_ref, g_ref, w_ref, tab_ref,
                 qa_ref, kat_ref, va_ref, qi_ref, kit_ref, wi_ref, qb_ref, kb_ref, vb_ref, gb_ref,
                 *, d_a, d_i, d_b):
    h = _rmsnorm_mod(x_ref[...], g_ref[...], sc_ref[0], sh_ref[0]).astype(jnp.bfloat16)
    pos = pos_ref[...]
    ang_a = pos * tab_ref[0:1, :]
    cos_a, sin_a = jnp.cos(ang_a), jnp.sin(ang_a)
    sa_lo, sa_hi = sin_a * tab_ref[1:2, :], sin_a * tab_ref[2:3, :]
    ang_b = pos * tab_ref[3:4, :]
    cos_b, sin_b = jnp.cos(ang_b), jnp.sin(ang_b)
    sb_lo, sb_hi = sin_b * tab_ref[4:5, :], sin_b * tab_ref[5:6, :]
    half_a, half_b = ROT_DIM // 2, HEAD_DIM // 2

    def seg(lo, width):
        return jnp.dot(h, w_ref[:, lo:lo + width], preferred_element_type=jnp.float32)

    o = 0
    qa = seg(o, d_a); o += d_a
    qa_ref[...] = (_rope_lanes(qa, cos_a, sa_lo, sa_hi, half_a) * (HEAD_DIM ** -0.5 * LOG2E)).astype(qa_ref.dtype)
    ka = seg(o, d_a); o += d_a
    kat_ref[0, 0] = _rope_lanes(ka, cos_a, sa_lo, sa_hi, half_a).T.astype(kat_ref.dtype)
    va_ref[...] = seg(o, d_a).astype(va_ref.dtype); o += d_a
    qi = seg(o, d_i); o += d_i
    qi_ref[...] = (_rope_lanes(qi, cos_a, sa_lo, sa_hi, half_a) * (IDX_DIM ** -0.5)).astype(qi_ref.dtype)
    kw = seg(o, LANES); o += LANES
    kit_ref[0, 0] = _rope_lanes(kw, cos_a, sa_lo, sa_hi, half_a).T[0:IDX_DIM, :].astype(kit_ref.dtype)
    wi_ref[...] = kw[:, IDX_DIM:IDX_DIM + N_IDX_HEADS] * (N_IDX_HEADS ** -0.5)
    qb = seg(o, d_b); o += d_b
    qb_ref[...] = _rope_lanes(qb, cos_b, sb_lo, sb_hi, half_b).astype(qb_ref.dtype)
    kb = seg(o, d_b); o += d_b
    kb_ref[...] = (_rope_lanes(kb, cos_b, sb_lo, sb_hi, half_b) * (HEAD_DIM ** -0.5)).astype(kb_ref.dtype)
    vb_ref[...] = seg(o, d_b).astype(vb_ref.dtype); o += d_b
    gb_ref[...] = seg(o, d_b)


def _rope_tables():
    lane = jnp.arange(LANES) % HEAD_DIM
    rows = []
    for rot, theta in ((ROT_DIM, ROPE_THETA), (HEAD_DIM, RET_THETA)):
        half = rot // 2
        inv_freq = theta ** (-jnp.arange(half, dtype=jnp.float32) / half)
        rows.append(jnp.where(lane < rot, inv_freq[lane % half], 0.0))
        rows.append(jnp.where(lane < half, -1.0, 0.0))
        rows.append(jnp.where((lane >= half) & (lane < rot), 1.0, 0.0))
    rows += [jnp.zeros((LANES,), jnp.float32)] * 2
    return jnp.stack(rows).astype(jnp.float32)


def _proj(x2, pos2, mod3, g_mix, w_cat, seq):
    n, d = x2.shape
    bsz = n // seq
    d_a = d // 2
    d_b = d // 2
    d_i = N_IDX_HEADS * IDX_DIM
    tiles_per_seq = seq // TM
    tab = _rope_tables()
    row = lambda i: (i, 0)
    const = lambda i: (0, 0)
    tile4 = lambda i: (i // tiles_per_seq, i % tiles_per_seq, 0, 0)
    bf = jnp.bfloat16
    out_shape = (
        jax.ShapeDtypeStruct((n, d_a), bf),
        jax.ShapeDtypeStruct((bsz, tiles_per_seq, d_a, TM), bf),
        jax.ShapeDtypeStruct((n, d_a), bf),
        jax.ShapeDtypeStruct((n, d_i), bf),
        jax.ShapeDtypeStruct((bsz, tiles_per_seq, IDX_DIM, TM), bf),
        jax.ShapeDtypeStruct((n, N_IDX_HEADS), jnp.float32),
        jax.ShapeDtypeStruct((n, d_b), bf),
        jax.ShapeDtypeStruct((n, d_b), bf),
        jax.ShapeDtypeStruct((n, d_b), bf),
        jax.ShapeDtypeStruct((n, d_b), jnp.float32),
    )
    out_specs = (
        pl.BlockSpec((TM, d_a), row),
        pl.BlockSpec((1, 1, d_a, TM), tile4),
        pl.BlockSpec((TM, d_a), row),
        pl.BlockSpec((TM, d_i), row),
        pl.BlockSpec((1, 1, IDX_DIM, TM), tile4),
        pl.BlockSpec((TM, N_IDX_HEADS), row),
        pl.BlockSpec((TM, d_b), row),
        pl.BlockSpec((TM, d_b), row),
        pl.BlockSpec((TM, d_b), row),
        pl.BlockSpec((TM, d_b), row),
    )
    return pl.pallas_call(
        functools.partial(_proj_kernel, d_a=d_a, d_i=d_i, d_b=d_b),
        grid=(n // TM,),
        in_specs=[pl.BlockSpec((TM, d), row),
                  pl.BlockSpec((TM, 1), row),
                  pl.BlockSpec((1, 1, d), lambda i: ((i // tiles_per_seq) * 6 + 1, 0, 0)),
                  pl.BlockSpec((1, 1, d), lambda i: ((i // tiles_per_seq) * 6 + 0, 0, 0)),
                  pl.BlockSpec((1, d), const),
                  pl.BlockSpec(w_cat.shape, const),
                  pl.BlockSpec(tab.shape, const)],
        out_specs=out_specs,
        out_shape=out_shape,
        compiler_params=_cparams(("arbitrary",)),
        name="proj",
    )(x2, pos2, mod3, mod3, g_mix, w_cat, tab)


def _lane_blocks(x):
    return [x[:, c * LANES:(c + 1) * LANES] for c in range(x.shape[1] // LANES)]


def _dsa_kernel(qa_ref, qi_ref, wi_ref, kit_ref, kat_ref, va_ref, o_ref,
                qih_ref, wb_ref, sc_ref, qm_ref, m_ref, l_ref, acc_ref, sa_ref, sb_ref, mxa_ref, mxb_ref,
                *, k_top, n_heads):
    i = pl.program_id(1)
    n_grp = (i * QB + QB + KG * TM - 1) // (KG * TM)
    n_kt = n_grp * KG

    for h in range(N_IDX_HEADS):
        qih_ref[h * QB:(h + 1) * QB, :] = qi_ref[:, h * IDX_DIM:(h + 1) * IDX_DIM]
        wb_ref[h] = jnp.broadcast_to(wi_ref[:, h:h + 1], (QB, TM))

    lane_q = lax.broadcasted_iota(jnp.int32, (QB, LANES), 1)
    for h in range(n_heads):
        pair = qa_ref[:, (h // 2) * LANES:(h // 2 + 1) * LANES]
        own = (lane_q < HEAD_DIM) if h % 2 == 0 else (lane_q >= HEAD_DIM)
        qm_ref[h] = jnp.where(own, pair, jnp.zeros_like(pair))

    q_chunk = (i * QB + lax.broadcasted_iota(jnp.int32, (QB, TM), 0)) // CHUNK
    key_in_tile = lax.broadcasted_iota(jnp.int32, (QB, TM), 1)

    def idx_grp(g, carry):
        lo, hi = carry
        for u in range(KG):
            j = g * KG + u
            d = jnp.dot(qih_ref[...], kit_ref[0, j], preferred_element_type=jnp.float32)
            acc = wb_ref[0] * jnp.maximum(d[0:QB], 0.0)
            for h in range(1, N_IDX_HEADS):
                acc = acc + wb_ref[h] * jnp.maximum(d[h * QB:(h + 1) * QB], 0.0)
            adm = (j * TM + key_in_tile) // CHUNK <= q_chunk
            sc_ref[j] = jnp.where(adm, acc, -jnp.inf)
            lo = jnp.minimum(lo, jnp.where(adm, acc, jnp.inf))
            hi = jnp.maximum(hi, jnp.where(adm, acc, -jnp.inf))
        return lo, hi

    lo, hi = lax.fori_loop(0, n_grp, idx_grp,
                           (jnp.full((QB, TM), jnp.inf, jnp.float32), jnp.full((QB, TM), -jnp.inf, jnp.float32)))
    lo = jnp.min(lo, axis=1, keepdims=True)
    hi = jnp.max(hi, axis=1, keepdims=True)
    n_adm = ((i * QB + lax.broadcasted_iota(jnp.int32, (QB, 1), 0)) // CHUNK + 1) * CHUNK

    def bisect_cond(carry):
        it, _, _, _, unsettled = carry
        return (it < MAX_BISECT) & (unsettled > 0.0)

    def bisect(carry):
        it, lo, hi, cnt_lo, _ = carry
        mid = lo + (hi - lo) * 0.5
        mid_b = jnp.broadcast_to(mid, (QB, LANES))

        def count_grp(g, cnt):
            for u in range(KG):
                for blk in _lane_blocks(sc_ref[g * KG + u]):
                    cnt = cnt + jnp.where(blk >= mid_b, 1.0, 0.0)
            return cnt

        cnt = lax.fori_loop(0, n_grp, count_grp, jnp.zeros((QB, LANES), jnp.float32))
        c = jnp.sum(cnt, axis=1, keepdims=True)
        ge = c >= k_top
        lo = jnp.where(ge, mid, lo)
        hi = jnp.where(ge, hi, mid)
        cnt_lo = jnp.where(ge, c, cnt_lo)
        return it + 1, lo, hi, cnt_lo, jnp.max(jnp.where(cnt_lo > k_top, 1.0, 0.0))

    cnt0 = n_adm.astype(jnp.float32)
    _, thr, _, _, _ = lax.while_loop(bisect_cond, bisect,
                                     (0, lo, hi, cnt0, jnp.max(jnp.where(cnt0 > k_top, 1.0, 0.0))))

    def bias_tile(j, carry):
        sc_ref[j] = jnp.where(sc_ref[j] >= thr, 0.0, NEG_BIG)
        return carry

    lax.fori_loop(0, n_kt, bias_tile, 0)

    m_ref[...] = jnp.full(m_ref.shape, NEG_BIG, jnp.float32)
    l_ref[...] = jnp.zeros(l_ref.shape, jnp.float32)
    acc_ref[...] = jnp.zeros(acc_ref.shape, jnp.float32)

    def pair_rows(h):
        return slice((h // 2) * LANES, (h // 2 + 1) * LANES)

    def logits_into(s_ref, mx_ref, j):
        bias = sc_ref[j]
        for h in range(n_heads):
            s = jnp.dot(qm_ref[h], kat_ref[0, j, pair_rows(h), :], preferred_element_type=jnp.float32) + bias
            s_ref[h] = s
            mx = functools.reduce(jnp.maximum, _lane_blocks(s))
            mx_ref[h] = jnp.broadcast_to(jnp.max(mx, axis=1, keepdims=True), (QB, LANES))

    def absorb(s_ref, mx_ref, j):
        for h in range(n_heads):
            m_old = m_ref[h]
            m_new = jnp.maximum(m_old, mx_ref[h])
            alpha = jnp.exp2(m_old - m_new)
            p = [jnp.exp2(b - m_new) for b in _lane_blocks(s_ref[h])]
            l_ref[h] = l_ref[h] * alpha + functools.reduce(jnp.add, p)
            pv = jnp.dot(jnp.concatenate(p, axis=1).astype(va_ref.dtype), va_ref[0, j, :, pair_rows(h)],
                         preferred_element_type=jnp.float32)
            acc_ref[h] = acc_ref[h] * alpha + pv
            m_ref[h] = m_new

    logits_into(sa_ref, mxa_ref, 0)

    def attn_grp(g, carry):
        logits_into(sb_ref, mxb_ref, 2 * g + 1)
        absorb(sa_ref, mxa_ref, 2 * g)
        logits_into(sa_ref, mxa_ref, jnp.minimum(2 * g + 2, n_kt - 1))
        absorb(sb_ref, mxb_ref, 2 * g + 1)
        return carry

    lax.fori_loop(0, n_grp, attn_grp, 0)

    for hp in range(n_heads // 2):
        even = acc_ref[2 * hp] / jnp.sum(l_ref[2 * hp], axis=1, keepdims=True)
        odd = acc_ref[2 * hp + 1] / jnp.sum(l_ref[2 * hp + 1], axis=1, keepdims=True)
        o_ref[:, hp * LANES:(hp + 1) * LANES] = jnp.where(lane_q < HEAD_DIM, even, odd).astype(o_ref.dtype)


def _dsa(qa, qi, wi, kit, kat, va4, seq):
    n, d_a = qa.shape
    bsz = n // seq
    n_heads = d_a // HEAD_DIM
    n_qb = seq // QB
    n_kt = seq // TM
    k_top = min(TOPK_MAX, seq // 4)
    qrow = lambda b, i: (b * n_qb + i, 0)
    per_b = lambda b, i: (b, 0, 0, 0)
    return pl.pallas_call(
        functools.partial(_dsa_kernel, k_top=float(k_top), n_heads=n_heads),
        grid=(bsz, n_qb),
        in_specs=[pl.BlockSpec((QB, d_a), qrow),
                  pl.BlockSpec((QB, qi.shape[1]), qrow),
                  pl.BlockSpec((QB, N_IDX_HEADS), qrow),
                  pl.BlockSpec((1, n_kt, IDX_DIM, TM), per_b, pipeline_mode=pl.Buffered(1)),
                  pl.BlockSpec((1, n_kt, d_a, TM), per_b, pipeline_mode=pl.Buffered(1)),
                  pl.BlockSpec((1, n_kt, TM, d_a), per_b, pipeline_mode=pl.Buffered(1))],
        out_specs=pl.BlockSpec((QB, d_a), qrow),
        out_shape=jax.ShapeDtypeStruct((n, d_a), jnp.bfloat16),
        scratch_shapes=[pltpu.VMEM((N_IDX_HEADS * QB, IDX_DIM), jnp.bfloat16),
                        pltpu.VMEM((N_IDX_HEADS, QB, TM), jnp.float32),
                        pltpu.VMEM((n_kt, QB, TM), jnp.float32),
                        pltpu.VMEM((n_heads, QB, LANES), jnp.bfloat16),
                        pltpu.VMEM((n_heads, QB, LANES), jnp.float32),
                        pltpu.VMEM((n_heads, QB, LANES), jnp.float32),
                        pltpu.VMEM((n_heads, QB, LANES), jnp.float32),
                        pltpu.VMEM((n_heads, QB, TM), jnp.float32),
                        pltpu.VMEM((n_heads, QB, TM), jnp.float32),
                        pltpu.VMEM((n_heads, QB, LANES), jnp.float32),
                        pltpu.VMEM((n_heads, QB, LANES), jnp.float32)],
        compiler_params=_cparams(("arbitrary", "arbitrary")),
        name="dsa",
    )(qa, qi, wi, kit, kat, va4)


def _ret_kernel(q_ref, k_ref, v_ref, g_ref, dec_ref, zeta_ref, xi_ref, gc_ref, o_ref, st_ref, *, n_heads):
    @pl.when(pl.program_id(1) == 0)
    def _():
        st_ref[...] = jnp.zeros_like(st_ref)

    for h in range(n_heads):
        sl = slice(h * HEAD_DIM, (h + 1) * HEAD_DIM)
        q, k, v = q_ref[:, sl], k_ref[:, sl], v_ref[:, sl]
        s = lax.dot_general(q, k, (((1,), (1,)), ((), ())), preferred_element_type=jnp.float32) * dec_ref[h]
        inner = jnp.dot(s.astype(v.dtype), v, preferred_element_type=jnp.float32)
        state = st_ref[h]
        cross = jnp.dot(q, state.astype(q.dtype), preferred_element_type=jnp.float32) * xi_ref[h]
        y = inner + cross
        mu = jnp.mean(y, axis=-1, keepdims=True)
        yc = y - mu
        yn = yc * lax.rsqrt(jnp.mean(yc * yc, axis=-1, keepdims=True) + EPS)
        g = g_ref[:, sl]
        o_ref[:, sl] = (g / (1.0 + jnp.exp(-g)) * yn).astype(o_ref.dtype)
        kz = (k.astype(jnp.float32) * zeta_ref[h]).astype(k.dtype)
        kv = lax.dot_general(kz, v, (((0,), (0,)), ((), ())), preferred_element_type=jnp.float32)
        st_ref[h] = state * gc_ref[h] + kv


def _ret_consts(n_heads):
    log_gamma = jnp.log1p(-jnp.exp2(-5.0 - jnp.arange(n_heads, dtype=jnp.float32)))
    pos = jnp.arange(RET_C, dtype=jnp.float32)
    diff = pos[:, None] - pos[None, :]
    dec = jnp.where(diff[None] >= 0, jnp.exp(jnp.maximum(diff, 0.0)[None] * log_gamma[:, None, None]), 0.0)
    zeta = jnp.exp((RET_C - 1.0 - pos)[None, :] * log_gamma[:, None])
    xi = jnp.exp((pos + 1.0)[None, :] * log_gamma[:, None])
    gc = jnp.exp(RET_C * log_gamma)
    bc = lambda a, shape: jnp.broadcast_to(a, shape).astype(jnp.float32)
    return (dec, bc(zeta[:, :, None], (n_heads, RET_C, HEAD_DIM)), bc(xi[:, :, None], (n_heads, RET_C, HEAD_DIM)),
            bc(gc[:, None, None], (n_heads, HEAD_DIM, HEAD_DIM)))


def _ret(qb, kb, vb, gb, seq):
    n, d_b = qb.shape
    bsz = n // seq
    n_heads = d_b // HEAD_DIM
    n_c = seq // RET_C
    dec, zeta, xi, gc = _ret_consts(n_heads)
    row = lambda b, c: (b * n_c + c, 0)
    const3 = lambda b, c: (0, 0, 0)
    return pl.pallas_call(
        functools.partial(_ret_kernel, n_heads=n_heads),
        grid=(bsz, n_c),
        in_specs=[pl.BlockSpec((RET_C, d_b), row)] * 4 + [
            pl.BlockSpec(dec.shape, const3), pl.BlockSpec(zeta.shape, const3),
            pl.BlockSpec(xi.shape, const3), pl.BlockSpec(gc.shape, const3)],
        out_specs=pl.BlockSpec((RET_C, d_b), row),
        out_shape=jax.ShapeDtypeStruct((n, d_b), jnp.bfloat16),
        scratch_shapes=[pltpu.VMEM((n_heads, HEAD_DIM, HEAD_DIM), jnp.float32)],
        compiler_params=_cparams(("arbitrary", "arbitrary")),
        name="ret",
    )(qb, kb, vb, gb, dec, zeta, xi, gc)


def _lane_first_eq(x, m, lane):
    return jnp.min(jnp.where(x == m, lane, float(LANES)), axis=1, keepdims=True)


def _mix_out_kernel(x_ref, ya_ref, yb_ref, woa_ref, wob_ref, gt_ref, sc_ref, sh_ref, g_ref, wr_ref, tri_ref,
                    x1_ref, h2_ref, route_ref, cnt_ref, carry_ref):
    @pl.when(pl.program_id(0) == 0)
    def _():
        carry_ref[...] = jnp.zeros_like(carry_ref)

    mix = (jnp.dot(ya_ref[...], woa_ref[...], preferred_element_type=jnp.float32)
           + jnp.dot(yb_ref[...], wob_ref[...], preferred_element_type=jnp.float32))
    x1 = x_ref[...] + gt_ref[0] * mix
    x1_ref[...] = x1
    h2 = _rmsnorm_mod(x1, g_ref[...], sc_ref[0], sh_ref[0])
    h2_ref[...] = h2

    lg = jnp.dot(h2.astype(jnp.bfloat16), wr_ref[...], preferred_element_type=jnp.float32)
    lane = lax.broadcasted_iota(jnp.int32, lg.shape, 1).astype(jnp.float32)
    is_grp = (lane >= N_EXPERTS) & (lane < N_EXPERTS + N_GROUPS)
    gl = jnp.where(is_grp, lg, -jnp.inf)
    gmax = jnp.max(gl, axis=1, keepdims=True)
    grp = _lane_first_eq(gl, gmax, lane) - N_EXPERTS
    p_grp = 1.0 / jnp.sum(jnp.exp(gl - gmax), axis=1, keepdims=True)
    in_grp = jnp.floor(lane * (1.0 / EXPERTS_PER_GROUP)) == grp
    f = jnp.where(in_grp & (lane < N_EXPERTS), lg, -jnp.inf)
    f1 = jnp.max(f, axis=1, keepdims=True)
    e1 = _lane_first_eq(f, f1, lane)
    f = jnp.where(lane == e1, -jnp.inf, f)
    f2 = jnp.max(f, axis=1, keepdims=True)
    e2 = _lane_first_eq(f, f2, lane)
    a2 = jnp.exp(f2 - f1)
    w1 = p_grp / (1.0 + a2)
    w2 = p_grp * a2 / (1.0 + a2)

    oh1 = jnp.where(lane == e1, 1.0, 0.0)
    oh2 = jnp.where(lane == e2, 1.0, 0.0)
    both = oh1 + oh2
    before = jnp.dot(tri_ref[...], both.astype(jnp.bfloat16), preferred_element_type=jnp.float32) + carry_ref[...]
    r1 = jnp.sum(before * oh1, axis=1, keepdims=True)
    r2 = jnp.sum(before * oh2, axis=1, keepdims=True)
    carry = carry_ref[...] + jnp.sum(both, axis=0, keepdims=True)
    carry_ref[...] = carry
    cnt_ref[...] = carry

    out = jnp.zeros(lg.shape, jnp.float32)
    for col, val in enumerate((e1, e2, w1, w2, r1, r2)):
        out = jnp.where(lane == col, val, out)
    route_ref[...] = out


def _mix_out(x2, ya, yb, wo_a, wo_b, mod3, g_ffn, w_route, seq):
    n, d = x2.shape
    tiles_per_seq = seq // TM
    tri = jnp.asarray(np.tril(np.ones((TM, TM), np.float32), -1), jnp.bfloat16)
    row = lambda i: (i, 0)
    const = lambda i: (0, 0)
    modk = lambda k: pl.BlockSpec((1, 1, d), lambda i: ((i // tiles_per_seq) * 6 + k, 0, 0))
    return pl.pallas_call(
        _mix_out_kernel,
        grid=(n // TM,),
        in_specs=[pl.BlockSpec((TM, d), row),
                  pl.BlockSpec((TM, ya.shape[1]), row),
                  pl.BlockSpec((TM, yb.shape[1]), row),
                  pl.BlockSpec(wo_a.shape, const),
                  pl.BlockSpec(wo_b.shape, const),
                  modk(2), modk(4), modk(3),
                  pl.BlockSpec((1, d), const),
                  pl.BlockSpec(w_route.shape, const),
                  pl.BlockSpec(tri.shape, const)],
        out_specs=(pl.BlockSpec((TM, d), row), pl.BlockSpec((TM, d), row),
                   pl.BlockSpec((TM, LANES), row), pl.BlockSpec((1, LANES), const)),
        out_shape=(jax.ShapeDtypeStruct((n, d), jnp.float32), jax.ShapeDtypeStruct((n, d), jnp.float32),
                   jax.ShapeDtypeStruct((n, LANES), jnp.float32), jax.ShapeDtypeStruct((1, LANES), jnp.float32)),
        scratch_shapes=[pltpu.VMEM((1, LANES), jnp.float32)],
        compiler_params=_cparams(("arbitrary",)),
        name="mix_out",
    )(x2, ya, yb, wo_a, wo_b, mod3, mod3, mod3, g_ffn, w_route, tri)


def _experts_kernel(blk_e_ref, n_used_ref, src_ref, src_next_ref, dst_ref, h2_hbm, wg_ref, wu_ref, wd_ref, y_hbm,
                    xbuf, ybuf, gsem, ssem, *, n_tok):
    j = pl.program_id(0)
    n_used = n_used_ref[0]
    slot = j % 2

    def gather(idx_ref, s):
        def issue(r, c):
            pltpu.make_async_copy(h2_hbm.at[pl.ds(idx_ref[0, 0, r], 1)], xbuf.at[s, pl.ds(r, 1)], gsem.at[s]).start()
            return c
        lax.fori_loop(0, BM, issue, 0)

    def wait_gather(s):
        pltpu.make_async_copy(h2_hbm.at[pl.ds(0, BM)], xbuf.at[s], gsem.at[s]).wait()

    def wait_scatter(s):
        pltpu.make_async_copy(ybuf.at[s], y_hbm.at[pl.ds(0, BM)], ssem.at[s]).wait()

    @pl.when(j == 0)
    def _():
        ybuf[1] = jnp.zeros(ybuf.shape[1:], ybuf.dtype)
        for g in range(GAP // BM):
            fill = pltpu.make_async_copy(ybuf.at[1], y_hbm.at[pl.ds(n_tok + g * BM, BM)], ssem.at[1])
            fill.start()
            fill.wait()
        gather(src_ref, 0)

    @pl.when(j + 1 < n_used)
    def _():
        gather(src_next_ref, 1 - slot)

    @pl.when(j < n_used)
    def _():
        wait_gather(slot)

        @pl.when(j >= 2)
        def _():
            wait_scatter(slot)

        x = xbuf[slot].astype(wg_ref.dtype)
        a = jnp.dot(x, wg_ref[0], preferred_element_type=jnp.float32)
        b = jnp.dot(x, wu_ref[0], preferred_element_type=jnp.float32)
        hmid = (a / (1.0 + jnp.exp(-a)) * b).astype(x.dtype)
        ybuf[slot] = jnp.dot(hmid, wd_ref[0], preferred_element_type=jnp.float32)

        def issue(r, c):
            pltpu.make_async_copy(ybuf.at[slot, pl.ds(r, 1)], y_hbm.at[pl.ds(dst_ref[0, 0, r], 1)],
                                  ssem.at[slot]).start()
            return c
        lax.fori_loop(0, BM, issue, 0)

    @pl.when(j == n_used - 1)
    def _():
        wait_scatter(slot)

        @pl.when(j >= 1)
        def _():
            wait_scatter(1 - slot)


def _experts(h2, wg, wu, wd, blk_e, n_used, src, dst, n_rows_out):
    n, d = h2.shape
    n_blk = src.shape[0]
    d_e = wg.shape[2]
    smem_row = lambda off: pl.BlockSpec((1, 1, BM), lambda j, be, nu: (jnp.minimum(j + off, n_blk - 1), 0, 0),
                                        memory_space=pltpu.SMEM)
    grid_spec = pltpu.PrefetchScalarGridSpec(
        num_scalar_prefetch=2,
        grid=(n_blk,),
        in_specs=[smem_row(0), smem_row(1), smem_row(0),
                  pl.BlockSpec(memory_space=pl.ANY),
                  pl.BlockSpec((1, d, d_e), lambda j, be, nu: (be[j], 0, 0)),
                  pl.BlockSpec((1, d, d_e), lambda j, be, nu: (be[j], 0, 0)),
                  pl.BlockSpec((1, d_e, d), lambda j, be, nu: (be[j], 0, 0))],
        out_specs=pl.BlockSpec(memory_space=pl.ANY),
        scratch_shapes=[pltpu.VMEM((2, BM, d), h2.dtype),
                        pltpu.VMEM((2, BM, d), jnp.float32),
                        pltpu.SemaphoreType.DMA((2,)),
                        pltpu.SemaphoreType.DMA((2,))],
    )
    return pl.pallas_call(
        functools.partial(_experts_kernel, n_tok=n),
        grid_spec=grid_spec,
        out_shape=jax.ShapeDtypeStruct((n_rows_out, d), jnp.float32),
        compiler_params=_cparams(("arbitrary",)),
        name="experts",
    )(blk_e, n_used, src, src, dst, h2, wg, wu, wd)


def _final_kernel(x1_ref, y0_ref, y1_ref, route_ref, gt_ref, g_ref, o_ref):
    w1 = route_ref[:, 2:3]
    w2 = route_ref[:, 3:4]
    x2 = x1_ref[...] + gt_ref[0] * (w1 * y0_ref[...] + w2 * y1_ref[...])
    o_ref[...] = x2 * lax.rsqrt(jnp.mean(x2 * x2, axis=-1, keepdims=True) + EPS) * g_ref[...]


def _final(x1, y, route, mod3, g_final, seq):
    n, d = x1.shape
    tiles_per_seq = seq // TM
    slot1 = (n + GAP) // TM
    row = lambda i: (i, 0)
    return pl.pallas_call(
        _final_kernel,
        grid=(n // TM,),
        in_specs=[pl.BlockSpec((TM, d), row),
                  pl.BlockSpec((TM, d), row),
                  pl.BlockSpec((TM, d), lambda i: (slot1 + i, 0)),
                  pl.BlockSpec((TM, LANES), row),
                  pl.BlockSpec((1, 1, d), lambda i: ((i // tiles_per_seq) * 6 + 5, 0, 0)),
                  pl.BlockSpec((1, d), lambda i: (0, 0))],
        out_specs=pl.BlockSpec((TM, d), row),
        out_shape=jax.ShapeDtypeStruct((n, d), jnp.float32),
        compiler_params=_cparams(("arbitrary",)),
        name="final",
    )(x1, y, y, route, mod3, g_final)


def _dispatch_plan(route, counts, n):
    e1 = route[:, 0].astype(jnp.int32)
    e2 = route[:, 1].astype(jnp.int32)
    r1 = route[:, 4].astype(jnp.int32)
    r2 = route[:, 5].astype(jnp.int32)
    cnt = counts[0, :N_EXPERTS].astype(jnp.int32)
    blocks = (cnt + BM - 1) // BM
    bends = jnp.cumsum(blocks)
    pstarts = (bends - blocks) * BM
    n_blk = (2 * n) // BM + N_EXPERTS
    tok = jnp.arange(n, dtype=jnp.int32)
    d1 = pstarts[e1] + r1
    d2 = pstarts[e2] + r2
    trash = n + jnp.arange(n_blk * BM, dtype=jnp.int32) % GAP
    src = jnp.zeros((n_blk * BM,), jnp.int32).at[d1].set(tok).at[d2].set(tok)
    dst = trash.at[d1].set(tok).at[d2].set(tok + n + GAP)
    n_used = bends[-1:].astype(jnp.int32)
    blk_e = jnp.minimum(jnp.searchsorted(bends, jnp.arange(n_blk, dtype=jnp.int32), side="right"),
                        N_EXPERTS - 1).astype(jnp.int32)
    last_e = blk_e[jnp.maximum(n_used[0] - 1, 0)]
    blk_e = jnp.where(jnp.arange(n_blk) < n_used[0], blk_e, last_e)
    return blk_e, n_used, src.reshape(n_blk, 1, BM), dst.reshape(n_blk, 1, BM)


def _layer(x2, c, pos2, w_ada, b_ada, g_mix, w_in, w_o, g_ffn, w_rg, w_re, w_up, w_gate, w_down, seq):
    n, d = x2.shape
    bsz = n // seq
    bf = jnp.bfloat16
    d_a = d // 2
    d_i = N_IDX_HEADS * IDX_DIM
    mod3 = _ada(c, w_ada, b_ada).reshape(bsz * 6, 1, d)

    c0 = 3 * d_a + d_i
    c1 = c0 + IDX_DIM + N_IDX_HEADS
    w_cat = jnp.concatenate([w_in[:, :c0], w_in[:, c0:c1],
                             jnp.zeros((d, LANES - (c1 - c0)), w_in.dtype), w_in[:, c1:]], axis=1).astype(bf)
    qa, kat, va, qi, kit, wi, qb, kb, vb, gb = _proj(x2, pos2, mod3, g_mix.reshape(1, d), w_cat, seq)

    ya = _dsa(qa, qi, wi, kit, kat, va.reshape(bsz, seq // TM, TM, d_a), seq)
    yb = _ret(qb, kb, vb, gb, seq)

    w_route = jnp.concatenate([jnp.transpose(w_re, (1, 0, 2)).reshape(d, N_EXPERTS), w_rg,
                               jnp.zeros((d, LANES - N_EXPERTS - N_GROUPS), w_rg.dtype)], axis=1).astype(bf)
    x1, h2, route, counts = _mix_out(x2, ya, yb, w_o[:d_a].astype(bf), w_o[d_a:].astype(bf), mod3,
                                     g_ffn.reshape(1, d), w_route, seq)

    blk_e, n_used, src, dst = _dispatch_plan(route, counts, n)
    y = _experts(h2, w_gate.astype(bf), w_up.astype(bf), w_down.astype(bf), blk_e, n_used, src, dst, 2 * n + GAP)
    return x1, y, route, mod3


def kernel(x, c, positions, w_ada, b_ada, g_norm_mix, w_in, w_o, g_norm_ffn, w_router_group, w_router_expert,
           w_up, w_gate, w_down, g_norm_final):
    bsz, seq, d = x.shape
    depth = w_ada.shape[0]
    assert depth == 1, "the final norm is fused into the last layer's combine kernel"
    assert seq % (KG * TM) == 0 and seq % RET_C == 0 and TM % QB == 0 and (2 * bsz * seq) % BM == 0 and GAP % TM == 0
    x2 = x.reshape(bsz * seq, d)
    pos2 = positions.astype(jnp.float32).reshape(bsz * seq, 1)
    x1, y, route, mod3 = _layer(x2, c, pos2, w_ada[0], b_ada[0], g_norm_mix[0], w_in[0], w_o[0], g_norm_ffn[0],
                                w_router_group[0], w_router_expert[0], w_up[0], w_gate[0], w_down[0], seq)
    out = _final(x1, y, route, mod3, g_norm_final.reshape(1, d), seq)
    return out.reshape(bsz, seq, d)
```

```python
import functools

import jax
import jax.numpy as jnp
import numpy as np
from jax import lax
from jax.experimental import pallas as pl
from jax.experimental.pallas import tpu as pltpu

CHUNK = 64
HEAD_DIM = 64
N_IDX_HEADS = 16
IDX_DIM = 64
TOPK_MAX = 256
ROPE_THETA = 500000.0
ROT_DIM = HEAD_DIM // 4
RET_THETA = 10000.0
N_GROUPS = 4
EXPERTS_PER_GROUP = 8
N_EXPERTS = N_GROUPS * EXPERTS_PER_GROUP
EPS = 1e-6

LANES = 128
SUBLANES = 8
VMEM_LIMIT = 56 * 1024 * 1024

TM = 256
QB = 128
RET_C = 256
BM = 256
KG = 2
MAX_BISECT = 40

NEG_BIG = -1e30
LOG2E = 1.4426950408889634


def _cparams(sem):
    return pltpu.CompilerParams(dimension_semantics=sem, vmem_limit_bytes=VMEM_LIMIT)


def _ada_kernel(c_ref, w_ref, b_ref, o_ref):
    o_ref[...] = jnp.dot(c_ref[...], w_ref[...], preferred_element_type=jnp.float32) + b_ref[...]


def _ada(c, w_ada, b_ada):
    bsz, d = c.shape
    n_out = w_ada.shape[1]
    return pl.pallas_call(
        _ada_kernel,
        grid=(n_out // d,),
        in_specs=[pl.BlockSpec((bsz, d), lambda j: (0, 0)),
                  pl.BlockSpec((d, d), lambda j: (0, j)),
                  pl.BlockSpec((1, d), lambda j: (0, j))],
        out_specs=pl.BlockSpec((bsz, d), lambda j: (0, j)),
        out_shape=jax.ShapeDtypeStruct((bsz, n_out), jnp.float32),
        compiler_params=_cparams(("arbitrary",)),
        name="ada",
    )(c, w_ada, b_ada.reshape(1, n_out))


def _rmsnorm_mod(x, g, sc, sh):
    xn = x * lax.rsqrt(jnp.mean(x * x, axis=-1, keepdims=True) + EPS)
    return xn * g * (1.0 + sc) + sh


def _rope_lanes(x, cos, sin_lo, sin_hi, half):
    cols = []
    for k in range(x.shape[1] // LANES):
        xb = x[:, k * LANES:(k + 1) * LANES]
        cols.append(xb * cos + pltpu.roll(xb, LANES - half, 1) * sin_lo + pltpu.roll(xb, half, 1) * sin_hi)
    return cols[0] if len(cols) == 1 else jnp.concatenate(cols, axis=1)


def _proj_kernel(x_ref, pos_ref, sc_ref, sh_ref, g_ref, w_ref, tab_ref,
                 qa_ref, kat_ref, va_ref, qi_ref, kit_ref, wi_ref, qb_ref, kb_ref, vb_ref, gb_ref,
                 *, d_a, d_i, d_b):
    h = _rmsnorm_mod(x_ref[...], g_ref[...], sc_ref[0], sh_ref[0]).astype(jnp.bfloat16)
    pos = pos_ref[...]
    ang_a = pos * tab_ref[0:1, :]
    cos_a, sin_a = jnp.cos(ang_a), jnp.sin(ang_a)
    sa_lo, sa_hi = sin_a * tab_ref[1:2, :], sin_a * tab_ref[2:3, :]
    ang_b = pos * tab_ref[3:4, :]
    cos_b, sin_b = jnp.cos(ang_b), jnp.sin(ang_b)
    sb_lo, sb_hi = sin_b * tab_ref[4:5, :], sin_b * tab_ref[5:6, :]
    half_a, half_b = ROT_DIM // 2, HEAD_DIM // 2

    def seg(lo, width):
        return jnp.dot(h, w_ref[:, lo:lo + width], preferred_element_type=jnp.float32)

    o = 0
    qa = seg(o, d_a); o += d_a
    qa_ref[...] = (_rope_lanes(qa, cos_a, sa_lo, sa_hi, half_a) * (HEAD_DIM ** -0.5 * LOG2E)).astype(qa_ref.dtype)
    ka = seg(o, d_a); o += d_a
    kat_ref[0, 0] = _rope_lanes(ka, cos_a, sa_lo, sa_hi, half_a).T.astype(kat_ref.dtype)
    va_ref[...] = seg(o, d_a).astype(va_ref.dtype); o += d_a
    qi = seg(o, d_i); o += d_i
    qi_ref[...] = (_rope_lanes(qi, cos_a, sa_lo, sa_hi, half_a) * (IDX_DIM ** -0.5)).astype(qi_ref.dtype)
    kw = seg(o, LANES); o += LANES
    kit_ref[0, 0] = _rope_lanes(kw, cos_a, sa_lo, sa_hi, half_a).T[0:IDX_DIM, :].astype(kit_ref.dtype)
    wi_ref[...] = kw[:, IDX_DIM:IDX_DIM + N_IDX_HEADS] * (N_IDX_HEADS ** -0.5)
    qb = seg(o, d_b); o += d_b
    qb_ref[...] = _rope_lanes(qb, cos_b, sb_lo, sb_hi, half_b).astype(qb_ref.dtype)
    kb = seg(o, d_b); o += d_b
    kb_ref[...] = (_rope_lanes(kb, cos_b, sb_lo, sb_hi, half_b) * (HEAD_DIM ** -0.5)).astype(kb_ref.dtype)
    vb_ref[...] = seg(o, d_b).astype(vb_ref.dtype); o += d_b
    gb_ref[...] = seg(o, d_b)


def _rope_tables():
    lane = jnp.arange(LANES) % HEAD_DIM
    rows = []
    for rot, theta in ((ROT_DIM, ROPE_THETA), (HEAD_DIM, RET_THETA)):
        half = rot // 2
        inv_freq = theta ** (-jnp.arange(half, dtype=jnp.float32) / half)
        rows.append(jnp.where(lane < rot, inv_freq[lane % half], 0.0))
        rows.append(jnp.where(lane < half, -1.0, 0.0))
        rows.append(jnp.where((lane >= half) & (lane < rot), 1.0, 0.0))
    rows += [jnp.zeros((LANES,), jnp.float32)] * 2
    return jnp.stack(rows).astype(jnp.float32)


def _proj(x2, pos2, mod3, g_mix, w_cat, seq):
    n, d = x2.shape
    bsz = n // seq
    d_a = d // 2
    d_b = d // 2
    d_i = N_IDX_HEADS * IDX_DIM
    tiles_per_seq = seq // TM
    tab = _rope_tables()
    row = lambda i: (i, 0)
    const = lambda i: (0, 0)
    tile4 = lambda i: (i // tiles_per_seq, i % tiles_per_seq, 0, 0)
    bf = jnp.bfloat16
    out_shape = (
        jax.ShapeDtypeStruct((n, d_a), bf),
        jax.ShapeDtypeStruct((bsz, tiles_per_seq, d_a, TM), bf),
        jax.ShapeDtypeStruct((n, d_a), bf),
        jax.ShapeDtypeStruct((n, d_i), bf),
        jax.ShapeDtypeStruct((bsz, tiles_per_seq, IDX_DIM, TM), bf),
        jax.ShapeDtypeStruct((n, N_IDX_HEADS), jnp.float32),
        jax.ShapeDtypeStruct((n, d_b), bf),
        jax.ShapeDtypeStruct((n, d_b), bf),
        jax.ShapeDtypeStruct((n, d_b), bf),
        jax.ShapeDtypeStruct((n, d_b), jnp.float32),
    )
    out_specs = (
        pl.BlockSpec((TM, d_a), row),
        pl.BlockSpec((1, 1, d_a, TM), tile4),
        pl.BlockSpec((TM, d_a), row),
        pl.BlockSpec((TM, d_i), row),
        pl.BlockSpec((1, 1, IDX_DIM, TM), tile4),
        pl.BlockSpec((TM, N_IDX_HEADS), row),
        pl.BlockSpec((TM, d_b), row),
        pl.BlockSpec((TM, d_b), row),
        pl.BlockSpec((TM, d_b), row),
        pl.BlockSpec((TM, d_b), row),
    )
    return pl.pallas_call(
        functools.partial(_proj_kernel, d_a=d_a, d_i=d_i, d_b=d_b),
        grid=(n // TM,),
        in_specs=[pl.BlockSpec((TM, d), row),
                  pl.BlockSpec((TM, 1), row),
                  pl.BlockSpec((1, 1, d), lambda i: ((i // tiles_per_seq) * 6 + 1, 0, 0)),
                  pl.BlockSpec((1, 1, d), lambda i: ((i // tiles_per_seq) * 6 + 0, 0, 0)),
                  pl.BlockSpec((1, d), const),
                  pl.BlockSpec(w_cat.shape, const),
                  pl.BlockSpec(tab.shape, const)],
        out_specs=out_specs,
        out_shape=out_shape,
        compiler_params=_cparams(("arbitrary",)),
        name="proj",
    )(x2, pos2, mod3, mod3, g_mix, w_cat, tab)


def _lane_blocks(x):
    return [x[:, c * LANES:(c + 1) * LANES] for c in range(x.shape[1] // LANES)]


def _dsa_kernel(qa_ref, qi_ref, wi_ref, kit_ref, kat_ref, va_ref, o_ref,
                qih_ref, wb_ref, sc_ref, qm_ref, m_ref, l_ref, acc_ref, sa_ref, sb_ref, mxa_ref, mxb_ref,
                *, k_top, n_heads):
    i = pl.program_id(1)
    n_grp = (i * QB + QB + KG * TM - 1) // (KG * TM)
    n_kt = n_grp * KG

    for h in range(N_IDX_HEADS):
        qih_ref[h * QB:(h + 1) * QB, :] = qi_ref[:, h * IDX_DIM:(h + 1) * IDX_DIM]
        wb_ref[h] = jnp.broadcast_to(wi_ref[:, h:h + 1], (QB, TM))

    lane_q = lax.broadcasted_iota(jnp.int32, (QB, LANES), 1)
    for h in range(n_heads):
        pair = qa_ref[:, (h // 2) * LANES:(h // 2 + 1) * LANES]
        own = (lane_q < HEAD_DIM) if h % 2 == 0 else (lane_q >= HEAD_DIM)
        qm_ref[h] = jnp.where(own, pair, jnp.zeros_like(pair))

    q_chunk = (i * QB + lax.broadcasted_iota(jnp.int32, (QB, TM), 0)) // CHUNK
    key_in_tile = lax.broadcasted_iota(jnp.int32, (QB, TM), 1)

    def idx_grp(g, carry):
        lo, hi = carry
        for u in range(KG):
            j = g * KG + u
            d = jnp.dot(qih_ref[...], kit_ref[0, j], preferred_element_type=jnp.float32)
            acc = wb_ref[0] * jnp.maximum(d[0:QB], 0.0)
            for h in range(1, N_IDX_HEADS):
                acc = acc + wb_ref[h] * jnp.maximum(d[h * QB:(h + 1) * QB], 0.0)
            adm = (j * TM + key_in_tile) // CHUNK <= q_chunk
            sc_ref[j] = jnp.where(adm, acc, -jnp.inf)
            lo = jnp.minimum(lo, jnp.where(adm, acc, jnp.inf))
            hi = jnp.maximum(hi, jnp.where(adm, acc, -jnp.inf))
        return lo, hi

    lo, hi = lax.fori_loop(0, n_grp, idx_grp,
                           (jnp.full((QB, TM), jnp.inf, jnp.float32), jnp.full((QB, TM), -jnp.inf, jnp.float32)))
    lo = jnp.min(lo, axis=1, keepdims=True)
    hi = jnp.max(hi, axis=1, keepdims=True)
    n_adm = ((i * QB + lax.broadcasted_iota(jnp.int32, (QB, 1), 0)) // CHUNK + 1) * CHUNK

    def bisect_cond(carry):
        it, _, _, _, unsettled = carry
        return (it < MAX_BISECT) & (unsettled > 0.0)

    def bisect(carry):
        it, lo, hi, cnt_lo, _ = carry
        mid = lo + (hi - lo) * 0.5
        mid_b = jnp.broadcast_to(mid, (QB, LANES))

        def count_grp(g, cnt):
            for u in range(KG):
                for blk in _lane_blocks(sc_ref[g * KG + u]):
                    cnt = cnt + jnp.where(blk >= mid_b, 1.0, 0.0)
            return cnt

        cnt = lax.fori_loop(0, n_grp, count_grp, jnp.zeros((QB, LANES), jnp.float32))
        c = jnp.sum(cnt, axis=1, keepdims=True)
        ge = c >= k_top
        lo = jnp.where(ge, mid, lo)
        hi = jnp.where(ge, hi, mid)
        cnt_lo = jnp.where(ge, c, cnt_lo)
        return it + 1, lo, hi, cnt_lo, jnp.max(jnp.where(cnt_lo > k_top, 1.0, 0.0))

    cnt0 = n_adm.astype(jnp.float32)
    _, thr, _, _, _ = lax.while_loop(bisect_cond, bisect,
                                     (0, lo, hi, cnt0, jnp.max(jnp.where(cnt0 > k_top, 1.0, 0.0))))

    def bias_tile(j, carry):
        sc_ref[j] = jnp.where(sc_ref[j] >= thr, 0.0, NEG_BIG)
        return carry

    lax.fori_loop(0, n_kt, bias_tile, 0)

    m_ref[...] = jnp.full(m_ref.shape, NEG_BIG, jnp.float32)
    l_ref[...] = jnp.zeros(l_ref.shape, jnp.float32)
    acc_ref[...] = jnp.zeros(acc_ref.shape, jnp.float32)

    def pair_rows(h):
        return slice((h // 2) * LANES, (h // 2 + 1) * LANES)

    def logits_into(s_ref, mx_ref, j):
        bias = sc_ref[j]
        for h in range(n_heads):
            s = jnp.dot(qm_ref[h], kat_ref[0, j, pair_rows(h), :], preferred_element_type=jnp.float32) + bias
            s_ref[h] = s
            mx = functools.reduce(jnp.maximum, _lane_blocks(s))
            mx_ref[h] = jnp.broadcast_to(jnp.max(mx, axis=1, keepdims=True), (QB, LANES))

    def absorb(s_ref, mx_ref, j):
        for h in range(n_heads):
            m_old = m_ref[h]
            m_new = jnp.maximum(m_old, mx_ref[h])
            alpha = jnp.exp2(m_old - m_new)
            p = [jnp.exp2(b - m_new) for b in _lane_blocks(s_ref[h])]
            l_ref[h] = l_ref[h] * alpha + functools.reduce(jnp.add, p)
            pv = jnp.dot(jnp.concatenate(p, axis=1).astype(va_ref.dtype), va_ref[0, j, :, pair_rows(h)],
                         preferred_element_type=jnp.float32)
            acc_ref[h] = acc_ref[h] * alpha + pv
            m_ref[h] = m_new

    logits_into(sa_ref, mxa_ref, 0)

    def attn_grp(g, carry):
        logits_into(sb_ref, mxb_ref, 2 * g + 1)
        absorb(sa_ref, mxa_ref, 2 * g)
        logits_into(sa_ref, mxa_ref, jnp.minimum(2 * g + 2, n_kt - 1))
        absorb(sb_ref, mxb_ref, 2 * g + 1)
        return carry

    lax.fori_loop(0, n_grp, attn_grp, 0)

    for hp in range(n_heads // 2):
        even = acc_ref[2 * hp] / jnp.sum(l_ref[2 * hp], axis=1, keepdims=True)
        odd = acc_ref[2 * hp + 1] / jnp.sum(l_ref[2 * hp + 1], axis=1, keepdims=True)
        o_ref[:, hp * LANES:(hp + 1) * LANES] = jnp.where(lane_q < HEAD_DIM, even, odd).astype(o_ref.dtype)


def _dsa(qa, qi, wi, kit, kat, va4, seq):
    n, d_a = qa.shape
    bsz = n // seq
    n_heads = d_a // HEAD_DIM
    n_qb = seq // QB
    n_kt = seq // TM
    k_top = min(TOPK_MAX, seq // 4)
    qrow = lambda b, i: (b * n_qb + i, 0)
    per_b = lambda b, i: (b, 0, 0, 0)
    return pl.pallas_call(
        functools.partial(_dsa_kernel, k_top=float(k_top), n_heads=n_heads),
        grid=(bsz, n_qb),
        in_specs=[pl.BlockSpec((QB, d_a), qrow),
                  pl.BlockSpec((QB, qi.shape[1]), qrow),
                  pl.BlockSpec((QB, N_IDX_HEADS), qrow),
                  pl.BlockSpec((1, n_kt, IDX_DIM, TM), per_b, pipeline_mode=pl.Buffered(1)),
                  pl.BlockSpec((1, n_kt, d_a, TM), per_b, pipeline_mode=pl.Buffered(1)),
                  pl.BlockSpec((1, n_kt, TM, d_a), per_b, pipeline_mode=pl.Buffered(1))],
        out_specs=pl.BlockSpec((QB, d_a), qrow),
        out_shape=jax.ShapeDtypeStruct((n, d_a), jnp.bfloat16),
        scratch_shapes=[pltpu.VMEM((N_IDX_HEADS * QB, IDX_DIM), jnp.bfloat16),
                        pltpu.VMEM((N_IDX_HEADS, QB, TM), jnp.float32),
                        pltpu.VMEM((n_kt, QB, TM), jnp.float32),
                        pltpu.VMEM((n_heads, QB, LANES), jnp.bfloat16),
                        pltpu.VMEM((n_heads, QB, LANES), jnp.float32),
                        pltpu.VMEM((n_heads, QB, LANES), jnp.float32),
                        pltpu.VMEM((n_heads, QB, LANES), jnp.float32),
                        pltpu.VMEM((n_heads, QB, TM), jnp.float32),
                        pltpu.VMEM((n_heads, QB, TM), jnp.float32),
                        pltpu.VMEM((n_heads, QB, LANES), jnp.float32),
                        pltpu.VMEM((n_heads, QB, LANES), jnp.float32)],
        compiler_params=_cparams(("arbitrary", "arbitrary")),
        name="dsa",
    )(qa, qi, wi, kit, kat, va4)


def _ret_kernel(q_ref, k_ref, v_ref, g_ref, dec_ref, zeta_ref, xi_ref, gc_ref, o_ref, st_ref, *, n_heads):
    @pl.when(pl.program_id(1) == 0)
    def _():
        st_ref[...] = jnp.zeros_like(st_ref)

    for h in range(n_heads):
        sl = slice(h * HEAD_DIM, (h + 1) * HEAD_DIM)
        q, k, v = q_ref[:, sl], k_ref[:, sl], v_ref[:, sl]
        s = lax.dot_general(q, k, (((1,), (1,)), ((), ())), preferred_element_type=jnp.float32) * dec_ref[h]
        inner = jnp.dot(s.astype(v.dtype), v, preferred_element_type=jnp.float32)
        state = st_ref[h]
        cross = jnp.dot(q, state.astype(q.dtype), preferred_element_type=jnp.float32) * xi_ref[h]
        y = inner + cross
        mu = jnp.mean(y, axis=-1, keepdims=True)
        yc = y - mu
        yn = yc * lax.rsqrt(jnp.mean(yc * yc, axis=-1, keepdims=True) + EPS)
        g = g_ref[:, sl]
        o_ref[:, sl] = (g / (1.0 + jnp.exp(-g)) * yn).astype(o_ref.dtype)
        kz = (k.astype(jnp.float32) * zeta_ref[h]).astype(k.dtype)
        kv = lax.dot_general(kz, v, (((0,), (0,)), ((), ())), preferred_element_type=jnp.float32)
        st_ref[h] = state * gc_ref[h] + kv


def _ret_consts(n_heads):
    log_gamma = jnp.log1p(-jnp.exp2(-5.0 - jnp.arange(n_heads, dtype=jnp.float32)))
    pos = jnp.arange(RET_C, dtype=jnp.float32)
    diff = pos[:, None] - pos[None, :]
    dec = jnp.where(diff[None] >= 0, jnp.exp(jnp.maximum(diff, 0.0)[None] * log_gamma[:, None, None]), 0.0)
    zeta = jnp.exp((RET_C - 1.0 - pos)[None, :] * log_gamma[:, None])
    xi = jnp.exp((pos + 1.0)[None, :] * log_gamma[:, None])
    gc = jnp.exp(RET_C * log_gamma)
    bc = lambda a, shape: jnp.broadcast_to(a, shape).astype(jnp.float32)
    return (dec, bc(zeta[:, :, None], (n_heads, RET_C, HEAD_DIM)), bc(xi[:, :, None], (n_heads, RET_C, HEAD_DIM)),
            bc(gc[:, None, None], (n_heads, HEAD_DIM, HEAD_DIM)))


def _ret(qb, kb, vb, gb, seq):
    n, d_b = qb.shape
    bsz = n // seq
    n_heads = d_b // HEAD_DIM
    n_c = seq // RET_C
    dec, zeta, xi, gc = _ret_consts(n_heads)
    row = lambda b, c: (b * n_c + c, 0)
    const3 = lambda b, c: (0, 0, 0)
    return pl.pallas_call(
        functools.partial(_ret_kernel, n_heads=n_heads),
        grid=(bsz, n_c),
        in_specs=[pl.BlockSpec((RET_C, d_b), row)] * 4 + [
            pl.BlockSpec(dec.shape, const3), pl.BlockSpec(zeta.shape, const3),
            pl.BlockSpec(xi.shape, const3), pl.BlockSpec(gc.shape, const3)],
        out_specs=pl.BlockSpec((RET_C, d_b), row),
        out_shape=jax.ShapeDtypeStruct((n, d_b), jnp.bfloat16),
        scratch_shapes=[pltpu.VMEM((n_heads, HEAD_DIM, HEAD_DIM), jnp.float32)],
        compiler_params=_cparams(("arbitrary", "arbitrary")),
        name="ret",
    )(qb, kb, vb, gb, dec, zeta, xi, gc)


def _lane_first_eq(x, m, lane):
    return jnp.min(jnp.where(x == m, lane, float(LANES)), axis=1, keepdims=True)


def _mix_out_kernel(x_ref, ya_ref, yb_ref, woa_ref, wob_ref, gt_ref, sc_ref, sh_ref, g_ref, wr_ref, tri_ref,
                    x1_ref, h2_ref, route_ref, cnt_ref, carry_ref):
    @pl.when(pl.program_id(0) == 0)
    def _():
        carry_ref[...] = jnp.zeros_like(carry_ref)

    mix = (jnp.dot(ya_ref[...], woa_ref[...], preferred_element_type=jnp.float32)
           + jnp.dot(yb_ref[...], wob_ref[...], preferred_element_type=jnp.float32))
    x1 = x_ref[...] + gt_ref[0] * mix
    x1_ref[...] = x1
    h2 = _rmsnorm_mod(x1, g_ref[...], sc_ref[0], sh_ref[0])
    h2_ref[...] = h2

    lg = jnp.dot(h2.astype(jnp.bfloat16), wr_ref[...], preferred_element_type=jnp.float32)
    lane = lax.broadcasted_iota(jnp.int32, lg.shape, 1).astype(jnp.float32)
    is_grp = (lane >= N_EXPERTS) & (lane < N_EXPERTS + N_GROUPS)
    gl = jnp.where(is_grp, lg, -jnp.inf)
    gmax = jnp.max(gl, axis=1, keepdims=True)
    grp = _lane_first_eq(gl, gmax, lane) - N_EXPERTS
    p_grp = 1.0 / jnp.sum(jnp.exp(gl - gmax), axis=1, keepdims=True)
    in_grp = jnp.floor(lane * (1.0 / EXPERTS_PER_GROUP)) == grp
    f = jnp.where(in_grp & (lane < N_EXPERTS), lg, -jnp.inf)
    f1 = jnp.max(f, axis=1, keepdims=True)
    e1 = _lane_first_eq(f, f1, lane)
    f = jnp.where(lane == e1, -jnp.inf, f)
    f2 = jnp.max(f, axis=1, keepdims=True)
    e2 = _lane_first_eq(f, f2, lane)
    a2 = jnp.exp(f2 - f1)
    w1 = p_grp / (1.0 + a2)
    w2 = p_grp * a2 / (1.0 + a2)

    oh1 = jnp.where(lane == e1, 1.0, 0.0)
    oh2 = jnp.where(lane == e2, 1.0, 0.0)
    both = oh1 + oh2
    before = jnp.dot(tri_ref[...], both.astype(jnp.bfloat16), preferred_element_type=jnp.float32) + carry_ref[...]
    r1 = jnp.sum(before * oh1, axis=1, keepdims=True)
    r2 = jnp.sum(before * oh2, axis=1, keepdims=True)
    carry = carry_ref[...] + jnp.sum(both, axis=0, keepdims=True)
    carry_ref[...] = carry
    cnt_ref[...] = carry

    out = jnp.zeros(lg.shape, jnp.float32)
    for col, val in enumerate((e1, e2, w1, w2, r1, r2)):
        out = jnp.where(lane == col, val, out)
    route_ref[...] = out


def _mix_out(x2, ya, yb, wo_a, wo_b, mod3, g_ffn, w_route, seq):
    n, d = x2.shape
    tiles_per_seq = seq // TM
    tri = jnp.asarray(np.tril(np.ones((TM, TM), np.float32), -1), jnp.bfloat16)
    row = lambda i: (i, 0)
    const = lambda i: (0, 0)
    modk = lambda k: pl.BlockSpec((1, 1, d), lambda i: ((i // tiles_per_seq) * 6 + k, 0, 0))
    return pl.pallas_call(
        _mix_out_kernel,
        grid=(n // TM,),
        in_specs=[pl.BlockSpec((TM, d), row),
                  pl.BlockSpec((TM, ya.shape[1]), row),
                  pl.BlockSpec((TM, yb.shape[1]), row),
                  pl.BlockSpec(wo_a.shape, const),
                  pl.BlockSpec(wo_b.shape, const),
                  modk(2), modk(4), modk(3),
                  pl.BlockSpec((1, d), const),
                  pl.BlockSpec(w_route.shape, const),
                  pl.BlockSpec(tri.shape, const)],
        out_specs=(pl.BlockSpec((TM, d), row), pl.BlockSpec((TM, d), row),
                   pl.BlockSpec((TM, LANES), row), pl.BlockSpec((1, LANES), const)),
        out_shape=(jax.ShapeDtypeStruct((n, d), jnp.float32), jax.ShapeDtypeStruct((n, d), jnp.float32),
                   jax.ShapeDtypeStruct((n, LANES), jnp.float32), jax.ShapeDtypeStruct((1, LANES), jnp.float32)),
        scratch_shapes=[pltpu.VMEM((1, LANES), jnp.float32)],
        compiler_params=_cparams(("arbitrary",)),
        name="mix_out",
    )(x2, ya, yb, wo_a, wo_b, mod3, mod3, mod3, g_ffn, w_route, tri)


def _plan_kernel(route_ref, pst_ref, dest_ref):
    r = route_ref[...]
    lane = lax.broadcasted_iota(jnp.int32, r.shape, 1).astype(jnp.float32)
    pst = pst_ref[...]
    d1 = jnp.sum(jnp.where(lane == r[:, 0:1], pst, 0.0), axis=1, keepdims=True) + r[:, 4:5]
    d2 = jnp.sum(jnp.where(lane == r[:, 1:2], pst, 0.0), axis=1, keepdims=True) + r[:, 5:6]
    both = jnp.where(lane == 0.0, d1, jnp.where(lane == 1.0, d2, 0.0))
    dest_ref[0] = both.T[0:8, :].astype(jnp.int32)


def _plan(route, pst_row):
    n = route.shape[0]
    return pl.pallas_call(
        _plan_kernel,
        grid=(n // TM,),
        in_specs=[pl.BlockSpec((TM, LANES), lambda i: (i, 0)), pl.BlockSpec((1, LANES), lambda i: (0, 0))],
        out_specs=pl.BlockSpec((1, 8, TM), lambda i: (i, 0, 0)),
        out_shape=jax.ShapeDtypeStruct((n // TM, 8, TM), jnp.int32),
        compiler_params=_cparams(("arbitrary",)),
        name="plan",
    )(route, pst_row)


def _dispatch_kernel(zrow_ref, n_used_ref, dest_ref, h2_hbm, xs_hbm, zbuf, sem, zsem, *, n_blk):
    i = pl.program_id(0)
    slot = i % 2

    @pl.when(i == 0)
    def _():
        zbuf[...] = jnp.zeros(zbuf.shape, zbuf.dtype)
        for e in range(N_EXPERTS):
            pltpu.make_async_copy(zbuf, xs_hbm.at[pl.ds(pl.multiple_of(zrow_ref[e], SUBLANES), BM)], zsem).start()
        for e in range(N_EXPERTS):
            pltpu.make_async_copy(zbuf, xs_hbm.at[pl.ds(0, BM)], zsem).wait()
        for b in range(N_EXPERTS + 1):
            @pl.when(n_used_ref[0] + b <= n_blk)
            def _():
                tail = pltpu.make_async_copy(zbuf, xs_hbm.at[pl.ds((n_used_ref[0] + b) * BM, BM)], zsem)
                tail.start()
                tail.wait()

    def issue(r, c):
        row = h2_hbm.at[pl.ds(i * TM + r, 1)]
        pltpu.make_async_copy(row, xs_hbm.at[pl.ds(dest_ref[0, 0, r], 1)], sem.at[slot]).start()
        pltpu.make_async_copy(row, xs_hbm.at[pl.ds(dest_ref[0, 1, r], 1)], sem.at[slot]).start()
        return c

    lax.fori_loop(0, TM, issue, 0)

    def wait_step(s):
        pltpu.make_async_copy(h2_hbm.at[pl.ds(0, 2 * TM)], xs_hbm.at[pl.ds(0, 2 * TM)], sem.at[s]).wait()

    @pl.when(i >= 1)
    def _():
        wait_step(1 - slot)

    @pl.when(i == pl.num_programs(0) - 1)
    def _():
        wait_step(slot)


def _dispatch(h2, dest, zrow, n_used, n_blk):
    n, d = h2.shape
    n_rows = (n_blk + 1) * BM
    grid_spec = pltpu.PrefetchScalarGridSpec(
        num_scalar_prefetch=2,
        grid=(n // TM,),
        in_specs=[pl.BlockSpec((1, 8, TM), lambda i, z, nu: (i, 0, 0), memory_space=pltpu.SMEM),
                  pl.BlockSpec(memory_space=pl.ANY)],
        out_specs=pl.BlockSpec(memory_space=pl.ANY),
        scratch_shapes=[pltpu.VMEM((BM, d), h2.dtype),
                        pltpu.SemaphoreType.DMA((2,)),
                        pltpu.SemaphoreType.DMA(())],
    )
    return pl.pallas_call(
        functools.partial(_dispatch_kernel, n_blk=n_blk),
        grid_spec=grid_spec,
        out_shape=jax.ShapeDtypeStruct((n_rows, d), h2.dtype),
        compiler_params=_cparams(("arbitrary",)),
        name="dispatch",
    )(zrow, n_used, dest, h2)


def _experts_kernel(blk_e_ref, n_used_ref, x_ref, wg_ref, wu_ref, wd_ref, y_ref):
    j = pl.program_id(0)

    @pl.when(j < n_used_ref[0])
    def _():
        x = x_ref[...].astype(wg_ref.dtype)
        a = jnp.dot(x, wg_ref[0], preferred_element_type=jnp.float32)
        b = jnp.dot(x, wu_ref[0], preferred_element_type=jnp.float32)
        hmid = (a / (1.0 + jnp.exp(-a)) * b).astype(x.dtype)
        y_ref[...] = jnp.dot(hmid, wd_ref[0], preferred_element_type=jnp.float32)

    @pl.when(j >= n_used_ref[0])
    def _():
        y_ref[...] = jnp.zeros(y_ref.shape, y_ref.dtype)


def _experts(xs, wg, wu, wd, blk_e, n_used):
    d = xs.shape[1]
    n_blk = blk_e.shape[0]
    d_e = wg.shape[2]
    grid_spec = pltpu.PrefetchScalarGridSpec(
        num_scalar_prefetch=2,
        grid=(n_blk,),
        in_specs=[pl.BlockSpec((BM, d), lambda j, be, nu: (jnp.minimum(j, nu[0] - 1), 0)),
                  pl.BlockSpec((1, d, d_e), lambda j, be, nu: (be[j], 0, 0)),
                  pl.BlockSpec((1, d, d_e), lambda j, be, nu: (be[j], 0, 0)),
                  pl.BlockSpec((1, d_e, d), lambda j, be, nu: (be[j], 0, 0))],
        out_specs=pl.BlockSpec((BM, d), lambda j, be, nu: (j, 0)),
    )
    return pl.pallas_call(
        _experts_kernel,
        grid_spec=grid_spec,
        out_shape=jax.ShapeDtypeStruct((n_blk * BM, d), jnp.float32),
        compiler_params=_cparams(("arbitrary",)),
        name="experts",
    )(blk_e, n_used, xs, wg, wu, wd)


def _final_kernel(dest_ref, dest_next_ref, x1_ref, route_ref, gt_ref, g_ref, ys_hbm, o_ref, ybuf, sem):
    i = pl.program_id(0)
    slot = i % 2

    def gather(d_ref, s):
        def issue(r, c):
            pltpu.make_async_copy(ys_hbm.at[pl.ds(d_ref[0, 0, r], 1)], ybuf.at[s, pl.ds(r, 1)], sem.at[s]).start()
            pltpu.make_async_copy(ys_hbm.at[pl.ds(d_ref[0, 1, r], 1)], ybuf.at[s, pl.ds(TM + r, 1)],
                                  sem.at[s]).start()
            return c
        lax.fori_loop(0, TM, issue, 0)

    @pl.when(i == 0)
    def _():
        gather(dest_ref, 0)

    @pl.when(i + 1 < pl.num_programs(0))
    def _():
        gather(dest_next_ref, 1 - slot)

    pltpu.make_async_copy(ys_hbm.at[pl.ds(0, 2 * TM)], ybuf.at[slot], sem.at[slot]).wait()
    w1 = route_ref[:, 2:3]
    w2 = route_ref[:, 3:4]
    x2 = x1_ref[...] + gt_ref[0] * (w1 * ybuf[slot, 0:TM, :] + w2 * ybuf[slot, TM:2 * TM, :])
    o_ref[...] = x2 * lax.rsqrt(jnp.mean(x2 * x2, axis=-1, keepdims=True) + EPS) * g_ref[...]


def _final(x1, ys, dest, route, mod3, g_final, seq):
    n, d = x1.shape
    n_tiles = n // TM
    tiles_per_seq = seq // TM
    row = lambda i: (i, 0)
    return pl.pallas_call(
        _final_kernel,
        grid=(n_tiles,),
        in_specs=[pl.BlockSpec((1, 8, TM), lambda i: (i, 0, 0), memory_space=pltpu.SMEM),
                  pl.BlockSpec((1, 8, TM), lambda i: (jnp.minimum(i + 1, n_tiles - 1), 0, 0),
                               memory_space=pltpu.SMEM),
                  pl.BlockSpec((TM, d), row),
                  pl.BlockSpec((TM, LANES), row),
                  pl.BlockSpec((1, 1, d), lambda i: ((i // tiles_per_seq) * 6 + 5, 0, 0)),
                  pl.BlockSpec((1, d), lambda i: (0, 0)),
                  pl.BlockSpec(memory_space=pl.ANY)],
        out_specs=pl.BlockSpec((TM, d), row),
        out_shape=jax.ShapeDtypeStruct((n, d), jnp.float32),
        scratch_shapes=[pltpu.VMEM((2, 2 * TM, d), jnp.float32), pltpu.SemaphoreType.DMA((2,))],
        compiler_params=_cparams(("arbitrary",)),
        name="final",
    )(dest, dest, x1, route, mod3, g_final, ys)


def _block_layout(counts, n):
    cnt = counts[0, :N_EXPERTS].astype(jnp.int32)
    blocks = (cnt + BM - 1) // BM
    bends = jnp.cumsum(blocks)
    pstarts = (bends - blocks) * BM
    n_blk = (2 * n) // BM + N_EXPERTS
    pst_row = jnp.zeros((1, LANES), jnp.float32).at[0, :N_EXPERTS].set(pstarts.astype(jnp.float32))
    zrow = ((pstarts + cnt) // SUBLANES * SUBLANES).astype(jnp.int32)
    n_used = bends[-1:].astype(jnp.int32)
    blk_e = jnp.sum(bends[None, :] <= jnp.arange(n_blk, dtype=jnp.int32)[:, None], axis=1)
    blk_e = jnp.minimum(blk_e, N_EXPERTS - 1).astype(jnp.int32)
    return pst_row, zrow, n_used, blk_e


def _layer(x2, c, pos2, w_ada, b_ada, g_mix, w_in, w_o, g_ffn, w_rg, w_re, w_up, w_gate, w_down, seq):
    n, d = x2.shape
    bsz = n // seq
    bf = jnp.bfloat16
    d_a = d // 2
    d_i = N_IDX_HEADS * IDX_DIM
    mod3 = _ada(c, w_ada, b_ada).reshape(bsz * 6, 1, d)

    c0 = 3 * d_a + d_i
    c1 = c0 + IDX_DIM + N_IDX_HEADS
    w_cat = jnp.concatenate([w_in[:, :c0], w_in[:, c0:c1],
                             jnp.zeros((d, LANES - (c1 - c0)), w_in.dtype), w_in[:, c1:]], axis=1).astype(bf)
    qa, kat, va, qi, kit, wi, qb, kb, vb, gb = _proj(x2, pos2, mod3, g_mix.reshape(1, d), w_cat, seq)

    ya = _dsa(qa, qi, wi, kit, kat, va.reshape(bsz, seq // TM, TM, d_a), seq)
    yb = _ret(qb, kb, vb, gb, seq)

    w_route = jnp.concatenate([jnp.transpose(w_re, (1, 0, 2)).reshape(d, N_EXPERTS), w_rg,
                               jnp.zeros((d, LANES - N_EXPERTS - N_GROUPS), w_rg.dtype)], axis=1).astype(bf)
    x1, h2, route, counts = _mix_out(x2, ya, yb, w_o[:d_a].astype(bf), w_o[d_a:].astype(bf), mod3,
                                     g_ffn.reshape(1, d), w_route, seq)

    pst_row, zrow, n_used, blk_e = _block_layout(counts, n)
    dest = _plan(route, pst_row)
    xs = _dispatch(h2, dest, zrow, n_used, blk_e.shape[0])
    ys = _experts(xs, w_gate.astype(bf), w_up.astype(bf), w_down.astype(bf), blk_e, n_used)
    return x1, ys, dest, route, mod3


def kernel(x, c, positions, w_ada, b_ada, g_norm_mix, w_in, w_o, g_norm_ffn, w_router_group, w_router_expert,
           w_up, w_gate, w_down, g_norm_final):
    bsz, seq, d = x.shape
    depth = w_ada.shape[0]
    assert depth == 1, "the final norm is fused into the last layer's combine kernel"
    assert seq % (KG * TM) == 0 and seq % RET_C == 0 and TM % QB == 0 and (2 * bsz * seq) % BM == 0
    x2 = x.reshape(bsz * seq, d)
    pos2 = positions.astype(jnp.float32).reshape(bsz * seq, 1)
    x1, ys, dest, route, mod3 = _layer(x2, c, pos2, w_ada[0], b_ada[0], g_norm_mix[0], w_in[0], w_o[0],
                                       g_norm_ffn[0], w_router_group[0], w_router_expert[0], w_up[0], w_gate[0],
                                       w_down[0], seq)
    out = _final(x1, ys, dest, route, mod3, g_norm_final.reshape(1, d), seq)
    return out.reshape(bsz, seq, d)
```

```python
import functools

import jax
import jax.numpy as jnp
import numpy as np
from jax import lax
from jax.experimental import pallas as pl
from jax.experimental.pallas import tpu as pltpu

CHUNK = 64
HEAD_DIM = 64
N_IDX_HEADS = 16
IDX_DIM = 64
TOPK_MAX = 256
ROPE_THETA = 500000.0
ROT_DIM = HEAD_DIM // 4
RET_THETA = 10000.0
N_GROUPS = 4
EXPERTS_PER_GROUP = 8
N_EXPERTS = N_GROUPS * EXPERTS_PER_GROUP
EPS = 1e-6

LANES = 128
SUBLANES = 8
VMEM_LIMIT = 56 * 1024 * 1024

TM = 256
QB = 128
RET_C = 256
BM = 256
KG = 2
MAX_BISECT = 40

NEG_BIG = -1e30
LOG2E = 1.4426950408889634


def _cparams(sem):
    return pltpu.CompilerParams(dimension_semantics=sem, vmem_limit_bytes=VMEM_LIMIT)


def _ada_kernel(c_ref, w_ref, b_ref, o_ref):
    o_ref[...] = jnp.dot(c_ref[...], w_ref[...], preferred_element_type=jnp.float32) + b_ref[...]


def _ada(c, w_ada, b_ada):
    bsz, d = c.shape
    n_out = w_ada.shape[1]
    return pl.pallas_call(
        _ada_kernel,
        grid=(n_out // d,),
        in_specs=[pl.BlockSpec((bsz, d), lambda j: (0, 0)),
                  pl.BlockSpec((d, d), lambda j: (0, j)),
                  pl.BlockSpec((1, d), lambda j: (0, j))],
        out_specs=pl.BlockSpec((bsz, d), lambda j: (0, j)),
        out_shape=jax.ShapeDtypeStruct((bsz, n_out), jnp.float32),
        compiler_params=_cparams(("arbitrary",)),
        name="ada",
    )(c, w_ada, b_ada.reshape(1, n_out))


def _rmsnorm_mod(x, g, sc, sh):
    xn = x * lax.rsqrt(jnp.mean(x * x, axis=-1, keepdims=True) + EPS)
    return xn * g * (1.0 + sc) + sh


def _rope_lanes(x, cos, sin_lo, sin_hi, half):
    cols = []
    for k in range(x.shape[1] // LANES):
        xb = x[:, k * LANES:(k + 1) * LANES]
        cols.append(xb * cos + pltpu.roll(xb, LANES - half, 1) * sin_lo + pltpu.roll(xb, half, 1) * sin_hi)
    return cols[0] if len(cols) == 1 else jnp.concatenate(cols, axis=1)


def _proj_kernel(x_ref, pos_ref, sc_ref, sh_ref, g_ref, w_ref, tab_ref,
                 qa_ref, kat_ref, va_ref, qi_ref, kit_ref, wi_ref, qb_ref, kb_ref, vb_ref, gb_ref,
                 *, d_a, d_i, d_b):
    h = _rmsnorm_mod(x_ref[...], g_ref[...], sc_ref[0], sh_ref[0]).astype(jnp.bfloat16)
    pos = pos_ref[...]
    ang_a = pos * tab_ref[0:1, :]
    cos_a, sin_a = jnp.cos(ang_a), jnp.sin(ang_a)
    sa_lo, sa_hi = sin_a * tab_ref[1:2, :], sin_a * tab_ref[2:3, :]
    ang_b = pos * tab_ref[3:4, :]
    cos_b, sin_b = jnp.cos(ang_b), jnp.sin(ang_b)
    sb_lo, sb_hi = sin_b * tab_ref[4:5, :], sin_b * tab_ref[5:6, :]
    half_a, half_b = ROT_DIM // 2, HEAD_DIM // 2

    def seg(lo, width):
        return jnp.dot(h, w_ref[:, lo:lo + width], preferred_element_type=jnp.float32)

    o = 0
    qa = seg(o, d_a); o += d_a
    qa_ref[...] = (_rope_lanes(qa, cos_a, sa_lo, sa_hi, half_a) * (HEAD_DIM ** -0.5 * LOG2E)).astype(qa_ref.dtype)
    ka = seg(o, d_a); o += d_a
    kat_ref[0, 0] = _rope_lanes(ka, cos_a, sa_lo, sa_hi, half_a).T.astype(kat_ref.dtype)
    va_ref[...] = seg(o, d_a).astype(va_ref.dtype); o += d_a
    qi = seg(o, d_i); o += d_i
    qi_ref[...] = (_rope_lanes(qi, cos_a, sa_lo, sa_hi, half_a) * (IDX_DIM ** -0.5)).astype(qi_ref.dtype)
    kw = seg(o, LANES); o += LANES
    kit_ref[0, 0] = _rope_lanes(kw, cos_a, sa_lo, sa_hi, half_a).T[0:IDX_DIM, :].astype(kit_ref.dtype)
    wi_ref[...] = kw[:, IDX_DIM:IDX_DIM + N_IDX_HEADS] * (N_IDX_HEADS ** -0.5)
    qb = seg(o, d_b); o += d_b
    qb_ref[...] = _rope_lanes(qb, cos_b, sb_lo, sb_hi, half_b).astype(qb_ref.dtype)
    kb = seg(o, d_b); o += d_b
    kb_ref[...] = (_rope_lanes(kb, cos_b, sb_lo, sb_hi, half_b) * (HEAD_DIM ** -0.5)).astype(kb_ref.dtype)
    vb_ref[...] = seg(o, d_b).astype(vb_ref.dtype); o += d_b
    gb_ref[...] = seg(o, d_b)


def _rope_tables():
    lane = jnp.arange(LANES) % HEAD_DIM
    rows = []
    for rot, theta in ((ROT_DIM, ROPE_THETA), (HEAD_DIM, RET_THETA)):
        half = rot // 2
        inv_freq = theta ** (-jnp.arange(half, dtype=jnp.float32) / half)
        rows.append(jnp.where(lane < rot, inv_freq[lane % half], 0.0))
        rows.append(jnp.where(lane < half, -1.0, 0.0))
        rows.append(jnp.where((lane >= half) & (lane < rot), 1.0, 0.0))
    rows += [jnp.zeros((LANES,), jnp.float32)] * 2
    return jnp.stack(rows).astype(jnp.float32)


def _proj(x2, pos2, mod3, g_mix, w_cat, seq):
    n, d = x2.shape
    bsz = n // seq
    d_a = d // 2
    d_b = d // 2
    d_i = N_IDX_HEADS * IDX_DIM
    tiles_per_seq = seq // TM
    tab = _rope_tables()
    row = lambda i: (i, 0)
    const = lambda i: (0, 0)
    tile4 = lambda i: (i // tiles_per_seq, i % tiles_per_seq, 0, 0)
    bf = jnp.bfloat16
    out_shape = (
        jax.ShapeDtypeStruct((n, d_a), bf),
        jax.ShapeDtypeStruct((bsz, tiles_per_seq, d_a, TM), bf),
        jax.ShapeDtypeStruct((n, d_a), bf),
        jax.ShapeDtypeStruct((n, d_i), bf),
        jax.ShapeDtypeStruct((bsz, tiles_per_seq, IDX_DIM, TM), bf),
        jax.ShapeDtypeStruct((n, N_IDX_HEADS), jnp.float32),
        jax.ShapeDtypeStruct((n, d_b), bf),
        jax.ShapeDtypeStruct((n, d_b), bf),
        jax.ShapeDtypeStruct((n, d_b), bf),
        jax.ShapeDtypeStruct((n, d_b), jnp.float32),
    )
    out_specs = (
        pl.BlockSpec((TM, d_a), row),
        pl.BlockSpec((1, 1, d_a, TM), tile4),
        pl.BlockSpec((TM, d_a), row),
        pl.BlockSpec((TM, d_i), row),
        pl.BlockSpec((1, 1, IDX_DIM, TM), tile4),
        pl.BlockSpec((TM, N_IDX_HEADS), row),
        pl.BlockSpec((TM, d_b), row),
        pl.BlockSpec((TM, d_b), row),
        pl.BlockSpec((TM, d_b), row),
        pl.BlockSpec((TM, d_b), row),
    )
    return pl.pallas_call(
        functools.partial(_proj_kernel, d_a=d_a, d_i=d_i, d_b=d_b),
        grid=(n // TM,),
        in_specs=[pl.BlockSpec((TM, d), row),
                  pl.BlockSpec((TM, 1), row),
                  pl.BlockSpec((1, 1, d), lambda i: ((i // tiles_per_seq) * 6 + 1, 0, 0)),
                  pl.BlockSpec((1, 1, d), lambda i: ((i // tiles_per_seq) * 6 + 0, 0, 0)),
                  pl.BlockSpec((1, d), const),
                  pl.BlockSpec(w_cat.shape, const),
                  pl.BlockSpec(tab.shape, const)],
        out_specs=out_specs,
        out_shape=out_shape,
        compiler_params=_cparams(("arbitrary",)),
        name="proj",
    )(x2, pos2, mod3, mod3, g_mix, w_cat, tab)


def _lane_blocks(x):
    return [x[:, c * LANES:(c + 1) * LANES] for c in range(x.shape[1] // LANES)]


def _dsa_kernel(qa_ref, qi_ref, wi_ref, kit_ref, kat_ref, va_ref, o_ref,
                qih_ref, wb_ref, sc_ref, qm_ref, m_ref, l_ref, acc_ref, sa_ref, sb_ref, mxa_ref, mxb_ref,
                *, k_top, n_heads):
    i = pl.program_id(1)
    n_grp = (i * QB + QB + KG * TM - 1) // (KG * TM)
    n_kt = n_grp * KG

    for h in range(N_IDX_HEADS):
        qih_ref[h * QB:(h + 1) * QB, :] = qi_ref[:, h * IDX_DIM:(h + 1) * IDX_DIM]
        wb_ref[h] = jnp.broadcast_to(wi_ref[:, h:h + 1], (QB, TM))

    lane_q = lax.broadcasted_iota(jnp.int32, (QB, LANES), 1)
    for h in range(n_heads):
        pair = qa_ref[:, (h // 2) * LANES:(h // 2 + 1) * LANES]
        own = (lane_q < HEAD_DIM) if h % 2 == 0 else (lane_q >= HEAD_DIM)
        qm_ref[h] = jnp.where(own, pair, jnp.zeros_like(pair))

    q_chunk = (i * QB + lax.broadcasted_iota(jnp.int32, (QB, TM), 0)) // CHUNK
    key_in_tile = lax.broadcasted_iota(jnp.int32, (QB, TM), 1)

    def idx_grp(g, carry):
        lo, hi = carry
        for u in range(KG):
            j = g * KG + u
            d = jnp.dot(qih_ref[...], kit_ref[0, j], preferred_element_type=jnp.float32)
            acc = wb_ref[0] * jnp.maximum(d[0:QB], 0.0)
            for h in range(1, N_IDX_HEADS):
                acc = acc + wb_ref[h] * jnp.maximum(d[h * QB:(h + 1) * QB], 0.0)
            adm = (j * TM + key_in_tile) // CHUNK <= q_chunk
            sc_ref[j] = jnp.where(adm, acc, -jnp.inf)
            lo = jnp.minimum(lo, jnp.where(adm, acc, jnp.inf))
            hi = jnp.maximum(hi, jnp.where(adm, acc, -jnp.inf))
        return lo, hi

    lo, hi = lax.fori_loop(0, n_grp, idx_grp,
                           (jnp.full((QB, TM), jnp.inf, jnp.float32), jnp.full((QB, TM), -jnp.inf, jnp.float32)))
    lo = jnp.min(lo, axis=1, keepdims=True)
    hi = jnp.max(hi, axis=1, keepdims=True)
    n_adm = ((i * QB + lax.broadcasted_iota(jnp.int32, (QB, 1), 0)) // CHUNK + 1) * CHUNK

    def bisect_cond(carry):
        it, _, _, _, unsettled = carry
        return (it < MAX_BISECT) & (unsettled > 0.0)

    def bisect(carry):
        it, lo, hi, cnt_lo, _ = carry
        mid = lo + (hi - lo) * 0.5
        mid_b = jnp.broadcast_to(mid, (QB, LANES))

        def count_grp(g, cnt):
            for u in range(KG):
                for blk in _lane_blocks(sc_ref[g * KG + u]):
                    cnt = cnt + jnp.where(blk >= mid_b, 1.0, 0.0)
            return cnt

        cnt = lax.fori_loop(0, n_grp, count_grp, jnp.zeros((QB, LANES), jnp.float32))
        c = jnp.sum(cnt, axis=1, keepdims=True)
        ge = c >= k_top
        lo = jnp.where(ge, mid, lo)
        hi = jnp.where(ge, hi, mid)
        cnt_lo = jnp.where(ge, c, cnt_lo)
        return it + 1, lo, hi, cnt_lo, jnp.max(jnp.where(cnt_lo > k_top, 1.0, 0.0))

    cnt0 = n_adm.astype(jnp.float32)
    _, thr, _, _, _ = lax.while_loop(bisect_cond, bisect,
                                     (0, lo, hi, cnt0, jnp.max(jnp.where(cnt0 > k_top, 1.0, 0.0))))

    def bias_tile(j, carry):
        sc_ref[j] = jnp.where(sc_ref[j] >= thr, 0.0, NEG_BIG)
        return carry

    lax.fori_loop(0, n_kt, bias_tile, 0)

    m_ref[...] = jnp.full(m_ref.shape, NEG_BIG, jnp.float32)
    l_ref[...] = jnp.zeros(l_ref.shape, jnp.float32)
    acc_ref[...] = jnp.zeros(acc_ref.shape, jnp.float32)

    def pair_rows(h):
        return slice((h // 2) * LANES, (h // 2 + 1) * LANES)

    def logits_into(s_ref, mx_ref, j):
        bias = sc_ref[j]
        for h in range(n_heads):
            s = jnp.dot(qm_ref[h], kat_ref[0, j, pair_rows(h), :], preferred_element_type=jnp.float32) + bias
            s_ref[h] = s
            mx = functools.reduce(jnp.maximum, _lane_blocks(s))
            mx_ref[h] = jnp.broadcast_to(jnp.max(mx, axis=1, keepdims=True), (QB, LANES))

    def absorb(s_ref, mx_ref, j):
        for h in range(n_heads):
            m_old = m_ref[h]
            m_new = jnp.maximum(m_old, mx_ref[h])
            alpha = jnp.exp2(m_old - m_new)
            p = [jnp.exp2(b - m_new) for b in _lane_blocks(s_ref[h])]
            l_ref[h] = l_ref[h] * alpha + functools.reduce(jnp.add, p)
            pv = jnp.dot(jnp.concatenate(p, axis=1).astype(va_ref.dtype), va_ref[0, j, :, pair_rows(h)],
                         preferred_element_type=jnp.float32)
            acc_ref[h] = acc_ref[h] * alpha + pv
            m_ref[h] = m_new

    logits_into(sa_ref, mxa_ref, 0)

    def attn_grp(g, carry):
        logits_into(sb_ref, mxb_ref, 2 * g + 1)
        absorb(sa_ref, mxa_ref, 2 * g)
        logits_into(sa_ref, mxa_ref, jnp.minimum(2 * g + 2, n_kt - 1))
        absorb(sb_ref, mxb_ref, 2 * g + 1)
        return carry

    lax.fori_loop(0, n_grp, attn_grp, 0)

    for hp in range(n_heads // 2):
        even = acc_ref[2 * hp] / jnp.sum(l_ref[2 * hp], axis=1, keepdims=True)
        odd = acc_ref[2 * hp + 1] / jnp.sum(l_ref[2 * hp + 1], axis=1, keepdims=True)
        o_ref[:, hp * LANES:(hp + 1) * LANES] = jnp.where(lane_q < HEAD_DIM, even, odd).astype(o_ref.dtype)


def _dsa(qa, qi, wi, kit, kat, va4, seq):
    n, d_a = qa.shape
    bsz = n // seq
    n_heads = d_a // HEAD_DIM
    n_qb = seq // QB
    n_kt = seq // TM
    k_top = min(TOPK_MAX, seq // 4)
    qrow = lambda b, i: (b * n_qb + i, 0)
    per_b = lambda b, i: (b, 0, 0, 0)
    return pl.pallas_call(
        functools.partial(_dsa_kernel, k_top=float(k_top), n_heads=n_heads),
        grid=(bsz, n_qb),
        in_specs=[pl.BlockSpec((QB, d_a), qrow),
                  pl.BlockSpec((QB, qi.shape[1]), qrow),
                  pl.BlockSpec((QB, N_IDX_HEADS), qrow),
                  pl.BlockSpec((1, n_kt, IDX_DIM, TM), per_b, pipeline_mode=pl.Buffered(1)),
                  pl.BlockSpec((1, n_kt, d_a, TM), per_b, pipeline_mode=pl.Buffered(1)),
                  pl.BlockSpec((1, n_kt, TM, d_a), per_b, pipeline_mode=pl.Buffered(1))],
        out_specs=pl.BlockSpec((QB, d_a), qrow),
        out_shape=jax.ShapeDtypeStruct((n, d_a), jnp.bfloat16),
        scratch_shapes=[pltpu.VMEM((N_IDX_HEADS * QB, IDX_DIM), jnp.bfloat16),
                        pltpu.VMEM((N_IDX_HEADS, QB, TM), jnp.float32),
                        pltpu.VMEM((n_kt, QB, TM), jnp.float32),
                        pltpu.VMEM((n_heads, QB, LANES), jnp.bfloat16),
                        pltpu.VMEM((n_heads, QB, LANES), jnp.float32),
                        pltpu.VMEM((n_heads, QB, LANES), jnp.float32),
                        pltpu.VMEM((n_heads, QB, LANES), jnp.float32),
                        pltpu.VMEM((n_heads, QB, TM), jnp.float32),
                        pltpu.VMEM((n_heads, QB, TM), jnp.float32),
                        pltpu.VMEM((n_heads, QB, LANES), jnp.float32),
                        pltpu.VMEM((n_heads, QB, LANES), jnp.float32)],
        compiler_params=_cparams(("arbitrary", "arbitrary")),
        name="dsa",
    )(qa, qi, wi, kit, kat, va4)


def _ret_kernel(q_ref, k_ref, v_ref, g_ref, dec_ref, zeta_ref, xi_ref, gc_ref, o_ref, st_ref, *, n_heads):
    @pl.when(pl.program_id(1) == 0)
    def _():
        st_ref[...] = jnp.zeros_like(st_ref)

    for h in range(n_heads):
        sl = slice(h * HEAD_DIM, (h + 1) * HEAD_DIM)
        q, k, v = q_ref[:, sl], k_ref[:, sl], v_ref[:, sl]
        s = lax.dot_general(q, k, (((1,), (1,)), ((), ())), preferred_element_type=jnp.float32) * dec_ref[h]
        inner = jnp.dot(s.astype(v.dtype), v, preferred_element_type=jnp.float32)
        state = st_ref[h]
        cross = jnp.dot(q, state.astype(q.dtype), preferred_element_type=jnp.float32) * xi_ref[h]
        y = inner + cross
        mu = jnp.mean(y, axis=-1, keepdims=True)
        yc = y - mu
        yn = yc * lax.rsqrt(jnp.mean(yc * yc, axis=-1, keepdims=True) + EPS)
        g = g_ref[:, sl]
        o_ref[:, sl] = (g / (1.0 + jnp.exp(-g)) * yn).astype(o_ref.dtype)
        kz = (k.astype(jnp.float32) * zeta_ref[h]).astype(k.dtype)
        kv = lax.dot_general(kz, v, (((0,), (0,)), ((), ())), preferred_element_type=jnp.float32)
        st_ref[h] = state * gc_ref[h] + kv


def _ret_consts(n_heads):
    log_gamma = jnp.log1p(-jnp.exp2(-5.0 - jnp.arange(n_heads, dtype=jnp.float32)))
    pos = jnp.arange(RET_C, dtype=jnp.float32)
    diff = pos[:, None] - pos[None, :]
    dec = jnp.where(diff[None] >= 0, jnp.exp(jnp.maximum(diff, 0.0)[None] * log_gamma[:, None, None]), 0.0)
    zeta = jnp.exp((RET_C - 1.0 - pos)[None, :] * log_gamma[:, None])
    xi = jnp.exp((pos + 1.0)[None, :] * log_gamma[:, None])
    gc = jnp.exp(RET_C * log_gamma)
    bc = lambda a, shape: jnp.broadcast_to(a, shape).astype(jnp.float32)
    return (dec, bc(zeta[:, :, None], (n_heads, RET_C, HEAD_DIM)), bc(xi[:, :, None], (n_heads, RET_C, HEAD_DIM)),
            bc(gc[:, None, None], (n_heads, HEAD_DIM, HEAD_DIM)))


def _ret(qb, kb, vb, gb, seq):
    n, d_b = qb.shape
    bsz = n // seq
    n_heads = d_b // HEAD_DIM
    n_c = seq // RET_C
    dec, zeta, xi, gc = _ret_consts(n_heads)
    row = lambda b, c: (b * n_c + c, 0)
    const3 = lambda b, c: (0, 0, 0)
    return pl.pallas_call(
        functools.partial(_ret_kernel, n_heads=n_heads),
        grid=(bsz, n_c),
        in_specs=[pl.BlockSpec((RET_C, d_b), row)] * 4 + [
            pl.BlockSpec(dec.shape, const3), pl.BlockSpec(zeta.shape, const3),
            pl.BlockSpec(xi.shape, const3), pl.BlockSpec(gc.shape, const3)],
        out_specs=pl.BlockSpec((RET_C, d_b), row),
        out_shape=jax.ShapeDtypeStruct((n, d_b), jnp.bfloat16),
        scratch_shapes=[pltpu.VMEM((n_heads, HEAD_DIM, HEAD_DIM), jnp.float32)],
        compiler_params=_cparams(("arbitrary", "arbitrary")),
        name="ret",
    )(qb, kb, vb, gb, dec, zeta, xi, gc)


def _lane_first_eq(x, m, lane):
    return jnp.min(jnp.where(x == m, lane, float(LANES)), axis=1, keepdims=True)


def _mix_out_kernel(x_ref, ya_ref, yb_ref, woa_ref, wob_ref, gt_ref, sc_ref, sh_ref, g_ref, wr_ref, tri_ref,
                    x1_ref, h2_ref, route_ref, cnt_ref, carry_ref):
    @pl.when(pl.program_id(0) == 0)
    def _():
        carry_ref[...] = jnp.zeros_like(carry_ref)

    mix = (jnp.dot(ya_ref[...], woa_ref[...], preferred_element_type=jnp.float32)
           + jnp.dot(yb_ref[...], wob_ref[...], preferred_element_type=jnp.float32))
    x1 = x_ref[...] + gt_ref[0] * mix
    x1_ref[...] = x1
    h2 = _rmsnorm_mod(x1, g_ref[...], sc_ref[0], sh_ref[0])
    h2_ref[...] = h2

    lg = jnp.dot(h2.astype(jnp.bfloat16), wr_ref[...], preferred_element_type=jnp.float32)
    lane = lax.broadcasted_iota(jnp.int32, lg.shape, 1).astype(jnp.float32)
    is_grp = (lane >= N_EXPERTS) & (lane < N_EXPERTS + N_GROUPS)
    gl = jnp.where(is_grp, lg, -jnp.inf)
    gmax = jnp.max(gl, axis=1, keepdims=True)
    grp = _lane_first_eq(gl, gmax, lane) - N_EXPERTS
    p_grp = 1.0 / jnp.sum(jnp.exp(gl - gmax), axis=1, keepdims=True)
    in_grp = jnp.floor(lane * (1.0 / EXPERTS_PER_GROUP)) == grp
    f = jnp.where(in_grp & (lane < N_EXPERTS), lg, -jnp.inf)
    f1 = jnp.max(f, axis=1, keepdims=True)
    e1 = _lane_first_eq(f, f1, lane)
    f = jnp.where(lane == e1, -jnp.inf, f)
    f2 = jnp.max(f, axis=1, keepdims=True)
    e2 = _lane_first_eq(f, f2, lane)
    a2 = jnp.exp(f2 - f1)
    w1 = p_grp / (1.0 + a2)
    w2 = p_grp * a2 / (1.0 + a2)

    oh1 = jnp.where(lane == e1, 1.0, 0.0)
    oh2 = jnp.where(lane == e2, 1.0, 0.0)
    both = oh1 + oh2
    before = jnp.dot(tri_ref[...], both.astype(jnp.bfloat16), preferred_element_type=jnp.float32) + carry_ref[...]
    r1 = jnp.sum(before * oh1, axis=1, keepdims=True)
    r2 = jnp.sum(before * oh2, axis=1, keepdims=True)
    carry = carry_ref[...] + jnp.sum(both, axis=0, keepdims=True)
    carry_ref[...] = carry
    cnt_ref[...] = carry

    out = jnp.zeros(lg.shape, jnp.float32)
    for col, val in enumerate((e1, e2, w1, w2, r1, r2)):
        out = jnp.where(lane == col, val, out)
    route_ref[...] = out


def _mix_out(x2, ya, yb, wo_a, wo_b, mod3, g_ffn, w_route, seq):
    n, d = x2.shape
    tiles_per_seq = seq // TM
    tri = jnp.asarray(np.tril(np.ones((TM, TM), np.float32), -1), jnp.bfloat16)
    row = lambda i: (i, 0)
    const = lambda i: (0, 0)
    modk = lambda k: pl.BlockSpec((1, 1, d), lambda i: ((i // tiles_per_seq) * 6 + k, 0, 0))
    return pl.pallas_call(
        _mix_out_kernel,
        grid=(n // TM,),
        in_specs=[pl.BlockSpec((TM, d), row),
                  pl.BlockSpec((TM, ya.shape[1]), row),
                  pl.BlockSpec((TM, yb.shape[1]), row),
                  pl.BlockSpec(wo_a.shape, const),
                  pl.BlockSpec(wo_b.shape, const),
                  modk(2), modk(4), modk(3),
                  pl.BlockSpec((1, d), const),
                  pl.BlockSpec(w_route.shape, const),
                  pl.BlockSpec(tri.shape, const)],
        out_specs=(pl.BlockSpec((TM, d), row), pl.BlockSpec((TM, d), row),
                   pl.BlockSpec((TM, LANES), row), pl.BlockSpec((1, LANES), const)),
        out_shape=(jax.ShapeDtypeStruct((n, d), jnp.float32), jax.ShapeDtypeStruct((n, d), jnp.float32),
                   jax.ShapeDtypeStruct((n, LANES), jnp.float32), jax.ShapeDtypeStruct((1, LANES), jnp.float32)),
        scratch_shapes=[pltpu.VMEM((1, LANES), jnp.float32)],
        compiler_params=_cparams(("arbitrary",)),
        name="mix_out",
    )(x2, ya, yb, wo_a, wo_b, mod3, mod3, mod3, g_ffn, w_route, tri)


def _plan_kernel(route_ref, pst_ref, dest_ref):
    r = route_ref[...]
    lane = lax.broadcasted_iota(jnp.int32, r.shape, 1).astype(jnp.float32)
    pst = pst_ref[...]
    d1 = jnp.sum(jnp.where(lane == r[:, 0:1], pst, 0.0), axis=1, keepdims=True) + r[:, 4:5]
    d2 = jnp.sum(jnp.where(lane == r[:, 1:2], pst, 0.0), axis=1, keepdims=True) + r[:, 5:6]
    both = jnp.where(lane == 0.0, d1, jnp.where(lane == 1.0, d2, 0.0))
    dest_ref[0] = both.T[0:8, :].astype(jnp.int32)


def _plan(route, pst_row):
    n = route.shape[0]
    return pl.pallas_call(
        _plan_kernel,
        grid=(n // TM,),
        in_specs=[pl.BlockSpec((TM, LANES), lambda i: (i, 0)), pl.BlockSpec((1, LANES), lambda i: (0, 0))],
        out_specs=pl.BlockSpec((1, 8, TM), lambda i: (i, 0, 0)),
        out_shape=jax.ShapeDtypeStruct((n // TM, 8, TM), jnp.int32),
        compiler_params=_cparams(("arbitrary",)),
        name="plan",
    )(route, pst_row)


def _dispatch_kernel(zrow_ref, n_used_ref, dest_ref, h2_ref, xs_hbm, zbuf, sem, zsem, *, n_blk):
    i = pl.program_id(0)

    @pl.when(i == 0)
    def _():
        zbuf[...] = jnp.zeros(zbuf.shape, zbuf.dtype)
        for e in range(N_EXPERTS):
            pltpu.make_async_copy(zbuf, xs_hbm.at[pl.ds(pl.multiple_of(zrow_ref[e], SUBLANES), BM)], zsem).start()
        for e in range(N_EXPERTS):
            pltpu.make_async_copy(zbuf, xs_hbm.at[pl.ds(0, BM)], zsem).wait()
        for b in range(N_EXPERTS + 1):
            @pl.when(n_used_ref[0] + b <= n_blk)
            def _():
                tail = pltpu.make_async_copy(zbuf, xs_hbm.at[pl.ds((n_used_ref[0] + b) * BM, BM)], zsem)
                tail.start()
                tail.wait()

    def issue(r, c):
        row = h2_ref.at[pl.ds(r, 1)]
        pltpu.make_async_copy(row, xs_hbm.at[pl.ds(dest_ref[0, 0, r], 1)], sem).start()
        pltpu.make_async_copy(row, xs_hbm.at[pl.ds(dest_ref[0, 1, r], 1)], sem).start()
        return c

    lax.fori_loop(0, TM, issue, 0)
    for _ in range(2):
        pltpu.make_async_copy(h2_ref, xs_hbm.at[pl.ds(0, TM)], sem).wait()


def _dispatch(h2, dest, zrow, n_used, n_blk):
    n, d = h2.shape
    n_rows = (n_blk + 1) * BM
    grid_spec = pltpu.PrefetchScalarGridSpec(
        num_scalar_prefetch=2,
        grid=(n // TM,),
        in_specs=[pl.BlockSpec((1, 8, TM), lambda i, z, nu: (i, 0, 0), memory_space=pltpu.SMEM),
                  pl.BlockSpec((TM, d), lambda i, z, nu: (i, 0))],
        out_specs=pl.BlockSpec(memory_space=pl.ANY),
        scratch_shapes=[pltpu.VMEM((BM, d), h2.dtype),
                        pltpu.SemaphoreType.DMA(()),
                        pltpu.SemaphoreType.DMA(())],
    )
    return pl.pallas_call(
        functools.partial(_dispatch_kernel, n_blk=n_blk),
        grid_spec=grid_spec,
        out_shape=jax.ShapeDtypeStruct((n_rows, d), h2.dtype),
        compiler_params=_cparams(("arbitrary",)),
        name="dispatch",
    )(zrow, n_used, dest, h2)


def _experts_kernel(blk_e_ref, n_used_ref, x_ref, wg_ref, wu_ref, wd_ref, y_ref):
    j = pl.program_id(0)

    @pl.when(j < n_used_ref[0])
    def _():
        x = x_ref[...].astype(wg_ref.dtype)
        a = jnp.dot(x, wg_ref[0], preferred_element_type=jnp.float32)
        b = jnp.dot(x, wu_ref[0], preferred_element_type=jnp.float32)
        hmid = (a / (1.0 + jnp.exp(-a)) * b).astype(x.dtype)
        y_ref[...] = jnp.dot(hmid, wd_ref[0], preferred_element_type=jnp.float32)

    @pl.when(j >= n_used_ref[0])
    def _():
        y_ref[...] = jnp.zeros(y_ref.shape, y_ref.dtype)


def _experts(xs, wg, wu, wd, blk_e, n_used):
    d = xs.shape[1]
    n_blk = blk_e.shape[0]
    d_e = wg.shape[2]
    grid_spec = pltpu.PrefetchScalarGridSpec(
        num_scalar_prefetch=2,
        grid=(n_blk,),
        in_specs=[pl.BlockSpec((BM, d), lambda j, be, nu: (jnp.minimum(j, nu[0] - 1), 0)),
                  pl.BlockSpec((1, d, d_e), lambda j, be, nu: (be[j], 0, 0)),
                  pl.BlockSpec((1, d, d_e), lambda j, be, nu: (be[j], 0, 0)),
                  pl.BlockSpec((1, d_e, d), lambda j, be, nu: (be[j], 0, 0))],
        out_specs=pl.BlockSpec((BM, d), lambda j, be, nu: (j, 0)),
    )
    return pl.pallas_call(
        _experts_kernel,
        grid_spec=grid_spec,
        out_shape=jax.ShapeDtypeStruct((n_blk * BM, d), jnp.float32),
        compiler_params=_cparams(("arbitrary",)),
        name="experts",
    )(blk_e, n_used, xs, wg, wu, wd)


def _final_kernel(dest_ref, dest_next_ref, x1_ref, route_ref, gt_ref, g_ref, ys_hbm, o_ref, ybuf, sem):
    i = pl.program_id(0)
    slot = i % 2

    def gather(d_ref, s):
        def issue(r, c):
            pltpu.make_async_copy(ys_hbm.at[pl.ds(d_ref[0, 0, r], 1)], ybuf.at[s, pl.ds(r, 1)], sem.at[s]).start()
            pltpu.make_async_copy(ys_hbm.at[pl.ds(d_ref[0, 1, r], 1)], ybuf.at[s, pl.ds(TM + r, 1)],
                                  sem.at[s]).start()
            return c
        lax.fori_loop(0, TM, issue, 0)

    @pl.when(i == 0)
    def _():
        gather(dest_ref, 0)

    @pl.when(i + 1 < pl.num_programs(0))
    def _():
        gather(dest_next_ref, 1 - slot)

    pltpu.make_async_copy(ys_hbm.at[pl.ds(0, 2 * TM)], ybuf.at[slot], sem.at[slot]).wait()
    w1 = route_ref[:, 2:3]
    w2 = route_ref[:, 3:4]
    x2 = x1_ref[...] + gt_ref[0] * (w1 * ybuf[slot, 0:TM, :] + w2 * ybuf[slot, TM:2 * TM, :])
    o_ref[...] = x2 * lax.rsqrt(jnp.mean(x2 * x2, axis=-1, keepdims=True) + EPS) * g_ref[...]


def _final(x1, ys, dest, route, mod3, g_final, seq):
    n, d = x1.shape
    n_tiles = n // TM
    tiles_per_seq = seq // TM
    row = lambda i: (i, 0)
    return pl.pallas_call(
        _final_kernel,
        grid=(n_tiles,),
        in_specs=[pl.BlockSpec((1, 8, TM), lambda i: (i, 0, 0), memory_space=pltpu.SMEM),
                  pl.BlockSpec((1, 8, TM), lambda i: (jnp.minimum(i + 1, n_tiles - 1), 0, 0),
                               memory_space=pltpu.SMEM),
                  pl.BlockSpec((TM, d), row),
                  pl.BlockSpec((TM, LANES), row),
                  pl.BlockSpec((1, 1, d), lambda i: ((i // tiles_per_seq) * 6 + 5, 0, 0)),
                  pl.BlockSpec((1, d), lambda i: (0, 0)),
                  pl.BlockSpec(memory_space=pl.ANY)],
        out_specs=pl.BlockSpec((TM, d), row),
        out_shape=jax.ShapeDtypeStruct((n, d), jnp.float32),
        scratch_shapes=[pltpu.VMEM((2, 2 * TM, d), jnp.float32), pltpu.SemaphoreType.DMA((2,))],
        compiler_params=_cparams(("arbitrary",)),
        name="final",
    )(dest, dest, x1, route, mod3, g_final, ys)


def _block_layout(counts, n):
    cnt = counts[0, :N_EXPERTS].astype(jnp.int32)
    blocks = (cnt + BM - 1) // BM
    bends = jnp.cumsum(blocks)
    pstarts = (bends - blocks) * BM
    n_blk = (2 * n) // BM + N_EXPERTS
    pst_row = jnp.zeros((1, LANES), jnp.float32).at[0, :N_EXPERTS].set(pstarts.astype(jnp.float32))
    zrow = ((pstarts + cnt) // SUBLANES * SUBLANES).astype(jnp.int32)
    n_used = bends[-1:].astype(jnp.int32)
    blk_e = jnp.sum(bends[None, :] <= jnp.arange(n_blk, dtype=jnp.int32)[:, None], axis=1)
    blk_e = jnp.minimum(blk_e, N_EXPERTS - 1).astype(jnp.int32)
    return pst_row, zrow, n_used, blk_e


def _layer(x2, c, pos2, w_ada, b_ada, g_mix, w_in, w_o, g_ffn, w_rg, w_re, w_up, w_gate, w_down, seq):
    n, d = x2.shape
    bsz = n // seq
    bf = jnp.bfloat16
    d_a = d // 2
    d_i = N_IDX_HEADS * IDX_DIM
    mod3 = _ada(c, w_ada, b_ada).reshape(bsz * 6, 1, d)

    c0 = 3 * d_a + d_i
    c1 = c0 + IDX_DIM + N_IDX_HEADS
    w_cat = jnp.concatenate([w_in[:, :c0], w_in[:, c0:c1],
                             jnp.zeros((d, LANES - (c1 - c0)), w_in.dtype), w_in[:, c1:]], axis=1).astype(bf)
    qa, kat, va, qi, kit, wi, qb, kb, vb, gb = _proj(x2, pos2, mod3, g_mix.reshape(1, d), w_cat, seq)

    ya = _dsa(qa, qi, wi, kit, kat, va.reshape(bsz, seq // TM, TM, d_a), seq)
    yb = _ret(qb, kb, vb, gb, seq)

    w_route = jnp.concatenate([jnp.transpose(w_re, (1, 0, 2)).reshape(d, N_EXPERTS), w_rg,
                               jnp.zeros((d, LANES - N_EXPERTS - N_GROUPS), w_rg.dtype)], axis=1).astype(bf)
    x1, h2, route, counts = _mix_out(x2, ya, yb, w_o[:d_a].astype(bf), w_o[d_a:].astype(bf), mod3,
                                     g_ffn.reshape(1, d), w_route, seq)

    pst_row, zrow, n_used, blk_e = _block_layout(counts, n)
    dest = _plan(route, pst_row)
    xs = _dispatch(h2, dest, zrow, n_used, blk_e.shape[0])
    ys = _experts(xs, w_gate.astype(bf), w_up.astype(bf), w_down.astype(bf), blk_e, n_used)
    return x1, ys, dest, route, mod3


def kernel(x, c, positions, w_ada, b_ada, g_norm_mix, w_in, w_o, g_norm_ffn, w_router_group, w_router_expert,
           w_up, w_gate, w_down, g_norm_final):
    bsz, seq, d = x.shape
    depth = w_ada.shape[0]
    assert depth == 1, "the final norm is fused into the last layer's combine kernel"
    assert seq % (KG * TM) == 0 and seq % RET_C == 0 and TM % QB == 0 and (2 * bsz * seq) % BM == 0
    x2 = x.reshape(bsz * seq, d)
    pos2 = positions.astype(jnp.float32).reshape(bsz * seq, 1)
    x1, ys, dest, route, mod3 = _layer(x2, c, pos2, w_ada[0], b_ada[0], g_norm_mix[0], w_in[0], w_o[0],
                                       g_norm_ffn[0], w_router_group[0], w_router_expert[0], w_up[0], w_gate[0],
                                       w_down[0], seq)
    out = _final(x1, ys, dest, route, mod3, g_norm_final.reshape(1, d), seq)
    return out.reshape(bsz, seq, d)
```

```python
import functools

import jax
import jax.numpy as jnp
import numpy as np
from jax import lax
from jax.experimental import pallas as pl
from jax.experimental.pallas import tpu as pltpu

CHUNK = 64
HEAD_DIM = 64
N_IDX_HEADS = 16
IDX_DIM = 64
TOPK_MAX = 256
ROPE_THETA = 500000.0
ROT_DIM = HEAD_DIM // 4
RET_THETA = 10000.0
N_GROUPS = 4
EXPERTS_PER_GROUP = 8
N_EXPERTS = N_GROUPS * EXPERTS_PER_GROUP
EPS = 1e-6

LANES = 128
SUBLANES = 8
VMEM_LIMIT = 56 * 1024 * 1024

TM = 256
QB = TM
RET_C = 256
BM = 256
CNT_ROWS = 64
ONES_ROWS = 16
MAX_BISECT = 40

NEG_BIG = -1e30
LOG2E = 1.4426950408889634


def _cparams(sem):
    return pltpu.CompilerParams(dimension_semantics=sem, vmem_limit_bytes=VMEM_LIMIT)


def _ada_kernel(c_ref, w_ref, b_ref, o_ref):
    o_ref[...] = jnp.dot(c_ref[...], w_ref[...], preferred_element_type=jnp.float32) + b_ref[...]


def _ada(c, w_ada, b_ada):
    bsz, d = c.shape
    n_out = w_ada.shape[1]
    return pl.pallas_call(
        _ada_kernel,
        grid=(n_out // d,),
        in_specs=[pl.BlockSpec((bsz, d), lambda j: (0, 0)),
                  pl.BlockSpec((d, d), lambda j: (0, j)),
                  pl.BlockSpec((1, d), lambda j: (0, j))],
        out_specs=pl.BlockSpec((bsz, d), lambda j: (0, j)),
        out_shape=jax.ShapeDtypeStruct((bsz, n_out), jnp.float32),
        compiler_params=_cparams(("arbitrary",)),
        name="ada",
    )(c, w_ada, b_ada.reshape(1, n_out))


def _rmsnorm_mod(x, g, sc, sh):
    xn = x * lax.rsqrt(jnp.mean(x * x, axis=-1, keepdims=True) + EPS)
    return xn * g * (1.0 + sc) + sh


def _rope_lanes(x, cos, sin_lo, sin_hi, half):
    cols = []
    for k in range(x.shape[1] // LANES):
        xb = x[:, k * LANES:(k + 1) * LANES]
        cols.append(xb * cos + pltpu.roll(xb, LANES - half, 1) * sin_lo + pltpu.roll(xb, half, 1) * sin_hi)
    return cols[0] if len(cols) == 1 else jnp.concatenate(cols, axis=1)


def _proj_kernel(x_ref, pos_ref, sc_ref, sh_ref, g_ref, w_ref, tab_ref,
                 qat_ref, ka_ref, vat_ref, qit_ref, kiw_ref, wit_ref, qb_ref, kb_ref, vb_ref, gb_ref,
                 *, d_a, d_i, d_b):
    h = _rmsnorm_mod(x_ref[...], g_ref[...], sc_ref[0], sh_ref[0]).astype(jnp.bfloat16)
    pos = pos_ref[...]
    ang_a = pos * tab_ref[0:1, :]
    cos_a, sin_a = jnp.cos(ang_a), jnp.sin(ang_a)
    sa_lo, sa_hi = sin_a * tab_ref[1:2, :], sin_a * tab_ref[2:3, :]
    ang_b = pos * tab_ref[3:4, :]
    cos_b, sin_b = jnp.cos(ang_b), jnp.sin(ang_b)
    sb_lo, sb_hi = sin_b * tab_ref[4:5, :], sin_b * tab_ref[5:6, :]
    half_a, half_b = ROT_DIM // 2, HEAD_DIM // 2

    def seg(lo, width):
        return jnp.dot(h, w_ref[:, lo:lo + width], preferred_element_type=jnp.float32)

    o = 0
    qa = seg(o, d_a); o += d_a
    qat_ref[0, 0] = (_rope_lanes(qa, cos_a, sa_lo, sa_hi, half_a) * (HEAD_DIM ** -0.5 * LOG2E)).T.astype(qat_ref.dtype)
    ka = seg(o, d_a); o += d_a
    ka_ref[...] = _rope_lanes(ka, cos_a, sa_lo, sa_hi, half_a).astype(ka_ref.dtype)
    vat_ref[0, 0] = seg(o, d_a).T.astype(vat_ref.dtype); o += d_a
    qi = seg(o, d_i); o += d_i
    qit_ref[0, 0] = (_rope_lanes(qi, cos_a, sa_lo, sa_hi, half_a) * (IDX_DIM ** -0.5)).T.astype(qit_ref.dtype)
    kw = seg(o, LANES); o += LANES
    kiw_ref[...] = _rope_lanes(kw, cos_a, sa_lo, sa_hi, half_a).astype(kiw_ref.dtype)
    wit_ref[0, 0] = kw.T[IDX_DIM:IDX_DIM + N_IDX_HEADS, :] * (N_IDX_HEADS ** -0.5)
    qb = seg(o, d_b); o += d_b
    qb_ref[...] = _rope_lanes(qb, cos_b, sb_lo, sb_hi, half_b).astype(qb_ref.dtype)
    kb = seg(o, d_b); o += d_b
    kb_ref[...] = (_rope_lanes(kb, cos_b, sb_lo, sb_hi, half_b) * (HEAD_DIM ** -0.5)).astype(kb_ref.dtype)
    vb_ref[...] = seg(o, d_b).astype(vb_ref.dtype); o += d_b
    gb_ref[...] = seg(o, d_b)


def _rope_tables():
    lane = jnp.arange(LANES) % HEAD_DIM
    rows = []
    for rot, theta in ((ROT_DIM, ROPE_THETA), (HEAD_DIM, RET_THETA)):
        half = rot // 2
        inv_freq = theta ** (-jnp.arange(half, dtype=jnp.float32) / half)
        rows.append(jnp.where(lane < rot, inv_freq[lane % half], 0.0))
        rows.append(jnp.where(lane < half, -1.0, 0.0))
        rows.append(jnp.where((lane >= half) & (lane < rot), 1.0, 0.0))
    rows += [jnp.zeros((LANES,), jnp.float32)] * 2
    return jnp.stack(rows).astype(jnp.float32)


def _proj(x2, pos2, mod3, g_mix, w_cat, seq):
    n, d = x2.shape
    bsz = n // seq
    d_a = d // 2
    d_b = d // 2
    d_i = N_IDX_HEADS * IDX_DIM
    tiles_per_seq = seq // TM
    tab = _rope_tables()
    row = lambda i: (i, 0)
    const = lambda i: (0, 0)
    tile4 = lambda i: (i // tiles_per_seq, i % tiles_per_seq, 0, 0)
    bf = jnp.bfloat16
    out_shape = (
        jax.ShapeDtypeStruct((bsz, tiles_per_seq, d_a, TM), bf),
        jax.ShapeDtypeStruct((n, d_a), bf),
        jax.ShapeDtypeStruct((bsz, tiles_per_seq, d_a, TM), bf),
        jax.ShapeDtypeStruct((bsz, tiles_per_seq, d_i, TM), bf),
        jax.ShapeDtypeStruct((n, LANES), bf),
        jax.ShapeDtypeStruct((bsz, tiles_per_seq, N_IDX_HEADS, TM), jnp.float32),
        jax.ShapeDtypeStruct((n, d_b), bf),
        jax.ShapeDtypeStruct((n, d_b), bf),
        jax.ShapeDtypeStruct((n, d_b), bf),
        jax.ShapeDtypeStruct((n, d_b), jnp.float32),
    )
    out_specs = (
        pl.BlockSpec((1, 1, d_a, TM), tile4),
        pl.BlockSpec((TM, d_a), row),
        pl.BlockSpec((1, 1, d_a, TM), tile4),
        pl.BlockSpec((1, 1, d_i, TM), tile4),
        pl.BlockSpec((TM, LANES), row),
        pl.BlockSpec((1, 1, N_IDX_HEADS, TM), tile4),
        pl.BlockSpec((TM, d_b), row),
        pl.BlockSpec((TM, d_b), row),
        pl.BlockSpec((TM, d_b), row),
        pl.BlockSpec((TM, d_b), row),
    )
    return pl.pallas_call(
        functools.partial(_proj_kernel, d_a=d_a, d_i=d_i, d_b=d_b),
        grid=(n // TM,),
        in_specs=[pl.BlockSpec((TM, d), row),
                  pl.BlockSpec((TM, 1), row),
                  pl.BlockSpec((1, 1, d), lambda i: ((i // tiles_per_seq) * 6 + 1, 0, 0)),
                  pl.BlockSpec((1, 1, d), lambda i: ((i // tiles_per_seq) * 6 + 0, 0, 0)),
                  pl.BlockSpec((1, d), const),
                  pl.BlockSpec(w_cat.shape, const),
                  pl.BlockSpec(tab.shape, const)],
        out_specs=out_specs,
        out_shape=out_shape,
        compiler_params=_cparams(("arbitrary",)),
        name="proj",
    )(x2, pos2, mod3, mod3, g_mix, w_cat, tab)


def _row_blocks(x, rows):
    return [x[r * rows:(r + 1) * rows] for r in range(x.shape[0] // rows)]


def _dsa_kernel(qat_ref, qit_ref, wit_ref, kiw_ref, ka_ref, vat_ref, o_ref,
                qix_ref, qmx_ref, sc_ref, m_ref, l_ref, acc_ref, sa_ref, sb_ref, mxa_ref, mxb_ref,
                *, k_top, n_heads):
    i = pl.program_id(1)
    n_grp = (i * QB + QB + 2 * TM - 1) // (2 * TM)
    n_kt = 2 * n_grp

    zero_rows = jnp.zeros((LANES - IDX_DIM, QB), qix_ref.dtype)
    for h in range(N_IDX_HEADS):
        qix_ref[h] = jnp.concatenate([qit_ref[0, 0, h * IDX_DIM:(h + 1) * IDX_DIM, :], zero_rows], axis=0)
    row_q = lax.broadcasted_iota(jnp.int32, (LANES, QB), 0)
    for h in range(n_heads):
        pair = qat_ref[0, 0, (h // 2) * LANES:(h // 2 + 1) * LANES, :]
        own = (row_q < HEAD_DIM) if h % 2 == 0 else (row_q >= HEAD_DIM)
        qmx_ref[h] = jnp.where(own, pair, jnp.zeros_like(pair))

    q_chunk = (i * QB + lax.broadcasted_iota(jnp.int32, (TM, QB), 1)) // CHUNK
    key_in_tile = lax.broadcasted_iota(jnp.int32, (TM, QB), 0)
    w_all = wit_ref[0, 0]

    def idx_pair(g, carry):
        lo, hi = carry
        for j in (2 * g, 2 * g + 1):
            kt = kiw_ref[0, j]
            acc = None
            for h in range(N_IDX_HEADS):
                d = jnp.dot(kt, qix_ref[h], preferred_element_type=jnp.float32)
                t = w_all[h:h + 1, :] * jnp.maximum(d, 0.0)
                acc = t if acc is None else acc + t
            adm = (j * TM + key_in_tile) // CHUNK <= q_chunk
            sc_ref[j] = jnp.where(adm, acc, -jnp.inf)
            lo = jnp.minimum(lo, functools.reduce(jnp.minimum, _row_blocks(jnp.where(adm, acc, jnp.inf), SUBLANES)))
            hi = jnp.maximum(hi, functools.reduce(jnp.maximum, _row_blocks(jnp.where(adm, acc, -jnp.inf), SUBLANES)))
        return lo, hi

    lo, hi = lax.fori_loop(0, n_grp, idx_pair, (jnp.full((SUBLANES, QB), jnp.inf, jnp.float32),
                                                jnp.full((SUBLANES, QB), -jnp.inf, jnp.float32)))
    lo = jnp.min(lo, axis=0, keepdims=True)
    hi = jnp.max(hi, axis=0, keepdims=True)
    n_adm = ((i * QB + lax.broadcasted_iota(jnp.int32, (1, QB), 1)) // CHUNK + 1) * CHUNK

    def bisect_cond(carry):
        it, _, _, _, unsettled = carry
        return (it < MAX_BISECT) & (unsettled > 0.0)

    def bisect(carry):
        it, lo, hi, cnt_lo, _ = carry
        mid = lo + (hi - lo) * 0.5
        mid_b = jnp.broadcast_to(mid, (CNT_ROWS, QB))

        def count_tile(j, cnt):
            for blk in _row_blocks(sc_ref[j], CNT_ROWS):
                cnt = cnt + jnp.where(blk >= mid_b, 1.0, 0.0)
            return cnt

        cnt = lax.fori_loop(0, n_kt, count_tile, jnp.zeros((CNT_ROWS, QB), jnp.float32))
        c = jnp.sum(cnt, axis=0, keepdims=True)
        ge = c >= k_top
        lo = jnp.where(ge, mid, lo)
        hi = jnp.where(ge, hi, mid)
        cnt_lo = jnp.where(ge, c, cnt_lo)
        return it + 1, lo, hi, cnt_lo, jnp.max(jnp.where(cnt_lo > k_top, 1.0, 0.0))

    cnt0 = n_adm.astype(jnp.float32)
    _, thr, _, _, _ = lax.while_loop(bisect_cond, bisect,
                                     (0, lo, hi, cnt0, jnp.max(jnp.where(cnt0 > k_top, 1.0, 0.0))))

    def bias_tile(j, carry):
        sc_ref[j] = jnp.where(sc_ref[j] >= thr, 0.0, NEG_BIG)
        return carry

    lax.fori_loop(0, n_kt, bias_tile, 0)

    m_ref[...] = jnp.full(m_ref.shape, NEG_BIG, jnp.float32)
    l_ref[...] = jnp.zeros(l_ref.shape, jnp.float32)
    acc_ref[...] = jnp.zeros(acc_ref.shape, jnp.float32)

    def pair(h):
        return slice((h // 2) * LANES, (h // 2 + 1) * LANES)

    def logits_into(s_ref, mx_ref, j):
        bias = sc_ref[j]
        for h in range(n_heads):
            s = jnp.dot(ka_ref[0, j, :, pair(h)], qmx_ref[h], preferred_element_type=jnp.float32) + bias
            s_ref[h] = s
            mx_ref[h] = jnp.max(s, axis=0, keepdims=True)

    ones_rows = jnp.ones((ONES_ROWS, TM), vat_ref.dtype)

    def absorb(s_ref, mx_ref, j):
        for h in range(n_heads):
            m_old = m_ref[h]
            m_new = jnp.maximum(m_old, mx_ref[h])
            alpha = jnp.exp2(m_old - m_new)
            p = jnp.exp2(s_ref[h] - m_new).astype(vat_ref.dtype)
            pv = jnp.dot(jnp.concatenate([vat_ref[0, j, pair(h), :], ones_rows], axis=0), p,
                         preferred_element_type=jnp.float32)
            acc_ref[h] = acc_ref[h] * alpha + pv[0:LANES]
            l_ref[h] = l_ref[h] * alpha + pv[LANES:LANES + SUBLANES]
            m_ref[h] = m_new

    logits_into(sa_ref, mxa_ref, 0)

    def attn_grp(g, carry):
        logits_into(sb_ref, mxb_ref, 2 * g + 1)
        absorb(sa_ref, mxa_ref, 2 * g)
        logits_into(sa_ref, mxa_ref, jnp.minimum(2 * g + 2, n_kt - 1))
        absorb(sb_ref, mxb_ref, 2 * g + 1)
        return carry

    lax.fori_loop(0, n_grp, attn_grp, 0)

    for hp in range(n_heads // 2):
        even = acc_ref[2 * hp] / l_ref[2 * hp, 0:1, :]
        odd = acc_ref[2 * hp + 1] / l_ref[2 * hp + 1, 0:1, :]
        o_ref[:, hp * LANES:(hp + 1) * LANES] = jnp.where(row_q < HEAD_DIM, even, odd).T.astype(o_ref.dtype)


def _dsa(qat, qit, wit, kiw4, ka4, vat, seq):
    bsz, n_kt, d_a, _ = qat.shape
    n_heads = d_a // HEAD_DIM
    n_qb = seq // QB
    k_top = min(TOPK_MAX, seq // 4)
    qtile = lambda b, i: (b, i, 0, 0)
    per_b = lambda b, i: (b, 0, 0, 0)
    f32 = jnp.float32
    return pl.pallas_call(
        functools.partial(_dsa_kernel, k_top=float(k_top), n_heads=n_heads),
        grid=(bsz, n_qb),
        in_specs=[pl.BlockSpec((1, 1, d_a, QB), qtile),
                  pl.BlockSpec((1, 1, qit.shape[2], QB), qtile),
                  pl.BlockSpec((1, 1, N_IDX_HEADS, QB), qtile),
                  pl.BlockSpec((1, n_kt, TM, LANES), per_b, pipeline_mode=pl.Buffered(1)),
                  pl.BlockSpec((1, n_kt, TM, d_a), per_b, pipeline_mode=pl.Buffered(1)),
                  pl.BlockSpec((1, n_kt, d_a, TM), per_b, pipeline_mode=pl.Buffered(1))],
        out_specs=pl.BlockSpec((QB, d_a), lambda b, i: (b * n_qb + i, 0)),
        out_shape=jax.ShapeDtypeStruct((bsz * seq, d_a), jnp.bfloat16),
        scratch_shapes=[pltpu.VMEM((N_IDX_HEADS, LANES, QB), jnp.bfloat16),
                        pltpu.VMEM((n_heads, LANES, QB), jnp.bfloat16),
                        pltpu.VMEM((n_kt, TM, QB), f32),
                        pltpu.VMEM((n_heads, 1, QB), f32),
                        pltpu.VMEM((n_heads, SUBLANES, QB), f32),
                        pltpu.VMEM((n_heads, LANES, QB), f32),
                        pltpu.VMEM((n_heads, TM, QB), f32),
                        pltpu.VMEM((n_heads, TM, QB), f32),
                        pltpu.VMEM((n_heads, 1, QB), f32),
                        pltpu.VMEM((n_heads, 1, QB), f32)],
        compiler_params=_cparams(("arbitrary", "arbitrary")),
        name="dsa",
    )(qat, qit, wit, kiw4, ka4, vat)


def _ret_kernel(q_ref, k_ref, v_ref, g_ref, dec_ref, zeta_ref, xi_ref, gc_ref, o_ref, st_ref, *, n_heads):
    @pl.when(pl.program_id(1) == 0)
    def _():
        st_ref[...] = jnp.zeros_like(st_ref)

    for h in range(n_heads):
        sl = slice(h * HEAD_DIM, (h + 1) * HEAD_DIM)
        q, k, v = q_ref[:, sl], k_ref[:, sl], v_ref[:, sl]
        s = lax.dot_general(q, k, (((1,), (1,)), ((), ())), preferred_element_type=jnp.float32) * dec_ref[h]
        inner = jnp.dot(s.astype(v.dtype), v, preferred_element_type=jnp.float32)
        state = st_ref[h]
        cross = jnp.dot(q, state.astype(q.dtype), preferred_element_type=jnp.float32) * xi_ref[h]
        y = inner + cross
        mu = jnp.mean(y, axis=-1, keepdims=True)
        yc = y - mu
        yn = yc * lax.rsqrt(jnp.mean(yc * yc, axis=-1, keepdims=True) + EPS)
        g = g_ref[:, sl]
        o_ref[:, sl] = (g / (1.0 + jnp.exp(-g)) * yn).astype(o_ref.dtype)
        kz = (k.astype(jnp.float32) * zeta_ref[h]).astype(k.dtype)
        kv = lax.dot_general(kz, v, (((0,), (0,)), ((), ())), preferred_element_type=jnp.float32)
        st_ref[h] = state * gc_ref[h] + kv


def _ret_consts(n_heads):
    log_gamma = jnp.log1p(-jnp.exp2(-5.0 - jnp.arange(n_heads, dtype=jnp.float32)))
    pos = jnp.arange(RET_C, dtype=jnp.float32)
    diff = pos[:, None] - pos[None, :]
    dec = jnp.where(diff[None] >= 0, jnp.exp(jnp.maximum(diff, 0.0)[None] * log_gamma[:, None, None]), 0.0)
    zeta = jnp.exp((RET_C - 1.0 - pos)[None, :] * log_gamma[:, None])
    xi = jnp.exp((pos + 1.0)[None, :] * log_gamma[:, None])
    gc = jnp.exp(RET_C * log_gamma)
    bc = lambda a, shape: jnp.broadcast_to(a, shape).astype(jnp.float32)
    return (dec, bc(zeta[:, :, None], (n_heads, RET_C, HEAD_DIM)), bc(xi[:, :, None], (n_heads, RET_C, HEAD_DIM)),
            bc(gc[:, None, None], (n_heads, HEAD_DIM, HEAD_DIM)))


def _ret(qb, kb, vb, gb, seq):
    n, d_b = qb.shape
    bsz = n // seq
    n_heads = d_b // HEAD_DIM
    n_c = seq // RET_C
    dec, zeta, xi, gc = _ret_consts(n_heads)
    row = lambda b, c: (b * n_c + c, 0)
    const3 = lambda b, c: (0, 0, 0)
    return pl.pallas_call(
        functools.partial(_ret_kernel, n_heads=n_heads),
        grid=(bsz, n_c),
        in_specs=[pl.BlockSpec((RET_C, d_b), row)] * 4 + [
            pl.BlockSpec(dec.shape, const3), pl.BlockSpec(zeta.shape, const3),
            pl.BlockSpec(xi.shape, const3), pl.BlockSpec(gc.shape, const3)],
        out_specs=pl.BlockSpec((RET_C, d_b), row),
        out_shape=jax.ShapeDtypeStruct((n, d_b), jnp.bfloat16),
        scratch_shapes=[pltpu.VMEM((n_heads, HEAD_DIM, HEAD_DIM), jnp.float32)],
        compiler_params=_cparams(("arbitrary", "arbitrary")),
        name="ret",
    )(qb, kb, vb, gb, dec, zeta, xi, gc)


def _lane_first_eq(x, m, lane):
    return jnp.min(jnp.where(x == m, lane, float(LANES)), axis=1, keepdims=True)


def _mix_out_kernel(x_ref, ya_ref, yb_ref, woa_ref, wob_ref, gt_ref, sc_ref, sh_ref, g_ref, wr_ref, tri_ref,
                    x1_ref, h2_ref, route_ref, cnt_ref, carry_ref):
    @pl.when(pl.program_id(0) == 0)
    def _():
        carry_ref[...] = jnp.zeros_like(carry_ref)

    mix = (jnp.dot(ya_ref[...], woa_ref[...], preferred_element_type=jnp.float32)
           + jnp.dot(yb_ref[...], wob_ref[...], preferred_element_type=jnp.float32))
    x1 = x_ref[...] + gt_ref[0] * mix
    x1_ref[...] = x1
    h2 = _rmsnorm_mod(x1, g_ref[...], sc_ref[0], sh_ref[0])
    h2_ref[...] = h2

    lg = jnp.dot(h2.astype(jnp.bfloat16), wr_ref[...], preferred_element_type=jnp.float32)
    lane = lax.broadcasted_iota(jnp.int32, lg.shape, 1).astype(jnp.float32)
    is_grp = (lane >= N_EXPERTS) & (lane < N_EXPERTS + N_GROUPS)
    gl = jnp.where(is_grp, lg, -jnp.inf)
    gmax = jnp.max(gl, axis=1, keepdims=True)
    grp = _lane_first_eq(gl, gmax, lane) - N_EXPERTS
    p_grp = 1.0 / jnp.sum(jnp.exp(gl - gmax), axis=1, keepdims=True)
    in_grp = jnp.floor(lane * (1.0 / EXPERTS_PER_GROUP)) == grp
    f = jnp.where(in_grp & (lane < N_EXPERTS), lg, -jnp.inf)
    f1 = jnp.max(f, axis=1, keepdims=True)
    e1 = _lane_first_eq(f, f1, lane)
    f = jnp.where(lane == e1, -jnp.inf, f)
    f2 = jnp.max(f, axis=1, keepdims=True)
    e2 = _lane_first_eq(f, f2, lane)
    a2 = jnp.exp(f2 - f1)
    w1 = p_grp / (1.0 + a2)
    w2 = p_grp * a2 / (1.0 + a2)

    oh1 = jnp.where(lane == e1, 1.0, 0.0)
    oh2 = jnp.where(lane == e2, 1.0, 0.0)
    both = oh1 + oh2
    before = jnp.dot(tri_ref[...], both.astype(jnp.bfloat16), preferred_element_type=jnp.float32) + carry_ref[...]
    r1 = jnp.sum(before * oh1, axis=1, keepdims=True)
    r2 = jnp.sum(before * oh2, axis=1, keepdims=True)
    carry = carry_ref[...] + jnp.sum(both, axis=0, keepdims=True)
    carry_ref[...] = carry
    cnt_ref[...] = carry

    out = jnp.zeros(lg.shape, jnp.float32)
    for col, val in enumerate((e1, e2, w1, w2, r1, r2)):
        out = jnp.where(lane == col, val, out)
    route_ref[...] = out


def _mix_out(x2, ya, yb, wo_a, wo_b, mod3, g_ffn, w_route, seq):
    n, d = x2.shape
    tiles_per_seq = seq // TM
    tri = jnp.asarray(np.tril(np.ones((TM, TM), np.float32), -1), jnp.bfloat16)
    row = lambda i: (i, 0)
    const = lambda i: (0, 0)
    modk = lambda k: pl.BlockSpec((1, 1, d), lambda i: ((i // tiles_per_seq) * 6 + k, 0, 0))
    return pl.pallas_call(
        _mix_out_kernel,
        grid=(n // TM,),
        in_specs=[pl.BlockSpec((TM, d), row),
                  pl.BlockSpec((TM, ya.shape[1]), row),
                  pl.BlockSpec((TM, yb.shape[1]), row),
                  pl.BlockSpec(wo_a.shape, const),
                  pl.BlockSpec(wo_b.shape, const),
                  modk(2), modk(4), modk(3),
                  pl.BlockSpec((1, d), const),
                  pl.BlockSpec(w_route.shape, const),
                  pl.BlockSpec(tri.shape, const)],
        out_specs=(pl.BlockSpec((TM, d), row), pl.BlockSpec((TM, d), row),
                   pl.BlockSpec((TM, LANES), row), pl.BlockSpec((1, LANES), const)),
        out_shape=(jax.ShapeDtypeStruct((n, d), jnp.float32), jax.ShapeDtypeStruct((n, d), jnp.float32),
                   jax.ShapeDtypeStruct((n, LANES), jnp.float32), jax.ShapeDtypeStruct((1, LANES), jnp.float32)),
        scratch_shapes=[pltpu.VMEM((1, LANES), jnp.float32)],
        compiler_params=_cparams(("arbitrary",)),
        name="mix_out",
    )(x2, ya, yb, wo_a, wo_b, mod3, mod3, mod3, g_ffn, w_route, tri)


def _plan_kernel(route_ref, pst_ref, dest_ref):
    r = route_ref[...]
    lane = lax.broadcasted_iota(jnp.int32, r.shape, 1).astype(jnp.float32)
    pst = pst_ref[...]
    d1 = jnp.sum(jnp.where(lane == r[:, 0:1], pst, 0.0), axis=1, keepdims=True) + r[:, 4:5]
    d2 = jnp.sum(jnp.where(lane == r[:, 1:2], pst, 0.0), axis=1, keepdims=True) + r[:, 5:6]
    both = jnp.where(lane == 0.0, d1, jnp.where(lane == 1.0, d2, 0.0))
    dest_ref[0] = both.T[0:8, :].astype(jnp.int32)


def _plan(route, pst_row):
    n = route.shape[0]
    return pl.pallas_call(
        _plan_kernel,
        grid=(n // TM,),
        in_specs=[pl.BlockSpec((TM, LANES), lambda i: (i, 0)), pl.BlockSpec((1, LANES), lambda i: (0, 0))],
        out_specs=pl.BlockSpec((1, 8, TM), lambda i: (i, 0, 0)),
        out_shape=jax.ShapeDtypeStruct((n // TM, 8, TM), jnp.int32),
        compiler_params=_cparams(("arbitrary",)),
        name="plan",
    )(route, pst_row)


def _dispatch_kernel(zrow_ref, n_used_ref, dest_ref, h2_ref, xs_hbm, zbuf, sem, zsem, *, n_blk):
    i = pl.program_id(0)

    @pl.when(i == 0)
    def _():
        zbuf[...] = jnp.zeros(zbuf.shape, zbuf.dtype)
        for e in range(N_EXPERTS):
            pltpu.make_async_copy(zbuf, xs_hbm.at[pl.ds(pl.multiple_of(zrow_ref[e], SUBLANES), BM)], zsem).start()
        for e in range(N_EXPERTS):
            pltpu.make_async_copy(zbuf, xs_hbm.at[pl.ds(0, BM)], zsem).wait()
        for b in range(N_EXPERTS + 1):
            @pl.when(n_used_ref[0] + b <= n_blk)
            def _():
                tail = pltpu.make_async_copy(zbuf, xs_hbm.at[pl.ds((n_used_ref[0] + b) * BM, BM)], zsem)
                tail.start()
                tail.wait()

    def issue(r, c):
        row = h2_ref.at[pl.ds(r, 1)]
        pltpu.make_async_copy(row, xs_hbm.at[pl.ds(dest_ref[0, 0, r], 1)], sem).start()
        pltpu.make_async_copy(row, xs_hbm.at[pl.ds(dest_ref[0, 1, r], 1)], sem).start()
        return c

    lax.fori_loop(0, TM, issue, 0)
    for _ in range(2):
        pltpu.make_async_copy(h2_ref, xs_hbm.at[pl.ds(0, TM)], sem).wait()


def _dispatch(h2, dest, zrow, n_used, n_blk):
    n, d = h2.shape
    n_rows = (n_blk + 1) * BM
    grid_spec = pltpu.PrefetchScalarGridSpec(
        num_scalar_prefetch=2,
        grid=(n // TM,),
        in_specs=[pl.BlockSpec((1, 8, TM), lambda i, z, nu: (i, 0, 0), memory_space=pltpu.SMEM),
                  pl.BlockSpec((TM, d), lambda i, z, nu: (i, 0))],
        out_specs=pl.BlockSpec(memory_space=pl.ANY),
        scratch_shapes=[pltpu.VMEM((BM, d), h2.dtype),
                        pltpu.SemaphoreType.DMA(()),
                        pltpu.SemaphoreType.DMA(())],
    )
    return pl.pallas_call(
        functools.partial(_dispatch_kernel, n_blk=n_blk),
        grid_spec=grid_spec,
        out_shape=jax.ShapeDtypeStruct((n_rows, d), h2.dtype),
        compiler_params=_cparams(("arbitrary",)),
        name="dispatch",
    )(zrow, n_used, dest, h2)


def _experts_kernel(blk_e_ref, n_used_ref, x_ref, wg_ref, wu_ref, wd_ref, y_ref):
    j = pl.program_id(0)

    @pl.when(j < n_used_ref[0])
    def _():
        x = x_ref[...].astype(wg_ref.dtype)
        a = jnp.dot(x, wg_ref[0], preferred_element_type=jnp.float32)
        b = jnp.dot(x, wu_ref[0], preferred_element_type=jnp.float32)
        hmid = (a / (1.0 + jnp.exp(-a)) * b).astype(x.dtype)
        y_ref[...] = jnp.dot(hmid, wd_ref[0], preferred_element_type=jnp.float32)

    @pl.when(j >= n_used_ref[0])
    def _():
        y_ref[...] = jnp.zeros(y_ref.shape, y_ref.dtype)


def _experts(xs, wg, wu, wd, blk_e, n_used):
    d = xs.shape[1]
    n_blk = blk_e.shape[0]
    d_e = wg.shape[2]
    grid_spec = pltpu.PrefetchScalarGridSpec(
        num_scalar_prefetch=2,
        grid=(n_blk,),
        in_specs=[pl.BlockSpec((BM, d), lambda j, be, nu: (jnp.minimum(j, nu[0] - 1), 0)),
                  pl.BlockSpec((1, d, d_e), lambda j, be, nu: (be[j], 0, 0)),
                  pl.BlockSpec((1, d, d_e), lambda j, be, nu: (be[j], 0, 0)),
                  pl.BlockSpec((1, d_e, d), lambda j, be, nu: (be[j], 0, 0))],
        out_specs=pl.BlockSpec((BM, d), lambda j, be, nu: (j, 0)),
    )
    return pl.pallas_call(
        _experts_kernel,
        grid_spec=grid_spec,
        out_shape=jax.ShapeDtypeStruct((n_blk * BM, d), jnp.float32),
        compiler_params=_cparams(("arbitrary",)),
        name="experts",
    )(blk_e, n_used, xs, wg, wu, wd)


def _final_kernel(dest_ref, dest_next_ref, x1_ref, route_ref, gt_ref, g_ref, ys_hbm, o_ref, ybuf, sem):
    i = pl.program_id(0)
    slot = i % 2

    def gather(d_ref, s):
        def issue(r, c):
            pltpu.make_async_copy(ys_hbm.at[pl.ds(d_ref[0, 0, r], 1)], ybuf.at[s, pl.ds(r, 1)], sem.at[s]).start()
            pltpu.make_async_copy(ys_hbm.at[pl.ds(d_ref[0, 1, r], 1)], ybuf.at[s, pl.ds(TM + r, 1)],
                                  sem.at[s]).start()
            return c
        lax.fori_loop(0, TM, issue, 0)

    @pl.when(i == 0)
    def _():
        gather(dest_ref, 0)

    @pl.when(i + 1 < pl.num_programs(0))
    def _():
        gather(dest_next_ref, 1 - slot)

    pltpu.make_async_copy(ys_hbm.at[pl.ds(0, 2 * TM)], ybuf.at[slot], sem.at[slot]).wait()
    w1 = route_ref[:, 2:3]
    w2 = route_ref[:, 3:4]
    x2 = x1_ref[...] + gt_ref[0] * (w1 * ybuf[slot, 0:TM, :] + w2 * ybuf[slot, TM:2 * TM, :])
    o_ref[...] = x2 * lax.rsqrt(jnp.mean(x2 * x2, axis=-1, keepdims=True) + EPS) * g_ref[...]


def _final(x1, ys, dest, route, mod3, g_final, seq):
    n, d = x1.shape
    n_tiles = n // TM
    tiles_per_seq = seq // TM
    row = lambda i: (i, 0)
    return pl.pallas_call(
        _final_kernel,
        grid=(n_tiles,),
        in_specs=[pl.BlockSpec((1, 8, TM), lambda i: (i, 0, 0), memory_space=pltpu.SMEM),
                  pl.BlockSpec((1, 8, TM), lambda i: (jnp.minimum(i + 1, n_tiles - 1), 0, 0),
                               memory_space=pltpu.SMEM),
                  pl.BlockSpec((TM, d), row),
                  pl.BlockSpec((TM, LANES), row),
                  pl.BlockSpec((1, 1, d), lambda i: ((i // tiles_per_seq) * 6 + 5, 0, 0)),
                  pl.BlockSpec((1, d), lambda i: (0, 0)),
                  pl.BlockSpec(memory_space=pl.ANY)],
        out_specs=pl.BlockSpec((TM, d), row),
        out_shape=jax.ShapeDtypeStruct((n, d), jnp.float32),
        scratch_shapes=[pltpu.VMEM((2, 2 * TM, d), jnp.float32), pltpu.SemaphoreType.DMA((2,))],
        compiler_params=_cparams(("arbitrary",)),
        name="final",
    )(dest, dest, x1, route, mod3, g_final, ys)


def _block_layout(counts, n):
    cnt = counts[0, :N_EXPERTS].astype(jnp.int32)
    blocks = (cnt + BM - 1) // BM
    bends = jnp.cumsum(blocks)
    pstarts = (bends - blocks) * BM
    n_blk = (2 * n) // BM + N_EXPERTS
    pst_row = jnp.zeros((1, LANES), jnp.float32).at[0, :N_EXPERTS].set(pstarts.astype(jnp.float32))
    zrow = ((pstarts + cnt) // SUBLANES * SUBLANES).astype(jnp.int32)
    n_used = bends[-1:].astype(jnp.int32)
    blk_e = jnp.sum(bends[None, :] <= jnp.arange(n_blk, dtype=jnp.int32)[:, None], axis=1)
    blk_e = jnp.minimum(blk_e, N_EXPERTS - 1).astype(jnp.int32)
    return pst_row, zrow, n_used, blk_e


def _layer(x2, c, pos2, w_ada, b_ada, g_mix, w_in, w_o, g_ffn, w_rg, w_re, w_up, w_gate, w_down, seq):
    n, d = x2.shape
    bsz = n // seq
    bf = jnp.bfloat16
    d_a = d // 2
    d_i = N_IDX_HEADS * IDX_DIM
    mod3 = _ada(c, w_ada, b_ada).reshape(bsz * 6, 1, d)

    c0 = 3 * d_a + d_i
    c1 = c0 + IDX_DIM + N_IDX_HEADS
    w_cat = jnp.concatenate([w_in[:, :c0], w_in[:, c0:c1],
                             jnp.zeros((d, LANES - (c1 - c0)), w_in.dtype), w_in[:, c1:]], axis=1).astype(bf)
    qat, ka, vat, qit, kiw, wit, qb, kb, vb, gb = _proj(x2, pos2, mod3, g_mix.reshape(1, d), w_cat, seq)

    n_kt = seq // TM
    ya = _dsa(qat, qit, wit, kiw.reshape(bsz, n_kt, TM, LANES), ka.reshape(bsz, n_kt, TM, d_a), vat, seq)
    yb = _ret(qb, kb, vb, gb, seq)

    w_route = jnp.concatenate([jnp.transpose(w_re, (1, 0, 2)).reshape(d, N_EXPERTS), w_rg,
                               jnp.zeros((d, LANES - N_EXPERTS - N_GROUPS), w_rg.dtype)], axis=1).astype(bf)
    x1, h2, route, counts = _mix_out(x2, ya, yb, w_o[:d_a].astype(bf), w_o[d_a:].astype(bf), mod3,
                                     g_ffn.reshape(1, d), w_route, seq)

    pst_row, zrow, n_used, blk_e = _block_layout(counts, n)
    dest = _plan(route, pst_row)
    xs = _dispatch(h2, dest, zrow, n_used, blk_e.shape[0])
    ys = _experts(xs, w_gate.astype(bf), w_up.astype(bf), w_down.astype(bf), blk_e, n_used)
    return x1, ys, dest, route, mod3


def kernel(x, c, positions, w_ada, b_ada, g_norm_mix, w_in, w_o, g_norm_ffn, w_router_group, w_router_expert,
           w_up, w_gate, w_down, g_norm_final):
    bsz, seq, d = x.shape
    depth = w_ada.shape[0]
    assert depth == 1, "the final norm is fused into the last layer's combine kernel"
    assert seq % (2 * TM) == 0 and seq % RET_C == 0 and (2 * bsz * seq) % BM == 0
    x2 = x.reshape(bsz * seq, d)
    pos2 = positions.astype(jnp.float32).reshape(bsz * seq, 1)
    x1, ys, dest, route, mod3 = _layer(x2, c, pos2, w_ada[0], b_ada[0], g_norm_mix[0], w_in[0], w_o[0],
                                       g_norm_ffn[0], w_router_group[0], w_router_expert[0], w_up[0], w_gate[0],
                                       w_down[0], seq)
    out = _final(x1, ys, dest, route, mod3, g_norm_final.reshape(1, d), seq)
    return out.reshape(bsz, seq, d)
```

```python
import functools

import jax
import jax.numpy as jnp
import numpy as np
from jax import lax
from jax.experimental import pallas as pl
from jax.experimental.pallas import tpu as pltpu

CHUNK = 64
HEAD_DIM = 64
N_IDX_HEADS = 16
IDX_DIM = 64
TOPK_MAX = 256
ROPE_THETA = 500000.0
ROT_DIM = HEAD_DIM // 4
RET_THETA = 10000.0
N_GROUPS = 4
EXPERTS_PER_GROUP = 8
N_EXPERTS = N_GROUPS * EXPERTS_PER_GROUP
EPS = 1e-6

LANES = 128
SUBLANES = 8
VMEM_LIMIT = 56 * 1024 * 1024

TM = 256
QB = TM
RET_C = 256
BM = 256
CNT_ROWS = 64
ONES_ROWS = 16
MAX_BISECT = 40

NEG_BIG = -1e30
LOG2E = 1.4426950408889634


def _cparams(sem):
    return pltpu.CompilerParams(dimension_semantics=sem, vmem_limit_bytes=VMEM_LIMIT)


def _ada_kernel(c_ref, w_ref, b_ref, o_ref):
    o_ref[...] = jnp.dot(c_ref[...], w_ref[...], preferred_element_type=jnp.float32) + b_ref[...]


def _ada(c, w_ada, b_ada):
    bsz, d = c.shape
    n_out = w_ada.shape[1]
    return pl.pallas_call(
        _ada_kernel,
        grid=(n_out // d,),
        in_specs=[pl.BlockSpec((bsz, d), lambda j: (0, 0)),
                  pl.BlockSpec((d, d), lambda j: (0, j)),
                  pl.BlockSpec((1, d), lambda j: (0, j))],
        out_specs=pl.BlockSpec((bsz, d), lambda j: (0, j)),
        out_shape=jax.ShapeDtypeStruct((bsz, n_out), jnp.float32),
        compiler_params=_cparams(("arbitrary",)),
        name="ada",
    )(c, w_ada, b_ada.reshape(1, n_out))


def _rmsnorm_mod(x, g, sc, sh):
    xn = x * lax.rsqrt(jnp.mean(x * x, axis=-1, keepdims=True) + EPS)
    return xn * g * (1.0 + sc) + sh


def _rope_lanes(x, cos, sin_lo, sin_hi, half):
    cols = []
    for k in range(x.shape[1] // LANES):
        xb = x[:, k * LANES:(k + 1) * LANES]
        cols.append(xb * cos + pltpu.roll(xb, LANES - half, 1) * sin_lo + pltpu.roll(xb, half, 1) * sin_hi)
    return cols[0] if len(cols) == 1 else jnp.concatenate(cols, axis=1)


def _proj_kernel(x_ref, pos_ref, sc_ref, sh_ref, g_ref, w_ref, tab_ref,
                 qat_ref, ka_ref, vat_ref, qit_ref, kiw_ref, wit_ref, qb_ref, kb_ref, vb_ref, gb_ref,
                 *, d_a, d_i, d_b):
    h = _rmsnorm_mod(x_ref[...], g_ref[...], sc_ref[0], sh_ref[0]).astype(jnp.bfloat16)
    pos = pos_ref[...]
    ang_a = pos * tab_ref[0:1, :]
    cos_a, sin_a = jnp.cos(ang_a), jnp.sin(ang_a)
    sa_lo, sa_hi = sin_a * tab_ref[1:2, :], sin_a * tab_ref[2:3, :]
    ang_b = pos * tab_ref[3:4, :]
    cos_b, sin_b = jnp.cos(ang_b), jnp.sin(ang_b)
    sb_lo, sb_hi = sin_b * tab_ref[4:5, :], sin_b * tab_ref[5:6, :]
    half_a, half_b = ROT_DIM // 2, HEAD_DIM // 2

    def seg(lo, width):
        return jnp.dot(h, w_ref[:, lo:lo + width], preferred_element_type=jnp.float32)

    o = 0
    qa = seg(o, d_a); o += d_a
    qat_ref[0, 0] = (_rope_lanes(qa, cos_a, sa_lo, sa_hi, half_a) * (HEAD_DIM ** -0.5 * LOG2E)).T.astype(qat_ref.dtype)
    ka = seg(o, d_a); o += d_a
    ka_ref[...] = _rope_lanes(ka, cos_a, sa_lo, sa_hi, half_a).astype(ka_ref.dtype)
    vat_ref[0, 0] = seg(o, d_a).T.astype(vat_ref.dtype); o += d_a
    qi = seg(o, d_i); o += d_i
    qit_ref[0, 0] = (_rope_lanes(qi, cos_a, sa_lo, sa_hi, half_a) * (IDX_DIM ** -0.5)).T.astype(qit_ref.dtype)
    kw = seg(o, LANES); o += LANES
    kiw_ref[...] = _rope_lanes(kw, cos_a, sa_lo, sa_hi, half_a).astype(kiw_ref.dtype)
    wit_ref[0, 0] = kw.T[IDX_DIM:IDX_DIM + N_IDX_HEADS, :] * (N_IDX_HEADS ** -0.5)
    qb = seg(o, d_b); o += d_b
    qb_ref[...] = _rope_lanes(qb, cos_b, sb_lo, sb_hi, half_b).astype(qb_ref.dtype)
    kb = seg(o, d_b); o += d_b
    kb_ref[...] = (_rope_lanes(kb, cos_b, sb_lo, sb_hi, half_b) * (HEAD_DIM ** -0.5)).astype(kb_ref.dtype)
    vb_ref[...] = seg(o, d_b).astype(vb_ref.dtype); o += d_b
    gb_ref[...] = seg(o, d_b)


def _rope_tables():
    lane = jnp.arange(LANES) % HEAD_DIM
    rows = []
    for rot, theta in ((ROT_DIM, ROPE_THETA), (HEAD_DIM, RET_THETA)):
        half = rot // 2
        inv_freq = theta ** (-jnp.arange(half, dtype=jnp.float32) / half)
        rows.append(jnp.where(lane < rot, inv_freq[lane % half], 0.0))
        rows.append(jnp.where(lane < half, -1.0, 0.0))
        rows.append(jnp.where((lane >= half) & (lane < rot), 1.0, 0.0))
    rows += [jnp.zeros((LANES,), jnp.float32)] * 2
    return jnp.stack(rows).astype(jnp.float32)


def _proj(x2, pos2, mod3, g_mix, w_cat, seq):
    n, d = x2.shape
    bsz = n // seq
    d_a = d // 2
    d_b = d // 2
    d_i = N_IDX_HEADS * IDX_DIM
    tiles_per_seq = seq // TM
    tab = _rope_tables()
    row = lambda i: (i, 0)
    const = lambda i: (0, 0)
    tile4 = lambda i: (i // tiles_per_seq, i % tiles_per_seq, 0, 0)
    bf = jnp.bfloat16
    out_shape = (
        jax.ShapeDtypeStruct((bsz, tiles_per_seq, d_a, TM), bf),
        jax.ShapeDtypeStruct((n, d_a), bf),
        jax.ShapeDtypeStruct((bsz, tiles_per_seq, d_a, TM), bf),
        jax.ShapeDtypeStruct((bsz, tiles_per_seq, d_i, TM), bf),
        jax.ShapeDtypeStruct((n, LANES), bf),
        jax.ShapeDtypeStruct((bsz, tiles_per_seq, N_IDX_HEADS, TM), jnp.float32),
        jax.ShapeDtypeStruct((n, d_b), bf),
        jax.ShapeDtypeStruct((n, d_b), bf),
        jax.ShapeDtypeStruct((n, d_b), bf),
        jax.ShapeDtypeStruct((n, d_b), jnp.float32),
    )
    out_specs = (
        pl.BlockSpec((1, 1, d_a, TM), tile4),
        pl.BlockSpec((TM, d_a), row),
        pl.BlockSpec((1, 1, d_a, TM), tile4),
        pl.BlockSpec((1, 1, d_i, TM), tile4),
        pl.BlockSpec((TM, LANES), row),
        pl.BlockSpec((1, 1, N_IDX_HEADS, TM), tile4),
        pl.BlockSpec((TM, d_b), row),
        pl.BlockSpec((TM, d_b), row),
        pl.BlockSpec((TM, d_b), row),
        pl.BlockSpec((TM, d_b), row),
    )
    return pl.pallas_call(
        functools.partial(_proj_kernel, d_a=d_a, d_i=d_i, d_b=d_b),
        grid=(n // TM,),
        in_specs=[pl.BlockSpec((TM, d), row),
                  pl.BlockSpec((TM, 1), row),
                  pl.BlockSpec((1, 1, d), lambda i: ((i // tiles_per_seq) * 6 + 1, 0, 0)),
                  pl.BlockSpec((1, 1, d), lambda i: ((i // tiles_per_seq) * 6 + 0, 0, 0)),
                  pl.BlockSpec((1, d), const),
                  pl.BlockSpec(w_cat.shape, const),
                  pl.BlockSpec(tab.shape, const)],
        out_specs=out_specs,
        out_shape=out_shape,
        compiler_params=_cparams(("arbitrary",)),
        name="proj",
    )(x2, pos2, mod3, mod3, g_mix, w_cat, tab)


def _row_blocks(x, rows):
    return [x[r * rows:(r + 1) * rows] for r in range(x.shape[0] // rows)]


def _dsa_kernel(qat_ref, qit_ref, wit_ref, kiw_ref, ka_ref, vat_ref, o_ref,
                qix_ref, qmx_ref, sc_ref, m_ref, l_ref, acc_ref, sa_ref, sb_ref, mxa_ref, mxb_ref,
                *, k_top, n_heads):
    i = pl.program_id(1)
    n_grp = (i * QB + QB + 2 * TM - 1) // (2 * TM)
    n_kt = 2 * n_grp
    n_real = (i * QB + QB + TM - 1) // TM

    zero_rows = jnp.zeros((LANES - IDX_DIM, QB), qix_ref.dtype)
    for h in range(N_IDX_HEADS):
        qix_ref[h] = jnp.concatenate([qit_ref[0, 0, h * IDX_DIM:(h + 1) * IDX_DIM, :], zero_rows], axis=0)
    row_q = lax.broadcasted_iota(jnp.int32, (LANES, QB), 0)
    for h in range(n_heads):
        pair = qat_ref[0, 0, (h // 2) * LANES:(h // 2 + 1) * LANES, :]
        own = (row_q < HEAD_DIM) if h % 2 == 0 else (row_q >= HEAD_DIM)
        qmx_ref[h] = jnp.where(own, pair, jnp.zeros_like(pair))

    q_chunk = (i * QB + lax.broadcasted_iota(jnp.int32, (TM, QB), 1)) // CHUNK
    key_in_tile = lax.broadcasted_iota(jnp.int32, (TM, QB), 0)
    w_all = wit_ref[0, 0]

    def idx_pair(g, carry):
        lo, hi = carry
        for j in (2 * g, 2 * g + 1):
            kt = kiw_ref[0, j]
            acc = None
            for h in range(N_IDX_HEADS):
                d = jnp.dot(kt, qix_ref[h], preferred_element_type=jnp.float32)
                t = w_all[h:h + 1, :] * jnp.maximum(d, 0.0)
                acc = t if acc is None else acc + t
            adm = (j * TM + key_in_tile) // CHUNK <= q_chunk
            sc_ref[j] = jnp.where(adm, acc, -jnp.inf)
            lo = jnp.minimum(lo, functools.reduce(jnp.minimum, _row_blocks(jnp.where(adm, acc, jnp.inf), SUBLANES)))
            hi = jnp.maximum(hi, functools.reduce(jnp.maximum, _row_blocks(jnp.where(adm, acc, -jnp.inf), SUBLANES)))
        return lo, hi

    lo, hi = lax.fori_loop(0, n_grp, idx_pair, (jnp.full((SUBLANES, QB), jnp.inf, jnp.float32),
                                                jnp.full((SUBLANES, QB), -jnp.inf, jnp.float32)))
    lo = jnp.min(lo, axis=0, keepdims=True)
    hi = jnp.max(hi, axis=0, keepdims=True)
    n_adm = ((i * QB + lax.broadcasted_iota(jnp.int32, (1, QB), 1)) // CHUNK + 1) * CHUNK

    def bisect_cond(carry):
        it, _, _, _, unsettled = carry
        return (it < MAX_BISECT) & (unsettled > 0.0)

    def bisect(carry):
        it, lo, hi, cnt_lo, _ = carry
        mid = lo + (hi - lo) * 0.5
        mid_b = jnp.broadcast_to(mid, (CNT_ROWS, QB))

        def count_tile(j, cnt):
            for blk in _row_blocks(sc_ref[j], CNT_ROWS):
                cnt = cnt + jnp.where(blk >= mid_b, 1.0, 0.0)
            return cnt

        cnt = lax.fori_loop(0, n_real, count_tile, jnp.zeros((CNT_ROWS, QB), jnp.float32))
        c = jnp.sum(cnt, axis=0, keepdims=True)
        ge = c >= k_top
        lo = jnp.where(ge, mid, lo)
        hi = jnp.where(ge, hi, mid)
        cnt_lo = jnp.where(ge, c, cnt_lo)
        return it + 1, lo, hi, cnt_lo, jnp.max(jnp.where(cnt_lo > k_top, 1.0, 0.0))

    cnt0 = n_adm.astype(jnp.float32)
    _, thr, _, _, _ = lax.while_loop(bisect_cond, bisect,
                                     (0, lo, hi, cnt0, jnp.max(jnp.where(cnt0 > k_top, 1.0, 0.0))))

    def bias_tile(j, carry):
        sc_ref[j] = jnp.where(sc_ref[j] >= thr, 0.0, NEG_BIG)
        return carry

    lax.fori_loop(0, n_kt, bias_tile, 0)

    m_ref[...] = jnp.full(m_ref.shape, NEG_BIG, jnp.float32)
    l_ref[...] = jnp.zeros(l_ref.shape, jnp.float32)
    acc_ref[...] = jnp.zeros(acc_ref.shape, jnp.float32)

    def pair(h):
        return slice((h // 2) * LANES, (h // 2 + 1) * LANES)

    def logits_into(s_ref, mx_ref, j):
        bias = sc_ref[j]
        for h in range(n_heads):
            s = jnp.dot(ka_ref[0, j, :, pair(h)], qmx_ref[h], preferred_element_type=jnp.float32) + bias
            s_ref[h] = s
            mx_ref[h] = jnp.max(s, axis=0, keepdims=True)

    ones_rows = jnp.ones((ONES_ROWS, TM), vat_ref.dtype)

    def absorb(s_ref, mx_ref, j):
        for h in range(n_heads):
            m_old = m_ref[h]
            m_new = jnp.maximum(m_old, mx_ref[h])
            alpha = jnp.exp2(m_old - m_new)
            p = jnp.exp2(s_ref[h] - m_new).astype(vat_ref.dtype)
            pv = jnp.dot(jnp.concatenate([vat_ref[0, j, pair(h), :], ones_rows], axis=0), p,
                         preferred_element_type=jnp.float32)
            acc_ref[h] = acc_ref[h] * alpha + pv[0:LANES]
            l_ref[h] = l_ref[h] * alpha + pv[LANES:LANES + SUBLANES]
            m_ref[h] = m_new

    logits_into(sa_ref, mxa_ref, 0)

    def attn_grp(g, carry):
        logits_into(sb_ref, mxb_ref, 2 * g + 1)
        absorb(sa_ref, mxa_ref, 2 * g)
        logits_into(sa_ref, mxa_ref, jnp.minimum(2 * g + 2, n_kt - 1))
        absorb(sb_ref, mxb_ref, 2 * g + 1)
        return carry

    lax.fori_loop(0, n_grp, attn_grp, 0)

    for hp in range(n_heads // 2):
        even = acc_ref[2 * hp] / l_ref[2 * hp, 0:1, :]
        odd = acc_ref[2 * hp + 1] / l_ref[2 * hp + 1, 0:1, :]
        o_ref[:, hp * LANES:(hp + 1) * LANES] = jnp.where(row_q < HEAD_DIM, even, odd).T.astype(o_ref.dtype)


def _dsa(qat, qit, wit, kiw4, ka4, vat, seq):
    bsz, n_kt, d_a, _ = qat.shape
    n_heads = d_a // HEAD_DIM
    n_qb = seq // QB
    k_top = min(TOPK_MAX, seq // 4)
    qtile = lambda b, i: (b, i, 0, 0)
    per_b = lambda b, i: (b, 0, 0, 0)
    f32 = jnp.float32
    return pl.pallas_call(
        functools.partial(_dsa_kernel, k_top=float(k_top), n_heads=n_heads),
        grid=(bsz, n_qb),
        in_specs=[pl.BlockSpec((1, 1, d_a, QB), qtile),
                  pl.BlockSpec((1, 1, qit.shape[2], QB), qtile),
                  pl.BlockSpec((1, 1, N_IDX_HEADS, QB), qtile),
                  pl.BlockSpec((1, n_kt, TM, LANES), per_b, pipeline_mode=pl.Buffered(1)),
                  pl.BlockSpec((1, n_kt, TM, d_a), per_b, pipeline_mode=pl.Buffered(1)),
                  pl.BlockSpec((1, n_kt, d_a, TM), per_b, pipeline_mode=pl.Buffered(1))],
        out_specs=pl.BlockSpec((QB, d_a), lambda b, i: (b * n_qb + i, 0)),
        out_shape=jax.ShapeDtypeStruct((bsz * seq, d_a), jnp.bfloat16),
        scratch_shapes=[pltpu.VMEM((N_IDX_HEADS, LANES, QB), jnp.bfloat16),
                        pltpu.VMEM((n_heads, LANES, QB), jnp.bfloat16),
                        pltpu.VMEM((n_kt, TM, QB), f32),
                        pltpu.VMEM((n_heads, 1, QB), f32),
                        pltpu.VMEM((n_heads, SUBLANES, QB), f32),
                        pltpu.VMEM((n_heads, LANES, QB), f32),
                        pltpu.VMEM((n_heads, TM, QB), f32),
                        pltpu.VMEM((n_heads, TM, QB), f32),
                        pltpu.VMEM((n_heads, 1, QB), f32),
                        pltpu.VMEM((n_heads, 1, QB), f32)],
        compiler_params=_cparams(("arbitrary", "arbitrary")),
        name="dsa",
    )(qat, qit, wit, kiw4, ka4, vat)


def _ret_kernel(q_ref, k_ref, v_ref, g_ref, dec_ref, zeta_ref, xi_ref, gc_ref, o_ref, st_ref, *, n_heads):
    @pl.when(pl.program_id(1) == 0)
    def _():
        st_ref[...] = jnp.zeros_like(st_ref)

    for h in range(n_heads):
        sl = slice(h * HEAD_DIM, (h + 1) * HEAD_DIM)
        q, k, v = q_ref[:, sl], k_ref[:, sl], v_ref[:, sl]
        s = lax.dot_general(q, k, (((1,), (1,)), ((), ())), preferred_element_type=jnp.float32) * dec_ref[h]
        inner = jnp.dot(s.astype(v.dtype), v, preferred_element_type=jnp.float32)
        state = st_ref[h]
        cross = jnp.dot(q, state.astype(q.dtype), preferred_element_type=jnp.float32) * xi_ref[h]
        y = inner + cross
        mu = jnp.mean(y, axis=-1, keepdims=True)
        yc = y - mu
        yn = yc * lax.rsqrt(jnp.mean(yc * yc, axis=-1, keepdims=True) + EPS)
        g = g_ref[:, sl]
        o_ref[:, sl] = (g / (1.0 + jnp.exp(-g)) * yn).astype(o_ref.dtype)
        kz = (k.astype(jnp.float32) * zeta_ref[h]).astype(k.dtype)
        kv = lax.dot_general(kz, v, (((0,), (0,)), ((), ())), preferred_element_type=jnp.float32)
        st_ref[h] = state * gc_ref[h] + kv


def _ret_consts(n_heads):
    log_gamma = jnp.log1p(-jnp.exp2(-5.0 - jnp.arange(n_heads, dtype=jnp.float32)))
    pos = jnp.arange(RET_C, dtype=jnp.float32)
    diff = pos[:, None] - pos[None, :]
    dec = jnp.where(diff[None] >= 0, jnp.exp(jnp.maximum(diff, 0.0)[None] * log_gamma[:, None, None]), 0.0)
    zeta = jnp.exp((RET_C - 1.0 - pos)[None, :] * log_gamma[:, None])
    xi = jnp.exp((pos + 1.0)[None, :] * log_gamma[:, None])
    gc = jnp.exp(RET_C * log_gamma)
    bc = lambda a, shape: jnp.broadcast_to(a, shape).astype(jnp.float32)
    return (dec, bc(zeta[:, :, None], (n_heads, RET_C, HEAD_DIM)), bc(xi[:, :, None], (n_heads, RET_C, HEAD_DIM)),
            bc(gc[:, None, None], (n_heads, HEAD_DIM, HEAD_DIM)))


def _ret(qb, kb, vb, gb, seq):
    n, d_b = qb.shape
    bsz = n // seq
    n_heads = d_b // HEAD_DIM
    n_c = seq // RET_C
    dec, zeta, xi, gc = _ret_consts(n_heads)
    row = lambda b, c: (b * n_c + c, 0)
    const3 = lambda b, c: (0, 0, 0)
    return pl.pallas_call(
        functools.partial(_ret_kernel, n_heads=n_heads),
        grid=(bsz, n_c),
        in_specs=[pl.BlockSpec((RET_C, d_b), row)] * 4 + [
            pl.BlockSpec(dec.shape, const3), pl.BlockSpec(zeta.shape, const3),
            pl.BlockSpec(xi.shape, const3), pl.BlockSpec(gc.shape, const3)],
        out_specs=pl.BlockSpec((RET_C, d_b), row),
        out_shape=jax.ShapeDtypeStruct((n, d_b), jnp.bfloat16),
        scratch_shapes=[pltpu.VMEM((n_heads, HEAD_DIM, HEAD_DIM), jnp.float32)],
        compiler_params=_cparams(("arbitrary", "arbitrary")),
        name="ret",
    )(qb, kb, vb, gb, dec, zeta, xi, gc)


def _tiles_shape(rows, d):
    return (rows // SUBLANES, d // LANES, SUBLANES, LANES)


def _to_tiles(ref, x):
    for s in range(ref.shape[1]):
        ref[:, s] = x[:, s * LANES:(s + 1) * LANES].reshape(ref.shape[0], SUBLANES, LANES)


def _from_tiles(ref):
    rows = ref.shape[0] * SUBLANES
    return jnp.concatenate([ref[:, s].reshape(rows, LANES) for s in range(ref.shape[1])], axis=1)


def _lane_first_eq(x, m, lane):
    return jnp.min(jnp.where(x == m, lane, float(LANES)), axis=1, keepdims=True)


def _mix_out_kernel(x_ref, ya_ref, yb_ref, woa_ref, wob_ref, gt_ref, sc_ref, sh_ref, g_ref, wr_ref, tri_ref,
                    x1_ref, h2_ref, route_ref, cnt_ref, carry_ref):
    @pl.when(pl.program_id(0) == 0)
    def _():
        carry_ref[...] = jnp.zeros_like(carry_ref)

    mix = (jnp.dot(ya_ref[...], woa_ref[...], preferred_element_type=jnp.float32)
           + jnp.dot(yb_ref[...], wob_ref[...], preferred_element_type=jnp.float32))
    x1 = x_ref[...] + gt_ref[0] * mix
    x1_ref[...] = x1
    h2 = _rmsnorm_mod(x1, g_ref[...], sc_ref[0], sh_ref[0])
    _to_tiles(h2_ref, h2)

    lg = jnp.dot(h2.astype(jnp.bfloat16), wr_ref[...], preferred_element_type=jnp.float32)
    lane = lax.broadcasted_iota(jnp.int32, lg.shape, 1).astype(jnp.float32)
    is_grp = (lane >= N_EXPERTS) & (lane < N_EXPERTS + N_GROUPS)
    gl = jnp.where(is_grp, lg, -jnp.inf)
    gmax = jnp.max(gl, axis=1, keepdims=True)
    grp = _lane_first_eq(gl, gmax, lane) - N_EXPERTS
    p_grp = 1.0 / jnp.sum(jnp.exp(gl - gmax), axis=1, keepdims=True)
    in_grp = jnp.floor(lane * (1.0 / EXPERTS_PER_GROUP)) == grp
    f = jnp.where(in_grp & (lane < N_EXPERTS), lg, -jnp.inf)
    f1 = jnp.max(f, axis=1, keepdims=True)
    e1 = _lane_first_eq(f, f1, lane)
    f = jnp.where(lane == e1, -jnp.inf, f)
    f2 = jnp.max(f, axis=1, keepdims=True)
    e2 = _lane_first_eq(f, f2, lane)
    a2 = jnp.exp(f2 - f1)
    w1 = p_grp / (1.0 + a2)
    w2 = p_grp * a2 / (1.0 + a2)

    oh1 = jnp.where(lane == e1, 1.0, 0.0)
    oh2 = jnp.where(lane == e2, 1.0, 0.0)
    both = oh1 + oh2
    before = jnp.dot(tri_ref[...], both.astype(jnp.bfloat16), preferred_element_type=jnp.float32) + carry_ref[...]
    r1 = jnp.sum(before * oh1, axis=1, keepdims=True)
    r2 = jnp.sum(before * oh2, axis=1, keepdims=True)
    carry = carry_ref[...] + jnp.sum(both, axis=0, keepdims=True)
    carry_ref[...] = carry
    cnt_ref[...] = carry

    out = jnp.zeros(lg.shape, jnp.float32)
    for col, val in enumerate((e1, e2, w1, w2, r1, r2)):
        out = jnp.where(lane == col, val, out)
    route_ref[...] = out


def _mix_out(x2, ya, yb, wo_a, wo_b, mod3, g_ffn, w_route, seq):
    n, d = x2.shape
    tiles_per_seq = seq // TM
    tri = jnp.asarray(np.tril(np.ones((TM, TM), np.float32), -1), jnp.bfloat16)
    row = lambda i: (i, 0)
    const = lambda i: (0, 0)
    modk = lambda k: pl.BlockSpec((1, 1, d), lambda i: ((i // tiles_per_seq) * 6 + k, 0, 0))
    return pl.pallas_call(
        _mix_out_kernel,
        grid=(n // TM,),
        in_specs=[pl.BlockSpec((TM, d), row),
                  pl.BlockSpec((TM, ya.shape[1]), row),
                  pl.BlockSpec((TM, yb.shape[1]), row),
                  pl.BlockSpec(wo_a.shape, const),
                  pl.BlockSpec(wo_b.shape, const),
                  modk(2), modk(4), modk(3),
                  pl.BlockSpec((1, d), const),
                  pl.BlockSpec(w_route.shape, const),
                  pl.BlockSpec(tri.shape, const)],
        out_specs=(pl.BlockSpec((TM, d), row), pl.BlockSpec(_tiles_shape(TM, d), lambda i: (i, 0, 0, 0)),
                   pl.BlockSpec((TM, LANES), row), pl.BlockSpec((1, LANES), const)),
        out_shape=(jax.ShapeDtypeStruct((n, d), jnp.float32), jax.ShapeDtypeStruct(_tiles_shape(n, d), jnp.float32),
                   jax.ShapeDtypeStruct((n, LANES), jnp.float32), jax.ShapeDtypeStruct((1, LANES), jnp.float32)),
        scratch_shapes=[pltpu.VMEM((1, LANES), jnp.float32)],
        compiler_params=_cparams(("arbitrary",)),
        name="mix_out",
    )(x2, ya, yb, wo_a, wo_b, mod3, mod3, mod3, g_ffn, w_route, tri)


def _plan_kernel(route_ref, pst_ref, dest_ref):
    r = route_ref[...]
    lane = lax.broadcasted_iota(jnp.int32, r.shape, 1).astype(jnp.float32)
    pst = pst_ref[...]
    d1 = jnp.sum(jnp.where(lane == r[:, 0:1], pst, 0.0), axis=1, keepdims=True) + r[:, 4:5]
    d2 = jnp.sum(jnp.where(lane == r[:, 1:2], pst, 0.0), axis=1, keepdims=True) + r[:, 5:6]
    t1 = jnp.floor(d1 * (1.0 / SUBLANES))
    t2 = jnp.floor(d2 * (1.0 / SUBLANES))
    packed = jnp.zeros(r.shape, jnp.float32)
    for k, v in enumerate((t1, d1 - t1 * SUBLANES, t2, d2 - t2 * SUBLANES)):
        packed = jnp.where(lane == float(k), v, packed)
    dest_ref[0] = packed.T[0:8, :].astype(jnp.int32)


def _plan(route, pst_row):
    n = route.shape[0]
    return pl.pallas_call(
        _plan_kernel,
        grid=(n // TM,),
        in_specs=[pl.BlockSpec((TM, LANES), lambda i: (i, 0)), pl.BlockSpec((1, LANES), lambda i: (0, 0))],
        out_specs=pl.BlockSpec((1, 8, TM), lambda i: (i, 0, 0)),
        out_shape=jax.ShapeDtypeStruct((n // TM, 8, TM), jnp.int32),
        compiler_params=_cparams(("arbitrary",)),
        name="plan",
    )(route, pst_row)


def _dispatch_kernel(zrow_ref, n_used_ref, dest_ref, h2_ref, xs_hbm, zbuf, sem, zsem, *, n_blk):
    i = pl.program_id(0)

    @pl.when(i == 0)
    def _():
        zbuf[...] = jnp.zeros(zbuf.shape, zbuf.dtype)
        blk_tiles = BM // SUBLANES
        for e in range(N_EXPERTS):
            pltpu.make_async_copy(zbuf, xs_hbm.at[pl.ds(zrow_ref[e], blk_tiles)], zsem).start()
        for e in range(N_EXPERTS):
            pltpu.make_async_copy(zbuf, xs_hbm.at[pl.ds(0, blk_tiles)], zsem).wait()
        for b in range(N_EXPERTS + 1):
            @pl.when(n_used_ref[0] + b <= n_blk)
            def _():
                tail = pltpu.make_async_copy(zbuf, xs_hbm.at[pl.ds((n_used_ref[0] + b) * blk_tiles, blk_tiles)], zsem)
                tail.start()
                tail.wait()

    def issue(k, c):
        for u in range(SUBLANES):
            r = k * SUBLANES + u
            row = h2_ref.at[k, :, u, :]
            pltpu.make_async_copy(row, xs_hbm.at[dest_ref[0, 0, r], :, dest_ref[0, 1, r], :], sem).start()
            pltpu.make_async_copy(row, xs_hbm.at[dest_ref[0, 2, r], :, dest_ref[0, 3, r], :], sem).start()
        return c

    lax.fori_loop(0, TM // SUBLANES, issue, 0)
    for _ in range(2):
        pltpu.make_async_copy(h2_ref, xs_hbm.at[pl.ds(0, TM // SUBLANES)], sem).wait()


def _dispatch(h2, dest, zrow, n_used, n_blk):
    n_rows = (n_blk + 1) * BM
    d = h2.shape[1] * LANES
    grid_spec = pltpu.PrefetchScalarGridSpec(
        num_scalar_prefetch=2,
        grid=(h2.shape[0] * SUBLANES // TM,),
        in_specs=[pl.BlockSpec((1, 8, TM), lambda i, z, nu: (i, 0, 0), memory_space=pltpu.SMEM),
                  pl.BlockSpec(_tiles_shape(TM, d), lambda i, z, nu: (i, 0, 0, 0))],
        out_specs=pl.BlockSpec(memory_space=pl.ANY),
        scratch_shapes=[pltpu.VMEM(_tiles_shape(BM, d), h2.dtype),
                        pltpu.SemaphoreType.DMA(()),
                        pltpu.SemaphoreType.DMA(())],
    )
    return pl.pallas_call(
        functools.partial(_dispatch_kernel, n_blk=n_blk),
        grid_spec=grid_spec,
        out_shape=jax.ShapeDtypeStruct(_tiles_shape(n_rows, d), h2.dtype),
        compiler_params=_cparams(("arbitrary",)),
        name="dispatch",
    )(zrow, n_used, dest, h2)


def _experts_kernel(blk_e_ref, n_used_ref, x_ref, wg_ref, wu_ref, wd_ref, y_ref):
    j = pl.program_id(0)

    @pl.when(j < n_used_ref[0])
    def _():
        x = _from_tiles(x_ref).astype(wg_ref.dtype)
        a = jnp.dot(x, wg_ref[0], preferred_element_type=jnp.float32)
        b = jnp.dot(x, wu_ref[0], preferred_element_type=jnp.float32)
        hmid = (a / (1.0 + jnp.exp(-a)) * b).astype(x.dtype)
        _to_tiles(y_ref, jnp.dot(hmid, wd_ref[0], preferred_element_type=jnp.float32))

    @pl.when(j >= n_used_ref[0])
    def _():
        y_ref[...] = jnp.zeros(y_ref.shape, y_ref.dtype)


def _experts(xs, wg, wu, wd, blk_e, n_used):
    n_blk = blk_e.shape[0]
    d, d_e = wg.shape[1], wg.shape[2]
    blk = _tiles_shape(BM, d)
    grid_spec = pltpu.PrefetchScalarGridSpec(
        num_scalar_prefetch=2,
        grid=(n_blk,),
        in_specs=[pl.BlockSpec(blk, lambda j, be, nu: (jnp.minimum(j, nu[0] - 1), 0, 0, 0)),
                  pl.BlockSpec((1, d, d_e), lambda j, be, nu: (be[j], 0, 0)),
                  pl.BlockSpec((1, d, d_e), lambda j, be, nu: (be[j], 0, 0)),
                  pl.BlockSpec((1, d_e, d), lambda j, be, nu: (be[j], 0, 0))],
        out_specs=pl.BlockSpec(blk, lambda j, be, nu: (j, 0, 0, 0)),
    )
    return pl.pallas_call(
        _experts_kernel,
        grid_spec=grid_spec,
        out_shape=jax.ShapeDtypeStruct(_tiles_shape(n_blk * BM, d), jnp.float32),
        compiler_params=_cparams(("arbitrary",)),
        name="experts",
    )(blk_e, n_used, xs, wg, wu, wd)


def _final_kernel(dest_ref, dest_next_ref, x1_ref, route_ref, gt_ref, g_ref, ys_hbm, o_ref, ybuf, sem):
    i = pl.program_id(0)
    slot = i % 2

    tiles = TM // SUBLANES

    def gather(d_ref, s):
        def issue(k, c):
            for u in range(SUBLANES):
                r = k * SUBLANES + u
                pltpu.make_async_copy(ys_hbm.at[d_ref[0, 0, r], :, d_ref[0, 1, r], :],
                                      ybuf.at[s, k, :, u, :], sem.at[s]).start()
                pltpu.make_async_copy(ys_hbm.at[d_ref[0, 2, r], :, d_ref[0, 3, r], :],
                                      ybuf.at[s, tiles + k, :, u, :], sem.at[s]).start()
            return c
        lax.fori_loop(0, tiles, issue, 0)

    @pl.when(i == 0)
    def _():
        gather(dest_ref, 0)

    @pl.when(i + 1 < pl.num_programs(0))
    def _():
        gather(dest_next_ref, 1 - slot)

    pltpu.make_async_copy(ys_hbm.at[pl.ds(0, 2 * tiles)], ybuf.at[slot], sem.at[slot]).wait()
    w1 = route_ref[:, 2:3]
    w2 = route_ref[:, 3:4]
    y0 = _from_tiles(ybuf.at[slot, 0:tiles])
    y1 = _from_tiles(ybuf.at[slot, tiles:2 * tiles])
    x2 = x1_ref[...] + gt_ref[0] * (w1 * y0 + w2 * y1)
    o_ref[...] = x2 * lax.rsqrt(jnp.mean(x2 * x2, axis=-1, keepdims=True) + EPS) * g_ref[...]


def _final(x1, ys, dest, route, mod3, g_final, seq):
    n, d = x1.shape
    n_tiles = n // TM
    tiles_per_seq = seq // TM
    row = lambda i: (i, 0)
    return pl.pallas_call(
        _final_kernel,
        grid=(n_tiles,),
        in_specs=[pl.BlockSpec((1, 8, TM), lambda i: (i, 0, 0), memory_space=pltpu.SMEM),
                  pl.BlockSpec((1, 8, TM), lambda i: (jnp.minimum(i + 1, n_tiles - 1), 0, 0),
                               memory_space=pltpu.SMEM),
                  pl.BlockSpec((TM, d), row),
                  pl.BlockSpec((TM, LANES), row),
                  pl.BlockSpec((1, 1, d), lambda i: ((i // tiles_per_seq) * 6 + 5, 0, 0)),
                  pl.BlockSpec((1, d), lambda i: (0, 0)),
                  pl.BlockSpec(memory_space=pl.ANY)],
        out_specs=pl.BlockSpec((TM, d), row),
        out_shape=jax.ShapeDtypeStruct((n, d), jnp.float32),
        scratch_shapes=[pltpu.VMEM((2,) + _tiles_shape(2 * TM, d), jnp.float32),
                        pltpu.SemaphoreType.DMA((2,))],
        compiler_params=_cparams(("arbitrary",)),
        name="final",
    )(dest, dest, x1, route, mod3, g_final, ys)


def _block_layout(counts, n):
    cnt = counts[0, :N_EXPERTS].astype(jnp.int32)
    blocks = (cnt + BM - 1) // BM
    bends = jnp.cumsum(blocks)
    pstarts = (bends - blocks) * BM
    n_blk = (2 * n) // BM + N_EXPERTS
    pst_row = jnp.zeros((1, LANES), jnp.float32).at[0, :N_EXPERTS].set(pstarts.astype(jnp.float32))
    zrow = ((pstarts + cnt) // SUBLANES).astype(jnp.int32)
    n_used = bends[-1:].astype(jnp.int32)
    blk_e = jnp.sum(bends[None, :] <= jnp.arange(n_blk, dtype=jnp.int32)[:, None], axis=1)
    blk_e = jnp.minimum(blk_e, N_EXPERTS - 1).astype(jnp.int32)
    return pst_row, zrow, n_used, blk_e


def _layer(x2, c, pos2, w_ada, b_ada, g_mix, w_in, w_o, g_ffn, w_rg, w_re, w_up, w_gate, w_down, seq):
    n, d = x2.shape
    bsz = n // seq
    bf = jnp.bfloat16
    d_a = d // 2
    d_i = N_IDX_HEADS * IDX_DIM
    mod3 = _ada(c, w_ada, b_ada).reshape(bsz * 6, 1, d)

    c0 = 3 * d_a + d_i
    c1 = c0 + IDX_DIM + N_IDX_HEADS
    w_cat = jnp.concatenate([w_in[:, :c0], w_in[:, c0:c1],
                             jnp.zeros((d, LANES - (c1 - c0)), w_in.dtype), w_in[:, c1:]], axis=1).astype(bf)
    qat, ka, vat, qit, kiw, wit, qb, kb, vb, gb = _proj(x2, pos2, mod3, g_mix.reshape(1, d), w_cat, seq)

    n_kt = seq // TM
    ya = _dsa(qat, qit, wit, kiw.reshape(bsz, n_kt, TM, LANES), ka.reshape(bsz, n_kt, TM, d_a), vat, seq)
    yb = _ret(qb, kb, vb, gb, seq)

    w_route = jnp.concatenate([jnp.transpose(w_re, (1, 0, 2)).reshape(d, N_EXPERTS), w_rg,
                               jnp.zeros((d, LANES - N_EXPERTS - N_GROUPS), w_rg.dtype)], axis=1).astype(bf)
    x1, h2, route, counts = _mix_out(x2, ya, yb, w_o[:d_a].astype(bf), w_o[d_a:].astype(bf), mod3,
                                     g_ffn.reshape(1, d), w_route, seq)

    pst_row, zrow, n_used, blk_e = _block_layout(counts, n)
    dest = _plan(route, pst_row)
    xs = _dispatch(h2, dest, zrow, n_used, blk_e.shape[0])
    ys = _experts(xs, w_gate.astype(bf), w_up.astype(bf), w_down.astype(bf), blk_e, n_used)
    return x1, ys, dest, route, mod3


def kernel(x, c, positions, w_ada, b_ada, g_norm_mix, w_in, w_o, g_norm_ffn, w_router_group, w_router_expert,
           w_up, w_gate, w_down, g_norm_final):
    bsz, seq, d = x.shape
    depth = w_ada.shape[0]
    assert depth == 1, "the final norm is fused into the last layer's combine kernel"
    assert seq % (2 * TM) == 0 and seq % RET_C == 0 and (2 * bsz * seq) % BM == 0
    x2 = x.reshape(bsz * seq, d)
    pos2 = positions.astype(jnp.float32).reshape(bsz * seq, 1)
    x1, ys, dest, route, mod3 = _layer(x2, c, pos2, w_ada[0], b_ada[0], g_norm_mix[0], w_in[0], w_o[0],
                                       g_norm_ffn[0], w_router_group[0], w_router_expert[0], w_up[0], w_gate[0],
                                       w_down[0], seq)
    out = _final(x1, ys, dest, route, mod3, g_norm_final.reshape(1, d), seq)
    return out.reshape(bsz, seq, d)
```

```python
import functools

import jax
import jax.numpy as jnp
import numpy as np
from jax import lax
from jax.experimental import pallas as pl
from jax.experimental.pallas import tpu as pltpu

CHUNK = 64
HEAD_DIM = 64
N_IDX_HEADS = 16
IDX_DIM = 64
TOPK_MAX = 256
ROPE_THETA = 500000.0
ROT_DIM = HEAD_DIM // 4
RET_THETA = 10000.0
N_GROUPS = 4
EXPERTS_PER_GROUP = 8
N_EXPERTS = N_GROUPS * EXPERTS_PER_GROUP
EPS = 1e-6

LANES = 128
SUBLANES = 8
VMEM_LIMIT = 56 * 1024 * 1024

TM = 256
QB = TM
RET_C = 256
BM = 256
CNT_ROWS = 64
ONES_ROWS = 16
HEAD_GROUP = 4
IDX_TILES = 4
BISECT_PER_CHECK = 2
MAX_BISECT = 40

NEG_BIG = -1e30
LOG2E = 1.4426950408889634


def _cparams(sem):
    return pltpu.CompilerParams(dimension_semantics=sem, vmem_limit_bytes=VMEM_LIMIT)


def _ada_kernel(c_ref, w_ref, b_ref, o_ref):
    o_ref[...] = jnp.dot(c_ref[...], w_ref[...], preferred_element_type=jnp.float32) + b_ref[...]


def _ada(c, w_ada, b_ada):
    bsz, d = c.shape
    n_out = w_ada.shape[1]
    return pl.pallas_call(
        _ada_kernel,
        grid=(n_out // d,),
        in_specs=[pl.BlockSpec((bsz, d), lambda j: (0, 0)),
                  pl.BlockSpec((d, d), lambda j: (0, j)),
                  pl.BlockSpec((1, d), lambda j: (0, j))],
        out_specs=pl.BlockSpec((bsz, d), lambda j: (0, j)),
        out_shape=jax.ShapeDtypeStruct((bsz, n_out), jnp.float32),
        compiler_params=_cparams(("arbitrary",)),
        name="ada",
    )(c, w_ada, b_ada.reshape(1, n_out))


def _rmsnorm_mod(x, g, sc, sh):
    xn = x * lax.rsqrt(jnp.mean(x * x, axis=-1, keepdims=True) + EPS)
    return xn * g * (1.0 + sc) + sh


def _rope_lanes(x, cos, sin_lo, sin_hi, half):
    cols = []
    for k in range(x.shape[1] // LANES):
        xb = x[:, k * LANES:(k + 1) * LANES]
        cols.append(xb * cos + pltpu.roll(xb, LANES - half, 1) * sin_lo + pltpu.roll(xb, half, 1) * sin_hi)
    return cols[0] if len(cols) == 1 else jnp.concatenate(cols, axis=1)


def _proj_kernel(x_ref, pos_ref, sc_ref, sh_ref, g_ref, w_ref, tab_ref,
                 qat_ref, ka_ref, vat_ref, qit_ref, kiw_ref, wit_ref, qb_ref, kb_ref, vb_ref, gb_ref,
                 *, d_a, d_i, d_b):
    h = _rmsnorm_mod(x_ref[...], g_ref[...], sc_ref[0], sh_ref[0]).astype(jnp.bfloat16)
    pos = pos_ref[...]
    ang_a = pos * tab_ref[0:1, :]
    cos_a, sin_a = jnp.cos(ang_a), jnp.sin(ang_a)
    sa_lo, sa_hi = sin_a * tab_ref[1:2, :], sin_a * tab_ref[2:3, :]
    ang_b = pos * tab_ref[3:4, :]
    cos_b, sin_b = jnp.cos(ang_b), jnp.sin(ang_b)
    sb_lo, sb_hi = sin_b * tab_ref[4:5, :], sin_b * tab_ref[5:6, :]
    half_a, half_b = ROT_DIM // 2, HEAD_DIM // 2

    def seg(lo, width):
        return jnp.dot(h, w_ref[:, lo:lo + width], preferred_element_type=jnp.float32)

    o = 0
    qa = seg(o, d_a); o += d_a
    qat_ref[0, 0] = (_rope_lanes(qa, cos_a, sa_lo, sa_hi, half_a) * (HEAD_DIM ** -0.5 * LOG2E)).T.astype(qat_ref.dtype)
    ka = seg(o, d_a); o += d_a
    ka_ref[...] = _rope_lanes(ka, cos_a, sa_lo, sa_hi, half_a).astype(ka_ref.dtype)
    vat_ref[0, 0] = seg(o, d_a).T.astype(vat_ref.dtype); o += d_a
    qi = seg(o, d_i); o += d_i
    qit_ref[0, 0] = (_rope_lanes(qi, cos_a, sa_lo, sa_hi, half_a) * (IDX_DIM ** -0.5)).T.astype(qit_ref.dtype)
    kw = seg(o, LANES); o += LANES
    kiw_ref[...] = _rope_lanes(kw, cos_a, sa_lo, sa_hi, half_a).astype(kiw_ref.dtype)
    wit_ref[0, 0] = kw.T[IDX_DIM:IDX_DIM + N_IDX_HEADS, :] * (N_IDX_HEADS ** -0.5)
    qb = seg(o, d_b); o += d_b
    qb_ref[...] = _rope_lanes(qb, cos_b, sb_lo, sb_hi, half_b).astype(qb_ref.dtype)
    kb = seg(o, d_b); o += d_b
    kb_ref[...] = (_rope_lanes(kb, cos_b, sb_lo, sb_hi, half_b) * (HEAD_DIM ** -0.5)).astype(kb_ref.dtype)
    vb_ref[...] = seg(o, d_b).astype(vb_ref.dtype); o += d_b
    gb_ref[...] = seg(o, d_b)


def _rope_tables():
    lane = jnp.arange(LANES) % HEAD_DIM
    rows = []
    for rot, theta in ((ROT_DIM, ROPE_THETA), (HEAD_DIM, RET_THETA)):
        half = rot // 2
        inv_freq = theta ** (-jnp.arange(half, dtype=jnp.float32) / half)
        rows.append(jnp.where(lane < rot, inv_freq[lane % half], 0.0))
        rows.append(jnp.where(lane < half, -1.0, 0.0))
        rows.append(jnp.where((lane >= half) & (lane < rot), 1.0, 0.0))
    rows += [jnp.zeros((LANES,), jnp.float32)] * 2
    return jnp.stack(rows).astype(jnp.float32)


def _proj(x2, pos2, mod3, g_mix, w_cat, seq):
    n, d = x2.shape
    bsz = n // seq
    d_a = d // 2
    d_b = d // 2
    d_i = N_IDX_HEADS * IDX_DIM
    tiles_per_seq = seq // TM
    tab = _rope_tables()
    row = lambda i: (i, 0)
    const = lambda i: (0, 0)
    tile4 = lambda i: (i // tiles_per_seq, i % tiles_per_seq, 0, 0)
    bf = jnp.bfloat16
    out_shape = (
        jax.ShapeDtypeStruct((bsz, tiles_per_seq, d_a, TM), bf),
        jax.ShapeDtypeStruct((n, d_a), bf),
        jax.ShapeDtypeStruct((bsz, tiles_per_seq, d_a, TM), bf),
        jax.ShapeDtypeStruct((bsz, tiles_per_seq, d_i, TM), bf),
        jax.ShapeDtypeStruct((n, LANES), bf),
        jax.ShapeDtypeStruct((bsz, tiles_per_seq, N_IDX_HEADS, TM), jnp.float32),
        jax.ShapeDtypeStruct((n, d_b), bf),
        jax.ShapeDtypeStruct((n, d_b), bf),
        jax.ShapeDtypeStruct((n, d_b), bf),
        jax.ShapeDtypeStruct((n, d_b), jnp.float32),
    )
    out_specs = (
        pl.BlockSpec((1, 1, d_a, TM), tile4),
        pl.BlockSpec((TM, d_a), row),
        pl.BlockSpec((1, 1, d_a, TM), tile4),
        pl.BlockSpec((1, 1, d_i, TM), tile4),
        pl.BlockSpec((TM, LANES), row),
        pl.BlockSpec((1, 1, N_IDX_HEADS, TM), tile4),
        pl.BlockSpec((TM, d_b), row),
        pl.BlockSpec((TM, d_b), row),
        pl.BlockSpec((TM, d_b), row),
        pl.BlockSpec((TM, d_b), row),
    )
    return pl.pallas_call(
        functools.partial(_proj_kernel, d_a=d_a, d_i=d_i, d_b=d_b),
        grid=(n // TM,),
        in_specs=[pl.BlockSpec((TM, d), row),
                  pl.BlockSpec((TM, 1), row),
                  pl.BlockSpec((1, 1, d), lambda i: ((i // tiles_per_seq) * 6 + 1, 0, 0)),
                  pl.BlockSpec((1, 1, d), lambda i: ((i // tiles_per_seq) * 6 + 0, 0, 0)),
                  pl.BlockSpec((1, d), const),
                  pl.BlockSpec(w_cat.shape, const),
                  pl.BlockSpec(tab.shape, const)],
        out_specs=out_specs,
        out_shape=out_shape,
        compiler_params=_cparams(("arbitrary",)),
        name="proj",
    )(x2, pos2, mod3, mod3, g_mix, w_cat, tab)


def _row_blocks(x, rows):
    return [x[r * rows:(r + 1) * rows] for r in range(x.shape[0] // rows)]


def _dsa_kernel(qat_ref, qit_ref, wit_ref, kiw_ref, ka_ref, vat_ref, o_ref,
                qix_ref, qmx_ref, sc_ref, m_ref, l_ref, acc_ref, sa_ref, sb_ref, mxa_ref, mxb_ref, lohi_ref,
                *, k_top, n_heads):
    i = pl.program_id(1)
    n_grp = (i * QB + QB + 2 * TM - 1) // (2 * TM)
    n_kt = 2 * n_grp
    n_real = (i * QB + QB + TM - 1) // TM

    zero_rows = jnp.zeros((LANES - IDX_DIM, QB), qix_ref.dtype)
    for h in range(N_IDX_HEADS):
        qix_ref[h] = jnp.concatenate([qit_ref[0, 0, h * IDX_DIM:(h + 1) * IDX_DIM, :], zero_rows], axis=0)
    row_q = lax.broadcasted_iota(jnp.int32, (LANES, QB), 0)
    for h in range(n_heads):
        pair = qat_ref[0, 0, (h // 2) * LANES:(h // 2 + 1) * LANES, :]
        own = (row_q < HEAD_DIM) if h % 2 == 0 else (row_q >= HEAD_DIM)
        qmx_ref[h] = jnp.where(own, pair, jnp.zeros_like(pair))

    q_chunk = (i * QB + lax.broadcasted_iota(jnp.int32, (TM, QB), 1)) // CHUNK
    key_in_tile = lax.broadcasted_iota(jnp.int32, (TM, QB), 0)
    w_all = wit_ref[0, 0]

    def idx_tiles(tiles):
        lo, hi = lohi_ref[0], lohi_ref[1]
        for j in tiles:
            kt = kiw_ref[0, j]
            acc = None
            for h in range(N_IDX_HEADS):
                d = jnp.dot(kt, qix_ref[h], preferred_element_type=jnp.float32)
                t = w_all[h:h + 1, :] * jnp.maximum(d, 0.0)
                acc = t if acc is None else acc + t
            adm = (j * TM + key_in_tile) // CHUNK <= q_chunk
            sc_ref[j] = jnp.where(adm, acc, -jnp.inf)
            lo = jnp.minimum(lo, functools.reduce(jnp.minimum, _row_blocks(jnp.where(adm, acc, jnp.inf), SUBLANES)))
            hi = jnp.maximum(hi, functools.reduce(jnp.maximum, _row_blocks(jnp.where(adm, acc, -jnp.inf), SUBLANES)))
        lohi_ref[0], lohi_ref[1] = lo, hi

    lohi_ref[0] = jnp.full((SUBLANES, QB), jnp.inf, jnp.float32)
    lohi_ref[1] = jnp.full((SUBLANES, QB), -jnp.inf, jnp.float32)

    def idx_step(g, carry):
        idx_tiles([IDX_TILES * g + u for u in range(IDX_TILES)])
        return carry

    lax.fori_loop(0, n_kt // IDX_TILES, idx_step, 0)

    @pl.when(n_kt % IDX_TILES != 0)
    def _():
        idx_tiles([n_kt - 2, n_kt - 1])

    lo = jnp.min(lohi_ref[0], axis=0, keepdims=True)
    hi = jnp.max(lohi_ref[1], axis=0, keepdims=True)
    n_adm = ((i * QB + lax.broadcasted_iota(jnp.int32, (1, QB), 1)) // CHUNK + 1) * CHUNK

    def bisect_cond(carry):
        it, _, _, _, unsettled = carry
        return (it < MAX_BISECT) & (unsettled > 0.0)

    def bisect(carry):
        it, lo, hi, cnt_lo, _ = carry
        for _ in range(BISECT_PER_CHECK):
            mid = lo + (hi - lo) * 0.5
            mid_b = jnp.broadcast_to(mid, (CNT_ROWS, QB))

            def count_tile(j, cnt, mid_b=mid_b):
                for blk in _row_blocks(sc_ref[j], CNT_ROWS):
                    cnt = cnt + jnp.where(blk >= mid_b, 1.0, 0.0)
                return cnt

            cnt = lax.fori_loop(0, n_real, count_tile, jnp.zeros((CNT_ROWS, QB), jnp.float32))
            c = jnp.sum(cnt, axis=0, keepdims=True)
            ge = c >= k_top
            lo = jnp.where(ge, mid, lo)
            hi = jnp.where(ge, hi, mid)
            cnt_lo = jnp.where(ge, c, cnt_lo)
        return it + BISECT_PER_CHECK, lo, hi, cnt_lo, jnp.max(jnp.where(cnt_lo > k_top, 1.0, 0.0))

    cnt0 = n_adm.astype(jnp.float32)
    _, thr, _, _, _ = lax.while_loop(bisect_cond, bisect,
                                     (0, lo, hi, cnt0, jnp.max(jnp.where(cnt0 > k_top, 1.0, 0.0))))

    def bias_tile(j, carry):
        sc_ref[j] = jnp.where(sc_ref[j] >= thr, 0.0, NEG_BIG)
        return carry

    lax.fori_loop(0, n_kt, bias_tile, 0)

    m_ref[...] = jnp.full(m_ref.shape, NEG_BIG, jnp.float32)
    l_ref[...] = jnp.zeros(l_ref.shape, jnp.float32)
    acc_ref[...] = jnp.zeros(acc_ref.shape, jnp.float32)

    def pair(h):
        return slice((h // 2) * LANES, (h // 2 + 1) * LANES)

    def logits_into(s_ref, mx_ref, j, heads):
        bias = sc_ref[j]
        for h in heads:
            s = jnp.dot(ka_ref[0, j, :, pair(h)], qmx_ref[h], preferred_element_type=jnp.float32) + bias
            s_ref[h] = s
            mx_ref[h] = jnp.max(s, axis=0, keepdims=True)

    ones_rows = jnp.ones((ONES_ROWS, TM), vat_ref.dtype)

    def absorb(s_ref, mx_ref, j, heads):
        for h in heads:
            m_old = m_ref[h]
            m_new = jnp.maximum(m_old, mx_ref[h])
            alpha = jnp.exp2(m_old - m_new)
            p = jnp.exp2(s_ref[h] - m_new).astype(vat_ref.dtype)
            pv = jnp.dot(jnp.concatenate([vat_ref[0, j, pair(h), :], ones_rows], axis=0), p,
                         preferred_element_type=jnp.float32)
            acc_ref[h] = acc_ref[h] * alpha + pv[0:LANES]
            l_ref[h] = l_ref[h] * alpha + pv[LANES:LANES + SUBLANES]
            m_ref[h] = m_new

    for h0 in range(0, n_heads, HEAD_GROUP):
        heads = range(h0, h0 + HEAD_GROUP)
        logits_into(sa_ref, mxa_ref, 0, heads)

        def attn_grp(g, carry, heads=heads):
            logits_into(sb_ref, mxb_ref, 2 * g + 1, heads)
            absorb(sa_ref, mxa_ref, 2 * g, heads)
            logits_into(sa_ref, mxa_ref, jnp.minimum(2 * g + 2, n_kt - 1), heads)
            absorb(sb_ref, mxb_ref, 2 * g + 1, heads)
            return carry

        lax.fori_loop(0, n_grp, attn_grp, 0)

    for hp in range(n_heads // 2):
        even = acc_ref[2 * hp] / l_ref[2 * hp, 0:1, :]
        odd = acc_ref[2 * hp + 1] / l_ref[2 * hp + 1, 0:1, :]
        o_ref[:, hp * LANES:(hp + 1) * LANES] = jnp.where(row_q < HEAD_DIM, even, odd).T.astype(o_ref.dtype)


def _dsa(qat, qit, wit, kiw4, ka4, vat, seq):
    bsz, n_kt, d_a, _ = qat.shape
    n_heads = d_a // HEAD_DIM
    n_qb = seq // QB
    k_top = min(TOPK_MAX, seq // 4)
    qtile = lambda b, i: (b, i, 0, 0)
    per_b = lambda b, i: (b, 0, 0, 0)
    f32 = jnp.float32
    return pl.pallas_call(
        functools.partial(_dsa_kernel, k_top=float(k_top), n_heads=n_heads),
        grid=(bsz, n_qb),
        in_specs=[pl.BlockSpec((1, 1, d_a, QB), qtile),
                  pl.BlockSpec((1, 1, qit.shape[2], QB), qtile),
                  pl.BlockSpec((1, 1, N_IDX_HEADS, QB), qtile),
                  pl.BlockSpec((1, n_kt, TM, LANES), per_b, pipeline_mode=pl.Buffered(1)),
                  pl.BlockSpec((1, n_kt, TM, d_a), per_b, pipeline_mode=pl.Buffered(1)),
                  pl.BlockSpec((1, n_kt, d_a, TM), per_b, pipeline_mode=pl.Buffered(1))],
        out_specs=pl.BlockSpec((QB, d_a), lambda b, i: (b * n_qb + i, 0)),
        out_shape=jax.ShapeDtypeStruct((bsz * seq, d_a), jnp.bfloat16),
        scratch_shapes=[pltpu.VMEM((N_IDX_HEADS, LANES, QB), jnp.bfloat16),
                        pltpu.VMEM((n_heads, LANES, QB), jnp.bfloat16),
                        pltpu.VMEM((n_kt, TM, QB), f32),
                        pltpu.VMEM((n_heads, 1, QB), f32),
                        pltpu.VMEM((n_heads, SUBLANES, QB), f32),
                        pltpu.VMEM((n_heads, LANES, QB), f32),
                        pltpu.VMEM((n_heads, TM, QB), f32),
                        pltpu.VMEM((n_heads, TM, QB), f32),
                        pltpu.VMEM((n_heads, 1, QB), f32),
                        pltpu.VMEM((n_heads, 1, QB), f32),
                        pltpu.VMEM((2, SUBLANES, QB), f32)],
        compiler_params=_cparams(("arbitrary", "arbitrary")),
        name="dsa",
    )(qat, qit, wit, kiw4, ka4, vat)


def _ret_kernel(q_ref, k_ref, v_ref, g_ref, dec_ref, zeta_ref, xi_ref, gc_ref, o_ref, st_ref, *, n_heads):
    @pl.when(pl.program_id(1) == 0)
    def _():
        st_ref[...] = jnp.zeros_like(st_ref)

    for h in range(n_heads):
        sl = slice(h * HEAD_DIM, (h + 1) * HEAD_DIM)
        q, k, v = q_ref[:, sl], k_ref[:, sl], v_ref[:, sl]
        s = lax.dot_general(q, k, (((1,), (1,)), ((), ())), preferred_element_type=jnp.float32) * dec_ref[h]
        inner = jnp.dot(s.astype(v.dtype), v, preferred_element_type=jnp.float32)
        state = st_ref[h]
        cross = jnp.dot(q, state.astype(q.dtype), preferred_element_type=jnp.float32) * xi_ref[h]
        y = inner + cross
        mu = jnp.mean(y, axis=-1, keepdims=True)
        yc = y - mu
        yn = yc * lax.rsqrt(jnp.mean(yc * yc, axis=-1, keepdims=True) + EPS)
        g = g_ref[:, sl]
        o_ref[:, sl] = (g / (1.0 + jnp.exp(-g)) * yn).astype(o_ref.dtype)
        kz = (k.astype(jnp.float32) * zeta_ref[h]).astype(k.dtype)
        kv = lax.dot_general(kz, v, (((0,), (0,)), ((), ())), preferred_element_type=jnp.float32)
        st_ref[h] = state * gc_ref[h] + kv


def _ret_consts(n_heads):
    log_gamma = jnp.log1p(-jnp.exp2(-5.0 - jnp.arange(n_heads, dtype=jnp.float32)))
    pos = jnp.arange(RET_C, dtype=jnp.float32)
    diff = pos[:, None] - pos[None, :]
    dec = jnp.where(diff[None] >= 0, jnp.exp(jnp.maximum(diff, 0.0)[None] * log_gamma[:, None, None]), 0.0)
    zeta = jnp.exp((RET_C - 1.0 - pos)[None, :] * log_gamma[:, None])
    xi = jnp.exp((pos + 1.0)[None, :] * log_gamma[:, None])
    gc = jnp.exp(RET_C * log_gamma)
    bc = lambda a, shape: jnp.broadcast_to(a, shape).astype(jnp.float32)
    return (dec, bc(zeta[:, :, None], (n_heads, RET_C, HEAD_DIM)), bc(xi[:, :, None], (n_heads, RET_C, HEAD_DIM)),
            bc(gc[:, None, None], (n_heads, HEAD_DIM, HEAD_DIM)))


def _ret(qb, kb, vb, gb, seq):
    n, d_b = qb.shape
    bsz = n // seq
    n_heads = d_b // HEAD_DIM
    n_c = seq // RET_C
    dec, zeta, xi, gc = _ret_consts(n_heads)
    row = lambda b, c: (b * n_c + c, 0)
    const3 = lambda b, c: (0, 0, 0)
    return pl.pallas_call(
        functools.partial(_ret_kernel, n_heads=n_heads),
        grid=(bsz, n_c),
        in_specs=[pl.BlockSpec((RET_C, d_b), row)] * 4 + [
            pl.BlockSpec(dec.shape, const3), pl.BlockSpec(zeta.shape, const3),
            pl.BlockSpec(xi.shape, const3), pl.BlockSpec(gc.shape, const3)],
        out_specs=pl.BlockSpec((RET_C, d_b), row),
        out_shape=jax.ShapeDtypeStruct((n, d_b), jnp.bfloat16),
        scratch_shapes=[pltpu.VMEM((n_heads, HEAD_DIM, HEAD_DIM), jnp.float32)],
        compiler_params=_cparams(("arbitrary", "arbitrary")),
        name="ret",
    )(qb, kb, vb, gb, dec, zeta, xi, gc)


def _tiles_shape(rows, d):
    return (rows // SUBLANES, d // LANES, SUBLANES, LANES)


def _to_tiles(ref, x):
    for s in range(ref.shape[1]):
        ref[:, s] = x[:, s * LANES:(s + 1) * LANES].reshape(ref.shape[0], SUBLANES, LANES)


def _from_tiles(ref):
    rows = ref.shape[0] * SUBLANES
    return jnp.concatenate([ref[:, s].reshape(rows, LANES) for s in range(ref.shape[1])], axis=1)


def _lane_first_eq(x, m, lane):
    return jnp.min(jnp.where(x == m, lane, float(LANES)), axis=1, keepdims=True)


def _mix_out_kernel(x_ref, ya_ref, yb_ref, woa_ref, wob_ref, gt_ref, sc_ref, sh_ref, g_ref, wr_ref, tri_ref,
                    x1_ref, h2_ref, route_ref, cnt_ref, carry_ref):
    @pl.when(pl.program_id(0) == 0)
    def _():
        carry_ref[...] = jnp.zeros_like(carry_ref)

    mix = (jnp.dot(ya_ref[...], woa_ref[...], preferred_element_type=jnp.float32)
           + jnp.dot(yb_ref[...], wob_ref[...], preferred_element_type=jnp.float32))
    x1 = x_ref[...] + gt_ref[0] * mix
    x1_ref[...] = x1
    h2 = _rmsnorm_mod(x1, g_ref[...], sc_ref[0], sh_ref[0])
    _to_tiles(h2_ref, h2)

    lg = jnp.dot(h2.astype(jnp.bfloat16), wr_ref[...], preferred_element_type=jnp.float32)
    lane = lax.broadcasted_iota(jnp.int32, lg.shape, 1).astype(jnp.float32)
    is_grp = (lane >= N_EXPERTS) & (lane < N_EXPERTS + N_GROUPS)
    gl = jnp.where(is_grp, lg, -jnp.inf)
    gmax = jnp.max(gl, axis=1, keepdims=True)
    grp = _lane_first_eq(gl, gmax, lane) - N_EXPERTS
    p_grp = 1.0 / jnp.sum(jnp.exp(gl - gmax), axis=1, keepdims=True)
    in_grp = jnp.floor(lane * (1.0 / EXPERTS_PER_GROUP)) == grp
    f = jnp.where(in_grp & (lane < N_EXPERTS), lg, -jnp.inf)
    f1 = jnp.max(f, axis=1, keepdims=True)
    e1 = _lane_first_eq(f, f1, lane)
    f = jnp.where(lane == e1, -jnp.inf, f)
    f2 = jnp.max(f, axis=1, keepdims=True)
    e2 = _lane_first_eq(f, f2, lane)
    a2 = jnp.exp(f2 - f1)
    w1 = p_grp / (1.0 + a2)
    w2 = p_grp * a2 / (1.0 + a2)

    oh1 = jnp.where(lane == e1, 1.0, 0.0)
    oh2 = jnp.where(lane == e2, 1.0, 0.0)
    both = oh1 + oh2
    before = jnp.dot(tri_ref[...], both.astype(jnp.bfloat16), preferred_element_type=jnp.float32) + carry_ref[...]
    r1 = jnp.sum(before * oh1, axis=1, keepdims=True)
    r2 = jnp.sum(before * oh2, axis=1, keepdims=True)
    carry = carry_ref[...] + jnp.sum(both, axis=0, keepdims=True)
    carry_ref[...] = carry
    cnt_ref[...] = carry

    out = jnp.zeros(lg.shape, jnp.float32)
    for col, val in enumerate((e1, e2, w1, w2, r1, r2)):
        out = jnp.where(lane == col, val, out)
    route_ref[...] = out


def _mix_out(x2, ya, yb, wo_a, wo_b, mod3, g_ffn, w_route, seq):
    n, d = x2.shape
    tiles_per_seq = seq // TM
    tri = jnp.asarray(np.tril(np.ones((TM, TM), np.float32), -1), jnp.bfloat16)
    row = lambda i: (i, 0)
    const = lambda i: (0, 0)
    modk = lambda k: pl.BlockSpec((1, 1, d), lambda i: ((i // tiles_per_seq) * 6 + k, 0, 0))
    return pl.pallas_call(
        _mix_out_kernel,
        grid=(n // TM,),
        in_specs=[pl.BlockSpec((TM, d), row),
                  pl.BlockSpec((TM, ya.shape[1]), row),
                  pl.BlockSpec((TM, yb.shape[1]), row),
                  pl.BlockSpec(wo_a.shape, const),
                  pl.BlockSpec(wo_b.shape, const),
                  modk(2), modk(4), modk(3),
                  pl.BlockSpec((1, d), const),
                  pl.BlockSpec(w_route.shape, const),
                  pl.BlockSpec(tri.shape, const)],
        out_specs=(pl.BlockSpec((TM, d), row), pl.BlockSpec(_tiles_shape(TM, d), lambda i: (i, 0, 0, 0)),
                   pl.BlockSpec((TM, LANES), row), pl.BlockSpec((1, LANES), const)),
        out_shape=(jax.ShapeDtypeStruct((n, d), jnp.float32), jax.ShapeDtypeStruct(_tiles_shape(n, d), jnp.float32),
                   jax.ShapeDtypeStruct((n, LANES), jnp.float32), jax.ShapeDtypeStruct((1, LANES), jnp.float32)),
        scratch_shapes=[pltpu.VMEM((1, LANES), jnp.float32)],
        compiler_params=_cparams(("arbitrary",)),
        name="mix_out",
    )(x2, ya, yb, wo_a, wo_b, mod3, mod3, mod3, g_ffn, w_route, tri)


def _plan_kernel(route_ref, pst_ref, dest_ref):
    r = route_ref[...]
    lane = lax.broadcasted_iota(jnp.int32, r.shape, 1).astype(jnp.float32)
    pst = pst_ref[...]
    d1 = jnp.sum(jnp.where(lane == r[:, 0:1], pst, 0.0), axis=1, keepdims=True) + r[:, 4:5]
    d2 = jnp.sum(jnp.where(lane == r[:, 1:2], pst, 0.0), axis=1, keepdims=True) + r[:, 5:6]
    t1 = jnp.floor(d1 * (1.0 / SUBLANES))
    t2 = jnp.floor(d2 * (1.0 / SUBLANES))
    packed = jnp.zeros(r.shape, jnp.float32)
    for k, v in enumerate((t1, d1 - t1 * SUBLANES, t2, d2 - t2 * SUBLANES)):
        packed = jnp.where(lane == float(k), v, packed)
    dest_ref[0] = packed.T[0:8, :].astype(jnp.int32)


def _plan(route, pst_row):
    n = route.shape[0]
    return pl.pallas_call(
        _plan_kernel,
        grid=(n // TM,),
        in_specs=[pl.BlockSpec((TM, LANES), lambda i: (i, 0)), pl.BlockSpec((1, LANES), lambda i: (0, 0))],
        out_specs=pl.BlockSpec((1, 8, TM), lambda i: (i, 0, 0)),
        out_shape=jax.ShapeDtypeStruct((n // TM, 8, TM), jnp.int32),
        compiler_params=_cparams(("arbitrary",)),
        name="plan",
    )(route, pst_row)


def _dispatch_kernel(zrow_ref, n_used_ref, dest_ref, h2_ref, xs_hbm, zbuf, sem, zsem, *, n_blk):
    i = pl.program_id(0)

    @pl.when(i == 0)
    def _():
        zbuf[...] = jnp.zeros(zbuf.shape, zbuf.dtype)
        blk_tiles = BM // SUBLANES
        for e in range(N_EXPERTS):
            pltpu.make_async_copy(zbuf, xs_hbm.at[pl.ds(zrow_ref[e], blk_tiles)], zsem).start()
        for e in range(N_EXPERTS):
            pltpu.make_async_copy(zbuf, xs_hbm.at[pl.ds(0, blk_tiles)], zsem).wait()
        for b in range(N_EXPERTS + 1):
            @pl.when(n_used_ref[0] + b <= n_blk)
            def _():
                tail = pltpu.make_async_copy(zbuf, xs_hbm.at[pl.ds((n_used_ref[0] + b) * blk_tiles, blk_tiles)], zsem)
                tail.start()
                tail.wait()

    def issue(k, c):
        for u in range(SUBLANES):
            r = k * SUBLANES + u
            row = h2_ref.at[k, :, u, :]
            pltpu.make_async_copy(row, xs_hbm.at[dest_ref[0, 0, r], :, dest_ref[0, 1, r], :], sem).start()
            pltpu.make_async_copy(row, xs_hbm.at[dest_ref[0, 2, r], :, dest_ref[0, 3, r], :], sem).start()
        return c

    lax.fori_loop(0, TM // SUBLANES, issue, 0)
    for _ in range(2):
        pltpu.make_async_copy(h2_ref, xs_hbm.at[pl.ds(0, TM // SUBLANES)], sem).wait()


def _dispatch(h2, dest, zrow, n_used, n_blk):
    n_rows = (n_blk + 1) * BM
    d = h2.shape[1] * LANES
    grid_spec = pltpu.PrefetchScalarGridSpec(
        num_scalar_prefetch=2,
        grid=(h2.shape[0] * SUBLANES // TM,),
        in_specs=[pl.BlockSpec((1, 8, TM), lambda i, z, nu: (i, 0, 0), memory_space=pltpu.SMEM),
                  pl.BlockSpec(_tiles_shape(TM, d), lambda i, z, nu: (i, 0, 0, 0))],
        out_specs=pl.BlockSpec(memory_space=pl.ANY),
        scratch_shapes=[pltpu.VMEM(_tiles_shape(BM, d), h2.dtype),
                        pltpu.SemaphoreType.DMA(()),
                        pltpu.SemaphoreType.DMA(())],
    )
    return pl.pallas_call(
        functools.partial(_dispatch_kernel, n_blk=n_blk),
        grid_spec=grid_spec,
        out_shape=jax.ShapeDtypeStruct(_tiles_shape(n_rows, d), h2.dtype),
        compiler_params=_cparams(("arbitrary",)),
        name="dispatch",
    )(zrow, n_used, dest, h2)


def _experts_kernel(blk_e_ref, n_used_ref, x_ref, wg_ref, wu_ref, wd_ref, y_ref):
    j = pl.program_id(0)

    @pl.when(j < n_used_ref[0])
    def _():
        x = _from_tiles(x_ref).astype(wg_ref.dtype)
        a = jnp.dot(x, wg_ref[0], preferred_element_type=jnp.float32)
        b = jnp.dot(x, wu_ref[0], preferred_element_type=jnp.float32)
        hmid = (a / (1.0 + jnp.exp(-a)) * b).astype(x.dtype)
        _to_tiles(y_ref, jnp.dot(hmid, wd_ref[0], preferred_element_type=jnp.float32))

    @pl.when(j >= n_used_ref[0])
    def _():
        y_ref[...] = jnp.zeros(y_ref.shape, y_ref.dtype)


def _experts(xs, wg, wu, wd, blk_e, n_used):
    n_blk = blk_e.shape[0]
    d, d_e = wg.shape[1], wg.shape[2]
    blk = _tiles_shape(BM, d)
    grid_spec = pltpu.PrefetchScalarGridSpec(
        num_scalar_prefetch=2,
        grid=(n_blk,),
        in_specs=[pl.BlockSpec(blk, lambda j, be, nu: (jnp.minimum(j, nu[0] - 1), 0, 0, 0)),
                  pl.BlockSpec((1, d, d_e), lambda j, be, nu: (be[j], 0, 0)),
                  pl.BlockSpec((1, d, d_e), lambda j, be, nu: (be[j], 0, 0)),
                  pl.BlockSpec((1, d_e, d), lambda j, be, nu: (be[j], 0, 0))],
        out_specs=pl.BlockSpec(blk, lambda j, be, nu: (j, 0, 0, 0)),
    )
    return pl.pallas_call(
        _experts_kernel,
        grid_spec=grid_spec,
        out_shape=jax.ShapeDtypeStruct(_tiles_shape(n_blk * BM, d), jnp.float32),
        compiler_params=_cparams(("arbitrary",)),
        name="experts",
    )(blk_e, n_used, xs, wg, wu, wd)


def _final_kernel(dest_ref, dest_next_ref, x1_ref, route_ref, gt_ref, g_ref, ys_hbm, o_ref, ybuf, sem):
    i = pl.program_id(0)
    slot = i % 2

    tiles = TM // SUBLANES

    def gather(d_ref, s):
        def issue(k, c):
            for u in range(SUBLANES):
                r = k * SUBLANES + u
                pltpu.make_async_copy(ys_hbm.at[d_ref[0, 0, r], :, d_ref[0, 1, r], :],
                                      ybuf.at[s, k, :, u, :], sem.at[s]).start()
                pltpu.make_async_copy(ys_hbm.at[d_ref[0, 2, r], :, d_ref[0, 3, r], :],
                                      ybuf.at[s, tiles + k, :, u, :], sem.at[s]).start()
            return c
        lax.fori_loop(0, tiles, issue, 0)

    @pl.when(i == 0)
    def _():
        gather(dest_ref, 0)

    @pl.when(i + 1 < pl.num_programs(0))
    def _():
        gather(dest_next_ref, 1 - slot)

    pltpu.make_async_copy(ys_hbm.at[pl.ds(0, 2 * tiles)], ybuf.at[slot], sem.at[slot]).wait()
    w1 = route_ref[:, 2:3]
    w2 = route_ref[:, 3:4]
    y0 = _from_tiles(ybuf.at[slot, 0:tiles])
    y1 = _from_tiles(ybuf.at[slot, tiles:2 * tiles])
    x2 = x1_ref[...] + gt_ref[0] * (w1 * y0 + w2 * y1)
    o_ref[...] = x2 * lax.rsqrt(jnp.mean(x2 * x2, axis=-1, keepdims=True) + EPS) * g_ref[...]


def _final(x1, ys, dest, route, mod3, g_final, seq):
    n, d = x1.shape
    n_tiles = n // TM
    tiles_per_seq = seq // TM
    row = lambda i: (i, 0)
    return pl.pallas_call(
        _final_kernel,
        grid=(n_tiles,),
        in_specs=[pl.BlockSpec((1, 8, TM), lambda i: (i, 0, 0), memory_space=pltpu.SMEM),
                  pl.BlockSpec((1, 8, TM), lambda i: (jnp.minimum(i + 1, n_tiles - 1), 0, 0),
                               memory_space=pltpu.SMEM),
                  pl.BlockSpec((TM, d), row),
                  pl.BlockSpec((TM, LANES), row),
                  pl.BlockSpec((1, 1, d), lambda i: ((i // tiles_per_seq) * 6 + 5, 0, 0)),
                  pl.BlockSpec((1, d), lambda i: (0, 0)),
                  pl.BlockSpec(memory_space=pl.ANY)],
        out_specs=pl.BlockSpec((TM, d), row),
        out_shape=jax.ShapeDtypeStruct((n, d), jnp.float32),
        scratch_shapes=[pltpu.VMEM((2,) + _tiles_shape(2 * TM, d), jnp.float32),
                        pltpu.SemaphoreType.DMA((2,))],
        compiler_params=_cparams(("arbitrary",)),
        name="final",
    )(dest, dest, x1, route, mod3, g_final, ys)


def _block_layout(counts, n):
    cnt = counts[0, :N_EXPERTS].astype(jnp.int32)
    blocks = (cnt + BM - 1) // BM
    bends = jnp.cumsum(blocks)
    pstarts = (bends - blocks) * BM
    n_blk = (2 * n) // BM + N_EXPERTS
    pst_row = jnp.zeros((1, LANES), jnp.float32).at[0, :N_EXPERTS].set(pstarts.astype(jnp.float32))
    zrow = ((pstarts + cnt) // SUBLANES).astype(jnp.int32)
    n_used = bends[-1:].astype(jnp.int32)
    blk_e = jnp.sum(bends[None, :] <= jnp.arange(n_blk, dtype=jnp.int32)[:, None], axis=1)
    blk_e = jnp.minimum(blk_e, N_EXPERTS - 1).astype(jnp.int32)
    return pst_row, zrow, n_used, blk_e


def _layer(x2, c, pos2, w_ada, b_ada, g_mix, w_in, w_o, g_ffn, w_rg, w_re, w_up, w_gate, w_down, seq):
    n, d = x2.shape
    bsz = n // seq
    bf = jnp.bfloat16
    d_a = d // 2
    d_i = N_IDX_HEADS * IDX_DIM
    mod3 = _ada(c, w_ada, b_ada).reshape(bsz * 6, 1, d)

    c0 = 3 * d_a + d_i
    c1 = c0 + IDX_DIM + N_IDX_HEADS
    w_cat = jnp.concatenate([w_in[:, :c0], w_in[:, c0:c1],
                             jnp.zeros((d, LANES - (c1 - c0)), w_in.dtype), w_in[:, c1:]], axis=1).astype(bf)
    qat, ka, vat, qit, kiw, wit, qb, kb, vb, gb = _proj(x2, pos2, mod3, g_mix.reshape(1, d), w_cat, seq)

    n_kt = seq // TM
    ya = _dsa(qat, qit, wit, kiw.reshape(bsz, n_kt, TM, LANES), ka.reshape(bsz, n_kt, TM, d_a), vat, seq)
    yb = _ret(qb, kb, vb, gb, seq)

    w_route = jnp.concatenate([jnp.transpose(w_re, (1, 0, 2)).reshape(d, N_EXPERTS), w_rg,
                               jnp.zeros((d, LANES - N_EXPERTS - N_GROUPS), w_rg.dtype)], axis=1).astype(bf)
    x1, h2, route, counts = _mix_out(x2, ya, yb, w_o[:d_a].astype(bf), w_o[d_a:].astype(bf), mod3,
                                     g_ffn.reshape(1, d), w_route, seq)

    pst_row, zrow, n_used, blk_e = _block_layout(counts, n)
    dest = _plan(route, pst_row)
    xs = _dispatch(h2, dest, zrow, n_used, blk_e.shape[0])
    ys = _experts(xs, w_gate.astype(bf), w_up.astype(bf), w_down.astype(bf), blk_e, n_used)
    return x1, ys, dest, route, mod3


def kernel(x, c, positions, w_ada, b_ada, g_norm_mix, w_in, w_o, g_norm_ffn, w_router_group, w_router_expert,
           w_up, w_gate, w_down, g_norm_final):
    bsz, seq, d = x.shape
    depth = w_ada.shape[0]
    assert depth == 1, "the final norm is fused into the last layer's combine kernel"
    assert seq % (2 * TM) == 0 and seq % RET_C == 0 and (2 * bsz * seq) % BM == 0
    x2 = x.reshape(bsz * seq, d)
    pos2 = positions.astype(jnp.float32).reshape(bsz * seq, 1)
    x1, ys, dest, route, mod3 = _layer(x2, c, pos2, w_ada[0], b_ada[0], g_norm_mix[0], w_in[0], w_o[0],
                                       g_norm_ffn[0], w_router_group[0], w_router_expert[0], w_up[0], w_gate[0],
                                       w_down[0], seq)
    out = _final(x1, ys, dest, route, mod3, g_norm_final.reshape(1, d), seq)
    return out.reshape(bsz, seq, d)
```

```python
import functools

import jax
import jax.numpy as jnp
import numpy as np
from jax import lax
from jax.experimental import pallas as pl
from jax.experimental.pallas import tpu as pltpu

CHUNK = 64
HEAD_DIM = 64
N_IDX_HEADS = 16
IDX_DIM = 64
TOPK_MAX = 256
ROPE_THETA = 500000.0
ROT_DIM = HEAD_DIM // 4
RET_THETA = 10000.0
N_GROUPS = 4
EXPERTS_PER_GROUP = 8
N_EXPERTS = N_GROUPS * EXPERTS_PER_GROUP
EPS = 1e-6

LANES = 128
SUBLANES = 8
VMEM_LIMIT = 56 * 1024 * 1024

TM = 256
QB = TM
RET_C = 256
BM = 256
CNT_ROWS = 64
ONES_ROWS = 16
HEAD_GROUP = 4
IDX_TILES = 4
BISECT_PER_CHECK = 2
MAX_BISECT = 40

NEG_BIG = -1e30
LOG2E = 1.4426950408889634


def _cparams(sem):
    return pltpu.CompilerParams(dimension_semantics=sem, vmem_limit_bytes=VMEM_LIMIT)


def _ada_kernel(c_ref, w_ref, b_ref, o_ref):
    o_ref[...] = jnp.dot(c_ref[...], w_ref[...], preferred_element_type=jnp.float32) + b_ref[...]


def _ada(c, w_ada, b_ada):
    bsz, d = c.shape
    n_out = w_ada.shape[1]
    return pl.pallas_call(
        _ada_kernel,
        grid=(n_out // d,),
        in_specs=[pl.BlockSpec((bsz, d), lambda j: (0, 0)),
                  pl.BlockSpec((d, d), lambda j: (0, j)),
                  pl.BlockSpec((1, d), lambda j: (0, j))],
        out_specs=pl.BlockSpec((bsz, d), lambda j: (0, j)),
        out_shape=jax.ShapeDtypeStruct((bsz, n_out), jnp.float32),
        compiler_params=_cparams(("arbitrary",)),
        name="ada",
    )(c, w_ada, b_ada.reshape(1, n_out))


def _rmsnorm_mod(x, g, sc, sh):
    xn = x * lax.rsqrt(jnp.mean(x * x, axis=-1, keepdims=True) + EPS)
    return xn * g * (1.0 + sc) + sh


def _rope_lanes(x, cos, sin_lo, sin_hi, half):
    cols = []
    for k in range(x.shape[1] // LANES):
        xb = x[:, k * LANES:(k + 1) * LANES]
        cols.append(xb * cos + pltpu.roll(xb, LANES - half, 1) * sin_lo + pltpu.roll(xb, half, 1) * sin_hi)
    return cols[0] if len(cols) == 1 else jnp.concatenate(cols, axis=1)


def _proj_kernel(x_ref, pos_ref, sc_ref, sh_ref, g_ref, w_ref, tab_ref,
                 qat_ref, ka_ref, vat_ref, qit_ref, kiw_ref, wit_ref, qb_ref, kb_ref, vb_ref, gb_ref,
                 *, d_a, d_i, d_b):
    h = _rmsnorm_mod(x_ref[...], g_ref[...], sc_ref[0], sh_ref[0]).astype(jnp.bfloat16)
    pos = pos_ref[...]
    ang_a = pos * tab_ref[0:1, :]
    cos_a, sin_a = jnp.cos(ang_a), jnp.sin(ang_a)
    sa_lo, sa_hi = sin_a * tab_ref[1:2, :], sin_a * tab_ref[2:3, :]
    ang_b = pos * tab_ref[3:4, :]
    cos_b, sin_b = jnp.cos(ang_b), jnp.sin(ang_b)
    sb_lo, sb_hi = sin_b * tab_ref[4:5, :], sin_b * tab_ref[5:6, :]
    half_a, half_b = ROT_DIM // 2, HEAD_DIM // 2

    def seg(lo, width):
        return jnp.dot(h, w_ref[:, lo:lo + width], preferred_element_type=jnp.float32)

    o = 0
    qa = seg(o, d_a); o += d_a
    qat_ref[0, 0] = (_rope_lanes(qa, cos_a, sa_lo, sa_hi, half_a) * (HEAD_DIM ** -0.5 * LOG2E)).T.astype(qat_ref.dtype)
    ka = seg(o, d_a); o += d_a
    ka_ref[...] = _rope_lanes(ka, cos_a, sa_lo, sa_hi, half_a).astype(ka_ref.dtype)
    vat_ref[0, 0] = seg(o, d_a).T.astype(vat_ref.dtype); o += d_a
    qi = seg(o, d_i); o += d_i
    qit_ref[0, 0] = (_rope_lanes(qi, cos_a, sa_lo, sa_hi, half_a) * (IDX_DIM ** -0.5)).T.astype(qit_ref.dtype)
    kw = seg(o, LANES); o += LANES
    kiw_ref[...] = _rope_lanes(kw, cos_a, sa_lo, sa_hi, half_a).astype(kiw_ref.dtype)
    wit_ref[0, 0] = kw.T[IDX_DIM:IDX_DIM + N_IDX_HEADS, :] * (N_IDX_HEADS ** -0.5)
    qb = seg(o, d_b); o += d_b
    qb_ref[...] = _rope_lanes(qb, cos_b, sb_lo, sb_hi, half_b).astype(qb_ref.dtype)
    kb = seg(o, d_b); o += d_b
    kb_ref[...] = (_rope_lanes(kb, cos_b, sb_lo, sb_hi, half_b) * (HEAD_DIM ** -0.5)).astype(kb_ref.dtype)
    vb_ref[...] = seg(o, d_b).astype(vb_ref.dtype); o += d_b
    gb_ref[...] = seg(o, d_b)


def _rope_tables():
    lane = jnp.arange(LANES) % HEAD_DIM
    rows = []
    for rot, theta in ((ROT_DIM, ROPE_THETA), (HEAD_DIM, RET_THETA)):
        half = rot // 2
        inv_freq = theta ** (-jnp.arange(half, dtype=jnp.float32) / half)
        rows.append(jnp.where(lane < rot, inv_freq[lane % half], 0.0))
        rows.append(jnp.where(lane < half, -1.0, 0.0))
        rows.append(jnp.where((lane >= half) & (lane < rot), 1.0, 0.0))
    rows += [jnp.zeros((LANES,), jnp.float32)] * 2
    return jnp.stack(rows).astype(jnp.float32)


def _proj(x2, pos2, mod3, g_mix, w_cat, seq):
    n, d = x2.shape
    bsz = n // seq
    d_a = d // 2
    d_b = d // 2
    d_i = N_IDX_HEADS * IDX_DIM
    tiles_per_seq = seq // TM
    tab = _rope_tables()
    row = lambda i: (i, 0)
    const = lambda i: (0, 0)
    tile4 = lambda i: (i // tiles_per_seq, i % tiles_per_seq, 0, 0)
    bf = jnp.bfloat16
    out_shape = (
        jax.ShapeDtypeStruct((bsz, tiles_per_seq, d_a, TM), bf),
        jax.ShapeDtypeStruct((n, d_a), bf),
        jax.ShapeDtypeStruct((bsz, tiles_per_seq, d_a, TM), bf),
        jax.ShapeDtypeStruct((bsz, tiles_per_seq, d_i, TM), bf),
        jax.ShapeDtypeStruct((n, LANES), bf),
        jax.ShapeDtypeStruct((bsz, tiles_per_seq, N_IDX_HEADS, TM), jnp.float32),
        jax.ShapeDtypeStruct((n, d_b), bf),
        jax.ShapeDtypeStruct((n, d_b), bf),
        jax.ShapeDtypeStruct((n, d_b), bf),
        jax.ShapeDtypeStruct((n, d_b), jnp.float32),
    )
    out_specs = (
        pl.BlockSpec((1, 1, d_a, TM), tile4),
        pl.BlockSpec((TM, d_a), row),
        pl.BlockSpec((1, 1, d_a, TM), tile4),
        pl.BlockSpec((1, 1, d_i, TM), tile4),
        pl.BlockSpec((TM, LANES), row),
        pl.BlockSpec((1, 1, N_IDX_HEADS, TM), tile4),
        pl.BlockSpec((TM, d_b), row),
        pl.BlockSpec((TM, d_b), row),
        pl.BlockSpec((TM, d_b), row),
        pl.BlockSpec((TM, d_b), row),
    )
    return pl.pallas_call(
        functools.partial(_proj_kernel, d_a=d_a, d_i=d_i, d_b=d_b),
        grid=(n // TM,),
        in_specs=[pl.BlockSpec((TM, d), row),
                  pl.BlockSpec((TM, 1), row),
                  pl.BlockSpec((1, 1, d), lambda i: ((i // tiles_per_seq) * 6 + 1, 0, 0)),
                  pl.BlockSpec((1, 1, d), lambda i: ((i // tiles_per_seq) * 6 + 0, 0, 0)),
                  pl.BlockSpec((1, d), const),
                  pl.BlockSpec(w_cat.shape, const),
                  pl.BlockSpec(tab.shape, const)],
        out_specs=out_specs,
        out_shape=out_shape,
        compiler_params=_cparams(("arbitrary",)),
        name="proj",
    )(x2, pos2, mod3, mod3, g_mix, w_cat, tab)


def _row_blocks(x, rows):
    return [x[r * rows:(r + 1) * rows] for r in range(x.shape[0] // rows)]


def _dsa_kernel(qat_ref, qit_ref, wit_ref, kiw_ref, ka_ref, vat_ref, o_ref,
                qix_ref, qmx_ref, sc_ref, m_ref, l_ref, acc_ref, sa_ref, sb_ref, mxa_ref, mxb_ref, lohi_ref,
                *, k_top, n_heads):
    i = pl.program_id(1)
    n_grp = (i * QB + QB + 2 * TM - 1) // (2 * TM)
    n_kt = 2 * n_grp
    n_real = (i * QB + QB + TM - 1) // TM

    zero_rows = jnp.zeros((LANES - IDX_DIM, QB), qix_ref.dtype)
    for h in range(N_IDX_HEADS):
        qix_ref[h] = jnp.concatenate([qit_ref[0, 0, h * IDX_DIM:(h + 1) * IDX_DIM, :], zero_rows], axis=0)
    row_q = lax.broadcasted_iota(jnp.int32, (LANES, QB), 0)
    for h in range(n_heads):
        pair = qat_ref[0, 0, (h // 2) * LANES:(h // 2 + 1) * LANES, :]
        own = (row_q < HEAD_DIM) if h % 2 == 0 else (row_q >= HEAD_DIM)
        qmx_ref[h] = jnp.where(own, pair, jnp.zeros_like(pair))

    q_chunk = (i * QB + lax.broadcasted_iota(jnp.int32, (TM, QB), 1)) // CHUNK
    key_in_tile = lax.broadcasted_iota(jnp.int32, (TM, QB), 0)
    w_all = wit_ref[0, 0]

    def idx_tiles(tiles):
        lo, hi = lohi_ref[0], lohi_ref[1]
        for j in tiles:
            kt = kiw_ref[0, j]
            acc = None
            for h in range(N_IDX_HEADS):
                d = jnp.dot(kt, qix_ref[h], preferred_element_type=jnp.float32)
                t = w_all[h:h + 1, :] * jnp.maximum(d, 0.0)
                acc = t if acc is None else acc + t
            adm = (j * TM + key_in_tile) // CHUNK <= q_chunk
            sc_ref[j] = jnp.where(adm, acc, -jnp.inf)
            lo = jnp.minimum(lo, functools.reduce(jnp.minimum, _row_blocks(jnp.where(adm, acc, jnp.inf), SUBLANES)))
            hi = jnp.maximum(hi, functools.reduce(jnp.maximum, _row_blocks(jnp.where(adm, acc, -jnp.inf), SUBLANES)))
        lohi_ref[0], lohi_ref[1] = lo, hi

    lohi_ref[0] = jnp.full((SUBLANES, QB), jnp.inf, jnp.float32)
    lohi_ref[1] = jnp.full((SUBLANES, QB), -jnp.inf, jnp.float32)

    def idx_step(g, carry):
        idx_tiles([IDX_TILES * g + u for u in range(IDX_TILES)])
        return carry

    lax.fori_loop(0, n_kt // IDX_TILES, idx_step, 0)

    @pl.when(n_kt % IDX_TILES != 0)
    def _():
        idx_tiles([n_kt - 2, n_kt - 1])

    lo = jnp.min(lohi_ref[0], axis=0, keepdims=True)
    hi = jnp.max(lohi_ref[1], axis=0, keepdims=True)
    n_adm = ((i * QB + lax.broadcasted_iota(jnp.int32, (1, QB), 1)) // CHUNK + 1) * CHUNK

    def bisect_cond(carry):
        it, _, _, _, unsettled = carry
        return (it < MAX_BISECT) & (unsettled > 0.0)

    def bisect(carry):
        it, lo, hi, cnt_lo, _ = carry
        for _ in range(BISECT_PER_CHECK):
            mid = lo + (hi - lo) * 0.5
            mid_b = jnp.broadcast_to(mid, (CNT_ROWS, QB))

            def count_tile(j, cnt, mid_b=mid_b):
                for blk in _row_blocks(sc_ref[j], CNT_ROWS):
                    cnt = cnt + jnp.where(blk >= mid_b, 1.0, 0.0)
                return cnt

            cnt = lax.fori_loop(0, n_real, count_tile, jnp.zeros((CNT_ROWS, QB), jnp.float32))
            c = jnp.sum(cnt, axis=0, keepdims=True)
            ge = c >= k_top
            lo = jnp.where(ge, mid, lo)
            hi = jnp.where(ge, hi, mid)
            cnt_lo = jnp.where(ge, c, cnt_lo)
        return it + BISECT_PER_CHECK, lo, hi, cnt_lo, jnp.max(jnp.where(cnt_lo > k_top, 1.0, 0.0))

    cnt0 = n_adm.astype(jnp.float32)
    _, thr, _, _, _ = lax.while_loop(bisect_cond, bisect,
                                     (0, lo, hi, cnt0, jnp.max(jnp.where(cnt0 > k_top, 1.0, 0.0))))

    def bias_tile(j, carry):
        sc_ref[j] = jnp.where(sc_ref[j] >= thr, 0.0, NEG_BIG)
        return carry

    lax.fori_loop(0, n_kt, bias_tile, 0)

    m_ref[...] = jnp.full(m_ref.shape, NEG_BIG, jnp.float32)
    l_ref[...] = jnp.zeros(l_ref.shape, jnp.float32)
    acc_ref[...] = jnp.zeros(acc_ref.shape, jnp.float32)

    def pair(h):
        return slice((h // 2) * LANES, (h // 2 + 1) * LANES)

    def logits_into(s_ref, mx_ref, j, heads):
        bias = sc_ref[j]
        for h in heads:
            s = jnp.dot(ka_ref[0, j, :, pair(h)], qmx_ref[h], preferred_element_type=jnp.float32) + bias
            s_ref[h] = s
            mx_ref[h] = jnp.max(s, axis=0, keepdims=True)

    ones_rows = jnp.ones((ONES_ROWS, TM), vat_ref.dtype)

    def absorb(s_ref, mx_ref, j, heads):
        for h in heads:
            m_old = m_ref[h]
            m_new = jnp.maximum(m_old, mx_ref[h])
            alpha = jnp.exp2(m_old - m_new)
            p = jnp.exp2(s_ref[h] - m_new).astype(vat_ref.dtype)
            pv = jnp.dot(jnp.concatenate([vat_ref[0, j, pair(h), :], ones_rows], axis=0), p,
                         preferred_element_type=jnp.float32)
            acc_ref[h] = acc_ref[h] * alpha + pv[0:LANES]
            l_ref[h] = l_ref[h] * alpha + pv[LANES:LANES + SUBLANES]
            m_ref[h] = m_new

    for h0 in range(0, n_heads, HEAD_GROUP):
        heads = range(h0, h0 + HEAD_GROUP)
        logits_into(sa_ref, mxa_ref, 0, heads)

        def attn_grp(g, carry, heads=heads):
            logits_into(sb_ref, mxb_ref, 2 * g + 1, heads)
            absorb(sa_ref, mxa_ref, 2 * g, heads)
            logits_into(sa_ref, mxa_ref, jnp.minimum(2 * g + 2, n_kt - 1), heads)
            absorb(sb_ref, mxb_ref, 2 * g + 1, heads)
            return carry

        lax.fori_loop(0, n_grp, attn_grp, 0)

    for hp in range(n_heads // 2):
        even = acc_ref[2 * hp] / l_ref[2 * hp, 0:1, :]
        odd = acc_ref[2 * hp + 1] / l_ref[2 * hp + 1, 0:1, :]
        o_ref[:, hp * LANES:(hp + 1) * LANES] = jnp.where(row_q < HEAD_DIM, even, odd).T.astype(o_ref.dtype)


def _dsa(qat, qit, wit, kiw4, ka4, vat, seq):
    bsz, n_kt, d_a, _ = qat.shape
    n_heads = d_a // HEAD_DIM
    n_qb = seq // QB
    k_top = min(TOPK_MAX, seq // 4)
    qtile = lambda b, i: (b, i, 0, 0)
    per_b = lambda b, i: (b, 0, 0, 0)
    f32 = jnp.float32
    return pl.pallas_call(
        functools.partial(_dsa_kernel, k_top=float(k_top), n_heads=n_heads),
        grid=(bsz, n_qb),
        in_specs=[pl.BlockSpec((1, 1, d_a, QB), qtile),
                  pl.BlockSpec((1, 1, qit.shape[2], QB), qtile),
                  pl.BlockSpec((1, 1, N_IDX_HEADS, QB), qtile),
                  pl.BlockSpec((1, n_kt, TM, LANES), per_b, pipeline_mode=pl.Buffered(1)),
                  pl.BlockSpec((1, n_kt, TM, d_a), per_b, pipeline_mode=pl.Buffered(1)),
                  pl.BlockSpec((1, n_kt, d_a, TM), per_b, pipeline_mode=pl.Buffered(1))],
        out_specs=pl.BlockSpec((QB, d_a), lambda b, i: (b * n_qb + i, 0)),
        out_shape=jax.ShapeDtypeStruct((bsz * seq, d_a), jnp.bfloat16),
        scratch_shapes=[pltpu.VMEM((N_IDX_HEADS, LANES, QB), jnp.bfloat16),
                        pltpu.VMEM((n_heads, LANES, QB), jnp.bfloat16),
                        pltpu.VMEM((n_kt, TM, QB), f32),
                        pltpu.VMEM((n_heads, 1, QB), f32),
                        pltpu.VMEM((n_heads, SUBLANES, QB), f32),
                        pltpu.VMEM((n_heads, LANES, QB), f32),
                        pltpu.VMEM((n_heads, TM, QB), f32),
                        pltpu.VMEM((n_heads, TM, QB), f32),
                        pltpu.VMEM((n_heads, 1, QB), f32),
                        pltpu.VMEM((n_heads, 1, QB), f32),
                        pltpu.VMEM((2, SUBLANES, QB), f32)],
        compiler_params=_cparams(("arbitrary", "arbitrary")),
        name="dsa",
    )(qat, qit, wit, kiw4, ka4, vat)


def _group_mean(y, avg):
    hi = y.astype(jnp.bfloat16)
    lo = (y - hi.astype(jnp.float32)).astype(jnp.bfloat16)
    return (jnp.dot(hi, avg, preferred_element_type=jnp.float32)
            + jnp.dot(lo, avg, preferred_element_type=jnp.float32))


def _ret_kernel(q_ref, k_ref, v_ref, g_ref, dec_ref, zt_ref, xi_ref, gc_ref, blk_ref, avg_ref, o_ref, st_ref,
                *, n_heads):
    @pl.when(pl.program_id(1) == 0)
    def _():
        st_ref[...] = jnp.zeros_like(st_ref)

    even = lax.broadcasted_iota(jnp.int32, (RET_C, LANES), 1) < HEAD_DIM
    avg = avg_ref[...]
    for p in range(n_heads // 2):
        sl = slice(p * LANES, (p + 1) * LANES)
        qp, kp, vp = q_ref[:, sl], k_ref[:, sl], v_ref[:, sl]
        kpt = kp.astype(jnp.float32).T
        kpt_b = kpt.astype(kp.dtype)
        inner = None
        for e in range(2):
            q_e = jnp.where(even if e == 0 else jnp.logical_not(even), qp, jnp.zeros_like(qp))
            s = jnp.dot(q_e, kpt_b, preferred_element_type=jnp.float32) * dec_ref[2 * p + e]
            t = jnp.dot(s.astype(vp.dtype), vp, preferred_element_type=jnp.float32)
            inner = t if e == 0 else jnp.where(even, inner, t)
        state = st_ref[p]
        cross = jnp.dot(qp, state.astype(qp.dtype), preferred_element_type=jnp.float32) * xi_ref[p]
        y = inner + cross
        yc = y - _group_mean(y, avg)
        yn = yc * lax.rsqrt(_group_mean(yc * yc, avg) + EPS)
        g = g_ref[:, sl]
        o_ref[:, sl] = (g / (1.0 + jnp.exp(-g)) * yn).astype(o_ref.dtype)
        kz = (kpt * zt_ref[p]).astype(kp.dtype)
        kv = jnp.dot(kz, vp, preferred_element_type=jnp.float32)
        st_ref[p] = state * gc_ref[p] + kv * blk_ref[...]


def _ret_consts(n_heads):
    log_gamma = jnp.log1p(-jnp.exp2(-5.0 - jnp.arange(n_heads, dtype=jnp.float32)))
    pos = jnp.arange(RET_C, dtype=jnp.float32)
    diff = pos[:, None] - pos[None, :]
    dec = jnp.where(diff[None] >= 0, jnp.exp(jnp.maximum(diff, 0.0)[None] * log_gamma[:, None, None]), 0.0)
    zeta = jnp.exp((RET_C - 1.0 - pos)[None, :] * log_gamma[:, None])
    xi = jnp.exp((pos + 1.0)[None, :] * log_gamma[:, None])
    gc = jnp.exp(RET_C * log_gamma)
    n_pairs = n_heads // 2
    lanes = lambda a: jnp.repeat(a.reshape(n_pairs, 2, -1), HEAD_DIM, axis=1)
    zt = lanes(zeta)
    xi_p = jnp.swapaxes(lanes(xi), 1, 2)
    gc_p = jnp.broadcast_to(lanes(gc[:, None]), (n_pairs, LANES, LANES))
    head_of = jnp.arange(LANES) // HEAD_DIM
    blk = (head_of[:, None] == head_of[None, :]).astype(jnp.float32)
    avg = (blk / HEAD_DIM).astype(jnp.bfloat16)
    f32 = lambda a: a.astype(jnp.float32)
    return dec, f32(zt), f32(xi_p), f32(gc_p), blk, avg


def _ret(qb, kb, vb, gb, seq):
    n, d_b = qb.shape
    bsz = n // seq
    n_heads = d_b // HEAD_DIM
    n_c = seq // RET_C
    consts = _ret_consts(n_heads)
    row = lambda b, c: (b * n_c + c, 0)
    const_spec = lambda a: pl.BlockSpec(a.shape, lambda b, c: (0,) * a.ndim)
    return pl.pallas_call(
        functools.partial(_ret_kernel, n_heads=n_heads),
        grid=(bsz, n_c),
        in_specs=[pl.BlockSpec((RET_C, d_b), row)] * 4 + [const_spec(a) for a in consts],
        out_specs=pl.BlockSpec((RET_C, d_b), row),
        out_shape=jax.ShapeDtypeStruct((n, d_b), jnp.bfloat16),
        scratch_shapes=[pltpu.VMEM((n_heads // 2, LANES, LANES), jnp.float32)],
        compiler_params=_cparams(("arbitrary", "arbitrary")),
        name="ret",
    )(qb, kb, vb, gb, *consts)


def _tiles_shape(rows, d):
    return (rows // SUBLANES, d // LANES, SUBLANES, LANES)


def _to_tiles(ref, x):
    for s in range(ref.shape[1]):
        ref[:, s] = x[:, s * LANES:(s + 1) * LANES].reshape(ref.shape[0], SUBLANES, LANES)


def _from_tiles(ref):
    rows = ref.shape[0] * SUBLANES
    return jnp.concatenate([ref[:, s].reshape(rows, LANES) for s in range(ref.shape[1])], axis=1)


def _lane_first_eq(x, m, lane):
    return jnp.min(jnp.where(x == m, lane, float(LANES)), axis=1, keepdims=True)


def _mix_out_kernel(x_ref, ya_ref, yb_ref, woa_ref, wob_ref, gt_ref, sc_ref, sh_ref, g_ref, wr_ref, tri_ref,
                    x1_ref, h2_ref, route_ref, cnt_ref, carry_ref):
    @pl.when(pl.program_id(0) == 0)
    def _():
        carry_ref[...] = jnp.zeros_like(carry_ref)

    mix = (jnp.dot(ya_ref[...], woa_ref[...], preferred_element_type=jnp.float32)
           + jnp.dot(yb_ref[...], wob_ref[...], preferred_element_type=jnp.float32))
    x1 = x_ref[...] + gt_ref[0] * mix
    x1_ref[...] = x1
    h2 = _rmsnorm_mod(x1, g_ref[...], sc_ref[0], sh_ref[0])
    _to_tiles(h2_ref, h2)

    lg = jnp.dot(h2.astype(jnp.bfloat16), wr_ref[...], preferred_element_type=jnp.float32)
    lane = lax.broadcasted_iota(jnp.int32, lg.shape, 1).astype(jnp.float32)
    is_grp = (lane >= N_EXPERTS) & (lane < N_EXPERTS + N_GROUPS)
    gl = jnp.where(is_grp, lg, -jnp.inf)
    gmax = jnp.max(gl, axis=1, keepdims=True)
    grp = _lane_first_eq(gl, gmax, lane) - N_EXPERTS
    p_grp = 1.0 / jnp.sum(jnp.exp(gl - gmax), axis=1, keepdims=True)
    in_grp = jnp.floor(lane * (1.0 / EXPERTS_PER_GROUP)) == grp
    f = jnp.where(in_grp & (lane < N_EXPERTS), lg, -jnp.inf)
    f1 = jnp.max(f, axis=1, keepdims=True)
    e1 = _lane_first_eq(f, f1, lane)
    f = jnp.where(lane == e1, -jnp.inf, f)
    f2 = jnp.max(f, axis=1, keepdims=True)
    e2 = _lane_first_eq(f, f2, lane)
    a2 = jnp.exp(f2 - f1)
    w1 = p_grp / (1.0 + a2)
    w2 = p_grp * a2 / (1.0 + a2)

    oh1 = jnp.where(lane == e1, 1.0, 0.0)
    oh2 = jnp.where(lane == e2, 1.0, 0.0)
    both = oh1 + oh2
    before = jnp.dot(tri_ref[...], both.astype(jnp.bfloat16), preferred_element_type=jnp.float32) + carry_ref[...]
    r1 = jnp.sum(before * oh1, axis=1, keepdims=True)
    r2 = jnp.sum(before * oh2, axis=1, keepdims=True)
    carry = carry_ref[...] + jnp.sum(both, axis=0, keepdims=True)
    carry_ref[...] = carry
    cnt_ref[...] = carry

    out = jnp.zeros(lg.shape, jnp.float32)
    for col, val in enumerate((e1, e2, w1, w2, r1, r2)):
        out = jnp.where(lane == col, val, out)
    route_ref[...] = out


def _mix_out(x2, ya, yb, wo_a, wo_b, mod3, g_ffn, w_route, seq):
    n, d = x2.shape
    tiles_per_seq = seq // TM
    tri = jnp.asarray(np.tril(np.ones((TM, TM), np.float32), -1), jnp.bfloat16)
    row = lambda i: (i, 0)
    const = lambda i: (0, 0)
    modk = lambda k: pl.BlockSpec((1, 1, d), lambda i: ((i // tiles_per_seq) * 6 + k, 0, 0))
    return pl.pallas_call(
        _mix_out_kernel,
        grid=(n // TM,),
        in_specs=[pl.BlockSpec((TM, d), row),
                  pl.BlockSpec((TM, ya.shape[1]), row),
                  pl.BlockSpec((TM, yb.shape[1]), row),
                  pl.BlockSpec(wo_a.shape, const),
                  pl.BlockSpec(wo_b.shape, const),
                  modk(2), modk(4), modk(3),
                  pl.BlockSpec((1, d), const),
                  pl.BlockSpec(w_route.shape, const),
                  pl.BlockSpec(tri.shape, const)],
        out_specs=(pl.BlockSpec((TM, d), row), pl.BlockSpec(_tiles_shape(TM, d), lambda i: (i, 0, 0, 0)),
                   pl.BlockSpec((TM, LANES), row), pl.BlockSpec((1, LANES), const)),
        out_shape=(jax.ShapeDtypeStruct((n, d), jnp.float32), jax.ShapeDtypeStruct(_tiles_shape(n, d), jnp.float32),
                   jax.ShapeDtypeStruct((n, LANES), jnp.float32), jax.ShapeDtypeStruct((1, LANES), jnp.float32)),
        scratch_shapes=[pltpu.VMEM((1, LANES), jnp.float32)],
        compiler_params=_cparams(("arbitrary",)),
        name="mix_out",
    )(x2, ya, yb, wo_a, wo_b, mod3, mod3, mod3, g_ffn, w_route, tri)


def _plan_kernel(route_ref, pst_ref, dest_ref):
    r = route_ref[...]
    lane = lax.broadcasted_iota(jnp.int32, r.shape, 1).astype(jnp.float32)
    pst = pst_ref[...]
    d1 = jnp.sum(jnp.where(lane == r[:, 0:1], pst, 0.0), axis=1, keepdims=True) + r[:, 4:5]
    d2 = jnp.sum(jnp.where(lane == r[:, 1:2], pst, 0.0), axis=1, keepdims=True) + r[:, 5:6]
    t1 = jnp.floor(d1 * (1.0 / SUBLANES))
    t2 = jnp.floor(d2 * (1.0 / SUBLANES))
    packed = jnp.zeros(r.shape, jnp.float32)
    for k, v in enumerate((t1, d1 - t1 * SUBLANES, t2, d2 - t2 * SUBLANES)):
        packed = jnp.where(lane == float(k), v, packed)
    dest_ref[0] = packed.T[0:8, :].astype(jnp.int32)


def _plan(route, pst_row):
    n = route.shape[0]
    return pl.pallas_call(
        _plan_kernel,
        grid=(n // TM,),
        in_specs=[pl.BlockSpec((TM, LANES), lambda i: (i, 0)), pl.BlockSpec((1, LANES), lambda i: (0, 0))],
        out_specs=pl.BlockSpec((1, 8, TM), lambda i: (i, 0, 0)),
        out_shape=jax.ShapeDtypeStruct((n // TM, 8, TM), jnp.int32),
        compiler_params=_cparams(("arbitrary",)),
        name="plan",
    )(route, pst_row)


def _dispatch_kernel(zrow_ref, n_used_ref, dest_ref, h2_ref, xs_hbm, zbuf, sem, zsem, *, n_blk):
    i = pl.program_id(0)

    @pl.when(i == 0)
    def _():
        zbuf[...] = jnp.zeros(zbuf.shape, zbuf.dtype)
        blk_tiles = BM // SUBLANES
        for e in range(N_EXPERTS):
            pltpu.make_async_copy(zbuf, xs_hbm.at[pl.ds(zrow_ref[e], blk_tiles)], zsem).start()
        for e in range(N_EXPERTS):
            pltpu.make_async_copy(zbuf, xs_hbm.at[pl.ds(0, blk_tiles)], zsem).wait()
        for b in range(N_EXPERTS + 1):
            @pl.when(n_used_ref[0] + b <= n_blk)
            def _():
                tail = pltpu.make_async_copy(zbuf, xs_hbm.at[pl.ds((n_used_ref[0] + b) * blk_tiles, blk_tiles)], zsem)
                tail.start()
                tail.wait()

    def issue(k, c):
        for u in range(SUBLANES):
            r = k * SUBLANES + u
            row = h2_ref.at[k, :, u, :]
            pltpu.make_async_copy(row, xs_hbm.at[dest_ref[0, 0, r], :, dest_ref[0, 1, r], :], sem).start()
            pltpu.make_async_copy(row, xs_hbm.at[dest_ref[0, 2, r], :, dest_ref[0, 3, r], :], sem).start()
        return c

    lax.fori_loop(0, TM // SUBLANES, issue, 0)
    for _ in range(2):
        pltpu.make_async_copy(h2_ref, xs_hbm.at[pl.ds(0, TM // SUBLANES)], sem).wait()


def _dispatch(h2, dest, zrow, n_used, n_blk):
    n_rows = (n_blk + 1) * BM
    d = h2.shape[1] * LANES
    grid_spec = pltpu.PrefetchScalarGridSpec(
        num_scalar_prefetch=2,
        grid=(h2.shape[0] * SUBLANES // TM,),
        in_specs=[pl.BlockSpec((1, 8, TM), lambda i, z, nu: (i, 0, 0), memory_space=pltpu.SMEM),
                  pl.BlockSpec(_tiles_shape(TM, d), lambda i, z, nu: (i, 0, 0, 0))],
        out_specs=pl.BlockSpec(memory_space=pl.ANY),
        scratch_shapes=[pltpu.VMEM(_tiles_shape(BM, d), h2.dtype),
                        pltpu.SemaphoreType.DMA(()),
                        pltpu.SemaphoreType.DMA(())],
    )
    return pl.pallas_call(
        functools.partial(_dispatch_kernel, n_blk=n_blk),
        grid_spec=grid_spec,
        out_shape=jax.ShapeDtypeStruct(_tiles_shape(n_rows, d), h2.dtype),
        compiler_params=_cparams(("arbitrary",)),
        name="dispatch",
    )(zrow, n_used, dest, h2)


def _experts_kernel(blk_e_ref, n_used_ref, x_ref, wg_ref, wu_ref, wd_ref, y_ref):
    j = pl.program_id(0)

    @pl.when(j < n_used_ref[0])
    def _():
        x = _from_tiles(x_ref).astype(wg_ref.dtype)
        a = jnp.dot(x, wg_ref[0], preferred_element_type=jnp.float32)
        b = jnp.dot(x, wu_ref[0], preferred_element_type=jnp.float32)
        hmid = (a / (1.0 + jnp.exp(-a)) * b).astype(x.dtype)
        _to_tiles(y_ref, jnp.dot(hmid, wd_ref[0], preferred_element_type=jnp.float32))

    @pl.when(j >= n_used_ref[0])
    def _():
        y_ref[...] = jnp.zeros(y_ref.shape, y_ref.dtype)


def _experts(xs, wg, wu, wd, blk_e, n_used):
    n_blk = blk_e.shape[0]
    d, d_e = wg.shape[1], wg.shape[2]
    blk = _tiles_shape(BM, d)
    grid_spec = pltpu.PrefetchScalarGridSpec(
        num_scalar_prefetch=2,
        grid=(n_blk,),
        in_specs=[pl.BlockSpec(blk, lambda j, be, nu: (jnp.minimum(j, nu[0] - 1), 0, 0, 0)),
                  pl.BlockSpec((1, d, d_e), lambda j, be, nu: (be[j], 0, 0)),
                  pl.BlockSpec((1, d, d_e), lambda j, be, nu: (be[j], 0, 0)),
                  pl.BlockSpec((1, d_e, d), lambda j, be, nu: (be[j], 0, 0))],
        out_specs=pl.BlockSpec(blk, lambda j, be, nu: (j, 0, 0, 0)),
    )
    return pl.pallas_call(
        _experts_kernel,
        grid_spec=grid_spec,
        out_shape=jax.ShapeDtypeStruct(_tiles_shape(n_blk * BM, d), jnp.float32),
        compiler_params=_cparams(("arbitrary",)),
        name="experts",
    )(blk_e, n_used, xs, wg, wu, wd)


def _final_kernel(dest_ref, dest_next_ref, x1_ref, route_ref, gt_ref, g_ref, ys_hbm, o_ref, ybuf, sem):
    i = pl.program_id(0)
    slot = i % 2

    tiles = TM // SUBLANES

    def gather(d_ref, s):
        def issue(k, c):
            for u in range(SUBLANES):
                r = k * SUBLANES + u
                pltpu.make_async_copy(ys_hbm.at[d_ref[0, 0, r], :, d_ref[0, 1, r], :],
                                      ybuf.at[s, k, :, u, :], sem.at[s]).start()
                pltpu.make_async_copy(ys_hbm.at[d_ref[0, 2, r], :, d_ref[0, 3, r], :],
                                      ybuf.at[s, tiles + k, :, u, :], sem.at[s]).start()
            return c
        lax.fori_loop(0, tiles, issue, 0)

    @pl.when(i == 0)
    def _():
        gather(dest_ref, 0)

    @pl.when(i + 1 < pl.num_programs(0))
    def _():
        gather(dest_next_ref, 1 - slot)

    pltpu.make_async_copy(ys_hbm.at[pl.ds(0, 2 * tiles)], ybuf.at[slot], sem.at[slot]).wait()
    w1 = route_ref[:, 2:3]
    w2 = route_ref[:, 3:4]
    y0 = _from_tiles(ybuf.at[slot, 0:tiles])
    y1 = _from_tiles(ybuf.at[slot, tiles:2 * tiles])
    x2 = x1_ref[...] + gt_ref[0] * (w1 * y0 + w2 * y1)
    o_ref[...] = x2 * lax.rsqrt(jnp.mean(x2 * x2, axis=-1, keepdims=True) + EPS) * g_ref[...]


def _final(x1, ys, dest, route, mod3, g_final, seq):
    n, d = x1.shape
    n_tiles = n // TM
    tiles_per_seq = seq // TM
    row = lambda i: (i, 0)
    return pl.pallas_call(
        _final_kernel,
        grid=(n_tiles,),
        in_specs=[pl.BlockSpec((1, 8, TM), lambda i: (i, 0, 0), memory_space=pltpu.SMEM),
                  pl.BlockSpec((1, 8, TM), lambda i: (jnp.minimum(i + 1, n_tiles - 1), 0, 0),
                               memory_space=pltpu.SMEM),
                  pl.BlockSpec((TM, d), row),
                  pl.BlockSpec((TM, LANES), row),
                  pl.BlockSpec((1, 1, d), lambda i: ((i // tiles_per_seq) * 6 + 5, 0, 0)),
                  pl.BlockSpec((1, d), lambda i: (0, 0)),
                  pl.BlockSpec(memory_space=pl.ANY)],
        out_specs=pl.BlockSpec((TM, d), row),
        out_shape=jax.ShapeDtypeStruct((n, d), jnp.float32),
        scratch_shapes=[pltpu.VMEM((2,) + _tiles_shape(2 * TM, d), jnp.float32),
                        pltpu.SemaphoreType.DMA((2,))],
        compiler_params=_cparams(("arbitrary",)),
        name="final",
    )(dest, dest, x1, route, mod3, g_final, ys)


def _block_layout(counts, n):
    cnt = counts[0, :N_EXPERTS].astype(jnp.int32)
    blocks = (cnt + BM - 1) // BM
    bends = jnp.cumsum(blocks)
    pstarts = (bends - blocks) * BM
    n_blk = (2 * n) // BM + N_EXPERTS
    pst_row = jnp.zeros((1, LANES), jnp.float32).at[0, :N_EXPERTS].set(pstarts.astype(jnp.float32))
    zrow = ((pstarts + cnt) // SUBLANES).astype(jnp.int32)
    n_used = bends[-1:].astype(jnp.int32)
    blk_e = jnp.sum(bends[None, :] <= jnp.arange(n_blk, dtype=jnp.int32)[:, None], axis=1)
    blk_e = jnp.minimum(blk_e, N_EXPERTS - 1).astype(jnp.int32)
    return pst_row, zrow, n_used, blk_e


def _layer(x2, c, pos2, w_ada, b_ada, g_mix, w_in, w_o, g_ffn, w_rg, w_re, w_up, w_gate, w_down, seq):
    n, d = x2.shape
    bsz = n // seq
    bf = jnp.bfloat16
    d_a = d // 2
    d_i = N_IDX_HEADS * IDX_DIM
    mod3 = _ada(c, w_ada, b_ada).reshape(bsz * 6, 1, d)

    c0 = 3 * d_a + d_i
    c1 = c0 + IDX_DIM + N_IDX_HEADS
    w_cat = jnp.concatenate([w_in[:, :c0], w_in[:, c0:c1],
                             jnp.zeros((d, LANES - (c1 - c0)), w_in.dtype), w_in[:, c1:]], axis=1).astype(bf)
    qat, ka, vat, qit, kiw, wit, qb, kb, vb, gb = _proj(x2, pos2, mod3, g_mix.reshape(1, d), w_cat, seq)

    n_kt = seq // TM
    ya = _dsa(qat, qit, wit, kiw.reshape(bsz, n_kt, TM, LANES), ka.reshape(bsz, n_kt, TM, d_a), vat, seq)
    yb = _ret(qb, kb, vb, gb, seq)

    w_route = jnp.concatenate([jnp.transpose(w_re, (1, 0, 2)).reshape(d, N_EXPERTS), w_rg,
                               jnp.zeros((d, LANES - N_EXPERTS - N_GROUPS), w_rg.dtype)], axis=1).astype(bf)
    x1, h2, route, counts = _mix_out(x2, ya, yb, w_o[:d_a].astype(bf), w_o[d_a:].astype(bf), mod3,
                                     g_ffn.reshape(1, d), w_route, seq)

    pst_row, zrow, n_used, blk_e = _block_layout(counts, n)
    dest = _plan(route, pst_row)
    xs = _dispatch(h2, dest, zrow, n_used, blk_e.shape[0])
    ys = _experts(xs, w_gate.astype(bf), w_up.astype(bf), w_down.astype(bf), blk_e, n_used)
    return x1, ys, dest, route, mod3


def kernel(x, c, positions, w_ada, b_ada, g_norm_mix, w_in, w_o, g_norm_ffn, w_router_group, w_router_expert,
           w_up, w_gate, w_down, g_norm_final):
    bsz, seq, d = x.shape
    depth = w_ada.shape[0]
    assert depth == 1, "the final norm is fused into the last layer's combine kernel"
    assert seq % (2 * TM) == 0 and seq % RET_C == 0 and (2 * bsz * seq) % BM == 0
    x2 = x.reshape(bsz * seq, d)
    pos2 = positions.astype(jnp.float32).reshape(bsz * seq, 1)
    x1, ys, dest, route, mod3 = _layer(x2, c, pos2, w_ada[0], b_ada[0], g_norm_mix[0], w_in[0], w_o[0],
                                       g_norm_ffn[0], w_router_group[0], w_router_expert[0], w_up[0], w_gate[0],
                                       w_down[0], seq)
    out = _final(x1, ys, dest, route, mod3, g_norm_final.reshape(1, d), seq)
    return out.reshape(bsz, seq, d)
```

```python
import functools

import jax
import jax.numpy as jnp
import numpy as np
from jax import lax
from jax.experimental import pallas as pl
from jax.experimental.pallas import tpu as pltpu

CHUNK = 64
HEAD_DIM = 64
N_IDX_HEADS = 16
IDX_DIM = 64
TOPK_MAX = 256
ROPE_THETA = 500000.0
ROT_DIM = HEAD_DIM // 4
RET_THETA = 10000.0
N_GROUPS = 4
EXPERTS_PER_GROUP = 8
N_EXPERTS = N_GROUPS * EXPERTS_PER_GROUP
EPS = 1e-6

LANES = 128
SUBLANES = 8
VMEM_LIMIT = 56 * 1024 * 1024

TM = 256
QB = TM
RET_C = 256
BM = 256
PLAN_TM = 1024
MIX_TM = 512
CNT_ROWS = 64
ONES_ROWS = 16
HEAD_GROUP = 4
IDX_TILES = 4
BISECT_PER_CHECK = 2
MAX_BISECT = 40

NEG_BIG = -1e30
LOG2E = 1.4426950408889634


def _cparams(sem):
    return pltpu.CompilerParams(dimension_semantics=sem, vmem_limit_bytes=VMEM_LIMIT)


def _ada_kernel(c_ref, w_ref, b_ref, o_ref):
    o_ref[...] = jnp.dot(c_ref[...], w_ref[...], preferred_element_type=jnp.float32) + b_ref[...]


def _ada(c, w_ada, b_ada):
    bsz, d = c.shape
    n_out = w_ada.shape[1]
    return pl.pallas_call(
        _ada_kernel,
        grid=(n_out // d,),
        in_specs=[pl.BlockSpec((bsz, d), lambda j: (0, 0)),
                  pl.BlockSpec((d, d), lambda j: (0, j)),
                  pl.BlockSpec((1, d), lambda j: (0, j))],
        out_specs=pl.BlockSpec((bsz, d), lambda j: (0, j)),
        out_shape=jax.ShapeDtypeStruct((bsz, n_out), jnp.float32),
        compiler_params=_cparams(("arbitrary",)),
        name="ada",
    )(c, w_ada, b_ada.reshape(1, n_out))


def _rmsnorm_mod(x, g, sc, sh):
    xn = x * lax.rsqrt(jnp.mean(x * x, axis=-1, keepdims=True) + EPS)
    return xn * g * (1.0 + sc) + sh


def _rope_lanes(x, cos, sin_lo, sin_hi, half):
    cols = []
    for k in range(x.shape[1] // LANES):
        xb = x[:, k * LANES:(k + 1) * LANES]
        cols.append(xb * cos + pltpu.roll(xb, LANES - half, 1) * sin_lo + pltpu.roll(xb, half, 1) * sin_hi)
    return cols[0] if len(cols) == 1 else jnp.concatenate(cols, axis=1)


def _proj_kernel(x_ref, pos_ref, sc_ref, sh_ref, g_ref, w_ref, tab_ref,
                 qat_ref, ka_ref, vat_ref, qit_ref, kiw_ref, wit_ref, qb_ref, kb_ref, vb_ref, gb_ref,
                 *, d_a, d_i, d_b):
    h = _rmsnorm_mod(x_ref[...], g_ref[...], sc_ref[0], sh_ref[0]).astype(jnp.bfloat16)
    pos = pos_ref[...]
    ang_a = pos * tab_ref[0:1, :]
    cos_a, sin_a = jnp.cos(ang_a), jnp.sin(ang_a)
    sa_lo, sa_hi = sin_a * tab_ref[1:2, :], sin_a * tab_ref[2:3, :]
    ang_b = pos * tab_ref[3:4, :]
    cos_b, sin_b = jnp.cos(ang_b), jnp.sin(ang_b)
    sb_lo, sb_hi = sin_b * tab_ref[4:5, :], sin_b * tab_ref[5:6, :]
    half_a, half_b = ROT_DIM // 2, HEAD_DIM // 2

    def seg(lo, width):
        return jnp.dot(h, w_ref[:, lo:lo + width], preferred_element_type=jnp.float32)

    o = 0
    qa = seg(o, d_a); o += d_a
    qat_ref[0, 0] = (_rope_lanes(qa, cos_a, sa_lo, sa_hi, half_a) * (HEAD_DIM ** -0.5 * LOG2E)).T.astype(qat_ref.dtype)
    ka = seg(o, d_a); o += d_a
    ka_ref[...] = _rope_lanes(ka, cos_a, sa_lo, sa_hi, half_a).astype(ka_ref.dtype)
    vat_ref[0, 0] = seg(o, d_a).T.astype(vat_ref.dtype); o += d_a
    qi = seg(o, d_i); o += d_i
    qit_ref[0, 0] = (_rope_lanes(qi, cos_a, sa_lo, sa_hi, half_a) * (IDX_DIM ** -0.5)).T.astype(qit_ref.dtype)
    kw = seg(o, LANES); o += LANES
    kiw_ref[...] = _rope_lanes(kw, cos_a, sa_lo, sa_hi, half_a).astype(kiw_ref.dtype)
    wit_ref[0, 0] = kw.T[IDX_DIM:IDX_DIM + N_IDX_HEADS, :] * (N_IDX_HEADS ** -0.5)
    qb = seg(o, d_b); o += d_b
    qb_ref[...] = _rope_lanes(qb, cos_b, sb_lo, sb_hi, half_b).astype(qb_ref.dtype)
    kb = seg(o, d_b); o += d_b
    kb_ref[...] = (_rope_lanes(kb, cos_b, sb_lo, sb_hi, half_b) * (HEAD_DIM ** -0.5)).astype(kb_ref.dtype)
    vb_ref[...] = seg(o, d_b).astype(vb_ref.dtype); o += d_b
    gb_ref[...] = seg(o, d_b)


def _rope_tables():
    lane = jnp.arange(LANES) % HEAD_DIM
    rows = []
    for rot, theta in ((ROT_DIM, ROPE_THETA), (HEAD_DIM, RET_THETA)):
        half = rot // 2
        inv_freq = theta ** (-jnp.arange(half, dtype=jnp.float32) / half)
        rows.append(jnp.where(lane < rot, inv_freq[lane % half], 0.0))
        rows.append(jnp.where(lane < half, -1.0, 0.0))
        rows.append(jnp.where((lane >= half) & (lane < rot), 1.0, 0.0))
    rows += [jnp.zeros((LANES,), jnp.float32)] * 2
    return jnp.stack(rows).astype(jnp.float32)


def _proj(x2, pos2, mod3, g_mix, w_cat, seq):
    n, d = x2.shape
    bsz = n // seq
    d_a = d // 2
    d_b = d // 2
    d_i = N_IDX_HEADS * IDX_DIM
    tiles_per_seq = seq // TM
    tab = _rope_tables()
    row = lambda i: (i, 0)
    const = lambda i: (0, 0)
    tile4 = lambda i: (i // tiles_per_seq, i % tiles_per_seq, 0, 0)
    bf = jnp.bfloat16
    out_shape = (
        jax.ShapeDtypeStruct((bsz, tiles_per_seq, d_a, TM), bf),
        jax.ShapeDtypeStruct((n, d_a), bf),
        jax.ShapeDtypeStruct((bsz, tiles_per_seq, d_a, TM), bf),
        jax.ShapeDtypeStruct((bsz, tiles_per_seq, d_i, TM), bf),
        jax.ShapeDtypeStruct((n, LANES), bf),
        jax.ShapeDtypeStruct((bsz, tiles_per_seq, N_IDX_HEADS, TM), jnp.float32),
        jax.ShapeDtypeStruct((n, d_b), bf),
        jax.ShapeDtypeStruct((n, d_b), bf),
        jax.ShapeDtypeStruct((n, d_b), bf),
        jax.ShapeDtypeStruct((n, d_b), jnp.float32),
    )
    out_specs = (
        pl.BlockSpec((1, 1, d_a, TM), tile4),
        pl.BlockSpec((TM, d_a), row),
        pl.BlockSpec((1, 1, d_a, TM), tile4),
        pl.BlockSpec((1, 1, d_i, TM), tile4),
        pl.BlockSpec((TM, LANES), row),
        pl.BlockSpec((1, 1, N_IDX_HEADS, TM), tile4),
        pl.BlockSpec((TM, d_b), row),
        pl.BlockSpec((TM, d_b), row),
        pl.BlockSpec((TM, d_b), row),
        pl.BlockSpec((TM, d_b), row),
    )
    return pl.pallas_call(
        functools.partial(_proj_kernel, d_a=d_a, d_i=d_i, d_b=d_b),
        grid=(n // TM,),
        in_specs=[pl.BlockSpec((TM, d), row),
                  pl.BlockSpec((TM, 1), row),
                  pl.BlockSpec((1, 1, d), lambda i: ((i // tiles_per_seq) * 6 + 1, 0, 0)),
                  pl.BlockSpec((1, 1, d), lambda i: ((i // tiles_per_seq) * 6 + 0, 0, 0)),
                  pl.BlockSpec((1, d), const),
                  pl.BlockSpec(w_cat.shape, const),
                  pl.BlockSpec(tab.shape, const)],
        out_specs=out_specs,
        out_shape=out_shape,
        compiler_params=_cparams(("arbitrary",)),
        name="proj",
    )(x2, pos2, mod3, mod3, g_mix, w_cat, tab)


def _row_blocks(x, rows):
    return [x[r * rows:(r + 1) * rows] for r in range(x.shape[0] // rows)]


def _dsa_kernel(qat_ref, qit_ref, wit_ref, kiw_ref, ka_ref, vat_ref, o_ref,
                qix_ref, qmx_ref, sc_ref, m_ref, l_ref, acc_ref, sa_ref, sb_ref, mxa_ref, mxb_ref, lohi_ref,
                *, k_top, n_heads):
    i = pl.program_id(1)
    n_grp = (i * QB + QB + 2 * TM - 1) // (2 * TM)
    n_kt = 2 * n_grp
    n_real = (i * QB + QB + TM - 1) // TM

    zero_rows = jnp.zeros((LANES - IDX_DIM, QB), qix_ref.dtype)
    for h in range(N_IDX_HEADS):
        qix_ref[h] = jnp.concatenate([qit_ref[0, 0, h * IDX_DIM:(h + 1) * IDX_DIM, :], zero_rows], axis=0)
    row_q = lax.broadcasted_iota(jnp.int32, (LANES, QB), 0)
    for h in range(n_heads):
        pair = qat_ref[0, 0, (h // 2) * LANES:(h // 2 + 1) * LANES, :]
        own = (row_q < HEAD_DIM) if h % 2 == 0 else (row_q >= HEAD_DIM)
        qmx_ref[h] = jnp.where(own, pair, jnp.zeros_like(pair))

    q_chunk = (i * QB + lax.broadcasted_iota(jnp.int32, (TM, QB), 1)) // CHUNK
    key_in_tile = lax.broadcasted_iota(jnp.int32, (TM, QB), 0)
    w_all = wit_ref[0, 0]

    def idx_tiles(tiles):
        lo, hi = lohi_ref[0], lohi_ref[1]
        for j in tiles:
            kt = kiw_ref[0, j]
            acc = None
            for h in range(N_IDX_HEADS):
                d = jnp.dot(kt, qix_ref[h], preferred_element_type=jnp.float32)
                t = w_all[h:h + 1, :] * jnp.maximum(d, 0.0)
                acc = t if acc is None else acc + t
            adm = (j * TM + key_in_tile) // CHUNK <= q_chunk
            sc_ref[j] = jnp.where(adm, acc, -jnp.inf)
            lo = jnp.minimum(lo, functools.reduce(jnp.minimum, _row_blocks(jnp.where(adm, acc, jnp.inf), SUBLANES)))
            hi = jnp.maximum(hi, functools.reduce(jnp.maximum, _row_blocks(jnp.where(adm, acc, -jnp.inf), SUBLANES)))
        lohi_ref[0], lohi_ref[1] = lo, hi

    lohi_ref[0] = jnp.full((SUBLANES, QB), jnp.inf, jnp.float32)
    lohi_ref[1] = jnp.full((SUBLANES, QB), -jnp.inf, jnp.float32)

    def idx_step(g, carry):
        idx_tiles([IDX_TILES * g + u for u in range(IDX_TILES)])
        return carry

    lax.fori_loop(0, n_kt // IDX_TILES, idx_step, 0)

    @pl.when(n_kt % IDX_TILES != 0)
    def _():
        idx_tiles([n_kt - 2, n_kt - 1])

    lo = jnp.min(lohi_ref[0], axis=0, keepdims=True)
    hi = jnp.max(lohi_ref[1], axis=0, keepdims=True)
    n_adm = ((i * QB + lax.broadcasted_iota(jnp.int32, (1, QB), 1)) // CHUNK + 1) * CHUNK

    def bisect_cond(carry):
        it, _, _, _, unsettled = carry
        return (it < MAX_BISECT) & (unsettled > 0.0)

    def bisect(carry):
        it, lo, hi, cnt_lo, _ = carry
        for _ in range(BISECT_PER_CHECK):
            mid = lo + (hi - lo) * 0.5
            mid_b = jnp.broadcast_to(mid, (CNT_ROWS, QB))

            def count_tile(j, cnt, mid_b=mid_b):
                for blk in _row_blocks(sc_ref[j], CNT_ROWS):
                    cnt = cnt + jnp.where(blk >= mid_b, 1.0, 0.0)
                return cnt

            cnt = lax.fori_loop(0, n_real, count_tile, jnp.zeros((CNT_ROWS, QB), jnp.float32))
            c = jnp.sum(cnt, axis=0, keepdims=True)
            ge = c >= k_top
            lo = jnp.where(ge, mid, lo)
            hi = jnp.where(ge, hi, mid)
            cnt_lo = jnp.where(ge, c, cnt_lo)
        return it + BISECT_PER_CHECK, lo, hi, cnt_lo, jnp.max(jnp.where(cnt_lo > k_top, 1.0, 0.0))

    cnt0 = n_adm.astype(jnp.float32)
    _, thr, _, _, _ = lax.while_loop(bisect_cond, bisect,
                                     (0, lo, hi, cnt0, jnp.max(jnp.where(cnt0 > k_top, 1.0, 0.0))))

    def bias_tile(j, carry):
        sc_ref[j] = jnp.where(sc_ref[j] >= thr, 0.0, NEG_BIG)
        return carry

    lax.fori_loop(0, n_kt, bias_tile, 0)

    m_ref[...] = jnp.full(m_ref.shape, NEG_BIG, jnp.float32)
    l_ref[...] = jnp.zeros(l_ref.shape, jnp.float32)
    acc_ref[...] = jnp.zeros(acc_ref.shape, jnp.float32)

    def pair(h):
        return slice((h // 2) * LANES, (h // 2 + 1) * LANES)

    def logits_into(s_ref, mx_ref, j, heads):
        bias = sc_ref[j]
        for h in heads:
            s = jnp.dot(ka_ref[0, j, :, pair(h)], qmx_ref[h], preferred_element_type=jnp.float32) + bias
            s_ref[h] = s
            mx_ref[h] = jnp.max(s, axis=0, keepdims=True)

    ones_rows = jnp.ones((ONES_ROWS, TM), vat_ref.dtype)

    def absorb(s_ref, mx_ref, j, heads):
        for h in heads:
            m_old = m_ref[h]
            m_new = jnp.maximum(m_old, mx_ref[h])
            alpha = jnp.exp2(m_old - m_new)
            p = jnp.exp2(s_ref[h] - m_new).astype(vat_ref.dtype)
            pv = jnp.dot(jnp.concatenate([vat_ref[0, j, pair(h), :], ones_rows], axis=0), p,
                         preferred_element_type=jnp.float32)
            acc_ref[h] = acc_ref[h] * alpha + pv[0:LANES]
            l_ref[h] = l_ref[h] * alpha + pv[LANES:LANES + SUBLANES]
            m_ref[h] = m_new

    for h0 in range(0, n_heads, HEAD_GROUP):
        heads = range(h0, h0 + HEAD_GROUP)
        logits_into(sa_ref, mxa_ref, 0, heads)

        def attn_grp(g, carry, heads=heads):
            logits_into(sb_ref, mxb_ref, 2 * g + 1, heads)
            absorb(sa_ref, mxa_ref, 2 * g, heads)
            logits_into(sa_ref, mxa_ref, jnp.minimum(2 * g + 2, n_kt - 1), heads)
            absorb(sb_ref, mxb_ref, 2 * g + 1, heads)
            return carry

        lax.fori_loop(0, n_grp, attn_grp, 0)

    for hp in range(n_heads // 2):
        even = acc_ref[2 * hp] / l_ref[2 * hp, 0:1, :]
        odd = acc_ref[2 * hp + 1] / l_ref[2 * hp + 1, 0:1, :]
        o_ref[:, hp * LANES:(hp + 1) * LANES] = jnp.where(row_q < HEAD_DIM, even, odd).T.astype(o_ref.dtype)


def _dsa(qat, qit, wit, kiw4, ka4, vat, seq):
    bsz, n_kt, d_a, _ = qat.shape
    n_heads = d_a // HEAD_DIM
    n_qb = seq // QB
    k_top = min(TOPK_MAX, seq // 4)
    qtile = lambda b, i: (b, i, 0, 0)
    per_b = lambda b, i: (b, 0, 0, 0)
    f32 = jnp.float32
    return pl.pallas_call(
        functools.partial(_dsa_kernel, k_top=float(k_top), n_heads=n_heads),
        grid=(bsz, n_qb),
        in_specs=[pl.BlockSpec((1, 1, d_a, QB), qtile),
                  pl.BlockSpec((1, 1, qit.shape[2], QB), qtile),
                  pl.BlockSpec((1, 1, N_IDX_HEADS, QB), qtile),
                  pl.BlockSpec((1, n_kt, TM, LANES), per_b, pipeline_mode=pl.Buffered(1)),
                  pl.BlockSpec((1, n_kt, TM, d_a), per_b, pipeline_mode=pl.Buffered(1)),
                  pl.BlockSpec((1, n_kt, d_a, TM), per_b, pipeline_mode=pl.Buffered(1))],
        out_specs=pl.BlockSpec((QB, d_a), lambda b, i: (b * n_qb + i, 0)),
        out_shape=jax.ShapeDtypeStruct((bsz * seq, d_a), jnp.bfloat16),
        scratch_shapes=[pltpu.VMEM((N_IDX_HEADS, LANES, QB), jnp.bfloat16),
                        pltpu.VMEM((n_heads, LANES, QB), jnp.bfloat16),
                        pltpu.VMEM((n_kt, TM, QB), f32),
                        pltpu.VMEM((n_heads, 1, QB), f32),
                        pltpu.VMEM((n_heads, SUBLANES, QB), f32),
                        pltpu.VMEM((n_heads, LANES, QB), f32),
                        pltpu.VMEM((n_heads, TM, QB), f32),
                        pltpu.VMEM((n_heads, TM, QB), f32),
                        pltpu.VMEM((n_heads, 1, QB), f32),
                        pltpu.VMEM((n_heads, 1, QB), f32),
                        pltpu.VMEM((2, SUBLANES, QB), f32)],
        compiler_params=_cparams(("arbitrary", "arbitrary")),
        name="dsa",
    )(qat, qit, wit, kiw4, ka4, vat)


def _group_mean(y, avg):
    hi = y.astype(jnp.bfloat16)
    lo = (y - hi.astype(jnp.float32)).astype(jnp.bfloat16)
    return (jnp.dot(hi, avg, preferred_element_type=jnp.float32)
            + jnp.dot(lo, avg, preferred_element_type=jnp.float32))


def _ret_kernel(q_ref, k_ref, v_ref, g_ref, dec_ref, zt_ref, xi_ref, gc_ref, blk_ref, avg_ref, o_ref, st_ref,
                *, n_heads):
    @pl.when(pl.program_id(1) == 0)
    def _():
        st_ref[...] = jnp.zeros_like(st_ref)

    even = lax.broadcasted_iota(jnp.int32, (RET_C, LANES), 1) < HEAD_DIM
    avg = avg_ref[...]
    for p in range(n_heads // 2):
        sl = slice(p * LANES, (p + 1) * LANES)
        qp, kp, vp = q_ref[:, sl], k_ref[:, sl], v_ref[:, sl]
        kpt = kp.astype(jnp.float32).T
        kpt_b = kpt.astype(kp.dtype)
        inner = None
        for e in range(2):
            q_e = jnp.where(even if e == 0 else jnp.logical_not(even), qp, jnp.zeros_like(qp))
            s = jnp.dot(q_e, kpt_b, preferred_element_type=jnp.float32) * dec_ref[2 * p + e]
            t = jnp.dot(s.astype(vp.dtype), vp, preferred_element_type=jnp.float32)
            inner = t if e == 0 else jnp.where(even, inner, t)
        state = st_ref[p]
        cross = jnp.dot(qp, state.astype(qp.dtype), preferred_element_type=jnp.float32) * xi_ref[p]
        y = inner + cross
        yc = y - _group_mean(y, avg)
        yn = yc * lax.rsqrt(_group_mean(yc * yc, avg) + EPS)
        g = g_ref[:, sl]
        o_ref[:, sl] = (g / (1.0 + jnp.exp(-g)) * yn).astype(o_ref.dtype)
        kz = (kpt * zt_ref[p]).astype(kp.dtype)
        kv = jnp.dot(kz, vp, preferred_element_type=jnp.float32)
        st_ref[p] = state * gc_ref[p] + kv * blk_ref[...]


def _ret_consts(n_heads):
    log_gamma = jnp.log1p(-jnp.exp2(-5.0 - jnp.arange(n_heads, dtype=jnp.float32)))
    pos = jnp.arange(RET_C, dtype=jnp.float32)
    diff = pos[:, None] - pos[None, :]
    dec = jnp.where(diff[None] >= 0, jnp.exp(jnp.maximum(diff, 0.0)[None] * log_gamma[:, None, None]), 0.0)
    zeta = jnp.exp((RET_C - 1.0 - pos)[None, :] * log_gamma[:, None])
    xi = jnp.exp((pos + 1.0)[None, :] * log_gamma[:, None])
    gc = jnp.exp(RET_C * log_gamma)
    n_pairs = n_heads // 2
    lanes = lambda a: jnp.repeat(a.reshape(n_pairs, 2, -1), HEAD_DIM, axis=1)
    zt = lanes(zeta)
    xi_p = jnp.swapaxes(lanes(xi), 1, 2)
    gc_p = jnp.broadcast_to(lanes(gc[:, None]), (n_pairs, LANES, LANES))
    head_of = jnp.arange(LANES) // HEAD_DIM
    blk = (head_of[:, None] == head_of[None, :]).astype(jnp.float32)
    avg = (blk / HEAD_DIM).astype(jnp.bfloat16)
    f32 = lambda a: a.astype(jnp.float32)
    return dec, f32(zt), f32(xi_p), f32(gc_p), blk, avg


def _ret(qb, kb, vb, gb, seq):
    n, d_b = qb.shape
    bsz = n // seq
    n_heads = d_b // HEAD_DIM
    n_c = seq // RET_C
    consts = _ret_consts(n_heads)
    row = lambda b, c: (b * n_c + c, 0)
    const_spec = lambda a: pl.BlockSpec(a.shape, lambda b, c: (0,) * a.ndim)
    return pl.pallas_call(
        functools.partial(_ret_kernel, n_heads=n_heads),
        grid=(bsz, n_c),
        in_specs=[pl.BlockSpec((RET_C, d_b), row)] * 4 + [const_spec(a) for a in consts],
        out_specs=pl.BlockSpec((RET_C, d_b), row),
        out_shape=jax.ShapeDtypeStruct((n, d_b), jnp.bfloat16),
        scratch_shapes=[pltpu.VMEM((n_heads // 2, LANES, LANES), jnp.float32)],
        compiler_params=_cparams(("arbitrary", "arbitrary")),
        name="ret",
    )(qb, kb, vb, gb, *consts)


def _tiles_shape(rows, d):
    return (rows // SUBLANES, d // LANES, SUBLANES, LANES)


def _to_tiles(ref, x):
    for s in range(ref.shape[1]):
        ref[:, s] = x[:, s * LANES:(s + 1) * LANES].reshape(ref.shape[0], SUBLANES, LANES)


def _from_tiles(ref):
    rows = ref.shape[0] * SUBLANES
    return jnp.concatenate([ref[:, s].reshape(rows, LANES) for s in range(ref.shape[1])], axis=1)


def _lane_first_eq(x, m, lane):
    return jnp.min(jnp.where(x == m, lane, float(LANES)), axis=1, keepdims=True)


def _mix_out_kernel(x_ref, ya_ref, yb_ref, woa_ref, wob_ref, gt_ref, sc_ref, sh_ref, g_ref, wr_ref, tri_ref,
                    x1_ref, h2_ref, route_ref, cnt_ref, carry_ref):
    @pl.when(pl.program_id(0) == 0)
    def _():
        carry_ref[...] = jnp.zeros_like(carry_ref)

    mix = (jnp.dot(ya_ref[...], woa_ref[...], preferred_element_type=jnp.float32)
           + jnp.dot(yb_ref[...], wob_ref[...], preferred_element_type=jnp.float32))
    x1 = x_ref[...] + gt_ref[0] * mix
    x1_ref[...] = x1
    h2 = _rmsnorm_mod(x1, g_ref[...], sc_ref[0], sh_ref[0])
    _to_tiles(h2_ref, h2)

    lg = jnp.dot(h2.astype(jnp.bfloat16), wr_ref[...], preferred_element_type=jnp.float32)
    lane = lax.broadcasted_iota(jnp.int32, lg.shape, 1).astype(jnp.float32)
    is_grp = (lane >= N_EXPERTS) & (lane < N_EXPERTS + N_GROUPS)
    gl = jnp.where(is_grp, lg, -jnp.inf)
    gmax = jnp.max(gl, axis=1, keepdims=True)
    grp = _lane_first_eq(gl, gmax, lane) - N_EXPERTS
    p_grp = 1.0 / jnp.sum(jnp.exp(gl - gmax), axis=1, keepdims=True)
    in_grp = jnp.floor(lane * (1.0 / EXPERTS_PER_GROUP)) == grp
    f = jnp.where(in_grp & (lane < N_EXPERTS), lg, -jnp.inf)
    f1 = jnp.max(f, axis=1, keepdims=True)
    e1 = _lane_first_eq(f, f1, lane)
    f = jnp.where(lane == e1, -jnp.inf, f)
    f2 = jnp.max(f, axis=1, keepdims=True)
    e2 = _lane_first_eq(f, f2, lane)
    a2 = jnp.exp(f2 - f1)
    w1 = p_grp / (1.0 + a2)
    w2 = p_grp * a2 / (1.0 + a2)

    oh1 = jnp.where(lane == e1, 1.0, 0.0)
    oh2 = jnp.where(lane == e2, 1.0, 0.0)
    both = oh1 + oh2
    before = jnp.dot(tri_ref[...], both.astype(jnp.bfloat16), preferred_element_type=jnp.float32) + carry_ref[...]
    r1 = jnp.sum(before * oh1, axis=1, keepdims=True)
    r2 = jnp.sum(before * oh2, axis=1, keepdims=True)
    carry = carry_ref[...] + jnp.sum(both, axis=0, keepdims=True)
    carry_ref[...] = carry
    cnt_ref[...] = carry

    out = jnp.zeros(lg.shape, jnp.float32)
    for col, val in enumerate((e1, e2, w1, w2, r1, r2)):
        out = jnp.where(lane == col, val, out)
    route_ref[...] = out


def _mix_out(x2, ya, yb, wo_a, wo_b, mod3, g_ffn, w_route, seq):
    n, d = x2.shape
    tm = MIX_TM
    tiles_per_seq = seq // tm
    tri = jnp.asarray(np.tril(np.ones((tm, tm), np.float32), -1), jnp.bfloat16)
    row = lambda i: (i, 0)
    const = lambda i: (0, 0)
    modk = lambda k: pl.BlockSpec((1, 1, d), lambda i: ((i // tiles_per_seq) * 6 + k, 0, 0))
    return pl.pallas_call(
        _mix_out_kernel,
        grid=(n // tm,),
        in_specs=[pl.BlockSpec((tm, d), row),
                  pl.BlockSpec((tm, ya.shape[1]), row),
                  pl.BlockSpec((tm, yb.shape[1]), row),
                  pl.BlockSpec(wo_a.shape, const),
                  pl.BlockSpec(wo_b.shape, const),
                  modk(2), modk(4), modk(3),
                  pl.BlockSpec((1, d), const),
                  pl.BlockSpec(w_route.shape, const),
                  pl.BlockSpec(tri.shape, const)],
        out_specs=(pl.BlockSpec((tm, d), row), pl.BlockSpec(_tiles_shape(tm, d), lambda i: (i, 0, 0, 0)),
                   pl.BlockSpec((tm, LANES), row), pl.BlockSpec((1, LANES), const)),
        out_shape=(jax.ShapeDtypeStruct((n, d), jnp.float32), jax.ShapeDtypeStruct(_tiles_shape(n, d), jnp.float32),
                   jax.ShapeDtypeStruct((n, LANES), jnp.float32), jax.ShapeDtypeStruct((1, LANES), jnp.float32)),
        scratch_shapes=[pltpu.VMEM((1, LANES), jnp.float32)],
        compiler_params=_cparams(("arbitrary",)),
        name="mix_out",
    )(x2, ya, yb, wo_a, wo_b, mod3, mod3, mod3, g_ffn, w_route, tri)


def _plan_kernel(route_ref, pst_ref, dest_ref):
    r = route_ref[...]
    lane = lax.broadcasted_iota(jnp.int32, r.shape, 1).astype(jnp.float32)
    pst = pst_ref[...]
    d1 = jnp.sum(jnp.where(lane == r[:, 0:1], pst, 0.0), axis=1, keepdims=True) + r[:, 4:5]
    d2 = jnp.sum(jnp.where(lane == r[:, 1:2], pst, 0.0), axis=1, keepdims=True) + r[:, 5:6]
    t1 = jnp.floor(d1 * (1.0 / SUBLANES))
    t2 = jnp.floor(d2 * (1.0 / SUBLANES))
    packed = jnp.zeros(r.shape, jnp.float32)
    for k, v in enumerate((t1, d1 - t1 * SUBLANES, t2, d2 - t2 * SUBLANES)):
        packed = jnp.where(lane == float(k), v, packed)
    dest_ref[0] = packed.T[0:8, :].astype(jnp.int32)


def _dest_block(i):
    return (i // (PLAN_TM // TM), 0, i % (PLAN_TM // TM))


def _plan(route, pst_row):
    n = route.shape[0]
    return pl.pallas_call(
        _plan_kernel,
        grid=(n // PLAN_TM,),
        in_specs=[pl.BlockSpec((PLAN_TM, LANES), lambda i: (i, 0)), pl.BlockSpec((1, LANES), lambda i: (0, 0))],
        out_specs=pl.BlockSpec((1, 8, PLAN_TM), lambda i: (i, 0, 0)),
        out_shape=jax.ShapeDtypeStruct((n // PLAN_TM, 8, PLAN_TM), jnp.int32),
        compiler_params=_cparams(("arbitrary",)),
        name="plan",
    )(route, pst_row)


def _dispatch_kernel(zrow_ref, n_used_ref, dest_ref, h2_ref, xs_hbm, zbuf, sem, zsem, *, n_blk):
    i = pl.program_id(0)

    @pl.when(i == 0)
    def _():
        zbuf[...] = jnp.zeros(zbuf.shape, zbuf.dtype)
        blk_tiles = BM // SUBLANES
        for e in range(N_EXPERTS):
            pltpu.make_async_copy(zbuf, xs_hbm.at[pl.ds(zrow_ref[e], blk_tiles)], zsem).start()
        for e in range(N_EXPERTS):
            pltpu.make_async_copy(zbuf, xs_hbm.at[pl.ds(0, blk_tiles)], zsem).wait()
        for b in range(N_EXPERTS + 1):
            @pl.when(n_used_ref[0] + b <= n_blk)
            def _():
                tail = pltpu.make_async_copy(zbuf, xs_hbm.at[pl.ds((n_used_ref[0] + b) * blk_tiles, blk_tiles)], zsem)
                tail.start()
                tail.wait()

    def issue(k, c):
        for u in range(SUBLANES):
            r = k * SUBLANES + u
            row = h2_ref.at[k, :, u, :]
            pltpu.make_async_copy(row, xs_hbm.at[dest_ref[0, 0, r], :, dest_ref[0, 1, r], :], sem).start()
            pltpu.make_async_copy(row, xs_hbm.at[dest_ref[0, 2, r], :, dest_ref[0, 3, r], :], sem).start()
        return c

    lax.fori_loop(0, TM // SUBLANES, issue, 0)
    for _ in range(2):
        pltpu.make_async_copy(h2_ref, xs_hbm.at[pl.ds(0, TM // SUBLANES)], sem).wait()


def _dispatch(h2, dest, zrow, n_used, n_blk):
    n_rows = (n_blk + 1) * BM
    d = h2.shape[1] * LANES
    grid_spec = pltpu.PrefetchScalarGridSpec(
        num_scalar_prefetch=2,
        grid=(h2.shape[0] * SUBLANES // TM,),
        in_specs=[pl.BlockSpec((1, 8, TM), lambda i, z, nu: _dest_block(i), memory_space=pltpu.SMEM),
                  pl.BlockSpec(_tiles_shape(TM, d), lambda i, z, nu: (i, 0, 0, 0))],
        out_specs=pl.BlockSpec(memory_space=pl.ANY),
        scratch_shapes=[pltpu.VMEM(_tiles_shape(BM, d), h2.dtype),
                        pltpu.SemaphoreType.DMA(()),
                        pltpu.SemaphoreType.DMA(())],
    )
    return pl.pallas_call(
        functools.partial(_dispatch_kernel, n_blk=n_blk),
        grid_spec=grid_spec,
        out_shape=jax.ShapeDtypeStruct(_tiles_shape(n_rows, d), h2.dtype),
        compiler_params=_cparams(("arbitrary",)),
        name="dispatch",
    )(zrow, n_used, dest, h2)


def _experts_kernel(blk_e_ref, n_used_ref, x_ref, wg_ref, wu_ref, wd_ref, y_ref, wg_bf, wu_bf, wd_bf):
    j = pl.program_id(0)

    @pl.when((j == 0) | (blk_e_ref[j] != blk_e_ref[jnp.maximum(j - 1, 0)]))
    def _():
        wg_bf[...] = wg_ref[0].astype(wg_bf.dtype)
        wu_bf[...] = wu_ref[0].astype(wu_bf.dtype)
        wd_bf[...] = wd_ref[0].astype(wd_bf.dtype)

    @pl.when(j < n_used_ref[0])
    def _():
        x = _from_tiles(x_ref).astype(wg_bf.dtype)
        a = jnp.dot(x, wg_bf[...], preferred_element_type=jnp.float32)
        b = jnp.dot(x, wu_bf[...], preferred_element_type=jnp.float32)
        hmid = (a / (1.0 + jnp.exp(-a)) * b).astype(x.dtype)
        _to_tiles(y_ref, jnp.dot(hmid, wd_bf[...], preferred_element_type=jnp.float32))

    @pl.when(j >= n_used_ref[0])
    def _():
        y_ref[...] = jnp.zeros(y_ref.shape, y_ref.dtype)


def _experts(xs, wg, wu, wd, blk_e, n_used):
    n_blk = blk_e.shape[0]
    d, d_e = wg.shape[1], wg.shape[2]
    blk = _tiles_shape(BM, d)
    grid_spec = pltpu.PrefetchScalarGridSpec(
        num_scalar_prefetch=2,
        grid=(n_blk,),
        in_specs=[pl.BlockSpec(blk, lambda j, be, nu: (jnp.minimum(j, nu[0] - 1), 0, 0, 0)),
                  pl.BlockSpec((1, d, d_e), lambda j, be, nu: (be[j], 0, 0)),
                  pl.BlockSpec((1, d, d_e), lambda j, be, nu: (be[j], 0, 0)),
                  pl.BlockSpec((1, d_e, d), lambda j, be, nu: (be[j], 0, 0))],
        out_specs=pl.BlockSpec(blk, lambda j, be, nu: (j, 0, 0, 0)),
        scratch_shapes=[pltpu.VMEM((d, d_e), jnp.bfloat16), pltpu.VMEM((d, d_e), jnp.bfloat16),
                        pltpu.VMEM((d_e, d), jnp.bfloat16)],
    )
    return pl.pallas_call(
        _experts_kernel,
        grid_spec=grid_spec,
        out_shape=jax.ShapeDtypeStruct(_tiles_shape(n_blk * BM, d), jnp.float32),
        compiler_params=_cparams(("arbitrary",)),
        name="experts",
    )(blk_e, n_used, xs, wg, wu, wd)


def _final_kernel(dest_ref, dest_next_ref, x1_ref, route_ref, gt_ref, g_ref, ys_hbm, o_ref, ybuf, sem):
    i = pl.program_id(0)
    slot = i % 2

    tiles = TM // SUBLANES

    def gather(d_ref, s):
        def issue(k, c):
            for u in range(SUBLANES):
                r = k * SUBLANES + u
                pltpu.make_async_copy(ys_hbm.at[d_ref[0, 0, r], :, d_ref[0, 1, r], :],
                                      ybuf.at[s, k, :, u, :], sem.at[s]).start()
                pltpu.make_async_copy(ys_hbm.at[d_ref[0, 2, r], :, d_ref[0, 3, r], :],
                                      ybuf.at[s, tiles + k, :, u, :], sem.at[s]).start()
            return c
        lax.fori_loop(0, tiles, issue, 0)

    @pl.when(i == 0)
    def _():
        gather(dest_ref, 0)

    @pl.when(i + 1 < pl.num_programs(0))
    def _():
        gather(dest_next_ref, 1 - slot)

    pltpu.make_async_copy(ys_hbm.at[pl.ds(0, 2 * tiles)], ybuf.at[slot], sem.at[slot]).wait()
    w1 = route_ref[:, 2:3]
    w2 = route_ref[:, 3:4]
    y0 = _from_tiles(ybuf.at[slot, 0:tiles])
    y1 = _from_tiles(ybuf.at[slot, tiles:2 * tiles])
    x2 = x1_ref[...] + gt_ref[0] * (w1 * y0 + w2 * y1)
    o_ref[...] = x2 * lax.rsqrt(jnp.mean(x2 * x2, axis=-1, keepdims=True) + EPS) * g_ref[...]


def _final(x1, ys, dest, route, mod3, g_final, seq):
    n, d = x1.shape
    n_tiles = n // TM
    tiles_per_seq = seq // TM
    row = lambda i: (i, 0)
    return pl.pallas_call(
        _final_kernel,
        grid=(n_tiles,),
        in_specs=[pl.BlockSpec((1, 8, TM), _dest_block, memory_space=pltpu.SMEM),
                  pl.BlockSpec((1, 8, TM), lambda i: _dest_block(jnp.minimum(i + 1, n_tiles - 1)),
                               memory_space=pltpu.SMEM),
                  pl.BlockSpec((TM, d), row),
                  pl.BlockSpec((TM, LANES), row),
                  pl.BlockSpec((1, 1, d), lambda i: ((i // tiles_per_seq) * 6 + 5, 0, 0)),
                  pl.BlockSpec((1, d), lambda i: (0, 0)),
                  pl.BlockSpec(memory_space=pl.ANY)],
        out_specs=pl.BlockSpec((TM, d), row),
        out_shape=jax.ShapeDtypeStruct((n, d), jnp.float32),
        scratch_shapes=[pltpu.VMEM((2,) + _tiles_shape(2 * TM, d), jnp.float32),
                        pltpu.SemaphoreType.DMA((2,))],
        compiler_params=_cparams(("arbitrary",)),
        name="final",
    )(dest, dest, x1, route, mod3, g_final, ys)


def _block_layout(counts, n):
    cnt = counts[0, :N_EXPERTS].astype(jnp.int32)
    blocks = (cnt + BM - 1) // BM
    bends = jnp.cumsum(blocks)
    pstarts = (bends - blocks) * BM
    n_blk = (2 * n) // BM + N_EXPERTS
    pst_row = jnp.zeros((1, LANES), jnp.float32).at[0, :N_EXPERTS].set(pstarts.astype(jnp.float32))
    zrow = ((pstarts + cnt) // SUBLANES).astype(jnp.int32)
    n_used = bends[-1:].astype(jnp.int32)
    blk_e = jnp.sum(bends[None, :] <= jnp.arange(n_blk, dtype=jnp.int32)[:, None], axis=1)
    blk_e = jnp.minimum(blk_e, N_EXPERTS - 1).astype(jnp.int32)
    return pst_row, zrow, n_used, blk_e


def _layer(x2, c, pos2, w_ada, b_ada, g_mix, w_in, w_o, g_ffn, w_rg, w_re, w_up, w_gate, w_down, seq):
    n, d = x2.shape
    bsz = n // seq
    bf = jnp.bfloat16
    d_a = d // 2
    d_i = N_IDX_HEADS * IDX_DIM
    mod3 = _ada(c, w_ada, b_ada).reshape(bsz * 6, 1, d)

    c0 = 3 * d_a + d_i
    c1 = c0 + IDX_DIM + N_IDX_HEADS
    w_cat = jnp.concatenate([w_in[:, :c0], w_in[:, c0:c1],
                             jnp.zeros((d, LANES - (c1 - c0)), w_in.dtype), w_in[:, c1:]], axis=1).astype(bf)
    qat, ka, vat, qit, kiw, wit, qb, kb, vb, gb = _proj(x2, pos2, mod3, g_mix.reshape(1, d), w_cat, seq)

    n_kt = seq // TM
    ya = _dsa(qat, qit, wit, kiw.reshape(bsz, n_kt, TM, LANES), ka.reshape(bsz, n_kt, TM, d_a), vat, seq)
    yb = _ret(qb, kb, vb, gb, seq)

    w_route = jnp.concatenate([jnp.transpose(w_re, (1, 0, 2)).reshape(d, N_EXPERTS), w_rg,
                               jnp.zeros((d, LANES - N_EXPERTS - N_GROUPS), w_rg.dtype)], axis=1).astype(bf)
    x1, h2, route, counts = _mix_out(x2, ya, yb, w_o[:d_a].astype(bf), w_o[d_a:].astype(bf), mod3,
                                     g_ffn.reshape(1, d), w_route, seq)

    pst_row, zrow, n_used, blk_e = _block_layout(counts, n)
    dest = _plan(route, pst_row)
    xs = _dispatch(h2, dest, zrow, n_used, blk_e.shape[0])
    ys = _experts(xs, w_gate, w_up, w_down, blk_e, n_used)
    return x1, ys, dest, route, mod3


def kernel(x, c, positions, w_ada, b_ada, g_norm_mix, w_in, w_o, g_norm_ffn, w_router_group, w_router_expert,
           w_up, w_gate, w_down, g_norm_final):
    bsz, seq, d = x.shape
    depth = w_ada.shape[0]
    assert depth == 1, "the final norm is fused into the last layer's combine kernel"
    assert seq % (2 * TM) == 0 and seq % RET_C == 0 and seq % MIX_TM == 0 and (2 * bsz * seq) % BM == 0
    assert (bsz * seq) % PLAN_TM == 0 and PLAN_TM % TM == 0
    x2 = x.reshape(bsz * seq, d)
    pos2 = positions.astype(jnp.float32).reshape(bsz * seq, 1)
    x1, ys, dest, route, mod3 = _layer(x2, c, pos2, w_ada[0], b_ada[0], g_norm_mix[0], w_in[0], w_o[0],
                                       g_norm_ffn[0], w_router_group[0], w_router_expert[0], w_up[0], w_gate[0],
                                       w_down[0], seq)
    out = _final(x1, ys, dest, route, mod3, g_norm_final.reshape(1, d), seq)
    return out.reshape(bsz, seq, d)
```

```python
import functools

import jax
import jax.numpy as jnp
import numpy as np
from jax import lax
from jax.experimental import pallas as pl
from jax.experimental.pallas import tpu as pltpu

CHUNK = 64
HEAD_DIM = 64
N_IDX_HEADS = 16
IDX_DIM = 64
TOPK_MAX = 256
ROPE_THETA = 500000.0
ROT_DIM = HEAD_DIM // 4
RET_THETA = 10000.0
N_GROUPS = 4
EXPERTS_PER_GROUP = 8
N_EXPERTS = N_GROUPS * EXPERTS_PER_GROUP
EPS = 1e-6

LANES = 128
SUBLANES = 8
VMEM_LIMIT = 56 * 1024 * 1024

TM = 256
QB = TM
RET_C = 256
BM = 256
PLAN_TM = 1024
MIX_TM = 512
CNT_ROWS = 64
ONES_ROWS = 16
HEAD_GROUP = 4
IDX_TILES = 4
BISECT_PER_CHECK = 2
MAX_BISECT = 40

NEG_BIG = -1e30
LOG2E = 1.4426950408889634


def _cparams(sem):
    return pltpu.CompilerParams(dimension_semantics=sem, vmem_limit_bytes=VMEM_LIMIT)


def _ada_kernel(c_ref, w_ref, b_ref, o_ref):
    o_ref[...] = jnp.dot(c_ref[...], w_ref[...], preferred_element_type=jnp.float32) + b_ref[...]


def _ada(c, w_ada, b_ada):
    bsz, d = c.shape
    n_out = w_ada.shape[1]
    return pl.pallas_call(
        _ada_kernel,
        grid=(n_out // d,),
        in_specs=[pl.BlockSpec((bsz, d), lambda j: (0, 0)),
                  pl.BlockSpec((d, d), lambda j: (0, j)),
                  pl.BlockSpec((1, d), lambda j: (0, j))],
        out_specs=pl.BlockSpec((bsz, d), lambda j: (0, j)),
        out_shape=jax.ShapeDtypeStruct((bsz, n_out), jnp.float32),
        compiler_params=_cparams(("arbitrary",)),
        name="ada",
    )(c, w_ada, b_ada.reshape(1, n_out))


def _rmsnorm_mod(x, g, sc, sh):
    xn = x * lax.rsqrt(jnp.mean(x * x, axis=-1, keepdims=True) + EPS)
    return xn * g * (1.0 + sc) + sh


def _rope_lanes(x, cos, sin_lo, sin_hi, half):
    cols = []
    for k in range(x.shape[1] // LANES):
        xb = x[:, k * LANES:(k + 1) * LANES]
        cols.append(xb * cos + pltpu.roll(xb, LANES - half, 1) * sin_lo + pltpu.roll(xb, half, 1) * sin_hi)
    return cols[0] if len(cols) == 1 else jnp.concatenate(cols, axis=1)


def _proj_kernel(x_ref, pos_ref, sc_ref, sh_ref, g_ref, w_ref, tab_ref,
                 qat_ref, ka_ref, vat_ref, qit_ref, kiw_ref, wit_ref, qb_ref, kb_ref, vb_ref, gb_ref,
                 *, d_a, d_i, d_b):
    h = _rmsnorm_mod(x_ref[...], g_ref[...], sc_ref[0], sh_ref[0]).astype(jnp.bfloat16)
    pos = pos_ref[...]
    ang_a = pos * tab_ref[0:1, :]
    cos_a, sin_a = jnp.cos(ang_a), jnp.sin(ang_a)
    sa_lo, sa_hi = sin_a * tab_ref[1:2, :], sin_a * tab_ref[2:3, :]
    ang_b = pos * tab_ref[3:4, :]
    cos_b, sin_b = jnp.cos(ang_b), jnp.sin(ang_b)
    sb_lo, sb_hi = sin_b * tab_ref[4:5, :], sin_b * tab_ref[5:6, :]
    half_a, half_b = ROT_DIM // 2, HEAD_DIM // 2

    def seg(lo, width):
        return jnp.dot(h, w_ref[:, lo:lo + width], preferred_element_type=jnp.float32)

    o = 0
    qa = seg(o, d_a); o += d_a
    qat_ref[0, 0] = (_rope_lanes(qa, cos_a, sa_lo, sa_hi, half_a) * (HEAD_DIM ** -0.5 * LOG2E)).T.astype(qat_ref.dtype)
    ka = seg(o, d_a); o += d_a
    ka_ref[...] = _rope_lanes(ka, cos_a, sa_lo, sa_hi, half_a).astype(ka_ref.dtype)
    vat_ref[0, 0] = seg(o, d_a).T.astype(vat_ref.dtype); o += d_a
    qi = seg(o, d_i); o += d_i
    qit_ref[0, 0] = (_rope_lanes(qi, cos_a, sa_lo, sa_hi, half_a) * (IDX_DIM ** -0.5)).T.astype(qit_ref.dtype)
    kw = seg(o, LANES); o += LANES
    kiw_ref[...] = _rope_lanes(kw, cos_a, sa_lo, sa_hi, half_a).astype(kiw_ref.dtype)
    wit_ref[0, 0] = kw.T[IDX_DIM:IDX_DIM + N_IDX_HEADS, :] * (N_IDX_HEADS ** -0.5)
    qb = seg(o, d_b); o += d_b
    qb_ref[...] = _rope_lanes(qb, cos_b, sb_lo, sb_hi, half_b).astype(qb_ref.dtype)
    kb = seg(o, d_b); o += d_b
    kb_ref[...] = (_rope_lanes(kb, cos_b, sb_lo, sb_hi, half_b) * (HEAD_DIM ** -0.5)).astype(kb_ref.dtype)
    vb_ref[...] = seg(o, d_b).astype(vb_ref.dtype); o += d_b
    gb_ref[...] = seg(o, d_b)


def _rope_tables():
    lane = jnp.arange(LANES) % HEAD_DIM
    rows = []
    for rot, theta in ((ROT_DIM, ROPE_THETA), (HEAD_DIM, RET_THETA)):
        half = rot // 2
        inv_freq = theta ** (-jnp.arange(half, dtype=jnp.float32) / half)
        rows.append(jnp.where(lane < rot, inv_freq[lane % half], 0.0))
        rows.append(jnp.where(lane < half, -1.0, 0.0))
        rows.append(jnp.where((lane >= half) & (lane < rot), 1.0, 0.0))
    rows += [jnp.zeros((LANES,), jnp.float32)] * 2
    return jnp.stack(rows).astype(jnp.float32)


def _proj(x2, pos2, mod3, g_mix, w_cat, seq):
    n, d = x2.shape
    bsz = n // seq
    d_a = d // 2
    d_b = d // 2
    d_i = N_IDX_HEADS * IDX_DIM
    tiles_per_seq = seq // TM
    tab = _rope_tables()
    row = lambda i: (i, 0)
    const = lambda i: (0, 0)
    tile4 = lambda i: (i // tiles_per_seq, i % tiles_per_seq, 0, 0)
    bf = jnp.bfloat16
    out_shape = (
        jax.ShapeDtypeStruct((bsz, tiles_per_seq, d_a, TM), bf),
        jax.ShapeDtypeStruct((n, d_a), bf),
        jax.ShapeDtypeStruct((bsz, tiles_per_seq, d_a, TM), bf),
        jax.ShapeDtypeStruct((bsz, tiles_per_seq, d_i, TM), bf),
        jax.ShapeDtypeStruct((n, LANES), bf),
        jax.ShapeDtypeStruct((bsz, tiles_per_seq, N_IDX_HEADS, TM), jnp.float32),
        jax.ShapeDtypeStruct((n, d_b), bf),
        jax.ShapeDtypeStruct((n, d_b), bf),
        jax.ShapeDtypeStruct((n, d_b), bf),
        jax.ShapeDtypeStruct((n, d_b), jnp.float32),
    )
    out_specs = (
        pl.BlockSpec((1, 1, d_a, TM), tile4),
        pl.BlockSpec((TM, d_a), row),
        pl.BlockSpec((1, 1, d_a, TM), tile4),
        pl.BlockSpec((1, 1, d_i, TM), tile4),
        pl.BlockSpec((TM, LANES), row),
        pl.BlockSpec((1, 1, N_IDX_HEADS, TM), tile4),
        pl.BlockSpec((TM, d_b), row),
        pl.BlockSpec((TM, d_b), row),
        pl.BlockSpec((TM, d_b), row),
        pl.BlockSpec((TM, d_b), row),
    )
    return pl.pallas_call(
        functools.partial(_proj_kernel, d_a=d_a, d_i=d_i, d_b=d_b),
        grid=(n // TM,),
        in_specs=[pl.BlockSpec((TM, d), row),
                  pl.BlockSpec((TM, 1), row),
                  pl.BlockSpec((1, 1, d), lambda i: ((i // tiles_per_seq) * 6 + 1, 0, 0)),
                  pl.BlockSpec((1, 1, d), lambda i: ((i // tiles_per_seq) * 6 + 0, 0, 0)),
                  pl.BlockSpec((1, d), const),
                  pl.BlockSpec(w_cat.shape, const),
                  pl.BlockSpec(tab.shape, const)],
        out_specs=out_specs,
        out_shape=out_shape,
        compiler_params=_cparams(("arbitrary",)),
        name="proj",
    )(x2, pos2, mod3, mod3, g_mix, w_cat, tab)


def _row_blocks(x, rows):
    return [x[r * rows:(r + 1) * rows] for r in range(x.shape[0] // rows)]


def _dsa_kernel(qat_ref, qit_ref, wit_ref, kiw_ref, ka_ref, vat_ref, o_ref,
                qix_ref, qmx_ref, sc_ref, m_ref, l_ref, acc_ref, sa_ref, sb_ref, mxa_ref, mxb_ref, lohi_ref,
                *, k_top, n_heads):
    i = pl.program_id(1)
    n_grp = (i * QB + QB + 2 * TM - 1) // (2 * TM)
    n_kt = 2 * n_grp
    n_real = (i * QB + QB + TM - 1) // TM

    zero_rows = jnp.zeros((LANES - IDX_DIM, QB), qix_ref.dtype)
    for h in range(N_IDX_HEADS):
        qix_ref[h] = jnp.concatenate([qit_ref[0, 0, h * IDX_DIM:(h + 1) * IDX_DIM, :], zero_rows], axis=0)
    row_q = lax.broadcasted_iota(jnp.int32, (LANES, QB), 0)
    for h in range(n_heads):
        pair = qat_ref[0, 0, (h // 2) * LANES:(h // 2 + 1) * LANES, :]
        own = (row_q < HEAD_DIM) if h % 2 == 0 else (row_q >= HEAD_DIM)
        qmx_ref[h] = jnp.where(own, pair, jnp.zeros_like(pair))

    q_chunk = (i * QB + lax.broadcasted_iota(jnp.int32, (TM, QB), 1)) // CHUNK
    key_in_tile = lax.broadcasted_iota(jnp.int32, (TM, QB), 0)
    w_all = wit_ref[0, 0]

    def idx_tiles(tiles):
        lo, hi = lohi_ref[0], lohi_ref[1]
        for j in tiles:
            kt = kiw_ref[0, j]
            acc = None
            for h in range(N_IDX_HEADS):
                d = jnp.dot(kt, qix_ref[h], preferred_element_type=jnp.float32)
                t = w_all[h:h + 1, :] * jnp.maximum(d, 0.0)
                acc = t if acc is None else acc + t
            adm = (j * TM + key_in_tile) // CHUNK <= q_chunk
            sc_ref[j] = jnp.where(adm, acc, -jnp.inf)
            lo = jnp.minimum(lo, functools.reduce(jnp.minimum, _row_blocks(jnp.where(adm, acc, jnp.inf), SUBLANES)))
            hi = jnp.maximum(hi, functools.reduce(jnp.maximum, _row_blocks(jnp.where(adm, acc, -jnp.inf), SUBLANES)))
        lohi_ref[0], lohi_ref[1] = lo, hi

    lohi_ref[0] = jnp.full((SUBLANES, QB), jnp.inf, jnp.float32)
    lohi_ref[1] = jnp.full((SUBLANES, QB), -jnp.inf, jnp.float32)

    def idx_step(g, carry):
        idx_tiles([IDX_TILES * g + u for u in range(IDX_TILES)])
        return carry

    lax.fori_loop(0, n_kt // IDX_TILES, idx_step, 0)

    @pl.when(n_kt % IDX_TILES != 0)
    def _():
        idx_tiles([n_kt - 2, n_kt - 1])

    lo = jnp.min(lohi_ref[0], axis=0, keepdims=True)
    hi = jnp.max(lohi_ref[1], axis=0, keepdims=True)
    n_adm = ((i * QB + lax.broadcasted_iota(jnp.int32, (1, QB), 1)) // CHUNK + 1) * CHUNK

    def bisect_cond(carry):
        it, _, _, _, unsettled = carry
        return (it < MAX_BISECT) & (unsettled > 0.0)

    def bisect(carry):
        it, lo, hi, cnt_lo, _ = carry
        for _ in range(BISECT_PER_CHECK):
            mid = lo + (hi - lo) * 0.5
            mid_b = jnp.broadcast_to(mid, (CNT_ROWS, QB))

            def count_tile(j, cnt, mid_b=mid_b):
                for blk in _row_blocks(sc_ref[j], CNT_ROWS):
                    cnt = cnt + jnp.where(blk >= mid_b, 1.0, 0.0)
                return cnt

            cnt = lax.fori_loop(0, n_real, count_tile, jnp.zeros((CNT_ROWS, QB), jnp.float32))
            c = jnp.sum(cnt, axis=0, keepdims=True)
            ge = c >= k_top
            lo = jnp.where(ge, mid, lo)
            hi = jnp.where(ge, hi, mid)
            cnt_lo = jnp.where(ge, c, cnt_lo)
        return it + BISECT_PER_CHECK, lo, hi, cnt_lo, jnp.max(jnp.where(cnt_lo > k_top, 1.0, 0.0))

    cnt0 = n_adm.astype(jnp.float32)
    _, thr, _, _, _ = lax.while_loop(bisect_cond, bisect,
                                     (0, lo, hi, cnt0, jnp.max(jnp.where(cnt0 > k_top, 1.0, 0.0))))

    def bias_tile(j, carry):
        sc_ref[j] = jnp.where(sc_ref[j] >= thr, 0.0, NEG_BIG)
        return carry

    lax.fori_loop(0, n_kt, bias_tile, 0)

    m_ref[...] = jnp.full(m_ref.shape, NEG_BIG, jnp.float32)
    l_ref[...] = jnp.zeros(l_ref.shape, jnp.float32)
    acc_ref[...] = jnp.zeros(acc_ref.shape, jnp.float32)

    def pair(h):
        return slice((h // 2) * LANES, (h // 2 + 1) * LANES)

    def logits_into(s_ref, mx_ref, j, heads):
        bias = sc_ref[j]
        for h in heads:
            s = jnp.dot(ka_ref[0, j, :, pair(h)], qmx_ref[h], preferred_element_type=jnp.float32) + bias
            s_ref[h] = s
            mx_ref[h] = jnp.max(s, axis=0, keepdims=True)

    ones_rows = jnp.ones((ONES_ROWS, TM), vat_ref.dtype)

    def absorb(s_ref, mx_ref, j, heads):
        for h in heads:
            m_old = m_ref[h]
            m_new = jnp.maximum(m_old, mx_ref[h])
            alpha = jnp.exp2(m_old - m_new)
            p = jnp.exp2(s_ref[h] - m_new).astype(vat_ref.dtype)
            pv = jnp.dot(jnp.concatenate([vat_ref[0, j, pair(h), :], ones_rows], axis=0), p,
                         preferred_element_type=jnp.float32)
            acc_ref[h] = acc_ref[h] * alpha + pv[0:LANES]
            l_ref[h] = l_ref[h] * alpha + pv[LANES:LANES + SUBLANES]
            m_ref[h] = m_new

    for h0 in range(0, n_heads, HEAD_GROUP):
        heads = range(h0, h0 + HEAD_GROUP)
        logits_into(sa_ref, mxa_ref, 0, heads)

        def tile_pair(t, heads=heads):
            logits_into(sb_ref, mxb_ref, t + 1, heads)
            absorb(sa_ref, mxa_ref, t, heads)
            logits_into(sa_ref, mxa_ref, jnp.minimum(t + 2, n_kt - 1), heads)
            absorb(sb_ref, mxb_ref, t + 1, heads)

        def attn_quad(g, carry):
            tile_pair(4 * g)
            tile_pair(4 * g + 2)
            return carry

        lax.fori_loop(0, n_kt // 4, attn_quad, 0)

        @pl.when(n_kt % 4 != 0)
        def _():
            tile_pair(n_kt - 2)

    for hp in range(n_heads // 2):
        even = acc_ref[2 * hp] / l_ref[2 * hp, 0:1, :]
        odd = acc_ref[2 * hp + 1] / l_ref[2 * hp + 1, 0:1, :]
        o_ref[:, hp * LANES:(hp + 1) * LANES] = jnp.where(row_q < HEAD_DIM, even, odd).T.astype(o_ref.dtype)


def _dsa(qat, qit, wit, kiw4, ka4, vat, seq):
    bsz, n_kt, d_a, _ = qat.shape
    n_heads = d_a // HEAD_DIM
    n_qb = seq // QB
    k_top = min(TOPK_MAX, seq // 4)
    qtile = lambda b, i: (b, i, 0, 0)
    per_b = lambda b, i: (b, 0, 0, 0)
    f32 = jnp.float32
    return pl.pallas_call(
        functools.partial(_dsa_kernel, k_top=float(k_top), n_heads=n_heads),
        grid=(bsz, n_qb),
        in_specs=[pl.BlockSpec((1, 1, d_a, QB), qtile),
                  pl.BlockSpec((1, 1, qit.shape[2], QB), qtile),
                  pl.BlockSpec((1, 1, N_IDX_HEADS, QB), qtile),
                  pl.BlockSpec((1, n_kt, TM, LANES), per_b, pipeline_mode=pl.Buffered(1)),
                  pl.BlockSpec((1, n_kt, TM, d_a), per_b, pipeline_mode=pl.Buffered(1)),
                  pl.BlockSpec((1, n_kt, d_a, TM), per_b, pipeline_mode=pl.Buffered(1))],
        out_specs=pl.BlockSpec((QB, d_a), lambda b, i: (b * n_qb + i, 0)),
        out_shape=jax.ShapeDtypeStruct((bsz * seq, d_a), jnp.bfloat16),
        scratch_shapes=[pltpu.VMEM((N_IDX_HEADS, LANES, QB), jnp.bfloat16),
                        pltpu.VMEM((n_heads, LANES, QB), jnp.bfloat16),
                        pltpu.VMEM((n_kt, TM, QB), f32),
                        pltpu.VMEM((n_heads, 1, QB), f32),
                        pltpu.VMEM((n_heads, SUBLANES, QB), f32),
                        pltpu.VMEM((n_heads, LANES, QB), f32),
                        pltpu.VMEM((n_heads, TM, QB), f32),
                        pltpu.VMEM((n_heads, TM, QB), f32),
                        pltpu.VMEM((n_heads, 1, QB), f32),
                        pltpu.VMEM((n_heads, 1, QB), f32),
                        pltpu.VMEM((2, SUBLANES, QB), f32)],
        compiler_params=_cparams(("arbitrary", "arbitrary")),
        name="dsa",
    )(qat, qit, wit, kiw4, ka4, vat)


def _group_mean(y, avg):
    hi = y.astype(jnp.bfloat16)
    lo = (y - hi.astype(jnp.float32)).astype(jnp.bfloat16)
    return (jnp.dot(hi, avg, preferred_element_type=jnp.float32)
            + jnp.dot(lo, avg, preferred_element_type=jnp.float32))


def _ret_kernel(q_ref, k_ref, v_ref, g_ref, dec_ref, zt_ref, xi_ref, gc_ref, blk_ref, avg_ref, o_ref, st_ref,
                *, n_heads):
    @pl.when(pl.program_id(1) == 0)
    def _():
        st_ref[...] = jnp.zeros_like(st_ref)

    even = lax.broadcasted_iota(jnp.int32, (RET_C, LANES), 1) < HEAD_DIM
    avg = avg_ref[...]
    for p in range(n_heads // 2):
        sl = slice(p * LANES, (p + 1) * LANES)
        qp, kp, vp = q_ref[:, sl], k_ref[:, sl], v_ref[:, sl]
        kpt = kp.astype(jnp.float32).T
        kpt_b = kpt.astype(kp.dtype)
        inner = None
        for e in range(2):
            q_e = jnp.where(even if e == 0 else jnp.logical_not(even), qp, jnp.zeros_like(qp))
            s = jnp.dot(q_e, kpt_b, preferred_element_type=jnp.float32) * dec_ref[2 * p + e]
            t = jnp.dot(s.astype(vp.dtype), vp, preferred_element_type=jnp.float32)
            inner = t if e == 0 else jnp.where(even, inner, t)
        state = st_ref[p]
        cross = jnp.dot(qp, state.astype(qp.dtype), preferred_element_type=jnp.float32) * xi_ref[p]
        y = inner + cross
        yc = y - _group_mean(y, avg)
        yn = yc * lax.rsqrt(_group_mean(yc * yc, avg) + EPS)
        g = g_ref[:, sl]
        o_ref[:, sl] = (g / (1.0 + jnp.exp(-g)) * yn).astype(o_ref.dtype)
        kz = (kpt * zt_ref[p]).astype(kp.dtype)
        kv = jnp.dot(kz, vp, preferred_element_type=jnp.float32)
        st_ref[p] = state * gc_ref[p] + kv * blk_ref[...]


def _ret_consts(n_heads):
    log_gamma = jnp.log1p(-jnp.exp2(-5.0 - jnp.arange(n_heads, dtype=jnp.float32)))
    pos = jnp.arange(RET_C, dtype=jnp.float32)
    diff = pos[:, None] - pos[None, :]
    dec = jnp.where(diff[None] >= 0, jnp.exp(jnp.maximum(diff, 0.0)[None] * log_gamma[:, None, None]), 0.0)
    zeta = jnp.exp((RET_C - 1.0 - pos)[None, :] * log_gamma[:, None])
    xi = jnp.exp((pos + 1.0)[None, :] * log_gamma[:, None])
    gc = jnp.exp(RET_C * log_gamma)
    n_pairs = n_heads // 2
    lanes = lambda a: jnp.repeat(a.reshape(n_pairs, 2, -1), HEAD_DIM, axis=1)
    zt = lanes(zeta)
    xi_p = jnp.swapaxes(lanes(xi), 1, 2)
    gc_p = jnp.broadcast_to(lanes(gc[:, None]), (n_pairs, LANES, LANES))
    head_of = jnp.arange(LANES) // HEAD_DIM
    blk = (head_of[:, None] == head_of[None, :]).astype(jnp.float32)
    avg = (blk / HEAD_DIM).astype(jnp.bfloat16)
    f32 = lambda a: a.astype(jnp.float32)
    return dec, f32(zt), f32(xi_p), f32(gc_p), blk, avg


def _ret(qb, kb, vb, gb, seq):
    n, d_b = qb.shape
    bsz = n // seq
    n_heads = d_b // HEAD_DIM
    n_c = seq // RET_C
    consts = _ret_consts(n_heads)
    row = lambda b, c: (b * n_c + c, 0)
    const_spec = lambda a: pl.BlockSpec(a.shape, lambda b, c: (0,) * a.ndim)
    return pl.pallas_call(
        functools.partial(_ret_kernel, n_heads=n_heads),
        grid=(bsz, n_c),
        in_specs=[pl.BlockSpec((RET_C, d_b), row)] * 4 + [const_spec(a) for a in consts],
        out_specs=pl.BlockSpec((RET_C, d_b), row),
        out_shape=jax.ShapeDtypeStruct((n, d_b), jnp.bfloat16),
        scratch_shapes=[pltpu.VMEM((n_heads // 2, LANES, LANES), jnp.float32)],
        compiler_params=_cparams(("arbitrary", "arbitrary")),
        name="ret",
    )(qb, kb, vb, gb, *consts)


def _tiles_shape(rows, d):
    return (rows // SUBLANES, d // LANES, SUBLANES, LANES)


def _to_tiles(ref, x):
    for s in range(ref.shape[1]):
        ref[:, s] = x[:, s * LANES:(s + 1) * LANES].reshape(ref.shape[0], SUBLANES, LANES)


def _from_tiles(ref):
    rows = ref.shape[0] * SUBLANES
    return jnp.concatenate([ref[:, s].reshape(rows, LANES) for s in range(ref.shape[1])], axis=1)


def _lane_first_eq(x, m, lane):
    return jnp.min(jnp.where(x == m, lane, float(LANES)), axis=1, keepdims=True)


def _mix_out_kernel(x_ref, ya_ref, yb_ref, woa_ref, wob_ref, gt_ref, sc_ref, sh_ref, g_ref, wr_ref, tri_ref,
                    x1_ref, h2_ref, route_ref, cnt_ref, carry_ref):
    @pl.when(pl.program_id(0) == 0)
    def _():
        carry_ref[...] = jnp.zeros_like(carry_ref)

    mix = (jnp.dot(ya_ref[...], woa_ref[...], preferred_element_type=jnp.float32)
           + jnp.dot(yb_ref[...], wob_ref[...], preferred_element_type=jnp.float32))
    x1 = x_ref[...] + gt_ref[0] * mix
    x1_ref[...] = x1
    h2 = _rmsnorm_mod(x1, g_ref[...], sc_ref[0], sh_ref[0])
    _to_tiles(h2_ref, h2)

    lg = jnp.dot(h2.astype(jnp.bfloat16), wr_ref[...], preferred_element_type=jnp.float32)
    lane = lax.broadcasted_iota(jnp.int32, lg.shape, 1).astype(jnp.float32)
    is_grp = (lane >= N_EXPERTS) & (lane < N_EXPERTS + N_GROUPS)
    gl = jnp.where(is_grp, lg, -jnp.inf)
    gmax = jnp.max(gl, axis=1, keepdims=True)
    grp = _lane_first_eq(gl, gmax, lane) - N_EXPERTS
    p_grp = 1.0 / jnp.sum(jnp.exp(gl - gmax), axis=1, keepdims=True)
    in_grp = jnp.floor(lane * (1.0 / EXPERTS_PER_GROUP)) == grp
    f = jnp.where(in_grp & (lane < N_EXPERTS), lg, -jnp.inf)
    f1 = jnp.max(f, axis=1, keepdims=True)
    e1 = _lane_first_eq(f, f1, lane)
    f = jnp.where(lane == e1, -jnp.inf, f)
    f2 = jnp.max(f, axis=1, keepdims=True)
    e2 = _lane_first_eq(f, f2, lane)
    a2 = jnp.exp(f2 - f1)
    w1 = p_grp / (1.0 + a2)
    w2 = p_grp * a2 / (1.0 + a2)

    oh1 = jnp.where(lane == e1, 1.0, 0.0)
    oh2 = jnp.where(lane == e2, 1.0, 0.0)
    both = oh1 + oh2
    before = jnp.dot(tri_ref[...], both.astype(jnp.bfloat16), preferred_element_type=jnp.float32) + carry_ref[...]
    r1 = jnp.sum(before * oh1, axis=1, keepdims=True)
    r2 = jnp.sum(before * oh2, axis=1, keepdims=True)
    carry = carry_ref[...] + jnp.sum(both, axis=0, keepdims=True)
    carry_ref[...] = carry
    cnt_ref[...] = carry

    out = jnp.zeros(lg.shape, jnp.float32)
    for col, val in enumerate((e1, e2, w1, w2, r1, r2)):
        out = jnp.where(lane == col, val, out)
    route_ref[...] = out


def _mix_out(x2, ya, yb, wo_a, wo_b, mod3, g_ffn, w_route, seq):
    n, d = x2.shape
    tm = MIX_TM
    tiles_per_seq = seq // tm
    tri = jnp.asarray(np.tril(np.ones((tm, tm), np.float32), -1), jnp.bfloat16)
    row = lambda i: (i, 0)
    const = lambda i: (0, 0)
    modk = lambda k: pl.BlockSpec((1, 1, d), lambda i: ((i // tiles_per_seq) * 6 + k, 0, 0))
    return pl.pallas_call(
        _mix_out_kernel,
        grid=(n // tm,),
        in_specs=[pl.BlockSpec((tm, d), row),
                  pl.BlockSpec((tm, ya.shape[1]), row),
                  pl.BlockSpec((tm, yb.shape[1]), row),
                  pl.BlockSpec(wo_a.shape, const),
                  pl.BlockSpec(wo_b.shape, const),
                  modk(2), modk(4), modk(3),
                  pl.BlockSpec((1, d), const),
                  pl.BlockSpec(w_route.shape, const),
                  pl.BlockSpec(tri.shape, const)],
        out_specs=(pl.BlockSpec((tm, d), row), pl.BlockSpec(_tiles_shape(tm, d), lambda i: (i, 0, 0, 0)),
                   pl.BlockSpec((tm, LANES), row), pl.BlockSpec((1, LANES), const)),
        out_shape=(jax.ShapeDtypeStruct((n, d), jnp.float32), jax.ShapeDtypeStruct(_tiles_shape(n, d), jnp.float32),
                   jax.ShapeDtypeStruct((n, LANES), jnp.float32), jax.ShapeDtypeStruct((1, LANES), jnp.float32)),
        scratch_shapes=[pltpu.VMEM((1, LANES), jnp.float32)],
        compiler_params=_cparams(("arbitrary",)),
        name="mix_out",
    )(x2, ya, yb, wo_a, wo_b, mod3, mod3, mod3, g_ffn, w_route, tri)


def _plan_kernel(route_ref, pst_ref, dest_ref):
    r = route_ref[...]
    lane = lax.broadcasted_iota(jnp.int32, r.shape, 1).astype(jnp.float32)
    pst = pst_ref[...]
    d1 = jnp.sum(jnp.where(lane == r[:, 0:1], pst, 0.0), axis=1, keepdims=True) + r[:, 4:5]
    d2 = jnp.sum(jnp.where(lane == r[:, 1:2], pst, 0.0), axis=1, keepdims=True) + r[:, 5:6]
    t1 = jnp.floor(d1 * (1.0 / SUBLANES))
    t2 = jnp.floor(d2 * (1.0 / SUBLANES))
    packed = jnp.zeros(r.shape, jnp.float32)
    for k, v in enumerate((t1, d1 - t1 * SUBLANES, t2, d2 - t2 * SUBLANES)):
        packed = jnp.where(lane == float(k), v, packed)
    dest_ref[0] = packed.T[0:8, :].astype(jnp.int32)


def _dest_block(i):
    return (i // (PLAN_TM // TM), 0, i % (PLAN_TM // TM))


def _plan(route, pst_row):
    n = route.shape[0]
    return pl.pallas_call(
        _plan_kernel,
        grid=(n // PLAN_TM,),
        in_specs=[pl.BlockSpec((PLAN_TM, LANES), lambda i: (i, 0)), pl.BlockSpec((1, LANES), lambda i: (0, 0))],
        out_specs=pl.BlockSpec((1, 8, PLAN_TM), lambda i: (i, 0, 0)),
        out_shape=jax.ShapeDtypeStruct((n // PLAN_TM, 8, PLAN_TM), jnp.int32),
        compiler_params=_cparams(("arbitrary",)),
        name="plan",
    )(route, pst_row)


def _dispatch_kernel(zrow_ref, n_used_ref, dest_ref, h2_ref, xs_hbm, zbuf, sem, zsem, *, n_blk):
    i = pl.program_id(0)

    @pl.when(i == 0)
    def _():
        zbuf[...] = jnp.zeros(zbuf.shape, zbuf.dtype)
        blk_tiles = BM // SUBLANES
        for e in range(N_EXPERTS):
            pltpu.make_async_copy(zbuf, xs_hbm.at[pl.ds(zrow_ref[e], blk_tiles)], zsem).start()
        for e in range(N_EXPERTS):
            pltpu.make_async_copy(zbuf, xs_hbm.at[pl.ds(0, blk_tiles)], zsem).wait()
        for b in range(N_EXPERTS + 1):
            @pl.when(n_used_ref[0] + b <= n_blk)
            def _():
                tail = pltpu.make_async_copy(zbuf, xs_hbm.at[pl.ds((n_used_ref[0] + b) * blk_tiles, blk_tiles)], zsem)
                tail.start()
                tail.wait()

    def issue(k, c):
        for u in range(SUBLANES):
            r = k * SUBLANES + u
            row = h2_ref.at[k, :, u, :]
            pltpu.make_async_copy(row, xs_hbm.at[dest_ref[0, 0, r], :, dest_ref[0, 1, r], :], sem).start()
            pltpu.make_async_copy(row, xs_hbm.at[dest_ref[0, 2, r], :, dest_ref[0, 3, r], :], sem).start()
        return c

    lax.fori_loop(0, TM // SUBLANES, issue, 0)
    for _ in range(2):
        pltpu.make_async_copy(h2_ref, xs_hbm.at[pl.ds(0, TM // SUBLANES)], sem).wait()


def _dispatch(h2, dest, zrow, n_used, n_blk):
    n_rows = (n_blk + 1) * BM
    d = h2.shape[1] * LANES
    grid_spec = pltpu.PrefetchScalarGridSpec(
        num_scalar_prefetch=2,
        grid=(h2.shape[0] * SUBLANES // TM,),
        in_specs=[pl.BlockSpec((1, 8, TM), lambda i, z, nu: _dest_block(i), memory_space=pltpu.SMEM),
                  pl.BlockSpec(_tiles_shape(TM, d), lambda i, z, nu: (i, 0, 0, 0))],
        out_specs=pl.BlockSpec(memory_space=pl.ANY),
        scratch_shapes=[pltpu.VMEM(_tiles_shape(BM, d), h2.dtype),
                        pltpu.SemaphoreType.DMA(()),
                        pltpu.SemaphoreType.DMA(())],
    )
    return pl.pallas_call(
        functools.partial(_dispatch_kernel, n_blk=n_blk),
        grid_spec=grid_spec,
        out_shape=jax.ShapeDtypeStruct(_tiles_shape(n_rows, d), h2.dtype),
        compiler_params=_cparams(("arbitrary",)),
        name="dispatch",
    )(zrow, n_used, dest, h2)


def _experts_kernel(blk_e_ref, n_used_ref, x_ref, wg_ref, wu_ref, wd_ref, y_ref, wg_bf, wu_bf, wd_bf):
    j = pl.program_id(0)

    @pl.when((j == 0) | (blk_e_ref[j] != blk_e_ref[jnp.maximum(j - 1, 0)]))
    def _():
        wg_bf[...] = wg_ref[0].astype(wg_bf.dtype)
        wu_bf[...] = wu_ref[0].astype(wu_bf.dtype)
        wd_bf[...] = wd_ref[0].astype(wd_bf.dtype)

    @pl.when(j < n_used_ref[0])
    def _():
        x = _from_tiles(x_ref).astype(wg_bf.dtype)
        a = jnp.dot(x, wg_bf[...], preferred_element_type=jnp.float32)
        b = jnp.dot(x, wu_bf[...], preferred_element_type=jnp.float32)
        hmid = (a / (1.0 + jnp.exp(-a)) * b).astype(x.dtype)
        _to_tiles(y_ref, jnp.dot(hmid, wd_bf[...], preferred_element_type=jnp.float32))

    @pl.when(j >= n_used_ref[0])
    def _():
        y_ref[...] = jnp.zeros(y_ref.shape, y_ref.dtype)


def _experts(xs, wg, wu, wd, blk_e, n_used):
    n_blk = blk_e.shape[0]
    d, d_e = wg.shape[1], wg.shape[2]
    blk = _tiles_shape(BM, d)
    grid_spec = pltpu.PrefetchScalarGridSpec(
        num_scalar_prefetch=2,
        grid=(n_blk,),
        in_specs=[pl.BlockSpec(blk, lambda j, be, nu: (jnp.minimum(j, nu[0] - 1), 0, 0, 0)),
                  pl.BlockSpec((1, d, d_e), lambda j, be, nu: (be[j], 0, 0)),
                  pl.BlockSpec((1, d, d_e), lambda j, be, nu: (be[j], 0, 0)),
                  pl.BlockSpec((1, d_e, d), lambda j, be, nu: (be[j], 0, 0))],
        out_specs=pl.BlockSpec(blk, lambda j, be, nu: (j, 0, 0, 0)),
        scratch_shapes=[pltpu.VMEM((d, d_e), jnp.bfloat16), pltpu.VMEM((d, d_e), jnp.bfloat16),
                        pltpu.VMEM((d_e, d), jnp.bfloat16)],
    )
    return pl.pallas_call(
        _experts_kernel,
        grid_spec=grid_spec,
        out_shape=jax.ShapeDtypeStruct(_tiles_shape(n_blk * BM, d), jnp.float32),
        compiler_params=_cparams(("arbitrary",)),
        name="experts",
    )(blk_e, n_used, xs, wg, wu, wd)


def _final_kernel(dest_ref, dest_next_ref, x1_ref, route_ref, gt_ref, g_ref, ys_hbm, o_ref, ybuf, sem):
    i = pl.program_id(0)
    slot = i % 2

    tiles = TM // SUBLANES

    def gather(d_ref, s):
        def issue(k, c):
            for u in range(SUBLANES):
                r = k * SUBLANES + u
                pltpu.make_async_copy(ys_hbm.at[d_ref[0, 0, r], :, d_ref[0, 1, r], :],
                                      ybuf.at[s, k, :, u, :], sem.at[s]).start()
                pltpu.make_async_copy(ys_hbm.at[d_ref[0, 2, r], :, d_ref[0, 3, r], :],
                                      ybuf.at[s, tiles + k, :, u, :], sem.at[s]).start()
            return c
        lax.fori_loop(0, tiles, issue, 0)

    @pl.when(i == 0)
    def _():
        gather(dest_ref, 0)

    @pl.when(i + 1 < pl.num_programs(0))
    def _():
        gather(dest_next_ref, 1 - slot)

    pltpu.make_async_copy(ys_hbm.at[pl.ds(0, 2 * tiles)], ybuf.at[slot], sem.at[slot]).wait()
    w1 = route_ref[:, 2:3]
    w2 = route_ref[:, 3:4]
    y0 = _from_tiles(ybuf.at[slot, 0:tiles])
    y1 = _from_tiles(ybuf.at[slot, tiles:2 * tiles])
    x2 = x1_ref[...] + gt_ref[0] * (w1 * y0 + w2 * y1)
    o_ref[...] = x2 * lax.rsqrt(jnp.mean(x2 * x2, axis=-1, keepdims=True) + EPS) * g_ref[...]


def _final(x1, ys, dest, route, mod3, g_final, seq):
    n, d = x1.shape
    n_tiles = n // TM
    tiles_per_seq = seq // TM
    row = lambda i: (i, 0)
    return pl.pallas_call(
        _final_kernel,
        grid=(n_tiles,),
        in_specs=[pl.BlockSpec((1, 8, TM), _dest_block, memory_space=pltpu.SMEM),
                  pl.BlockSpec((1, 8, TM), lambda i: _dest_block(jnp.minimum(i + 1, n_tiles - 1)),
                               memory_space=pltpu.SMEM),
                  pl.BlockSpec((TM, d), row),
                  pl.BlockSpec((TM, LANES), row),
                  pl.BlockSpec((1, 1, d), lambda i: ((i // tiles_per_seq) * 6 + 5, 0, 0)),
                  pl.BlockSpec((1, d), lambda i: (0, 0)),
                  pl.BlockSpec(memory_space=pl.ANY)],
        out_specs=pl.BlockSpec((TM, d), row),
        out_shape=jax.ShapeDtypeStruct((n, d), jnp.float32),
        scratch_shapes=[pltpu.VMEM((2,) + _tiles_shape(2 * TM, d), jnp.float32),
                        pltpu.SemaphoreType.DMA((2,))],
        compiler_params=_cparams(("arbitrary",)),
        name="final",
    )(dest, dest, x1, route, mod3, g_final, ys)


def _block_layout(counts, n):
    cnt = counts[0, :N_EXPERTS].astype(jnp.int32)
    blocks = (cnt + BM - 1) // BM
    bends = jnp.cumsum(blocks)
    pstarts = (bends - blocks) * BM
    n_blk = (2 * n) // BM + N_EXPERTS
    pst_row = jnp.zeros((1, LANES), jnp.float32).at[0, :N_EXPERTS].set(pstarts.astype(jnp.float32))
    zrow = ((pstarts + cnt) // SUBLANES).astype(jnp.int32)
    n_used = bends[-1:].astype(jnp.int32)
    blk_e = jnp.sum(bends[None, :] <= jnp.arange(n_blk, dtype=jnp.int32)[:, None], axis=1)
    blk_e = jnp.minimum(blk_e, N_EXPERTS - 1).astype(jnp.int32)
    return pst_row, zrow, n_used, blk_e


def _layer(x2, c, pos2, w_ada, b_ada, g_mix, w_in, w_o, g_ffn, w_rg, w_re, w_up, w_gate, w_down, seq):
    n, d = x2.shape
    bsz = n // seq
    bf = jnp.bfloat16
    d_a = d // 2
    d_i = N_IDX_HEADS * IDX_DIM
    mod3 = _ada(c, w_ada, b_ada).reshape(bsz * 6, 1, d)

    c0 = 3 * d_a + d_i
    c1 = c0 + IDX_DIM + N_IDX_HEADS
    w_cat = jnp.concatenate([w_in[:, :c0], w_in[:, c0:c1],
                             jnp.zeros((d, LANES - (c1 - c0)), w_in.dtype), w_in[:, c1:]], axis=1).astype(bf)
    qat, ka, vat, qit, kiw, wit, qb, kb, vb, gb = _proj(x2, pos2, mod3, g_mix.reshape(1, d), w_cat, seq)

    n_kt = seq // TM
    ya = _dsa(qat, qit, wit, kiw.reshape(bsz, n_kt, TM, LANES), ka.reshape(bsz, n_kt, TM, d_a), vat, seq)
    yb = _ret(qb, kb, vb, gb, seq)

    w_route = jnp.concatenate([jnp.transpose(w_re, (1, 0, 2)).reshape(d, N_EXPERTS), w_rg,
                               jnp.zeros((d, LANES - N_EXPERTS - N_GROUPS), w_rg.dtype)], axis=1).astype(bf)
    x1, h2, route, counts = _mix_out(x2, ya, yb, w_o[:d_a].astype(bf), w_o[d_a:].astype(bf), mod3,
                                     g_ffn.reshape(1, d), w_route, seq)

    pst_row, zrow, n_used, blk_e = _block_layout(counts, n)
    dest = _plan(route, pst_row)
    xs = _dispatch(h2, dest, zrow, n_used, blk_e.shape[0])
    ys = _experts(xs, w_gate, w_up, w_down, blk_e, n_used)
    return x1, ys, dest, route, mod3


def kernel(x, c, positions, w_ada, b_ada, g_norm_mix, w_in, w_o, g_norm_ffn, w_router_group, w_router_expert,
           w_up, w_gate, w_down, g_norm_final):
    bsz, seq, d = x.shape
    depth = w_ada.shape[0]
    assert depth == 1, "the final norm is fused into the last layer's combine kernel"
    assert seq % (2 * TM) == 0 and seq % RET_C == 0 and seq % MIX_TM == 0 and (2 * bsz * seq) % BM == 0
    assert (bsz * seq) % PLAN_TM == 0 and PLAN_TM % TM == 0
    x2 = x.reshape(bsz * seq, d)
    pos2 = positions.astype(jnp.float32).reshape(bsz * seq, 1)
    x1, ys, dest, route, mod3 = _layer(x2, c, pos2, w_ada[0], b_ada[0], g_norm_mix[0], w_in[0], w_o[0],
                                       g_norm_ffn[0], w_router_group[0], w_router_expert[0], w_up[0], w_gate[0],
                                       w_down[0], seq)
    out = _final(x1, ys, dest, route, mod3, g_norm_final.reshape(1, d), seq)
    return out.reshape(bsz, seq, d)
```

```python
import functools

import jax
import jax.numpy as jnp
import numpy as np
from jax import lax
from jax.experimental import pallas as pl
from jax.experimental.pallas import tpu as pltpu

CHUNK = 64
HEAD_DIM = 64
N_IDX_HEADS = 16
IDX_DIM = 64
TOPK_MAX = 256
ROPE_THETA = 500000.0
ROT_DIM = HEAD_DIM // 4
RET_THETA = 10000.0
N_GROUPS = 4
EXPERTS_PER_GROUP = 8
N_EXPERTS = N_GROUPS * EXPERTS_PER_GROUP
EPS = 1e-6

LANES = 128
SUBLANES = 8
VMEM_LIMIT = 56 * 1024 * 1024

TM = 256
QB = TM
RET_C = 256
BM = 256
PLAN_TM = 1024
MIX_TM = 512
CNT_ROWS = 64
ONES_ROWS = 16
HEAD_GROUP = 4
IDX_TILES = 4
BISECT_PER_CHECK = 2
MAX_BISECT = 40

NEG_BIG = -1e30
LOG2E = 1.4426950408889634


def _cparams(sem):
    return pltpu.CompilerParams(dimension_semantics=sem, vmem_limit_bytes=VMEM_LIMIT)


def _ada_kernel(c_ref, w_ref, b_ref, o_ref):
    o_ref[...] = jnp.dot(c_ref[...], w_ref[...], preferred_element_type=jnp.float32) + b_ref[...]


def _ada(c, w_ada, b_ada):
    bsz, d = c.shape
    n_out = w_ada.shape[1]
    return pl.pallas_call(
        _ada_kernel,
        grid=(n_out // d,),
        in_specs=[pl.BlockSpec((bsz, d), lambda j: (0, 0)),
                  pl.BlockSpec((d, d), lambda j: (0, j)),
                  pl.BlockSpec((1, d), lambda j: (0, j))],
        out_specs=pl.BlockSpec((bsz, d), lambda j: (0, j)),
        out_shape=jax.ShapeDtypeStruct((bsz, n_out), jnp.float32),
        compiler_params=_cparams(("arbitrary",)),
        name="ada",
    )(c, w_ada, b_ada.reshape(1, n_out))


def _rmsnorm_mod(x, g, sc, sh):
    xn = x * lax.rsqrt(jnp.mean(x * x, axis=-1, keepdims=True) + EPS)
    return xn * g * (1.0 + sc) + sh


def _rope_lanes(x, cos, sin_lo, sin_hi, half):
    cols = []
    for k in range(x.shape[1] // LANES):
        xb = x[:, k * LANES:(k + 1) * LANES]
        cols.append(xb * cos + pltpu.roll(xb, LANES - half, 1) * sin_lo + pltpu.roll(xb, half, 1) * sin_hi)
    return cols[0] if len(cols) == 1 else jnp.concatenate(cols, axis=1)


def _proj_kernel(x_ref, pos_ref, sc_ref, sh_ref, g_ref, w_ref, tab_ref,
                 qat_ref, ka_ref, vat_ref, qit_ref, kiw_ref, wit_ref, qb_ref, kb_ref, vb_ref, gb_ref,
                 *, d_a, d_i, d_b):
    h = _rmsnorm_mod(x_ref[...], g_ref[...], sc_ref[0], sh_ref[0]).astype(jnp.bfloat16)
    pos = pos_ref[...]
    ang_a = pos * tab_ref[0:1, :]
    cos_a, sin_a = jnp.cos(ang_a), jnp.sin(ang_a)
    sa_lo, sa_hi = sin_a * tab_ref[1:2, :], sin_a * tab_ref[2:3, :]
    ang_b = pos * tab_ref[3:4, :]
    cos_b, sin_b = jnp.cos(ang_b), jnp.sin(ang_b)
    sb_lo, sb_hi = sin_b * tab_ref[4:5, :], sin_b * tab_ref[5:6, :]
    half_a, half_b = ROT_DIM // 2, HEAD_DIM // 2

    def seg(lo, width):
        return jnp.dot(h, w_ref[:, lo:lo + width], preferred_element_type=jnp.float32)

    o = 0
    qa = seg(o, d_a); o += d_a
    qat_ref[0, 0] = (_rope_lanes(qa, cos_a, sa_lo, sa_hi, half_a) * (HEAD_DIM ** -0.5 * LOG2E)).T.astype(qat_ref.dtype)
    ka = seg(o, d_a); o += d_a
    ka_ref[...] = _rope_lanes(ka, cos_a, sa_lo, sa_hi, half_a).astype(ka_ref.dtype)
    vat_ref[0, 0] = seg(o, d_a).T.astype(vat_ref.dtype); o += d_a
    qi = seg(o, d_i); o += d_i
    qit_ref[0, 0] = (_rope_lanes(qi, cos_a, sa_lo, sa_hi, half_a) * (IDX_DIM ** -0.5)).T.astype(qit_ref.dtype)
    kw = seg(o, LANES); o += LANES
    kiw_ref[...] = _rope_lanes(kw, cos_a, sa_lo, sa_hi, half_a).astype(kiw_ref.dtype)
    wit_ref[0, 0] = kw.T[IDX_DIM:IDX_DIM + N_IDX_HEADS, :] * (N_IDX_HEADS ** -0.5)
    qb = seg(o, d_b); o += d_b
    qb_ref[...] = _rope_lanes(qb, cos_b, sb_lo, sb_hi, half_b).astype(qb_ref.dtype)
    kb = seg(o, d_b); o += d_b
    kb_ref[...] = (_rope_lanes(kb, cos_b, sb_lo, sb_hi, half_b) * (HEAD_DIM ** -0.5)).astype(kb_ref.dtype)
    vb_ref[...] = seg(o, d_b).astype(vb_ref.dtype); o += d_b
    gb_ref[...] = seg(o, d_b)


def _rope_tables():
    lane = jnp.arange(LANES) % HEAD_DIM
    rows = []
    for rot, theta in ((ROT_DIM, ROPE_THETA), (HEAD_DIM, RET_THETA)):
        half = rot // 2
        inv_freq = theta ** (-jnp.arange(half, dtype=jnp.float32) / half)
        rows.append(jnp.where(lane < rot, inv_freq[lane % half], 0.0))
        rows.append(jnp.where(lane < half, -1.0, 0.0))
        rows.append(jnp.where((lane >= half) & (lane < rot), 1.0, 0.0))
    rows += [jnp.zeros((LANES,), jnp.float32)] * 2
    return jnp.stack(rows).astype(jnp.float32)


def _proj(x2, pos2, mod3, g_mix, w_cat, seq):
    n, d = x2.shape
    bsz = n // seq
    d_a = d // 2
    d_b = d // 2
    d_i = N_IDX_HEADS * IDX_DIM
    tiles_per_seq = seq // TM
    tab = _rope_tables()
    row = lambda i: (i, 0)
    const = lambda i: (0, 0)
    tile4 = lambda i: (i // tiles_per_seq, i % tiles_per_seq, 0, 0)
    bf = jnp.bfloat16
    out_shape = (
        jax.ShapeDtypeStruct((bsz, tiles_per_seq, d_a, TM), bf),
        jax.ShapeDtypeStruct((n, d_a), bf),
        jax.ShapeDtypeStruct((bsz, tiles_per_seq, d_a, TM), bf),
        jax.ShapeDtypeStruct((bsz, tiles_per_seq, d_i, TM), bf),
        jax.ShapeDtypeStruct((n, LANES), bf),
        jax.ShapeDtypeStruct((bsz, tiles_per_seq, N_IDX_HEADS, TM), jnp.float32),
        jax.ShapeDtypeStruct((n, d_b), bf),
        jax.ShapeDtypeStruct((n, d_b), bf),
        jax.ShapeDtypeStruct((n, d_b), bf),
        jax.ShapeDtypeStruct((n, d_b), jnp.float32),
    )
    out_specs = (
        pl.BlockSpec((1, 1, d_a, TM), tile4),
        pl.BlockSpec((TM, d_a), row),
        pl.BlockSpec((1, 1, d_a, TM), tile4),
        pl.BlockSpec((1, 1, d_i, TM), tile4),
        pl.BlockSpec((TM, LANES), row),
        pl.BlockSpec((1, 1, N_IDX_HEADS, TM), tile4),
        pl.BlockSpec((TM, d_b), row),
        pl.BlockSpec((TM, d_b), row),
        pl.BlockSpec((TM, d_b), row),
        pl.BlockSpec((TM, d_b), row),
    )
    return pl.pallas_call(
        functools.partial(_proj_kernel, d_a=d_a, d_i=d_i, d_b=d_b),
        grid=(n // TM,),
        in_specs=[pl.BlockSpec((TM, d), row),
                  pl.BlockSpec((TM, 1), row),
                  pl.BlockSpec((1, 1, d), lambda i: ((i // tiles_per_seq) * 6 + 1, 0, 0)),
                  pl.BlockSpec((1, 1, d), lambda i: ((i // tiles_per_seq) * 6 + 0, 0, 0)),
                  pl.BlockSpec((1, d), const),
                  pl.BlockSpec(w_cat.shape, const),
                  pl.BlockSpec(tab.shape, const)],
        out_specs=out_specs,
        out_shape=out_shape,
        compiler_params=_cparams(("arbitrary",)),
        name="proj",
    )(x2, pos2, mod3, mod3, g_mix, w_cat, tab)


def _row_blocks(x, rows):
    return [x[r * rows:(r + 1) * rows] for r in range(x.shape[0] // rows)]


def _dsa_kernel(qat_ref, qit_ref, wit_ref, kiw_ref, ka_ref, vat_ref, tri_ref, o_ref,
                qix_ref, qmx_ref, sc_ref, m_ref, l_ref, acc_ref, sa_ref, sb_ref, mxa_ref, mxb_ref, lohi_ref,
                *, k_top, n_heads):
    i = pl.program_id(1)
    n_grp = (i * QB + QB + 2 * TM - 1) // (2 * TM)
    n_kt = 2 * n_grp
    n_real = (i * QB + QB + TM - 1) // TM

    zero_rows = jnp.zeros((LANES - IDX_DIM, QB), qix_ref.dtype)
    for h in range(N_IDX_HEADS):
        qix_ref[h] = jnp.concatenate([qit_ref[0, 0, h * IDX_DIM:(h + 1) * IDX_DIM, :], zero_rows], axis=0)
    row_q = lax.broadcasted_iota(jnp.int32, (LANES, QB), 0)
    for h in range(n_heads):
        pair = qat_ref[0, 0, (h // 2) * LANES:(h // 2 + 1) * LANES, :]
        own = (row_q < HEAD_DIM) if h % 2 == 0 else (row_q >= HEAD_DIM)
        qmx_ref[h] = jnp.where(own, pair, jnp.zeros_like(pair))

    q_chunk = (i * QB + lax.broadcasted_iota(jnp.int32, (TM, QB), 1)) // CHUNK
    key_in_tile = lax.broadcasted_iota(jnp.int32, (TM, QB), 0)
    w_all = wit_ref[0, 0]

    def idx_tiles(tiles):
        lo, hi = lohi_ref[0], lohi_ref[1]
        for j in tiles:
            kt = kiw_ref[0, j]
            acc = None
            for h in range(N_IDX_HEADS):
                d = jnp.dot(kt, qix_ref[h], preferred_element_type=jnp.float32)
                t = w_all[h:h + 1, :] * jnp.maximum(d, 0.0)
                acc = t if acc is None else acc + t
            adm = (j * TM + key_in_tile) // CHUNK <= q_chunk
            sc_ref[j] = jnp.where(adm, acc, -jnp.inf)
            lo = jnp.minimum(lo, functools.reduce(jnp.minimum, _row_blocks(jnp.where(adm, acc, jnp.inf), SUBLANES)))
            hi = jnp.maximum(hi, functools.reduce(jnp.maximum, _row_blocks(jnp.where(adm, acc, -jnp.inf), SUBLANES)))
        lohi_ref[0], lohi_ref[1] = lo, hi

    lohi_ref[0] = jnp.full((SUBLANES, QB), jnp.inf, jnp.float32)
    lohi_ref[1] = jnp.full((SUBLANES, QB), -jnp.inf, jnp.float32)

    def idx_step(g, carry):
        idx_tiles([IDX_TILES * g + u for u in range(IDX_TILES)])
        return carry

    lax.fori_loop(0, n_kt // IDX_TILES, idx_step, 0)

    @pl.when(n_kt % IDX_TILES != 0)
    def _():
        idx_tiles([n_kt - 2, n_kt - 1])

    lo = jnp.min(lohi_ref[0], axis=0, keepdims=True)
    hi = jnp.max(lohi_ref[1], axis=0, keepdims=True)
    n_adm = ((i * QB + lax.broadcasted_iota(jnp.int32, (1, QB), 1)) // CHUNK + 1) * CHUNK

    def bisect_cond(carry):
        it, _, _, _, unsettled = carry
        return (it < MAX_BISECT) & (unsettled > 0.0)

    def bisect(carry):
        it, lo, hi, cnt_lo, _ = carry
        for _ in range(BISECT_PER_CHECK):
            mid = lo + (hi - lo) * 0.5
            mid_b = jnp.broadcast_to(mid, (CNT_ROWS, QB))

            def count_tile(j, cnt, mid_b=mid_b):
                for blk in _row_blocks(sc_ref[j], CNT_ROWS):
                    cnt = cnt + jnp.where(blk >= mid_b, 1.0, 0.0)
                return cnt

            cnt = lax.fori_loop(0, n_real, count_tile, jnp.zeros((CNT_ROWS, QB), jnp.float32))
            c = jnp.sum(cnt, axis=0, keepdims=True)
            ge = c >= k_top
            lo = jnp.where(ge, mid, lo)
            hi = jnp.where(ge, hi, mid)
            cnt_lo = jnp.where(ge, c, cnt_lo)
        return it + BISECT_PER_CHECK, lo, hi, cnt_lo, jnp.max(jnp.where(cnt_lo > k_top, 1.0, 0.0))

    cnt0 = n_adm.astype(jnp.float32)
    _, thr, thr_hi, _, unsettled = lax.while_loop(bisect_cond, bisect,
                                                  (0, lo, hi, cnt0, jnp.max(jnp.where(cnt0 > k_top, 1.0, 0.0))))

    @pl.when(unsettled <= 0.0)
    def _():
        def bias_tile(j, carry):
            sc_ref[j] = jnp.where(sc_ref[j] >= thr, 0.0, NEG_BIG)
            return carry

        lax.fori_loop(0, n_kt, bias_tile, 0)

    @pl.when(unsettled > 0.0)
    def _():
        def count_hi(j, cnt):
            return cnt + functools.reduce(jnp.add, _row_blocks(jnp.where(sc_ref[j] >= thr_hi, 1.0, 0.0), SUBLANES))

        above = jnp.sum(lax.fori_loop(0, n_kt, count_hi, jnp.zeros((SUBLANES, QB), jnp.float32)),
                        axis=0, keepdims=True)
        top = jnp.where(above < k_top, thr_hi, jnp.inf)
        room = k_top - jnp.where(above < k_top, above, 0.0)

        def bias_tile(j, taken):
            s = sc_ref[j]
            tied = (s >= thr) & (s < top)
            before = jnp.dot(tri_ref[...], jnp.where(tied, 1.0, 0.0).astype(tri_ref.dtype),
                             preferred_element_type=jnp.float32)
            keep = (s >= top) | (tied & (taken + before <= room))
            sc_ref[j] = jnp.where(keep, 0.0, NEG_BIG)
            return taken + before[TM - 1:TM, :]

        lax.fori_loop(0, n_kt, bias_tile, jnp.zeros((1, QB), jnp.float32))

    m_ref[...] = jnp.full(m_ref.shape, NEG_BIG, jnp.float32)
    l_ref[...] = jnp.zeros(l_ref.shape, jnp.float32)
    acc_ref[...] = jnp.zeros(acc_ref.shape, jnp.float32)

    def pair(h):
        return slice((h // 2) * LANES, (h // 2 + 1) * LANES)

    def logits_into(s_ref, mx_ref, j, heads):
        bias = sc_ref[j]
        for h in heads:
            s = jnp.dot(ka_ref[0, j, :, pair(h)], qmx_ref[h], preferred_element_type=jnp.float32) + bias
            s_ref[h] = s
            mx_ref[h] = jnp.max(s, axis=0, keepdims=True)

    ones_rows = jnp.ones((ONES_ROWS, TM), vat_ref.dtype)

    def absorb(s_ref, mx_ref, j, heads):
        for h in heads:
            m_old = m_ref[h]
            m_new = jnp.maximum(m_old, mx_ref[h])
            alpha = jnp.exp2(m_old - m_new)
            p = jnp.exp2(s_ref[h] - m_new).astype(vat_ref.dtype)
            pv = jnp.dot(jnp.concatenate([vat_ref[0, j, pair(h), :], ones_rows], axis=0), p,
                         preferred_element_type=jnp.float32)
            acc_ref[h] = acc_ref[h] * alpha + pv[0:LANES]
            l_ref[h] = l_ref[h] * alpha + pv[LANES:LANES + SUBLANES]
            m_ref[h] = m_new

    for h0 in range(0, n_heads, HEAD_GROUP):
        heads = range(h0, h0 + HEAD_GROUP)
        logits_into(sa_ref, mxa_ref, 0, heads)

        def tile_pair(t, heads=heads):
            logits_into(sb_ref, mxb_ref, t + 1, heads)
            absorb(sa_ref, mxa_ref, t, heads)
            logits_into(sa_ref, mxa_ref, jnp.minimum(t + 2, n_kt - 1), heads)
            absorb(sb_ref, mxb_ref, t + 1, heads)

        def attn_quad(g, carry):
            tile_pair(4 * g)
            tile_pair(4 * g + 2)
            return carry

        lax.fori_loop(0, n_kt // 4, attn_quad, 0)

        @pl.when(n_kt % 4 != 0)
        def _():
            tile_pair(n_kt - 2)

    for hp in range(n_heads // 2):
        even = acc_ref[2 * hp] / l_ref[2 * hp, 0:1, :]
        odd = acc_ref[2 * hp + 1] / l_ref[2 * hp + 1, 0:1, :]
        o_ref[:, hp * LANES:(hp + 1) * LANES] = jnp.where(row_q < HEAD_DIM, even, odd).T.astype(o_ref.dtype)


def _dsa(qat, qit, wit, kiw4, ka4, vat, seq):
    bsz, n_kt, d_a, _ = qat.shape
    n_heads = d_a // HEAD_DIM
    n_qb = seq // QB
    k_top = min(TOPK_MAX, seq // 4)
    qtile = lambda b, i: (b, i, 0, 0)
    per_b = lambda b, i: (b, 0, 0, 0)
    f32 = jnp.float32
    return pl.pallas_call(
        functools.partial(_dsa_kernel, k_top=float(k_top), n_heads=n_heads),
        grid=(bsz, n_qb),
        in_specs=[pl.BlockSpec((1, 1, d_a, QB), qtile),
                  pl.BlockSpec((1, 1, qit.shape[2], QB), qtile),
                  pl.BlockSpec((1, 1, N_IDX_HEADS, QB), qtile),
                  pl.BlockSpec((1, n_kt, TM, LANES), per_b, pipeline_mode=pl.Buffered(1)),
                  pl.BlockSpec((1, n_kt, TM, d_a), per_b, pipeline_mode=pl.Buffered(1)),
                  pl.BlockSpec((1, n_kt, d_a, TM), per_b, pipeline_mode=pl.Buffered(1)),
                  pl.BlockSpec((TM, TM), lambda b, i: (0, 0), pipeline_mode=pl.Buffered(1))],
        out_specs=pl.BlockSpec((QB, d_a), lambda b, i: (b * n_qb + i, 0)),
        out_shape=jax.ShapeDtypeStruct((bsz * seq, d_a), jnp.bfloat16),
        scratch_shapes=[pltpu.VMEM((N_IDX_HEADS, LANES, QB), jnp.bfloat16),
                        pltpu.VMEM((n_heads, LANES, QB), jnp.bfloat16),
                        pltpu.VMEM((n_kt, TM, QB), f32),
                        pltpu.VMEM((n_heads, 1, QB), f32),
                        pltpu.VMEM((n_heads, SUBLANES, QB), f32),
                        pltpu.VMEM((n_heads, LANES, QB), f32),
                        pltpu.VMEM((n_heads, TM, QB), f32),
                        pltpu.VMEM((n_heads, TM, QB), f32),
                        pltpu.VMEM((n_heads, 1, QB), f32),
                        pltpu.VMEM((n_heads, 1, QB), f32),
                        pltpu.VMEM((2, SUBLANES, QB), f32)],
        compiler_params=_cparams(("arbitrary", "arbitrary")),
        name="dsa",
    )(qat, qit, wit, kiw4, ka4, vat, jnp.asarray(np.tril(np.ones((TM, TM), np.float32)), jnp.bfloat16))


def _group_mean(y, avg):
    hi = y.astype(jnp.bfloat16)
    lo = (y - hi.astype(jnp.float32)).astype(jnp.bfloat16)
    return (jnp.dot(hi, avg, preferred_element_type=jnp.float32)
            + jnp.dot(lo, avg, preferred_element_type=jnp.float32))


def _ret_kernel(q_ref, k_ref, v_ref, g_ref, dec_ref, zt_ref, xi_ref, gc_ref, blk_ref, avg_ref, o_ref, st_ref,
                *, n_heads):
    @pl.when(pl.program_id(1) == 0)
    def _():
        st_ref[...] = jnp.zeros_like(st_ref)

    even = lax.broadcasted_iota(jnp.int32, (RET_C, LANES), 1) < HEAD_DIM
    avg = avg_ref[...]
    for p in range(n_heads // 2):
        sl = slice(p * LANES, (p + 1) * LANES)
        qp, kp, vp = q_ref[:, sl], k_ref[:, sl], v_ref[:, sl]
        kpt = kp.astype(jnp.float32).T
        kpt_b = kpt.astype(kp.dtype)
        inner = None
        for e in range(2):
            q_e = jnp.where(even if e == 0 else jnp.logical_not(even), qp, jnp.zeros_like(qp))
            s = jnp.dot(q_e, kpt_b, preferred_element_type=jnp.float32) * dec_ref[2 * p + e]
            t = jnp.dot(s.astype(vp.dtype), vp, preferred_element_type=jnp.float32)
            inner = t if e == 0 else jnp.where(even, inner, t)
        state = st_ref[p]
        cross = jnp.dot(qp, state.astype(qp.dtype), preferred_element_type=jnp.float32) * xi_ref[p]
        y = inner + cross
        yc = y - _group_mean(y, avg)
        yn = yc * lax.rsqrt(_group_mean(yc * yc, avg) + EPS)
        g = g_ref[:, sl]
        o_ref[:, sl] = (g / (1.0 + jnp.exp(-g)) * yn).astype(o_ref.dtype)
        kz = (kpt * zt_ref[p]).astype(kp.dtype)
        kv = jnp.dot(kz, vp, preferred_element_type=jnp.float32)
        st_ref[p] = state * gc_ref[p] + kv * blk_ref[...]


def _ret_consts(n_heads):
    log_gamma = jnp.log1p(-jnp.exp2(-5.0 - jnp.arange(n_heads, dtype=jnp.float32)))
    pos = jnp.arange(RET_C, dtype=jnp.float32)
    diff = pos[:, None] - pos[None, :]
    dec = jnp.where(diff[None] >= 0, jnp.exp(jnp.maximum(diff, 0.0)[None] * log_gamma[:, None, None]), 0.0)
    zeta = jnp.exp((RET_C - 1.0 - pos)[None, :] * log_gamma[:, None])
    xi = jnp.exp((pos + 1.0)[None, :] * log_gamma[:, None])
    gc = jnp.exp(RET_C * log_gamma)
    n_pairs = n_heads // 2
    lanes = lambda a: jnp.repeat(a.reshape(n_pairs, 2, -1), HEAD_DIM, axis=1)
    zt = lanes(zeta)
    xi_p = jnp.swapaxes(lanes(xi), 1, 2)
    gc_p = jnp.broadcast_to(lanes(gc[:, None]), (n_pairs, LANES, LANES))
    head_of = jnp.arange(LANES) // HEAD_DIM
    blk = (head_of[:, None] == head_of[None, :]).astype(jnp.float32)
    avg = (blk / HEAD_DIM).astype(jnp.bfloat16)
    f32 = lambda a: a.astype(jnp.float32)
    return dec, f32(zt), f32(xi_p), f32(gc_p), blk, avg


def _ret(qb, kb, vb, gb, seq):
    n, d_b = qb.shape
    bsz = n // seq
    n_heads = d_b // HEAD_DIM
    n_c = seq // RET_C
    consts = _ret_consts(n_heads)
    row = lambda b, c: (b * n_c + c, 0)
    const_spec = lambda a: pl.BlockSpec(a.shape, lambda b, c: (0,) * a.ndim)
    return pl.pallas_call(
        functools.partial(_ret_kernel, n_heads=n_heads),
        grid=(bsz, n_c),
        in_specs=[pl.BlockSpec((RET_C, d_b), row)] * 4 + [const_spec(a) for a in consts],
        out_specs=pl.BlockSpec((RET_C, d_b), row),
        out_shape=jax.ShapeDtypeStruct((n, d_b), jnp.bfloat16),
        scratch_shapes=[pltpu.VMEM((n_heads // 2, LANES, LANES), jnp.float32)],
        compiler_params=_cparams(("arbitrary", "arbitrary")),
        name="ret",
    )(qb, kb, vb, gb, *consts)


def _tiles_shape(rows, d):
    return (rows // SUBLANES, d // LANES, SUBLANES, LANES)


def _to_tiles(ref, x):
    for s in range(ref.shape[1]):
        ref[:, s] = x[:, s * LANES:(s + 1) * LANES].reshape(ref.shape[0], SUBLANES, LANES)


def _from_tiles(ref):
    rows = ref.shape[0] * SUBLANES
    return jnp.concatenate([ref[:, s].reshape(rows, LANES) for s in range(ref.shape[1])], axis=1)


def _lane_first_eq(x, m, lane):
    return jnp.min(jnp.where(x == m, lane, float(LANES)), axis=1, keepdims=True)


def _mix_out_kernel(x_ref, ya_ref, yb_ref, woa_ref, wob_ref, gt_ref, sc_ref, sh_ref, g_ref, wr_ref, tri_ref,
                    x1_ref, h2_ref, route_ref, cnt_ref, carry_ref):
    @pl.when(pl.program_id(0) == 0)
    def _():
        carry_ref[...] = jnp.zeros_like(carry_ref)

    mix = (jnp.dot(ya_ref[...], woa_ref[...], preferred_element_type=jnp.float32)
           + jnp.dot(yb_ref[...], wob_ref[...], preferred_element_type=jnp.float32))
    x1 = x_ref[...] + gt_ref[0] * mix
    x1_ref[...] = x1
    h2 = _rmsnorm_mod(x1, g_ref[...], sc_ref[0], sh_ref[0])
    _to_tiles(h2_ref, h2)

    lg = jnp.dot(h2.astype(jnp.bfloat16), wr_ref[...], preferred_element_type=jnp.float32)
    lane = lax.broadcasted_iota(jnp.int32, lg.shape, 1).astype(jnp.float32)
    is_grp = (lane >= N_EXPERTS) & (lane < N_EXPERTS + N_GROUPS)
    gl = jnp.where(is_grp, lg, -jnp.inf)
    gmax = jnp.max(gl, axis=1, keepdims=True)
    grp = _lane_first_eq(gl, gmax, lane) - N_EXPERTS
    p_grp = 1.0 / jnp.sum(jnp.exp(gl - gmax), axis=1, keepdims=True)
    in_grp = jnp.floor(lane * (1.0 / EXPERTS_PER_GROUP)) == grp
    f = jnp.where(in_grp & (lane < N_EXPERTS), lg, -jnp.inf)
    f1 = jnp.max(f, axis=1, keepdims=True)
    e1 = _lane_first_eq(f, f1, lane)
    f = jnp.where(lane == e1, -jnp.inf, f)
    f2 = jnp.max(f, axis=1, keepdims=True)
    e2 = _lane_first_eq(f, f2, lane)
    a2 = jnp.exp(f2 - f1)
    w1 = p_grp / (1.0 + a2)
    w2 = p_grp * a2 / (1.0 + a2)

    oh1 = jnp.where(lane == e1, 1.0, 0.0)
    oh2 = jnp.where(lane == e2, 1.0, 0.0)
    both = oh1 + oh2
    before = jnp.dot(tri_ref[...], both.astype(jnp.bfloat16), preferred_element_type=jnp.float32) + carry_ref[...]
    r1 = jnp.sum(before * oh1, axis=1, keepdims=True)
    r2 = jnp.sum(before * oh2, axis=1, keepdims=True)
    carry = carry_ref[...] + jnp.sum(both, axis=0, keepdims=True)
    carry_ref[...] = carry
    cnt_ref[...] = carry

    out = jnp.zeros(lg.shape, jnp.float32)
    for col, val in enumerate((e1, e2, w1, w2, r1, r2)):
        out = jnp.where(lane == col, val, out)
    route_ref[...] = out


def _mix_out(x2, ya, yb, wo_a, wo_b, mod3, g_ffn, w_route, seq):
    n, d = x2.shape
    tm = MIX_TM
    tiles_per_seq = seq // tm
    tri = jnp.asarray(np.tril(np.ones((tm, tm), np.float32), -1), jnp.bfloat16)
    row = lambda i: (i, 0)
    const = lambda i: (0, 0)
    modk = lambda k: pl.BlockSpec((1, 1, d), lambda i: ((i // tiles_per_seq) * 6 + k, 0, 0))
    return pl.pallas_call(
        _mix_out_kernel,
        grid=(n // tm,),
        in_specs=[pl.BlockSpec((tm, d), row),
                  pl.BlockSpec((tm, ya.shape[1]), row),
                  pl.BlockSpec((tm, yb.shape[1]), row),
                  pl.BlockSpec(wo_a.shape, const),
                  pl.BlockSpec(wo_b.shape, const),
                  modk(2), modk(4), modk(3),
                  pl.BlockSpec((1, d), const),
                  pl.BlockSpec(w_route.shape, const),
                  pl.BlockSpec(tri.shape, const)],
        out_specs=(pl.BlockSpec((tm, d), row), pl.BlockSpec(_tiles_shape(tm, d), lambda i: (i, 0, 0, 0)),
                   pl.BlockSpec((tm, LANES), row), pl.BlockSpec((1, LANES), const)),
        out_shape=(jax.ShapeDtypeStruct((n, d), jnp.float32), jax.ShapeDtypeStruct(_tiles_shape(n, d), jnp.float32),
                   jax.ShapeDtypeStruct((n, LANES), jnp.float32), jax.ShapeDtypeStruct((1, LANES), jnp.float32)),
        scratch_shapes=[pltpu.VMEM((1, LANES), jnp.float32)],
        compiler_params=_cparams(("arbitrary",)),
        name="mix_out",
    )(x2, ya, yb, wo_a, wo_b, mod3, mod3, mod3, g_ffn, w_route, tri)


def _plan_kernel(route_ref, pst_ref, dest_ref):
    r = route_ref[...]
    lane = lax.broadcasted_iota(jnp.int32, r.shape, 1).astype(jnp.float32)
    pst = pst_ref[...]
    d1 = jnp.sum(jnp.where(lane == r[:, 0:1], pst, 0.0), axis=1, keepdims=True) + r[:, 4:5]
    d2 = jnp.sum(jnp.where(lane == r[:, 1:2], pst, 0.0), axis=1, keepdims=True) + r[:, 5:6]
    t1 = jnp.floor(d1 * (1.0 / SUBLANES))
    t2 = jnp.floor(d2 * (1.0 / SUBLANES))
    packed = jnp.zeros(r.shape, jnp.float32)
    for k, v in enumerate((t1, d1 - t1 * SUBLANES, t2, d2 - t2 * SUBLANES)):
        packed = jnp.where(lane == float(k), v, packed)
    dest_ref[0] = packed.T[0:8, :].astype(jnp.int32)


def _dest_block(i):
    return (i // (PLAN_TM // TM), 0, i % (PLAN_TM // TM))


def _plan(route, pst_row):
    n = route.shape[0]
    return pl.pallas_call(
        _plan_kernel,
        grid=(n // PLAN_TM,),
        in_specs=[pl.BlockSpec((PLAN_TM, LANES), lambda i: (i, 0)), pl.BlockSpec((1, LANES), lambda i: (0, 0))],
        out_specs=pl.BlockSpec((1, 8, PLAN_TM), lambda i: (i, 0, 0)),
        out_shape=jax.ShapeDtypeStruct((n // PLAN_TM, 8, PLAN_TM), jnp.int32),
        compiler_params=_cparams(("arbitrary",)),
        name="plan",
    )(route, pst_row)


def _dispatch_kernel(zrow_ref, n_used_ref, dest_ref, h2_ref, xs_hbm, zbuf, sem, zsem, *, n_blk):
    i = pl.program_id(0)

    @pl.when(i == 0)
    def _():
        zbuf[...] = jnp.zeros(zbuf.shape, zbuf.dtype)
        blk_tiles = BM // SUBLANES
        for e in range(N_EXPERTS):
            pltpu.make_async_copy(zbuf, xs_hbm.at[pl.ds(zrow_ref[e], blk_tiles)], zsem).start()
        for e in range(N_EXPERTS):
            pltpu.make_async_copy(zbuf, xs_hbm.at[pl.ds(0, blk_tiles)], zsem).wait()
        for b in range(N_EXPERTS + 1):
            @pl.when(n_used_ref[0] + b <= n_blk)
            def _():
                tail = pltpu.make_async_copy(zbuf, xs_hbm.at[pl.ds((n_used_ref[0] + b) * blk_tiles, blk_tiles)], zsem)
                tail.start()
                tail.wait()

    def issue(k, c):
        for u in range(SUBLANES):
            r = k * SUBLANES + u
            row = h2_ref.at[k, :, u, :]
            pltpu.make_async_copy(row, xs_hbm.at[dest_ref[0, 0, r], :, dest_ref[0, 1, r], :], sem).start()
            pltpu.make_async_copy(row, xs_hbm.at[dest_ref[0, 2, r], :, dest_ref[0, 3, r], :], sem).start()
        return c

    lax.fori_loop(0, TM // SUBLANES, issue, 0)
    for _ in range(2):
        pltpu.make_async_copy(h2_ref, xs_hbm.at[pl.ds(0, TM // SUBLANES)], sem).wait()


def _dispatch(h2, dest, zrow, n_used, n_blk):
    n_rows = (n_blk + 1) * BM
    d = h2.shape[1] * LANES
    grid_spec = pltpu.PrefetchScalarGridSpec(
        num_scalar_prefetch=2,
        grid=(h2.shape[0] * SUBLANES // TM,),
        in_specs=[pl.BlockSpec((1, 8, TM), lambda i, z, nu: _dest_block(i), memory_space=pltpu.SMEM),
                  pl.BlockSpec(_tiles_shape(TM, d), lambda i, z, nu: (i, 0, 0, 0))],
        out_specs=pl.BlockSpec(memory_space=pl.ANY),
        scratch_shapes=[pltpu.VMEM(_tiles_shape(BM, d), h2.dtype),
                        pltpu.SemaphoreType.DMA(()),
                        pltpu.SemaphoreType.DMA(())],
    )
    return pl.pallas_call(
        functools.partial(_dispatch_kernel, n_blk=n_blk),
        grid_spec=grid_spec,
        out_shape=jax.ShapeDtypeStruct(_tiles_shape(n_rows, d), h2.dtype),
        compiler_params=_cparams(("arbitrary",)),
        name="dispatch",
    )(zrow, n_used, dest, h2)


def _experts_kernel(blk_e_ref, n_used_ref, x_ref, wg_ref, wu_ref, wd_ref, y_ref, wg_bf, wu_bf, wd_bf):
    j = pl.program_id(0)

    @pl.when((j == 0) | (blk_e_ref[j] != blk_e_ref[jnp.maximum(j - 1, 0)]))
    def _():
        wg_bf[...] = wg_ref[0].astype(wg_bf.dtype)
        wu_bf[...] = wu_ref[0].astype(wu_bf.dtype)
        wd_bf[...] = wd_ref[0].astype(wd_bf.dtype)

    @pl.when(j < n_used_ref[0])
    def _():
        x = _from_tiles(x_ref).astype(wg_bf.dtype)
        a = jnp.dot(x, wg_bf[...], preferred_element_type=jnp.float32)
        b = jnp.dot(x, wu_bf[...], preferred_element_type=jnp.float32)
        hmid = (a / (1.0 + jnp.exp(-a)) * b).astype(x.dtype)
        _to_tiles(y_ref, jnp.dot(hmid, wd_bf[...], preferred_element_type=jnp.float32))

    @pl.when(j >= n_used_ref[0])
    def _():
        y_ref[...] = jnp.zeros(y_ref.shape, y_ref.dtype)


def _experts(xs, wg, wu, wd, blk_e, n_used):
    n_blk = blk_e.shape[0]
    d, d_e = wg.shape[1], wg.shape[2]
    blk = _tiles_shape(BM, d)
    grid_spec = pltpu.PrefetchScalarGridSpec(
        num_scalar_prefetch=2,
        grid=(n_blk,),
        in_specs=[pl.BlockSpec(blk, lambda j, be, nu: (jnp.minimum(j, nu[0] - 1), 0, 0, 0)),
                  pl.BlockSpec((1, d, d_e), lambda j, be, nu: (be[j], 0, 0)),
                  pl.BlockSpec((1, d, d_e), lambda j, be, nu: (be[j], 0, 0)),
                  pl.BlockSpec((1, d_e, d), lambda j, be, nu: (be[j], 0, 0))],
        out_specs=pl.BlockSpec(blk, lambda j, be, nu: (j, 0, 0, 0)),
        scratch_shapes=[pltpu.VMEM((d, d_e), jnp.bfloat16), pltpu.VMEM((d, d_e), jnp.bfloat16),
                        pltpu.VMEM((d_e, d), jnp.bfloat16)],
    )
    return pl.pallas_call(
        _experts_kernel,
        grid_spec=grid_spec,
        out_shape=jax.ShapeDtypeStruct(_tiles_shape(n_blk * BM, d), jnp.float32),
        compiler_params=_cparams(("arbitrary",)),
        name="experts",
    )(blk_e, n_used, xs, wg, wu, wd)


def _final_kernel(dest_ref, dest_next_ref, x1_ref, route_ref, gt_ref, g_ref, ys_hbm, o_ref, ybuf, sem):
    i = pl.program_id(0)
    slot = i % 2

    tiles = TM // SUBLANES

    def gather(d_ref, s):
        def issue(k, c):
            for u in range(SUBLANES):
                r = k * SUBLANES + u
                pltpu.make_async_copy(ys_hbm.at[d_ref[0, 0, r], :, d_ref[0, 1, r], :],
                                      ybuf.at[s, k, :, u, :], sem.at[s]).start()
                pltpu.make_async_copy(ys_hbm.at[d_ref[0, 2, r], :, d_ref[0, 3, r], :],
                                      ybuf.at[s, tiles + k, :, u, :], sem.at[s]).start()
            return c
        lax.fori_loop(0, tiles, issue, 0)

    @pl.when(i == 0)
    def _():
        gather(dest_ref, 0)

    @pl.when(i + 1 < pl.num_programs(0))
    def _():
        gather(dest_next_ref, 1 - slot)

    pltpu.make_async_copy(ys_hbm.at[pl.ds(0, 2 * tiles)], ybuf.at[slot], sem.at[slot]).wait()
    w1 = route_ref[:, 2:3]
    w2 = route_ref[:, 3:4]
    y0 = _from_tiles(ybuf.at[slot, 0:tiles])
    y1 = _from_tiles(ybuf.at[slot, tiles:2 * tiles])
    x2 = x1_ref[...] + gt_ref[0] * (w1 * y0 + w2 * y1)
    o_ref[...] = x2 * lax.rsqrt(jnp.mean(x2 * x2, axis=-1, keepdims=True) + EPS) * g_ref[...]


def _final(x1, ys, dest, route, mod3, g_final, seq):
    n, d = x1.shape
    n_tiles = n // TM
    tiles_per_seq = seq // TM
    row = lambda i: (i, 0)
    return pl.pallas_call(
        _final_kernel,
        grid=(n_tiles,),
        in_specs=[pl.BlockSpec((1, 8, TM), _dest_block, memory_space=pltpu.SMEM),
                  pl.BlockSpec((1, 8, TM), lambda i: _dest_block(jnp.minimum(i + 1, n_tiles - 1)),
                               memory_space=pltpu.SMEM),
                  pl.BlockSpec((TM, d), row),
                  pl.BlockSpec((TM, LANES), row),
                  pl.BlockSpec((1, 1, d), lambda i: ((i // tiles_per_seq) * 6 + 5, 0, 0)),
                  pl.BlockSpec((1, d), lambda i: (0, 0)),
                  pl.BlockSpec(memory_space=pl.ANY)],
        out_specs=pl.BlockSpec((TM, d), row),
        out_shape=jax.ShapeDtypeStruct((n, d), jnp.float32),
        scratch_shapes=[pltpu.VMEM((2,) + _tiles_shape(2 * TM, d), jnp.float32),
                        pltpu.SemaphoreType.DMA((2,))],
        compiler_params=_cparams(("arbitrary",)),
        name="final",
    )(dest, dest, x1, route, mod3, g_final, ys)


def _block_layout(counts, n):
    cnt = counts[0, :N_EXPERTS].astype(jnp.int32)
    blocks = (cnt + BM - 1) // BM
    bends = jnp.cumsum(blocks)
    pstarts = (bends - blocks) * BM
    n_blk = (2 * n) // BM + N_EXPERTS
    pst_row = jnp.zeros((1, LANES), jnp.float32).at[0, :N_EXPERTS].set(pstarts.astype(jnp.float32))
    zrow = ((pstarts + cnt) // SUBLANES).astype(jnp.int32)
    n_used = bends[-1:].astype(jnp.int32)
    blk_e = jnp.sum(bends[None, :] <= jnp.arange(n_blk, dtype=jnp.int32)[:, None], axis=1)
    blk_e = jnp.minimum(blk_e, N_EXPERTS - 1).astype(jnp.int32)
    return pst_row, zrow, n_used, blk_e


def _layer(x2, c, pos2, w_ada, b_ada, g_mix, w_in, w_o, g_ffn, w_rg, w_re, w_up, w_gate, w_down, seq):
    n, d = x2.shape
    bsz = n // seq
    bf = jnp.bfloat16
    d_a = d // 2
    d_i = N_IDX_HEADS * IDX_DIM
    mod3 = _ada(c, w_ada, b_ada).reshape(bsz * 6, 1, d)

    c0 = 3 * d_a + d_i
    c1 = c0 + IDX_DIM + N_IDX_HEADS
    w_cat = jnp.concatenate([w_in[:, :c0], w_in[:, c0:c1],
                             jnp.zeros((d, LANES - (c1 - c0)), w_in.dtype), w_in[:, c1:]], axis=1).astype(bf)
    qat, ka, vat, qit, kiw, wit, qb, kb, vb, gb = _proj(x2, pos2, mod3, g_mix.reshape(1, d), w_cat, seq)

    n_kt = seq // TM
    ya = _dsa(qat, qit, wit, kiw.reshape(bsz, n_kt, TM, LANES), ka.reshape(bsz, n_kt, TM, d_a), vat, seq)
    yb = _ret(qb, kb, vb, gb, seq)

    w_route = jnp.concatenate([jnp.transpose(w_re, (1, 0, 2)).reshape(d, N_EXPERTS), w_rg,
                               jnp.zeros((d, LANES - N_EXPERTS - N_GROUPS), w_rg.dtype)], axis=1).astype(bf)
    x1, h2, route, counts = _mix_out(x2, ya, yb, w_o[:d_a].astype(bf), w_o[d_a:].astype(bf), mod3,
                                     g_ffn.reshape(1, d), w_route, seq)

    pst_row, zrow, n_used, blk_e = _block_layout(counts, n)
    dest = _plan(route, pst_row)
    xs = _dispatch(h2, dest, zrow, n_used, blk_e.shape[0])
    ys = _experts(xs, w_gate, w_up, w_down, blk_e, n_used)
    return x1, ys, dest, route, mod3


def kernel(x, c, positions, w_ada, b_ada, g_norm_mix, w_in, w_o, g_norm_ffn, w_router_group, w_router_expert,
           w_up, w_gate, w_down, g_norm_final):
    bsz, seq, d = x.shape
    depth = w_ada.shape[0]
    assert depth == 1, "the final norm is fused into the last layer's combine kernel"
    assert seq % (2 * TM) == 0 and seq % RET_C == 0 and seq % MIX_TM == 0 and (2 * bsz * seq) % BM == 0
    assert (bsz * seq) % PLAN_TM == 0 and PLAN_TM % TM == 0
    x2 = x.reshape(bsz * seq, d)
    pos2 = positions.astype(jnp.float32).reshape(bsz * seq, 1)
    x1, ys, dest, route, mod3 = _layer(x2, c, pos2, w_ada[0], b_ada[0], g_norm_mix[0], w_in[0], w_o[0],
                                       g_norm_ffn[0], w_router_group[0], w_router_expert[0], w_up[0], w_gate[0],
                                       w_down[0], seq)
    out = _final(x1, ys, dest, route, mod3, g_norm_final.reshape(1, d), seq)
    return out.reshape(bsz, seq, d)
```

```python
import functools

import jax
import jax.numpy as jnp
import numpy as np
from jax import lax
from jax.experimental import pallas as pl
from jax.experimental.pallas import tpu as pltpu

CHUNK = 64
HEAD_DIM = 64
N_IDX_HEADS = 16
IDX_DIM = 64
TOPK_MAX = 256
ROPE_THETA = 500000.0
ROT_DIM = HEAD_DIM // 4
RET_THETA = 10000.0
N_GROUPS = 4
EXPERTS_PER_GROUP = 8
N_EXPERTS = N_GROUPS * EXPERTS_PER_GROUP
EPS = 1e-6

LANES = 128
SUBLANES = 8
VMEM_LIMIT = 56 * 1024 * 1024

TM = 256
QB = TM
RET_C = 256
RET_CHUNKS = 2
BM = 256
PLAN_TM = 1024
MIX_TM = 512
CNT_ROWS = 64
ONES_ROWS = 16
HEAD_GROUP = 4
IDX_TILES = 4
BISECT_PER_CHECK = 2
MAX_BISECT = 40

NEG_BIG = -1e30
LOG2E = 1.4426950408889634


def _cparams(sem):
    return pltpu.CompilerParams(dimension_semantics=sem, vmem_limit_bytes=VMEM_LIMIT)


def _ada_kernel(c_ref, w_ref, b_ref, o_ref):
    o_ref[...] = jnp.dot(c_ref[...], w_ref[...], preferred_element_type=jnp.float32) + b_ref[...]


def _ada(c, w_ada, b_ada):
    bsz, d = c.shape
    n_out = w_ada.shape[1]
    return pl.pallas_call(
        _ada_kernel,
        grid=(n_out // d,),
        in_specs=[pl.BlockSpec((bsz, d), lambda j: (0, 0)),
                  pl.BlockSpec((d, d), lambda j: (0, j)),
                  pl.BlockSpec((1, d), lambda j: (0, j))],
        out_specs=pl.BlockSpec((bsz, d), lambda j: (0, j)),
        out_shape=jax.ShapeDtypeStruct((bsz, n_out), jnp.float32),
        compiler_params=_cparams(("arbitrary",)),
        name="ada",
    )(c, w_ada, b_ada.reshape(1, n_out))


def _rmsnorm_mod(x, g, sc, sh):
    xn = x * lax.rsqrt(jnp.mean(x * x, axis=-1, keepdims=True) + EPS)
    return xn * g * (1.0 + sc) + sh


def _rope_lanes(x, cos, sin_lo, sin_hi, half):
    cols = []
    for k in range(x.shape[1] // LANES):
        xb = x[:, k * LANES:(k + 1) * LANES]
        cols.append(xb * cos + pltpu.roll(xb, LANES - half, 1) * sin_lo + pltpu.roll(xb, half, 1) * sin_hi)
    return cols[0] if len(cols) == 1 else jnp.concatenate(cols, axis=1)


def _proj_kernel(x_ref, pos_ref, sc_ref, sh_ref, g_ref, w_ref, tab_ref,
                 qat_ref, ka_ref, vat_ref, qit_ref, kiw_ref, wit_ref, qb_ref, kb_ref, vb_ref, gb_ref,
                 *, d_a, d_i, d_b):
    h = _rmsnorm_mod(x_ref[...], g_ref[...], sc_ref[0], sh_ref[0]).astype(jnp.bfloat16)
    pos = pos_ref[...]
    ang_a = pos * tab_ref[0:1, :]
    cos_a, sin_a = jnp.cos(ang_a), jnp.sin(ang_a)
    sa_lo, sa_hi = sin_a * tab_ref[1:2, :], sin_a * tab_ref[2:3, :]
    ang_b = pos * tab_ref[3:4, :]
    cos_b, sin_b = jnp.cos(ang_b), jnp.sin(ang_b)
    sb_lo, sb_hi = sin_b * tab_ref[4:5, :], sin_b * tab_ref[5:6, :]
    half_a, half_b = ROT_DIM // 2, HEAD_DIM // 2

    def seg(lo, width):
        return jnp.dot(h, w_ref[:, lo:lo + width], preferred_element_type=jnp.float32)

    o = 0
    qa = seg(o, d_a); o += d_a
    qat_ref[0, 0] = (_rope_lanes(qa, cos_a, sa_lo, sa_hi, half_a) * (HEAD_DIM ** -0.5 * LOG2E)).T.astype(qat_ref.dtype)
    ka = seg(o, d_a); o += d_a
    ka_ref[...] = _rope_lanes(ka, cos_a, sa_lo, sa_hi, half_a).astype(ka_ref.dtype)
    vat_ref[0, 0] = seg(o, d_a).T.astype(vat_ref.dtype); o += d_a
    qi = seg(o, d_i); o += d_i
    qit_ref[0, 0] = (_rope_lanes(qi, cos_a, sa_lo, sa_hi, half_a) * (IDX_DIM ** -0.5)).T.astype(qit_ref.dtype)
    kw = seg(o, LANES); o += LANES
    kiw_ref[...] = _rope_lanes(kw, cos_a, sa_lo, sa_hi, half_a).astype(kiw_ref.dtype)
    wit_ref[0, 0] = kw.T[IDX_DIM:IDX_DIM + N_IDX_HEADS, :] * (N_IDX_HEADS ** -0.5)
    qb = seg(o, d_b); o += d_b
    qb_ref[...] = _rope_lanes(qb, cos_b, sb_lo, sb_hi, half_b).astype(qb_ref.dtype)
    kb = seg(o, d_b); o += d_b
    kb_ref[...] = (_rope_lanes(kb, cos_b, sb_lo, sb_hi, half_b) * (HEAD_DIM ** -0.5)).astype(kb_ref.dtype)
    vb_ref[...] = seg(o, d_b).astype(vb_ref.dtype); o += d_b
    gb_ref[...] = seg(o, d_b)


def _rope_tables():
    lane = jnp.arange(LANES) % HEAD_DIM
    rows = []
    for rot, theta in ((ROT_DIM, ROPE_THETA), (HEAD_DIM, RET_THETA)):
        half = rot // 2
        inv_freq = theta ** (-jnp.arange(half, dtype=jnp.float32) / half)
        rows.append(jnp.where(lane < rot, inv_freq[lane % half], 0.0))
        rows.append(jnp.where(lane < half, -1.0, 0.0))
        rows.append(jnp.where((lane >= half) & (lane < rot), 1.0, 0.0))
    rows += [jnp.zeros((LANES,), jnp.float32)] * 2
    return jnp.stack(rows).astype(jnp.float32)


def _proj(x2, pos2, mod3, g_mix, w_cat, seq):
    n, d = x2.shape
    bsz = n // seq
    d_a = d // 2
    d_b = d // 2
    d_i = N_IDX_HEADS * IDX_DIM
    tiles_per_seq = seq // TM
    tab = _rope_tables()
    row = lambda i: (i, 0)
    const = lambda i: (0, 0)
    tile4 = lambda i: (i // tiles_per_seq, i % tiles_per_seq, 0, 0)
    bf = jnp.bfloat16
    out_shape = (
        jax.ShapeDtypeStruct((bsz, tiles_per_seq, d_a, TM), bf),
        jax.ShapeDtypeStruct((n, d_a), bf),
        jax.ShapeDtypeStruct((bsz, tiles_per_seq, d_a, TM), bf),
        jax.ShapeDtypeStruct((bsz, tiles_per_seq, d_i, TM), bf),
        jax.ShapeDtypeStruct((n, LANES), bf),
        jax.ShapeDtypeStruct((bsz, tiles_per_seq, N_IDX_HEADS, TM), jnp.float32),
        jax.ShapeDtypeStruct((n, d_b), bf),
        jax.ShapeDtypeStruct((n, d_b), bf),
        jax.ShapeDtypeStruct((n, d_b), bf),
        jax.ShapeDtypeStruct((n, d_b), jnp.float32),
    )
    out_specs = (
        pl.BlockSpec((1, 1, d_a, TM), tile4),
        pl.BlockSpec((TM, d_a), row),
        pl.BlockSpec((1, 1, d_a, TM), tile4),
        pl.BlockSpec((1, 1, d_i, TM), tile4),
        pl.BlockSpec((TM, LANES), row),
        pl.BlockSpec((1, 1, N_IDX_HEADS, TM), tile4),
        pl.BlockSpec((TM, d_b), row),
        pl.BlockSpec((TM, d_b), row),
        pl.BlockSpec((TM, d_b), row),
        pl.BlockSpec((TM, d_b), row),
    )
    return pl.pallas_call(
        functools.partial(_proj_kernel, d_a=d_a, d_i=d_i, d_b=d_b),
        grid=(n // TM,),
        in_specs=[pl.BlockSpec((TM, d), row),
                  pl.BlockSpec((TM, 1), row),
                  pl.BlockSpec((1, 1, d), lambda i: ((i // tiles_per_seq) * 6 + 1, 0, 0)),
                  pl.BlockSpec((1, 1, d), lambda i: ((i // tiles_per_seq) * 6 + 0, 0, 0)),
                  pl.BlockSpec((1, d), const),
                  pl.BlockSpec(w_cat.shape, const),
                  pl.BlockSpec(tab.shape, const)],
        out_specs=out_specs,
        out_shape=out_shape,
        compiler_params=_cparams(("arbitrary",)),
        name="proj",
    )(x2, pos2, mod3, mod3, g_mix, w_cat, tab)


def _row_blocks(x, rows):
    return [x[r * rows:(r + 1) * rows] for r in range(x.shape[0] // rows)]


def _dsa_kernel(qat_ref, qit_ref, wit_ref, kiw_ref, ka_ref, vat_ref, tri_ref, o_ref,
                qix_ref, qmx_ref, sc_ref, m_ref, l_ref, acc_ref, sa_ref, sb_ref, mxa_ref, mxb_ref, lohi_ref,
                *, k_top, n_heads):
    i = pl.program_id(1)
    n_grp = (i * QB + QB + 2 * TM - 1) // (2 * TM)
    n_kt = 2 * n_grp
    n_real = (i * QB + QB + TM - 1) // TM

    zero_rows = jnp.zeros((LANES - IDX_DIM, QB), qix_ref.dtype)
    for h in range(N_IDX_HEADS):
        qix_ref[h] = jnp.concatenate([qit_ref[0, 0, h * IDX_DIM:(h + 1) * IDX_DIM, :], zero_rows], axis=0)
    row_q = lax.broadcasted_iota(jnp.int32, (LANES, QB), 0)
    for h in range(n_heads):
        pair = qat_ref[0, 0, (h // 2) * LANES:(h // 2 + 1) * LANES, :]
        own = (row_q < HEAD_DIM) if h % 2 == 0 else (row_q >= HEAD_DIM)
        qmx_ref[h] = jnp.where(own, pair, jnp.zeros_like(pair))

    q_chunk = (i * QB + lax.broadcasted_iota(jnp.int32, (1, QB), 1)) // CHUNK
    key_chunk_in_tile = lax.broadcasted_iota(jnp.int32, (TM, QB), 0) // CHUNK
    w_all = wit_ref[0, 0]

    def idx_tiles(tiles):
        lo, hi = lohi_ref[0], lohi_ref[1]
        for j in tiles:
            kt = kiw_ref[0, j]
            acc = None
            for h in range(N_IDX_HEADS):
                d = jnp.dot(kt, qix_ref[h], preferred_element_type=jnp.float32)
                t = w_all[h:h + 1, :] * jnp.maximum(d, 0.0)
                acc = t if acc is None else acc + t
            adm = key_chunk_in_tile <= q_chunk - j * (TM // CHUNK)
            s = jnp.where(adm, acc, -jnp.inf)
            sc_ref[j] = s
            lo = jnp.minimum(lo, functools.reduce(jnp.minimum, _row_blocks(jnp.where(adm, acc, jnp.inf), SUBLANES)))
            hi = jnp.maximum(hi, functools.reduce(jnp.maximum, _row_blocks(s, SUBLANES)))
        lohi_ref[0], lohi_ref[1] = lo, hi

    lohi_ref[0] = jnp.full((SUBLANES, QB), jnp.inf, jnp.float32)
    lohi_ref[1] = jnp.full((SUBLANES, QB), -jnp.inf, jnp.float32)

    def idx_step(g, carry):
        idx_tiles([IDX_TILES * g + u for u in range(IDX_TILES)])
        return carry

    lax.fori_loop(0, n_kt // IDX_TILES, idx_step, 0)

    @pl.when(n_kt % IDX_TILES != 0)
    def _():
        idx_tiles([n_kt - 2, n_kt - 1])

    lo = jnp.min(lohi_ref[0], axis=0, keepdims=True)
    hi = jnp.max(lohi_ref[1], axis=0, keepdims=True)
    n_adm = ((i * QB + lax.broadcasted_iota(jnp.int32, (1, QB), 1)) // CHUNK + 1) * CHUNK

    def bisect_cond(carry):
        it, _, _, _, unsettled = carry
        return (it < MAX_BISECT) & (unsettled > 0.0)

    def bisect(carry):
        it, lo, hi, cnt_lo, _ = carry
        for _ in range(BISECT_PER_CHECK):
            mid = lo + (hi - lo) * 0.5
            mid_b = jnp.broadcast_to(mid, (CNT_ROWS, QB))

            def count_tile(j, cnt, mid_b=mid_b):
                for blk in _row_blocks(sc_ref[j], CNT_ROWS):
                    cnt = cnt + jnp.where(blk >= mid_b, 1.0, 0.0)
                return cnt

            cnt = lax.fori_loop(0, n_real, count_tile, jnp.zeros((CNT_ROWS, QB), jnp.float32))
            c = jnp.sum(cnt, axis=0, keepdims=True)
            ge = c >= k_top
            lo = jnp.where(ge, mid, lo)
            hi = jnp.where(ge, hi, mid)
            cnt_lo = jnp.where(ge, c, cnt_lo)
        return it + BISECT_PER_CHECK, lo, hi, cnt_lo, jnp.max(jnp.where(cnt_lo > k_top, 1.0, 0.0))

    cnt0 = n_adm.astype(jnp.float32)
    _, thr, thr_hi, _, unsettled = lax.while_loop(bisect_cond, bisect,
                                                  (0, lo, hi, cnt0, jnp.max(jnp.where(cnt0 > k_top, 1.0, 0.0))))

    @pl.when(unsettled <= 0.0)
    def _():
        def bias_tile(j, carry):
            sc_ref[j] = jnp.where(sc_ref[j] >= thr, 0.0, NEG_BIG)
            return carry

        lax.fori_loop(0, n_kt, bias_tile, 0)

    @pl.when(unsettled > 0.0)
    def _():
        def count_hi(j, cnt):
            return cnt + functools.reduce(jnp.add, _row_blocks(jnp.where(sc_ref[j] >= thr_hi, 1.0, 0.0), SUBLANES))

        above = jnp.sum(lax.fori_loop(0, n_kt, count_hi, jnp.zeros((SUBLANES, QB), jnp.float32)),
                        axis=0, keepdims=True)
        top = jnp.where(above < k_top, thr_hi, jnp.inf)
        room = k_top - jnp.where(above < k_top, above, 0.0)

        def bias_tile(j, taken):
            s = sc_ref[j]
            tied = (s >= thr) & (s < top)
            before = jnp.dot(tri_ref[...], jnp.where(tied, 1.0, 0.0).astype(tri_ref.dtype),
                             preferred_element_type=jnp.float32)
            keep = (s >= top) | (tied & (taken + before <= room))
            sc_ref[j] = jnp.where(keep, 0.0, NEG_BIG)
            return taken + before[TM - 1:TM, :]

        lax.fori_loop(0, n_kt, bias_tile, jnp.zeros((1, QB), jnp.float32))

    m_ref[...] = jnp.full(m_ref.shape, NEG_BIG, jnp.float32)
    l_ref[...] = jnp.zeros(l_ref.shape, jnp.float32)
    acc_ref[...] = jnp.zeros(acc_ref.shape, jnp.float32)

    def pair(h):
        return slice((h // 2) * LANES, (h // 2 + 1) * LANES)

    def logits_into(s_ref, mx_ref, j, heads):
        bias = sc_ref[j]
        for h in heads:
            s = jnp.dot(ka_ref[0, j, :, pair(h)], qmx_ref[h], preferred_element_type=jnp.float32) + bias
            s_ref[h] = s
            mx_ref[h] = jnp.max(s, axis=0, keepdims=True)

    ones_rows = jnp.ones((ONES_ROWS, TM), vat_ref.dtype)

    def absorb(s_ref, mx_ref, j, heads):
        for h in heads:
            m_old = m_ref[h]
            m_new = jnp.maximum(m_old, mx_ref[h])
            alpha = jnp.exp2(m_old - m_new)
            p = jnp.exp2(s_ref[h] - m_new).astype(vat_ref.dtype)
            pv = jnp.dot(jnp.concatenate([vat_ref[0, j, pair(h), :], ones_rows], axis=0), p,
                         preferred_element_type=jnp.float32)
            acc_ref[h] = acc_ref[h] * alpha + pv[0:LANES]
            l_ref[h] = l_ref[h] * alpha + pv[LANES:LANES + SUBLANES]
            m_ref[h] = m_new

    for h0 in range(0, n_heads, HEAD_GROUP):
        heads = range(h0, h0 + HEAD_GROUP)
        logits_into(sa_ref, mxa_ref, 0, heads)

        def tile_pair(t, heads=heads):
            logits_into(sb_ref, mxb_ref, t + 1, heads)
            absorb(sa_ref, mxa_ref, t, heads)
            logits_into(sa_ref, mxa_ref, jnp.minimum(t + 2, n_kt - 1), heads)
            absorb(sb_ref, mxb_ref, t + 1, heads)

        def attn_quad(g, carry):
            tile_pair(4 * g)
            tile_pair(4 * g + 2)
            return carry

        lax.fori_loop(0, n_kt // 4, attn_quad, 0)

        @pl.when(n_kt % 4 != 0)
        def _():
            tile_pair(n_kt - 2)

    for hp in range(n_heads // 2):
        even = acc_ref[2 * hp] / l_ref[2 * hp, 0:1, :]
        odd = acc_ref[2 * hp + 1] / l_ref[2 * hp + 1, 0:1, :]
        o_ref[:, hp * LANES:(hp + 1) * LANES] = jnp.where(row_q < HEAD_DIM, even, odd).T.astype(o_ref.dtype)


def _dsa(qat, qit, wit, kiw4, ka4, vat, seq):
    bsz, n_kt, d_a, _ = qat.shape
    n_heads = d_a // HEAD_DIM
    n_qb = seq // QB
    k_top = min(TOPK_MAX, seq // 4)
    qtile = lambda b, i: (b, i, 0, 0)
    per_b = lambda b, i: (b, 0, 0, 0)
    f32 = jnp.float32
    return pl.pallas_call(
        functools.partial(_dsa_kernel, k_top=float(k_top), n_heads=n_heads),
        grid=(bsz, n_qb),
        in_specs=[pl.BlockSpec((1, 1, d_a, QB), qtile),
                  pl.BlockSpec((1, 1, qit.shape[2], QB), qtile),
                  pl.BlockSpec((1, 1, N_IDX_HEADS, QB), qtile),
                  pl.BlockSpec((1, n_kt, TM, LANES), per_b, pipeline_mode=pl.Buffered(1)),
                  pl.BlockSpec((1, n_kt, TM, d_a), per_b, pipeline_mode=pl.Buffered(1)),
                  pl.BlockSpec((1, n_kt, d_a, TM), per_b, pipeline_mode=pl.Buffered(1)),
                  pl.BlockSpec((TM, TM), lambda b, i: (0, 0), pipeline_mode=pl.Buffered(1))],
        out_specs=pl.BlockSpec((QB, d_a), lambda b, i: (b * n_qb + i, 0)),
        out_shape=jax.ShapeDtypeStruct((bsz * seq, d_a), jnp.bfloat16),
        scratch_shapes=[pltpu.VMEM((N_IDX_HEADS, LANES, QB), jnp.bfloat16),
                        pltpu.VMEM((n_heads, LANES, QB), jnp.bfloat16),
                        pltpu.VMEM((n_kt, TM, QB), f32),
                        pltpu.VMEM((n_heads, 1, QB), f32),
                        pltpu.VMEM((n_heads, SUBLANES, QB), f32),
                        pltpu.VMEM((n_heads, LANES, QB), f32),
                        pltpu.VMEM((n_heads, TM, QB), f32),
                        pltpu.VMEM((n_heads, TM, QB), f32),
                        pltpu.VMEM((n_heads, 1, QB), f32),
                        pltpu.VMEM((n_heads, 1, QB), f32),
                        pltpu.VMEM((2, SUBLANES, QB), f32)],
        compiler_params=_cparams(("arbitrary", "arbitrary")),
        name="dsa",
    )(qat, qit, wit, kiw4, ka4, vat, jnp.asarray(np.tril(np.ones((TM, TM), np.float32)), jnp.bfloat16))


def _group_mean(y, avg):
    hi = y.astype(jnp.bfloat16)
    lo = (y - hi.astype(jnp.float32)).astype(jnp.bfloat16)
    return (jnp.dot(hi, avg, preferred_element_type=jnp.float32)
            + jnp.dot(lo, avg, preferred_element_type=jnp.float32))


def _ret_kernel(q_ref, k_ref, v_ref, g_ref, dec_ref, zt_ref, xi_ref, gc_ref, blk_ref, avg_ref, o_ref, st_ref,
                *, n_heads):
    @pl.when(pl.program_id(1) == 0)
    def _():
        st_ref[...] = jnp.zeros_like(st_ref)

    even = lax.broadcasted_iota(jnp.int32, (RET_C, LANES), 1) < HEAD_DIM
    avg = avg_ref[...]
    for p in range(n_heads // 2):
        sl = slice(p * LANES, (p + 1) * LANES)
        state = st_ref[p]
        for c in range(q_ref.shape[0] // RET_C):
            rows = slice(c * RET_C, (c + 1) * RET_C)
            qp, kp, vp = q_ref[rows, sl], k_ref[rows, sl], v_ref[rows, sl]
            kpt = kp.astype(jnp.float32).T
            kpt_b = kpt.astype(kp.dtype)
            inner = None
            for e in range(2):
                q_e = jnp.where(even if e == 0 else jnp.logical_not(even), qp, jnp.zeros_like(qp))
                s = jnp.dot(q_e, kpt_b, preferred_element_type=jnp.float32) * dec_ref[2 * p + e]
                t = jnp.dot(s.astype(vp.dtype), vp, preferred_element_type=jnp.float32)
                inner = t if e == 0 else jnp.where(even, inner, t)
            cross = jnp.dot(qp, state.astype(qp.dtype), preferred_element_type=jnp.float32) * xi_ref[p]
            y = inner + cross
            yc = y - _group_mean(y, avg)
            yn = yc * lax.rsqrt(_group_mean(yc * yc, avg) + EPS)
            g = g_ref[rows, sl]
            o_ref[rows, sl] = (g / (1.0 + jnp.exp(-g)) * yn).astype(o_ref.dtype)
            kz = (kpt * zt_ref[p]).astype(kp.dtype)
            kv = jnp.dot(kz, vp, preferred_element_type=jnp.float32)
            state = state * gc_ref[p] + kv * blk_ref[...]
        st_ref[p] = state


def _ret_consts(n_heads):
    log_gamma = jnp.log1p(-jnp.exp2(-5.0 - jnp.arange(n_heads, dtype=jnp.float32)))
    pos = jnp.arange(RET_C, dtype=jnp.float32)
    diff = pos[:, None] - pos[None, :]
    dec = jnp.where(diff[None] >= 0, jnp.exp(jnp.maximum(diff, 0.0)[None] * log_gamma[:, None, None]), 0.0)
    zeta = jnp.exp((RET_C - 1.0 - pos)[None, :] * log_gamma[:, None])
    xi = jnp.exp((pos + 1.0)[None, :] * log_gamma[:, None])
    gc = jnp.exp(RET_C * log_gamma)
    n_pairs = n_heads // 2
    lanes = lambda a: jnp.repeat(a.reshape(n_pairs, 2, -1), HEAD_DIM, axis=1)
    zt = lanes(zeta)
    xi_p = jnp.swapaxes(lanes(xi), 1, 2)
    gc_p = jnp.broadcast_to(lanes(gc[:, None]), (n_pairs, LANES, LANES))
    head_of = jnp.arange(LANES) // HEAD_DIM
    blk = (head_of[:, None] == head_of[None, :]).astype(jnp.float32)
    avg = (blk / HEAD_DIM).astype(jnp.bfloat16)
    f32 = lambda a: a.astype(jnp.float32)
    return dec, f32(zt), f32(xi_p), f32(gc_p), blk, avg


def _ret(qb, kb, vb, gb, seq):
    n, d_b = qb.shape
    bsz = n // seq
    n_heads = d_b // HEAD_DIM
    step = RET_C * RET_CHUNKS
    n_c = seq // step
    consts = _ret_consts(n_heads)
    row = lambda b, c: (b * n_c + c, 0)
    const_spec = lambda a: pl.BlockSpec(a.shape, lambda b, c: (0,) * a.ndim)
    return pl.pallas_call(
        functools.partial(_ret_kernel, n_heads=n_heads),
        grid=(bsz, n_c),
        in_specs=[pl.BlockSpec((step, d_b), row)] * 4 + [const_spec(a) for a in consts],
        out_specs=pl.BlockSpec((step, d_b), row),
        out_shape=jax.ShapeDtypeStruct((n, d_b), jnp.bfloat16),
        scratch_shapes=[pltpu.VMEM((n_heads // 2, LANES, LANES), jnp.float32)],
        compiler_params=_cparams(("arbitrary", "arbitrary")),
        name="ret",
    )(qb, kb, vb, gb, *consts)


def _tiles_shape(rows, d):
    return (rows // SUBLANES, d // LANES, SUBLANES, LANES)


def _to_tiles(ref, x):
    for s in range(ref.shape[1]):
        ref[:, s] = x[:, s * LANES:(s + 1) * LANES].reshape(ref.shape[0], SUBLANES, LANES)


def _from_tiles(ref):
    rows = ref.shape[0] * SUBLANES
    return jnp.concatenate([ref[:, s].reshape(rows, LANES) for s in range(ref.shape[1])], axis=1)


def _lane_first_eq(x, m, lane):
    return jnp.min(jnp.where(x == m, lane, float(LANES)), axis=1, keepdims=True)


def _mix_out_kernel(x_ref, ya_ref, yb_ref, woa_ref, wob_ref, gt_ref, sc_ref, sh_ref, g_ref, wr_ref, tri_ref,
                    x1_ref, h2_ref, route_ref, cnt_ref, carry_ref):
    @pl.when(pl.program_id(0) == 0)
    def _():
        carry_ref[...] = jnp.zeros_like(carry_ref)

    mix = (jnp.dot(ya_ref[...], woa_ref[...], preferred_element_type=jnp.float32)
           + jnp.dot(yb_ref[...], wob_ref[...], preferred_element_type=jnp.float32))
    x1 = x_ref[...] + gt_ref[0] * mix
    x1_ref[...] = x1
    h2 = _rmsnorm_mod(x1, g_ref[...], sc_ref[0], sh_ref[0])
    _to_tiles(h2_ref, h2)

    lg = jnp.dot(h2.astype(jnp.bfloat16), wr_ref[...], preferred_element_type=jnp.float32)
    lane = lax.broadcasted_iota(jnp.int32, lg.shape, 1).astype(jnp.float32)
    is_grp = (lane >= N_EXPERTS) & (lane < N_EXPERTS + N_GROUPS)
    gl = jnp.where(is_grp, lg, -jnp.inf)
    gmax = jnp.max(gl, axis=1, keepdims=True)
    grp = _lane_first_eq(gl, gmax, lane) - N_EXPERTS
    p_grp = 1.0 / jnp.sum(jnp.exp(gl - gmax), axis=1, keepdims=True)
    in_grp = jnp.floor(lane * (1.0 / EXPERTS_PER_GROUP)) == grp
    f = jnp.where(in_grp & (lane < N_EXPERTS), lg, -jnp.inf)
    f1 = jnp.max(f, axis=1, keepdims=True)
    e1 = _lane_first_eq(f, f1, lane)
    f = jnp.where(lane == e1, -jnp.inf, f)
    f2 = jnp.max(f, axis=1, keepdims=True)
    e2 = _lane_first_eq(f, f2, lane)
    a2 = jnp.exp(f2 - f1)
    w1 = p_grp / (1.0 + a2)
    w2 = p_grp * a2 / (1.0 + a2)

    oh1 = jnp.where(lane == e1, 1.0, 0.0)
    oh2 = jnp.where(lane == e2, 1.0, 0.0)
    both = oh1 + oh2
    before = jnp.dot(tri_ref[...], both.astype(jnp.bfloat16), preferred_element_type=jnp.float32) + carry_ref[...]
    r1 = jnp.sum(before * oh1, axis=1, keepdims=True)
    r2 = jnp.sum(before * oh2, axis=1, keepdims=True)
    carry = carry_ref[...] + jnp.sum(both, axis=0, keepdims=True)
    carry_ref[...] = carry
    cnt_ref[...] = carry

    out = jnp.zeros(lg.shape, jnp.float32)
    for col, val in enumerate((e1, e2, w1, w2, r1, r2)):
        out = jnp.where(lane == col, val, out)
    route_ref[...] = out


def _mix_out(x2, ya, yb, wo_a, wo_b, mod3, g_ffn, w_route, seq):
    n, d = x2.shape
    tm = MIX_TM
    tiles_per_seq = seq // tm
    tri = jnp.asarray(np.tril(np.ones((tm, tm), np.float32), -1), jnp.bfloat16)
    row = lambda i: (i, 0)
    const = lambda i: (0, 0)
    modk = lambda k: pl.BlockSpec((1, 1, d), lambda i: ((i // tiles_per_seq) * 6 + k, 0, 0))
    return pl.pallas_call(
        _mix_out_kernel,
        grid=(n // tm,),
        in_specs=[pl.BlockSpec((tm, d), row),
                  pl.BlockSpec((tm, ya.shape[1]), row),
                  pl.BlockSpec((tm, yb.shape[1]), row),
                  pl.BlockSpec(wo_a.shape, const),
                  pl.BlockSpec(wo_b.shape, const),
                  modk(2), modk(4), modk(3),
                  pl.BlockSpec((1, d), const),
                  pl.BlockSpec(w_route.shape, const),
                  pl.BlockSpec(tri.shape, const)],
        out_specs=(pl.BlockSpec((tm, d), row), pl.BlockSpec(_tiles_shape(tm, d), lambda i: (i, 0, 0, 0)),
                   pl.BlockSpec((tm, LANES), row), pl.BlockSpec((1, LANES), const)),
        out_shape=(jax.ShapeDtypeStruct((n, d), jnp.float32), jax.ShapeDtypeStruct(_tiles_shape(n, d), jnp.float32),
                   jax.ShapeDtypeStruct((n, LANES), jnp.float32), jax.ShapeDtypeStruct((1, LANES), jnp.float32)),
        scratch_shapes=[pltpu.VMEM((1, LANES), jnp.float32)],
        compiler_params=_cparams(("arbitrary",)),
        name="mix_out",
    )(x2, ya, yb, wo_a, wo_b, mod3, mod3, mod3, g_ffn, w_route, tri)


def _plan_kernel(route_ref, pst_ref, dest_ref):
    r = route_ref[...]
    lane = lax.broadcasted_iota(jnp.int32, r.shape, 1).astype(jnp.float32)
    pst = pst_ref[...]
    d1 = jnp.sum(jnp.where(lane == r[:, 0:1], pst, 0.0), axis=1, keepdims=True) + r[:, 4:5]
    d2 = jnp.sum(jnp.where(lane == r[:, 1:2], pst, 0.0), axis=1, keepdims=True) + r[:, 5:6]
    t1 = jnp.floor(d1 * (1.0 / SUBLANES))
    t2 = jnp.floor(d2 * (1.0 / SUBLANES))
    packed = jnp.zeros(r.shape, jnp.float32)
    for k, v in enumerate((t1, d1 - t1 * SUBLANES, t2, d2 - t2 * SUBLANES)):
        packed = jnp.where(lane == float(k), v, packed)
    dest_ref[0] = packed.T[0:8, :].astype(jnp.int32)


def _dest_block(i):
    return (i // (PLAN_TM // TM), 0, i % (PLAN_TM // TM))


def _plan(route, pst_row):
    n = route.shape[0]
    return pl.pallas_call(
        _plan_kernel,
        grid=(n // PLAN_TM,),
        in_specs=[pl.BlockSpec((PLAN_TM, LANES), lambda i: (i, 0)), pl.BlockSpec((1, LANES), lambda i: (0, 0))],
        out_specs=pl.BlockSpec((1, 8, PLAN_TM), lambda i: (i, 0, 0)),
        out_shape=jax.ShapeDtypeStruct((n // PLAN_TM, 8, PLAN_TM), jnp.int32),
        compiler_params=_cparams(("arbitrary",)),
        name="plan",
    )(route, pst_row)


def _dispatch_kernel(zrow_ref, n_used_ref, dest_ref, h2_ref, xs_hbm, zbuf, sem, zsem, *, n_blk):
    i = pl.program_id(0)

    @pl.when(i == 0)
    def _():
        zbuf[...] = jnp.zeros(zbuf.shape, zbuf.dtype)
        blk_tiles = BM // SUBLANES
        for e in range(N_EXPERTS):
            pltpu.make_async_copy(zbuf, xs_hbm.at[pl.ds(zrow_ref[e], blk_tiles)], zsem).start()
        for e in range(N_EXPERTS):
            pltpu.make_async_copy(zbuf, xs_hbm.at[pl.ds(0, blk_tiles)], zsem).wait()
        for b in range(N_EXPERTS + 1):
            @pl.when(n_used_ref[0] + b <= n_blk)
            def _():
                tail = pltpu.make_async_copy(zbuf, xs_hbm.at[pl.ds((n_used_ref[0] + b) * blk_tiles, blk_tiles)], zsem)
                tail.start()
                tail.wait()

    def issue(k, c):
        for u in range(SUBLANES):
            r = k * SUBLANES + u
            row = h2_ref.at[k, :, u, :]
            pltpu.make_async_copy(row, xs_hbm.at[dest_ref[0, 0, r], :, dest_ref[0, 1, r], :], sem).start()
            pltpu.make_async_copy(row, xs_hbm.at[dest_ref[0, 2, r], :, dest_ref[0, 3, r], :], sem).start()
        return c

    lax.fori_loop(0, TM // SUBLANES, issue, 0)
    for _ in range(2):
        pltpu.make_async_copy(h2_ref, xs_hbm.at[pl.ds(0, TM // SUBLANES)], sem).wait()


def _dispatch(h2, dest, zrow, n_used, n_blk):
    n_rows = (n_blk + 1) * BM
    d = h2.shape[1] * LANES
    grid_spec = pltpu.PrefetchScalarGridSpec(
        num_scalar_prefetch=2,
        grid=(h2.shape[0] * SUBLANES // TM,),
        in_specs=[pl.BlockSpec((1, 8, TM), lambda i, z, nu: _dest_block(i), memory_space=pltpu.SMEM),
                  pl.BlockSpec(_tiles_shape(TM, d), lambda i, z, nu: (i, 0, 0, 0))],
        out_specs=pl.BlockSpec(memory_space=pl.ANY),
        scratch_shapes=[pltpu.VMEM(_tiles_shape(BM, d), h2.dtype),
                        pltpu.SemaphoreType.DMA(()),
                        pltpu.SemaphoreType.DMA(())],
    )
    return pl.pallas_call(
        functools.partial(_dispatch_kernel, n_blk=n_blk),
        grid_spec=grid_spec,
        out_shape=jax.ShapeDtypeStruct(_tiles_shape(n_rows, d), h2.dtype),
        compiler_params=_cparams(("arbitrary",)),
        name="dispatch",
    )(zrow, n_used, dest, h2)


def _experts_kernel(blk_e_ref, n_used_ref, x_ref, wg_ref, wu_ref, wd_ref, y_ref, wg_bf, wu_bf, wd_bf):
    j = pl.program_id(0)

    @pl.when((j == 0) | (blk_e_ref[j] != blk_e_ref[jnp.maximum(j - 1, 0)]))
    def _():
        wg_bf[...] = wg_ref[0].astype(wg_bf.dtype)
        wu_bf[...] = wu_ref[0].astype(wu_bf.dtype)
        wd_bf[...] = wd_ref[0].astype(wd_bf.dtype)

    @pl.when(j < n_used_ref[0])
    def _():
        x = _from_tiles(x_ref).astype(wg_bf.dtype)
        a = jnp.dot(x, wg_bf[...], preferred_element_type=jnp.float32)
        b = jnp.dot(x, wu_bf[...], preferred_element_type=jnp.float32)
        hmid = (a / (1.0 + jnp.exp(-a)) * b).astype(x.dtype)
        _to_tiles(y_ref, jnp.dot(hmid, wd_bf[...], preferred_element_type=jnp.float32))

    @pl.when(j >= n_used_ref[0])
    def _():
        y_ref[...] = jnp.zeros(y_ref.shape, y_ref.dtype)


def _experts(xs, wg, wu, wd, blk_e, n_used):
    n_blk = blk_e.shape[0]
    d, d_e = wg.shape[1], wg.shape[2]
    blk = _tiles_shape(BM, d)
    grid_spec = pltpu.PrefetchScalarGridSpec(
        num_scalar_prefetch=2,
        grid=(n_blk,),
        in_specs=[pl.BlockSpec(blk, lambda j, be, nu: (jnp.minimum(j, nu[0] - 1), 0, 0, 0)),
                  pl.BlockSpec((1, d, d_e), lambda j, be, nu: (be[j], 0, 0)),
                  pl.BlockSpec((1, d, d_e), lambda j, be, nu: (be[j], 0, 0)),
                  pl.BlockSpec((1, d_e, d), lambda j, be, nu: (be[j], 0, 0))],
        out_specs=pl.BlockSpec(blk, lambda j, be, nu: (j, 0, 0, 0)),
        scratch_shapes=[pltpu.VMEM((d, d_e), jnp.bfloat16), pltpu.VMEM((d, d_e), jnp.bfloat16),
                        pltpu.VMEM((d_e, d), jnp.bfloat16)],
    )
    return pl.pallas_call(
        _experts_kernel,
        grid_spec=grid_spec,
        out_shape=jax.ShapeDtypeStruct(_tiles_shape(n_blk * BM, d), jnp.float32),
        compiler_params=_cparams(("arbitrary",)),
        name="experts",
    )(blk_e, n_used, xs, wg, wu, wd)


def _final_kernel(dest_ref, dest_next_ref, x1_ref, route_ref, gt_ref, g_ref, ys_hbm, o_ref, ybuf, sem):
    i = pl.program_id(0)
    slot = i % 2

    tiles = TM // SUBLANES

    def gather(d_ref, s):
        def issue(k, c):
            for u in range(SUBLANES):
                r = k * SUBLANES + u
                pltpu.make_async_copy(ys_hbm.at[d_ref[0, 0, r], :, d_ref[0, 1, r], :],
                                      ybuf.at[s, k, :, u, :], sem.at[s]).start()
                pltpu.make_async_copy(ys_hbm.at[d_ref[0, 2, r], :, d_ref[0, 3, r], :],
                                      ybuf.at[s, tiles + k, :, u, :], sem.at[s]).start()
            return c
        lax.fori_loop(0, tiles, issue, 0)

    @pl.when(i == 0)
    def _():
        gather(dest_ref, 0)

    @pl.when(i + 1 < pl.num_programs(0))
    def _():
        gather(dest_next_ref, 1 - slot)

    pltpu.make_async_copy(ys_hbm.at[pl.ds(0, 2 * tiles)], ybuf.at[slot], sem.at[slot]).wait()
    w1 = route_ref[:, 2:3]
    w2 = route_ref[:, 3:4]
    y0 = _from_tiles(ybuf.at[slot, 0:tiles])
    y1 = _from_tiles(ybuf.at[slot, tiles:2 * tiles])
    x2 = x1_ref[...] + gt_ref[0] * (w1 * y0 + w2 * y1)
    o_ref[...] = x2 * lax.rsqrt(jnp.mean(x2 * x2, axis=-1, keepdims=True) + EPS) * g_ref[...]


def _final(x1, ys, dest, route, mod3, g_final, seq):
    n, d = x1.shape
    n_tiles = n // TM
    tiles_per_seq = seq // TM
    row = lambda i: (i, 0)
    return pl.pallas_call(
        _final_kernel,
        grid=(n_tiles,),
        in_specs=[pl.BlockSpec((1, 8, TM), _dest_block, memory_space=pltpu.SMEM),
                  pl.BlockSpec((1, 8, TM), lambda i: _dest_block(jnp.minimum(i + 1, n_tiles - 1)),
                               memory_space=pltpu.SMEM),
                  pl.BlockSpec((TM, d), row),
                  pl.BlockSpec((TM, LANES), row),
                  pl.BlockSpec((1, 1, d), lambda i: ((i // tiles_per_seq) * 6 + 5, 0, 0)),
                  pl.BlockSpec((1, d), lambda i: (0, 0)),
                  pl.BlockSpec(memory_space=pl.ANY)],
        out_specs=pl.BlockSpec((TM, d), row),
        out_shape=jax.ShapeDtypeStruct((n, d), jnp.float32),
        scratch_shapes=[pltpu.VMEM((2,) + _tiles_shape(2 * TM, d), jnp.float32),
                        pltpu.SemaphoreType.DMA((2,))],
        compiler_params=_cparams(("arbitrary",)),
        name="final",
    )(dest, dest, x1, route, mod3, g_final, ys)


def _block_layout(counts, n):
    cnt = counts[0, :N_EXPERTS].astype(jnp.int32)
    blocks = (cnt + BM - 1) // BM
    bends = jnp.cumsum(blocks)
    pstarts = (bends - blocks) * BM
    n_blk = (2 * n) // BM + N_EXPERTS
    pst_row = jnp.zeros((1, LANES), jnp.float32).at[0, :N_EXPERTS].set(pstarts.astype(jnp.float32))
    zrow = ((pstarts + cnt) // SUBLANES).astype(jnp.int32)
    n_used = bends[-1:].astype(jnp.int32)
    blk_e = jnp.sum(bends[None, :] <= jnp.arange(n_blk, dtype=jnp.int32)[:, None], axis=1)
    blk_e = jnp.minimum(blk_e, N_EXPERTS - 1).astype(jnp.int32)
    return pst_row, zrow, n_used, blk_e


def _layer(x2, c, pos2, w_ada, b_ada, g_mix, w_in, w_o, g_ffn, w_rg, w_re, w_up, w_gate, w_down, seq):
    n, d = x2.shape
    bsz = n // seq
    bf = jnp.bfloat16
    d_a = d // 2
    d_i = N_IDX_HEADS * IDX_DIM
    mod3 = _ada(c, w_ada, b_ada).reshape(bsz * 6, 1, d)

    c0 = 3 * d_a + d_i
    c1 = c0 + IDX_DIM + N_IDX_HEADS
    w_cat = jnp.concatenate([w_in[:, :c0], w_in[:, c0:c1],
                             jnp.zeros((d, LANES - (c1 - c0)), w_in.dtype), w_in[:, c1:]], axis=1).astype(bf)
    qat, ka, vat, qit, kiw, wit, qb, kb, vb, gb = _proj(x2, pos2, mod3, g_mix.reshape(1, d), w_cat, seq)

    n_kt = seq // TM
    ya = _dsa(qat, qit, wit, kiw.reshape(bsz, n_kt, TM, LANES), ka.reshape(bsz, n_kt, TM, d_a), vat, seq)
    yb = _ret(qb, kb, vb, gb, seq)

    w_route = jnp.concatenate([jnp.transpose(w_re, (1, 0, 2)).reshape(d, N_EXPERTS), w_rg,
                               jnp.zeros((d, LANES - N_EXPERTS - N_GROUPS), w_rg.dtype)], axis=1).astype(bf)
    x1, h2, route, counts = _mix_out(x2, ya, yb, w_o[:d_a].astype(bf), w_o[d_a:].astype(bf), mod3,
                                     g_ffn.reshape(1, d), w_route, seq)

    pst_row, zrow, n_used, blk_e = _block_layout(counts, n)
    dest = _plan(route, pst_row)
    xs = _dispatch(h2, dest, zrow, n_used, blk_e.shape[0])
    ys = _experts(xs, w_gate, w_up, w_down, blk_e, n_used)
    return x1, ys, dest, route, mod3


def kernel(x, c, positions, w_ada, b_ada, g_norm_mix, w_in, w_o, g_norm_ffn, w_router_group, w_router_expert,
           w_up, w_gate, w_down, g_norm_final):
    bsz, seq, d = x.shape
    depth = w_ada.shape[0]
    assert depth == 1, "the final norm is fused into the last layer's combine kernel"
    assert seq % (2 * TM) == 0 and seq % (RET_C * RET_CHUNKS) == 0 and seq % MIX_TM == 0 and (2 * bsz * seq) % BM == 0
    assert (bsz * seq) % PLAN_TM == 0 and PLAN_TM % TM == 0
    x2 = x.reshape(bsz * seq, d)
    pos2 = positions.astype(jnp.float32).reshape(bsz * seq, 1)
    x1, ys, dest, route, mod3 = _layer(x2, c, pos2, w_ada[0], b_ada[0], g_norm_mix[0], w_in[0], w_o[0],
                                       g_norm_ffn[0], w_router_group[0], w_router_expert[0], w_up[0], w_gate[0],
                                       w_down[0], seq)
    out = _final(x1, ys, dest, route, mod3, g_norm_final.reshape(1, d), seq)
    return out.reshape(bsz, seq, d)
```

```python
import functools

import jax
import jax.numpy as jnp
import numpy as np
from jax import lax
from jax.experimental import pallas as pl
from jax.experimental.pallas import tpu as pltpu

CHUNK = 64
HEAD_DIM = 64
N_IDX_HEADS = 16
IDX_DIM = 64
TOPK_MAX = 256
ROPE_THETA = 500000.0
ROT_DIM = HEAD_DIM // 4
RET_THETA = 10000.0
N_GROUPS = 4
EXPERTS_PER_GROUP = 8
N_EXPERTS = N_GROUPS * EXPERTS_PER_GROUP
EPS = 1e-6

LANES = 128
SUBLANES = 8
VMEM_LIMIT = 56 * 1024 * 1024

TM = 256
QB = TM
RET_C = 256
RET_CHUNKS = 2
BM = 256
PLAN_TM = 1024
MIX_TM = 512
CNT_ROWS = 64
ONES_ROWS = 16
HEAD_GROUP = 4
IDX_TILES = 8
ATT_TILES = 8
BISECT_PER_CHECK = 2
MAX_BISECT = 40

NEG_BIG = -1e30
LOG2E = 1.4426950408889634


def _cparams(sem):
    return pltpu.CompilerParams(dimension_semantics=sem, vmem_limit_bytes=VMEM_LIMIT)


def _ada_kernel(c_ref, w_ref, b_ref, o_ref):
    o_ref[...] = jnp.dot(c_ref[...], w_ref[...], preferred_element_type=jnp.float32) + b_ref[...]


def _ada(c, w_ada, b_ada):
    bsz, d = c.shape
    n_out = w_ada.shape[1]
    return pl.pallas_call(
        _ada_kernel,
        grid=(n_out // d,),
        in_specs=[pl.BlockSpec((bsz, d), lambda j: (0, 0)),
                  pl.BlockSpec((d, d), lambda j: (0, j)),
                  pl.BlockSpec((1, d), lambda j: (0, j))],
        out_specs=pl.BlockSpec((bsz, d), lambda j: (0, j)),
        out_shape=jax.ShapeDtypeStruct((bsz, n_out), jnp.float32),
        compiler_params=_cparams(("arbitrary",)),
        name="ada",
    )(c, w_ada, b_ada.reshape(1, n_out))


def _rmsnorm_mod(x, g, sc, sh):
    xn = x * lax.rsqrt(jnp.mean(x * x, axis=-1, keepdims=True) + EPS)
    return xn * g * (1.0 + sc) + sh


def _rope_lanes(x, cos, sin_lo, sin_hi, half):
    cols = []
    for k in range(x.shape[1] // LANES):
        xb = x[:, k * LANES:(k + 1) * LANES]
        cols.append(xb * cos + pltpu.roll(xb, LANES - half, 1) * sin_lo + pltpu.roll(xb, half, 1) * sin_hi)
    return cols[0] if len(cols) == 1 else jnp.concatenate(cols, axis=1)


def _proj_kernel(x_ref, pos_ref, sc_ref, sh_ref, g_ref, w_ref, tab_ref,
                 qat_ref, ka_ref, vat_ref, qit_ref, kiw_ref, wit_ref, qb_ref, kb_ref, vb_ref, gb_ref,
                 *, d_a, d_i, d_b):
    h = _rmsnorm_mod(x_ref[...], g_ref[...], sc_ref[0], sh_ref[0]).astype(jnp.bfloat16)
    pos = pos_ref[...]
    ang_a = pos * tab_ref[0:1, :]
    cos_a, sin_a = jnp.cos(ang_a), jnp.sin(ang_a)
    sa_lo, sa_hi = sin_a * tab_ref[1:2, :], sin_a * tab_ref[2:3, :]
    ang_b = pos * tab_ref[3:4, :]
    cos_b, sin_b = jnp.cos(ang_b), jnp.sin(ang_b)
    sb_lo, sb_hi = sin_b * tab_ref[4:5, :], sin_b * tab_ref[5:6, :]
    half_a, half_b = ROT_DIM // 2, HEAD_DIM // 2

    def seg(lo, width):
        return jnp.dot(h, w_ref[:, lo:lo + width], preferred_element_type=jnp.float32)

    o = 0
    qa = seg(o, d_a); o += d_a
    qat_ref[0, 0] = (_rope_lanes(qa, cos_a, sa_lo, sa_hi, half_a) * (HEAD_DIM ** -0.5 * LOG2E)).T.astype(qat_ref.dtype)
    ka = seg(o, d_a); o += d_a
    ka_ref[...] = _rope_lanes(ka, cos_a, sa_lo, sa_hi, half_a).astype(ka_ref.dtype)
    vat_ref[0, 0] = seg(o, d_a).T.astype(vat_ref.dtype); o += d_a
    qi = seg(o, d_i); o += d_i
    qit_ref[0, 0] = (_rope_lanes(qi, cos_a, sa_lo, sa_hi, half_a) * (IDX_DIM ** -0.5)).T.astype(qit_ref.dtype)
    kw = seg(o, LANES); o += LANES
    kiw_ref[...] = _rope_lanes(kw, cos_a, sa_lo, sa_hi, half_a).astype(kiw_ref.dtype)
    wit_ref[0, 0] = kw.T[IDX_DIM:IDX_DIM + N_IDX_HEADS, :] * (N_IDX_HEADS ** -0.5)
    qb = seg(o, d_b); o += d_b
    qb_ref[...] = _rope_lanes(qb, cos_b, sb_lo, sb_hi, half_b).astype(qb_ref.dtype)
    kb = seg(o, d_b); o += d_b
    kb_ref[...] = (_rope_lanes(kb, cos_b, sb_lo, sb_hi, half_b) * (HEAD_DIM ** -0.5)).astype(kb_ref.dtype)
    vb_ref[...] = seg(o, d_b).astype(vb_ref.dtype); o += d_b
    gb_ref[...] = seg(o, d_b)


def _rope_tables():
    lane = jnp.arange(LANES) % HEAD_DIM
    rows = []
    for rot, theta in ((ROT_DIM, ROPE_THETA), (HEAD_DIM, RET_THETA)):
        half = rot // 2
        inv_freq = theta ** (-jnp.arange(half, dtype=jnp.float32) / half)
        rows.append(jnp.where(lane < rot, inv_freq[lane % half], 0.0))
        rows.append(jnp.where(lane < half, -1.0, 0.0))
        rows.append(jnp.where((lane >= half) & (lane < rot), 1.0, 0.0))
    rows += [jnp.zeros((LANES,), jnp.float32)] * 2
    return jnp.stack(rows).astype(jnp.float32)


def _proj(x2, pos2, mod3, g_mix, w_cat, seq):
    n, d = x2.shape
    bsz = n // seq
    d_a = d // 2
    d_b = d // 2
    d_i = N_IDX_HEADS * IDX_DIM
    tiles_per_seq = seq // TM
    tab = _rope_tables()
    row = lambda i: (i, 0)
    const = lambda i: (0, 0)
    tile4 = lambda i: (i // tiles_per_seq, i % tiles_per_seq, 0, 0)
    bf = jnp.bfloat16
    out_shape = (
        jax.ShapeDtypeStruct((bsz, tiles_per_seq, d_a, TM), bf),
        jax.ShapeDtypeStruct((n, d_a), bf),
        jax.ShapeDtypeStruct((bsz, tiles_per_seq, d_a, TM), bf),
        jax.ShapeDtypeStruct((bsz, tiles_per_seq, d_i, TM), bf),
        jax.ShapeDtypeStruct((n, LANES), bf),
        jax.ShapeDtypeStruct((bsz, tiles_per_seq, N_IDX_HEADS, TM), jnp.float32),
        jax.ShapeDtypeStruct((n, d_b), bf),
        jax.ShapeDtypeStruct((n, d_b), bf),
        jax.ShapeDtypeStruct((n, d_b), bf),
        jax.ShapeDtypeStruct((n, d_b), jnp.float32),
    )
    out_specs = (
        pl.BlockSpec((1, 1, d_a, TM), tile4),
        pl.BlockSpec((TM, d_a), row),
        pl.BlockSpec((1, 1, d_a, TM), tile4),
        pl.BlockSpec((1, 1, d_i, TM), tile4),
        pl.BlockSpec((TM, LANES), row),
        pl.BlockSpec((1, 1, N_IDX_HEADS, TM), tile4),
        pl.BlockSpec((TM, d_b), row),
        pl.BlockSpec((TM, d_b), row),
        pl.BlockSpec((TM, d_b), row),
        pl.BlockSpec((TM, d_b), row),
    )
    return pl.pallas_call(
        functools.partial(_proj_kernel, d_a=d_a, d_i=d_i, d_b=d_b),
        grid=(n // TM,),
        in_specs=[pl.BlockSpec((TM, d), row),
                  pl.BlockSpec((TM, 1), row),
                  pl.BlockSpec((1, 1, d), lambda i: ((i // tiles_per_seq) * 6 + 1, 0, 0)),
                  pl.BlockSpec((1, 1, d), lambda i: ((i // tiles_per_seq) * 6 + 0, 0, 0)),
                  pl.BlockSpec((1, d), const),
                  pl.BlockSpec(w_cat.shape, const),
                  pl.BlockSpec(tab.shape, const)],
        out_specs=out_specs,
        out_shape=out_shape,
        compiler_params=_cparams(("arbitrary",)),
        name="proj",
    )(x2, pos2, mod3, mod3, g_mix, w_cat, tab)


def _row_blocks(x, rows):
    return [x[r * rows:(r + 1) * rows] for r in range(x.shape[0] // rows)]


def _dsa_kernel(qat_ref, qit_ref, wit_ref, kiw_ref, ka_ref, vat_ref, tri_ref, o_ref,
                qix_ref, qmx_ref, sc_ref, m_ref, l_ref, acc_ref, sa_ref, sb_ref, mxa_ref, mxb_ref, lohi_ref,
                *, k_top, n_heads):
    i = pl.program_id(1)
    n_grp = (i * QB + QB + 2 * TM - 1) // (2 * TM)
    n_kt = 2 * n_grp
    n_real = (i * QB + QB + TM - 1) // TM

    zero_rows = jnp.zeros((LANES - IDX_DIM, QB), qix_ref.dtype)
    for h in range(N_IDX_HEADS):
        qix_ref[h] = jnp.concatenate([qit_ref[0, 0, h * IDX_DIM:(h + 1) * IDX_DIM, :], zero_rows], axis=0)
    row_q = lax.broadcasted_iota(jnp.int32, (LANES, QB), 0)
    for h in range(n_heads):
        pair = qat_ref[0, 0, (h // 2) * LANES:(h // 2 + 1) * LANES, :]
        own = (row_q < HEAD_DIM) if h % 2 == 0 else (row_q >= HEAD_DIM)
        qmx_ref[h] = jnp.where(own, pair, jnp.zeros_like(pair))

    q_chunk = (i * QB + lax.broadcasted_iota(jnp.int32, (1, QB), 1)) // CHUNK
    key_chunk_in_tile = lax.broadcasted_iota(jnp.int32, (TM, QB), 0) // CHUNK
    w_all = wit_ref[0, 0]

    def idx_tiles(tiles):
        lo, hi = lohi_ref[0], lohi_ref[1]
        for j in tiles:
            kt = kiw_ref[0, j]
            acc = None
            for h in range(N_IDX_HEADS):
                d = jnp.dot(kt, qix_ref[h], preferred_element_type=jnp.float32)
                t = w_all[h:h + 1, :] * jnp.maximum(d, 0.0)
                acc = t if acc is None else acc + t
            adm = key_chunk_in_tile <= q_chunk - j * (TM // CHUNK)
            s = jnp.where(adm, acc, -jnp.inf)
            sc_ref[j] = s
            lo = jnp.minimum(lo, functools.reduce(jnp.minimum, _row_blocks(jnp.where(adm, acc, jnp.inf), SUBLANES)))
            hi = jnp.maximum(hi, functools.reduce(jnp.maximum, _row_blocks(s, SUBLANES)))
        lohi_ref[0], lohi_ref[1] = lo, hi

    lohi_ref[0] = jnp.full((SUBLANES, QB), jnp.inf, jnp.float32)
    lohi_ref[1] = jnp.full((SUBLANES, QB), -jnp.inf, jnp.float32)

    def idx_step(g, carry):
        idx_tiles([IDX_TILES * g + u for u in range(IDX_TILES)])
        return carry

    lax.fori_loop(0, n_kt // IDX_TILES, idx_step, 0)

    def idx_tail(p, carry):
        first = n_kt // IDX_TILES * IDX_TILES + 2 * p
        idx_tiles([first, first + 1])
        return carry

    lax.fori_loop(0, n_kt % IDX_TILES // 2, idx_tail, 0)

    lo = jnp.min(lohi_ref[0], axis=0, keepdims=True)
    hi = jnp.max(lohi_ref[1], axis=0, keepdims=True)
    n_adm = ((i * QB + lax.broadcasted_iota(jnp.int32, (1, QB), 1)) // CHUNK + 1) * CHUNK

    def bisect_cond(carry):
        it, _, _, _, unsettled = carry
        return (it < MAX_BISECT) & (unsettled > 0.0)

    def bisect(carry):
        it, lo, hi, cnt_lo, _ = carry
        for _ in range(BISECT_PER_CHECK):
            mid = lo + (hi - lo) * 0.5
            mid_b = jnp.broadcast_to(mid, (CNT_ROWS, QB))

            def count_tile(j, cnt, mid_b=mid_b):
                for blk in _row_blocks(sc_ref[j], CNT_ROWS):
                    cnt = cnt + jnp.where(blk >= mid_b, 1.0, 0.0)
                return cnt

            cnt = lax.fori_loop(0, n_real, count_tile, jnp.zeros((CNT_ROWS, QB), jnp.float32))
            c = jnp.sum(cnt, axis=0, keepdims=True)
            ge = c >= k_top
            lo = jnp.where(ge, mid, lo)
            hi = jnp.where(ge, hi, mid)
            cnt_lo = jnp.where(ge, c, cnt_lo)
        return it + BISECT_PER_CHECK, lo, hi, cnt_lo, jnp.max(jnp.where(cnt_lo > k_top, 1.0, 0.0))

    cnt0 = n_adm.astype(jnp.float32)
    _, thr, thr_hi, _, unsettled = lax.while_loop(bisect_cond, bisect,
                                                  (0, lo, hi, cnt0, jnp.max(jnp.where(cnt0 > k_top, 1.0, 0.0))))

    @pl.when(unsettled <= 0.0)
    def _():
        def bias_tile(j, carry):
            sc_ref[j] = jnp.where(sc_ref[j] >= thr, 0.0, NEG_BIG)
            return carry

        lax.fori_loop(0, n_kt, bias_tile, 0)

    @pl.when(unsettled > 0.0)
    def _():
        def count_hi(j, cnt):
            return cnt + functools.reduce(jnp.add, _row_blocks(jnp.where(sc_ref[j] >= thr_hi, 1.0, 0.0), SUBLANES))

        above = jnp.sum(lax.fori_loop(0, n_kt, count_hi, jnp.zeros((SUBLANES, QB), jnp.float32)),
                        axis=0, keepdims=True)
        top = jnp.where(above < k_top, thr_hi, jnp.inf)
        room = k_top - jnp.where(above < k_top, above, 0.0)

        def bias_tile(j, taken):
            s = sc_ref[j]
            tied = (s >= thr) & (s < top)
            before = jnp.dot(tri_ref[...], jnp.where(tied, 1.0, 0.0).astype(tri_ref.dtype),
                             preferred_element_type=jnp.float32)
            keep = (s >= top) | (tied & (taken + before <= room))
            sc_ref[j] = jnp.where(keep, 0.0, NEG_BIG)
            return taken + before[TM - 1:TM, :]

        lax.fori_loop(0, n_kt, bias_tile, jnp.zeros((1, QB), jnp.float32))

    m_ref[...] = jnp.full(m_ref.shape, NEG_BIG, jnp.float32)
    l_ref[...] = jnp.zeros(l_ref.shape, jnp.float32)
    acc_ref[...] = jnp.zeros(acc_ref.shape, jnp.float32)

    def pair(h):
        return slice((h // 2) * LANES, (h // 2 + 1) * LANES)

    def logits_into(s_ref, mx_ref, j, heads):
        bias = sc_ref[j]
        for h in heads:
            s = jnp.dot(ka_ref[0, j, :, pair(h)], qmx_ref[h], preferred_element_type=jnp.float32) + bias
            s_ref[h] = s
            mx_ref[h] = jnp.max(s, axis=0, keepdims=True)

    ones_rows = jnp.ones((ONES_ROWS, TM), vat_ref.dtype)

    def absorb(s_ref, mx_ref, j, heads):
        for h in heads:
            m_old = m_ref[h]
            m_new = jnp.maximum(m_old, mx_ref[h])
            alpha = jnp.exp2(m_old - m_new)
            p = jnp.exp2(s_ref[h] - m_new).astype(vat_ref.dtype)
            pv = jnp.dot(jnp.concatenate([vat_ref[0, j, pair(h), :], ones_rows], axis=0), p,
                         preferred_element_type=jnp.float32)
            acc_ref[h] = acc_ref[h] * alpha + pv[0:LANES]
            l_ref[h] = l_ref[h] * alpha + pv[LANES:LANES + SUBLANES]
            m_ref[h] = m_new

    for h0 in range(0, n_heads, HEAD_GROUP):
        heads = range(h0, h0 + HEAD_GROUP)
        logits_into(sa_ref, mxa_ref, 0, heads)

        def tile_pair(t, heads=heads):
            logits_into(sb_ref, mxb_ref, t + 1, heads)
            absorb(sa_ref, mxa_ref, t, heads)
            logits_into(sa_ref, mxa_ref, jnp.minimum(t + 2, n_kt - 1), heads)
            absorb(sb_ref, mxb_ref, t + 1, heads)

        def attn_step(g, carry):
            for u in range(0, ATT_TILES, 2):
                tile_pair(ATT_TILES * g + u)
            return carry

        lax.fori_loop(0, n_kt // ATT_TILES, attn_step, 0)

        def attn_tail(p, carry):
            tile_pair(n_kt // ATT_TILES * ATT_TILES + 2 * p)
            return carry

        lax.fori_loop(0, n_kt % ATT_TILES // 2, attn_tail, 0)

    for hp in range(n_heads // 2):
        even = acc_ref[2 * hp] / l_ref[2 * hp, 0:1, :]
        odd = acc_ref[2 * hp + 1] / l_ref[2 * hp + 1, 0:1, :]
        o_ref[:, hp * LANES:(hp + 1) * LANES] = jnp.where(row_q < HEAD_DIM, even, odd).T.astype(o_ref.dtype)


def _dsa(qat, qit, wit, kiw4, ka4, vat, seq):
    bsz, n_kt, d_a, _ = qat.shape
    n_heads = d_a // HEAD_DIM
    n_qb = seq // QB
    k_top = min(TOPK_MAX, seq // 4)
    qtile = lambda b, i: (b, i, 0, 0)
    per_b = lambda b, i: (b, 0, 0, 0)
    f32 = jnp.float32
    return pl.pallas_call(
        functools.partial(_dsa_kernel, k_top=float(k_top), n_heads=n_heads),
        grid=(bsz, n_qb),
        in_specs=[pl.BlockSpec((1, 1, d_a, QB), qtile),
                  pl.BlockSpec((1, 1, qit.shape[2], QB), qtile),
                  pl.BlockSpec((1, 1, N_IDX_HEADS, QB), qtile),
                  pl.BlockSpec((1, n_kt, TM, LANES), per_b, pipeline_mode=pl.Buffered(1)),
                  pl.BlockSpec((1, n_kt, TM, d_a), per_b, pipeline_mode=pl.Buffered(1)),
                  pl.BlockSpec((1, n_kt, d_a, TM), per_b, pipeline_mode=pl.Buffered(1)),
                  pl.BlockSpec((TM, TM), lambda b, i: (0, 0), pipeline_mode=pl.Buffered(1))],
        out_specs=pl.BlockSpec((QB, d_a), lambda b, i: (b * n_qb + i, 0)),
        out_shape=jax.ShapeDtypeStruct((bsz * seq, d_a), jnp.bfloat16),
        scratch_shapes=[pltpu.VMEM((N_IDX_HEADS, LANES, QB), jnp.bfloat16),
                        pltpu.VMEM((n_heads, LANES, QB), jnp.bfloat16),
                        pltpu.VMEM((n_kt, TM, QB), f32),
                        pltpu.VMEM((n_heads, 1, QB), f32),
                        pltpu.VMEM((n_heads, SUBLANES, QB), f32),
                        pltpu.VMEM((n_heads, LANES, QB), f32),
                        pltpu.VMEM((n_heads, TM, QB), f32),
                        pltpu.VMEM((n_heads, TM, QB), f32),
                        pltpu.VMEM((n_heads, 1, QB), f32),
                        pltpu.VMEM((n_heads, 1, QB), f32),
                        pltpu.VMEM((2, SUBLANES, QB), f32)],
        compiler_params=_cparams(("arbitrary", "arbitrary")),
        name="dsa",
    )(qat, qit, wit, kiw4, ka4, vat, jnp.asarray(np.tril(np.ones((TM, TM), np.float32)), jnp.bfloat16))


def _group_mean(y, avg):
    hi = y.astype(jnp.bfloat16)
    lo = (y - hi.astype(jnp.float32)).astype(jnp.bfloat16)
    return (jnp.dot(hi, avg, preferred_element_type=jnp.float32)
            + jnp.dot(lo, avg, preferred_element_type=jnp.float32))


def _ret_kernel(q_ref, k_ref, v_ref, g_ref, dec_ref, zt_ref, xi_ref, gc_ref, blk_ref, avg_ref, o_ref, st_ref,
                *, n_heads):
    @pl.when(pl.program_id(1) == 0)
    def _():
        st_ref[...] = jnp.zeros_like(st_ref)

    even = lax.broadcasted_iota(jnp.int32, (RET_C, LANES), 1) < HEAD_DIM
    avg = avg_ref[...]
    for p in range(n_heads // 2):
        sl = slice(p * LANES, (p + 1) * LANES)
        state = st_ref[p]
        for c in range(q_ref.shape[0] // RET_C):
            rows = slice(c * RET_C, (c + 1) * RET_C)
            qp, kp, vp = q_ref[rows, sl], k_ref[rows, sl], v_ref[rows, sl]
            kpt = kp.astype(jnp.float32).T
            kpt_b = kpt.astype(kp.dtype)
            inner = None
            for e in range(2):
                q_e = jnp.where(even if e == 0 else jnp.logical_not(even), qp, jnp.zeros_like(qp))
                s = jnp.dot(q_e, kpt_b, preferred_element_type=jnp.float32) * dec_ref[2 * p + e]
                t = jnp.dot(s.astype(vp.dtype), vp, preferred_element_type=jnp.float32)
                inner = t if e == 0 else jnp.where(even, inner, t)
            cross = jnp.dot(qp, state.astype(qp.dtype), preferred_element_type=jnp.float32) * xi_ref[p]
            y = inner + cross
            yc = y - _group_mean(y, avg)
            yn = yc * lax.rsqrt(_group_mean(yc * yc, avg) + EPS)
            g = g_ref[rows, sl]
            o_ref[rows, sl] = (g / (1.0 + jnp.exp(-g)) * yn).astype(o_ref.dtype)
            kz = (kpt * zt_ref[p]).astype(kp.dtype)
            kv = jnp.dot(kz, vp, preferred_element_type=jnp.float32)
            state = state * gc_ref[p] + kv * blk_ref[...]
        st_ref[p] = state


def _ret_consts(n_heads):
    log_gamma = jnp.log1p(-jnp.exp2(-5.0 - jnp.arange(n_heads, dtype=jnp.float32)))
    pos = jnp.arange(RET_C, dtype=jnp.float32)
    diff = pos[:, None] - pos[None, :]
    dec = jnp.where(diff[None] >= 0, jnp.exp(jnp.maximum(diff, 0.0)[None] * log_gamma[:, None, None]), 0.0)
    zeta = jnp.exp((RET_C - 1.0 - pos)[None, :] * log_gamma[:, None])
    xi = jnp.exp((pos + 1.0)[None, :] * log_gamma[:, None])
    gc = jnp.exp(RET_C * log_gamma)
    n_pairs = n_heads // 2
    lanes = lambda a: jnp.repeat(a.reshape(n_pairs, 2, -1), HEAD_DIM, axis=1)
    zt = lanes(zeta)
    xi_p = jnp.swapaxes(lanes(xi), 1, 2)
    gc_p = jnp.broadcast_to(lanes(gc[:, None]), (n_pairs, LANES, LANES))
    head_of = jnp.arange(LANES) // HEAD_DIM
    blk = (head_of[:, None] == head_of[None, :]).astype(jnp.float32)
    avg = (blk / HEAD_DIM).astype(jnp.bfloat16)
    f32 = lambda a: a.astype(jnp.float32)
    return dec, f32(zt), f32(xi_p), f32(gc_p), blk, avg


def _ret(qb, kb, vb, gb, seq):
    n, d_b = qb.shape
    bsz = n // seq
    n_heads = d_b // HEAD_DIM
    step = RET_C * RET_CHUNKS
    n_c = seq // step
    consts = _ret_consts(n_heads)
    row = lambda b, c: (b * n_c + c, 0)
    const_spec = lambda a: pl.BlockSpec(a.shape, lambda b, c: (0,) * a.ndim)
    return pl.pallas_call(
        functools.partial(_ret_kernel, n_heads=n_heads),
        grid=(bsz, n_c),
        in_specs=[pl.BlockSpec((step, d_b), row)] * 4 + [const_spec(a) for a in consts],
        out_specs=pl.BlockSpec((step, d_b), row),
        out_shape=jax.ShapeDtypeStruct((n, d_b), jnp.bfloat16),
        scratch_shapes=[pltpu.VMEM((n_heads // 2, LANES, LANES), jnp.float32)],
        compiler_params=_cparams(("arbitrary", "arbitrary")),
        name="ret",
    )(qb, kb, vb, gb, *consts)


def _tiles_shape(rows, d):
    return (rows // SUBLANES, d // LANES, SUBLANES, LANES)


def _to_tiles(ref, x):
    for s in range(ref.shape[1]):
        ref[:, s] = x[:, s * LANES:(s + 1) * LANES].reshape(ref.shape[0], SUBLANES, LANES)


def _from_tiles(ref):
    rows = ref.shape[0] * SUBLANES
    return jnp.concatenate([ref[:, s].reshape(rows, LANES) for s in range(ref.shape[1])], axis=1)


def _lane_first_eq(x, m, lane):
    return jnp.min(jnp.where(x == m, lane, float(LANES)), axis=1, keepdims=True)


def _mix_out_kernel(x_ref, ya_ref, yb_ref, woa_ref, wob_ref, gt_ref, sc_ref, sh_ref, g_ref, wr_ref, tri_ref,
                    x1_ref, h2_ref, route_ref, cnt_ref, carry_ref):
    @pl.when(pl.program_id(0) == 0)
    def _():
        carry_ref[...] = jnp.zeros_like(carry_ref)

    mix = (jnp.dot(ya_ref[...], woa_ref[...], preferred_element_type=jnp.float32)
           + jnp.dot(yb_ref[...], wob_ref[...], preferred_element_type=jnp.float32))
    x1 = x_ref[...] + gt_ref[0] * mix
    x1_ref[...] = x1
    h2 = _rmsnorm_mod(x1, g_ref[...], sc_ref[0], sh_ref[0])
    _to_tiles(h2_ref, h2)

    lg = jnp.dot(h2.astype(jnp.bfloat16), wr_ref[...], preferred_element_type=jnp.float32)
    lane = lax.broadcasted_iota(jnp.int32, lg.shape, 1).astype(jnp.float32)
    is_grp = (lane >= N_EXPERTS) & (lane < N_EXPERTS + N_GROUPS)
    gl = jnp.where(is_grp, lg, -jnp.inf)
    gmax = jnp.max(gl, axis=1, keepdims=True)
    grp = _lane_first_eq(gl, gmax, lane) - N_EXPERTS
    p_grp = 1.0 / jnp.sum(jnp.exp(gl - gmax), axis=1, keepdims=True)
    in_grp = jnp.floor(lane * (1.0 / EXPERTS_PER_GROUP)) == grp
    f = jnp.where(in_grp & (lane < N_EXPERTS), lg, -jnp.inf)
    f1 = jnp.max(f, axis=1, keepdims=True)
    e1 = _lane_first_eq(f, f1, lane)
    f = jnp.where(lane == e1, -jnp.inf, f)
    f2 = jnp.max(f, axis=1, keepdims=True)
    e2 = _lane_first_eq(f, f2, lane)
    a2 = jnp.exp(f2 - f1)
    w1 = p_grp / (1.0 + a2)
    w2 = p_grp * a2 / (1.0 + a2)

    oh1 = jnp.where(lane == e1, 1.0, 0.0)
    oh2 = jnp.where(lane == e2, 1.0, 0.0)
    both = oh1 + oh2
    before = jnp.dot(tri_ref[...], both.astype(jnp.bfloat16), preferred_element_type=jnp.float32) + carry_ref[...]
    r1 = jnp.sum(before * oh1, axis=1, keepdims=True)
    r2 = jnp.sum(before * oh2, axis=1, keepdims=True)
    carry = carry_ref[...] + jnp.sum(both, axis=0, keepdims=True)
    carry_ref[...] = carry
    cnt_ref[...] = carry

    out = jnp.zeros(lg.shape, jnp.float32)
    for col, val in enumerate((e1, e2, w1, w2, r1, r2)):
        out = jnp.where(lane == col, val, out)
    route_ref[...] = out


def _mix_out(x2, ya, yb, wo_a, wo_b, mod3, g_ffn, w_route, seq):
    n, d = x2.shape
    tm = MIX_TM
    tiles_per_seq = seq // tm
    tri = jnp.asarray(np.tril(np.ones((tm, tm), np.float32), -1), jnp.bfloat16)
    row = lambda i: (i, 0)
    const = lambda i: (0, 0)
    modk = lambda k: pl.BlockSpec((1, 1, d), lambda i: ((i // tiles_per_seq) * 6 + k, 0, 0))
    return pl.pallas_call(
        _mix_out_kernel,
        grid=(n // tm,),
        in_specs=[pl.BlockSpec((tm, d), row),
                  pl.BlockSpec((tm, ya.shape[1]), row),
                  pl.BlockSpec((tm, yb.shape[1]), row),
                  pl.BlockSpec(wo_a.shape, const),
                  pl.BlockSpec(wo_b.shape, const),
                  modk(2), modk(4), modk(3),
                  pl.BlockSpec((1, d), const),
                  pl.BlockSpec(w_route.shape, const),
                  pl.BlockSpec(tri.shape, const)],
        out_specs=(pl.BlockSpec((tm, d), row), pl.BlockSpec(_tiles_shape(tm, d), lambda i: (i, 0, 0, 0)),
                   pl.BlockSpec((tm, LANES), row), pl.BlockSpec((1, LANES), const)),
        out_shape=(jax.ShapeDtypeStruct((n, d), jnp.float32), jax.ShapeDtypeStruct(_tiles_shape(n, d), jnp.float32),
                   jax.ShapeDtypeStruct((n, LANES), jnp.float32), jax.ShapeDtypeStruct((1, LANES), jnp.float32)),
        scratch_shapes=[pltpu.VMEM((1, LANES), jnp.float32)],
        compiler_params=_cparams(("arbitrary",)),
        name="mix_out",
    )(x2, ya, yb, wo_a, wo_b, mod3, mod3, mod3, g_ffn, w_route, tri)


def _plan_kernel(route_ref, pst_ref, dest_ref):
    r = route_ref[...]
    lane = lax.broadcasted_iota(jnp.int32, r.shape, 1).astype(jnp.float32)
    pst = pst_ref[...]
    d1 = jnp.sum(jnp.where(lane == r[:, 0:1], pst, 0.0), axis=1, keepdims=True) + r[:, 4:5]
    d2 = jnp.sum(jnp.where(lane == r[:, 1:2], pst, 0.0), axis=1, keepdims=True) + r[:, 5:6]
    t1 = jnp.floor(d1 * (1.0 / SUBLANES))
    t2 = jnp.floor(d2 * (1.0 / SUBLANES))
    packed = jnp.zeros(r.shape, jnp.float32)
    for k, v in enumerate((t1, d1 - t1 * SUBLANES, t2, d2 - t2 * SUBLANES)):
        packed = jnp.where(lane == float(k), v, packed)
    dest_ref[0] = packed.T[0:8, :].astype(jnp.int32)


def _dest_block(i):
    return (i // (PLAN_TM // TM), 0, i % (PLAN_TM // TM))


def _plan(route, pst_row):
    n = route.shape[0]
    return pl.pallas_call(
        _plan_kernel,
        grid=(n // PLAN_TM,),
        in_specs=[pl.BlockSpec((PLAN_TM, LANES), lambda i: (i, 0)), pl.BlockSpec((1, LANES), lambda i: (0, 0))],
        out_specs=pl.BlockSpec((1, 8, PLAN_TM), lambda i: (i, 0, 0)),
        out_shape=jax.ShapeDtypeStruct((n // PLAN_TM, 8, PLAN_TM), jnp.int32),
        compiler_params=_cparams(("arbitrary",)),
        name="plan",
    )(route, pst_row)


def _dispatch_kernel(zrow_ref, n_used_ref, dest_ref, h2_ref, xs_hbm, zbuf, sem, zsem, *, n_blk):
    i = pl.program_id(0)

    @pl.when(i == 0)
    def _():
        zbuf[...] = jnp.zeros(zbuf.shape, zbuf.dtype)
        blk_tiles = BM // SUBLANES
        for e in range(N_EXPERTS):
            pltpu.make_async_copy(zbuf, xs_hbm.at[pl.ds(zrow_ref[e], blk_tiles)], zsem).start()
        for e in range(N_EXPERTS):
            pltpu.make_async_copy(zbuf, xs_hbm.at[pl.ds(0, blk_tiles)], zsem).wait()
        for b in range(N_EXPERTS + 1):
            @pl.when(n_used_ref[0] + b <= n_blk)
            def _():
                tail = pltpu.make_async_copy(zbuf, xs_hbm.at[pl.ds((n_used_ref[0] + b) * blk_tiles, blk_tiles)], zsem)
                tail.start()
                tail.wait()

    def issue(k, c):
        for u in range(SUBLANES):
            r = k * SUBLANES + u
            row = h2_ref.at[k, :, u, :]
            pltpu.make_async_copy(row, xs_hbm.at[dest_ref[0, 0, r], :, dest_ref[0, 1, r], :], sem).start()
            pltpu.make_async_copy(row, xs_hbm.at[dest_ref[0, 2, r], :, dest_ref[0, 3, r], :], sem).start()
        return c

    lax.fori_loop(0, TM // SUBLANES, issue, 0)
    for _ in range(2):
        pltpu.make_async_copy(h2_ref, xs_hbm.at[pl.ds(0, TM // SUBLANES)], sem).wait()


def _dispatch(h2, dest, zrow, n_used, n_blk):
    n_rows = (n_blk + 1) * BM
    d = h2.shape[1] * LANES
    grid_spec = pltpu.PrefetchScalarGridSpec(
        num_scalar_prefetch=2,
        grid=(h2.shape[0] * SUBLANES // TM,),
        in_specs=[pl.BlockSpec((1, 8, TM), lambda i, z, nu: _dest_block(i), memory_space=pltpu.SMEM),
                  pl.BlockSpec(_tiles_shape(TM, d), lambda i, z, nu: (i, 0, 0, 0))],
        out_specs=pl.BlockSpec(memory_space=pl.ANY),
        scratch_shapes=[pltpu.VMEM(_tiles_shape(BM, d), h2.dtype),
                        pltpu.SemaphoreType.DMA(()),
                        pltpu.SemaphoreType.DMA(())],
    )
    return pl.pallas_call(
        functools.partial(_dispatch_kernel, n_blk=n_blk),
        grid_spec=grid_spec,
        out_shape=jax.ShapeDtypeStruct(_tiles_shape(n_rows, d), h2.dtype),
        compiler_params=_cparams(("arbitrary",)),
        name="dispatch",
    )(zrow, n_used, dest, h2)


def _experts_kernel(blk_e_ref, n_used_ref, x_ref, wg_ref, wu_ref, wd_ref, y_ref, wg_bf, wu_bf, wd_bf):
    j = pl.program_id(0)

    @pl.when((j == 0) | (blk_e_ref[j] != blk_e_ref[jnp.maximum(j - 1, 0)]))
    def _():
        wg_bf[...] = wg_ref[0].astype(wg_bf.dtype)
        wu_bf[...] = wu_ref[0].astype(wu_bf.dtype)
        wd_bf[...] = wd_ref[0].astype(wd_bf.dtype)

    @pl.when(j < n_used_ref[0])
    def _():
        x = _from_tiles(x_ref).astype(wg_bf.dtype)
        a = jnp.dot(x, wg_bf[...], preferred_element_type=jnp.float32)
        b = jnp.dot(x, wu_bf[...], preferred_element_type=jnp.float32)
        hmid = (a / (1.0 + jnp.exp(-a)) * b).astype(x.dtype)
        _to_tiles(y_ref, jnp.dot(hmid, wd_bf[...], preferred_element_type=jnp.float32))

    @pl.when(j >= n_used_ref[0])
    def _():
        y_ref[...] = jnp.zeros(y_ref.shape, y_ref.dtype)


def _experts(xs, wg, wu, wd, blk_e, n_used):
    n_blk = blk_e.shape[0]
    d, d_e = wg.shape[1], wg.shape[2]
    blk = _tiles_shape(BM, d)
    grid_spec = pltpu.PrefetchScalarGridSpec(
        num_scalar_prefetch=2,
        grid=(n_blk,),
        in_specs=[pl.BlockSpec(blk, lambda j, be, nu: (jnp.minimum(j, nu[0] - 1), 0, 0, 0)),
                  pl.BlockSpec((1, d, d_e), lambda j, be, nu: (be[j], 0, 0)),
                  pl.BlockSpec((1, d, d_e), lambda j, be, nu: (be[j], 0, 0)),
                  pl.BlockSpec((1, d_e, d), lambda j, be, nu: (be[j], 0, 0))],
        out_specs=pl.BlockSpec(blk, lambda j, be, nu: (j, 0, 0, 0)),
        scratch_shapes=[pltpu.VMEM((d, d_e), jnp.bfloat16), pltpu.VMEM((d, d_e), jnp.bfloat16),
                        pltpu.VMEM((d_e, d), jnp.bfloat16)],
    )
    return pl.pallas_call(
        _experts_kernel,
        grid_spec=grid_spec,
        out_shape=jax.ShapeDtypeStruct(_tiles_shape(n_blk * BM, d), jnp.float32),
        compiler_params=_cparams(("arbitrary",)),
        name="experts",
    )(blk_e, n_used, xs, wg, wu, wd)


def _final_kernel(dest_ref, dest_next_ref, x1_ref, route_ref, gt_ref, g_ref, ys_hbm, o_ref, ybuf, sem):
    i = pl.program_id(0)
    slot = i % 2

    tiles = TM // SUBLANES

    def gather(d_ref, s):
        def issue(k, c):
            for u in range(SUBLANES):
                r = k * SUBLANES + u
                pltpu.make_async_copy(ys_hbm.at[d_ref[0, 0, r], :, d_ref[0, 1, r], :],
                                      ybuf.at[s, k, :, u, :], sem.at[s]).start()
                pltpu.make_async_copy(ys_hbm.at[d_ref[0, 2, r], :, d_ref[0, 3, r], :],
                                      ybuf.at[s, tiles + k, :, u, :], sem.at[s]).start()
            return c
        lax.fori_loop(0, tiles, issue, 0)

    @pl.when(i == 0)
    def _():
        gather(dest_ref, 0)

    @pl.when(i + 1 < pl.num_programs(0))
    def _():
        gather(dest_next_ref, 1 - slot)

    pltpu.make_async_copy(ys_hbm.at[pl.ds(0, 2 * tiles)], ybuf.at[slot], sem.at[slot]).wait()
    w1 = route_ref[:, 2:3]
    w2 = route_ref[:, 3:4]
    y0 = _from_tiles(ybuf.at[slot, 0:tiles])
    y1 = _from_tiles(ybuf.at[slot, tiles:2 * tiles])
    x2 = x1_ref[...] + gt_ref[0] * (w1 * y0 + w2 * y1)
    o_ref[...] = x2 * lax.rsqrt(jnp.mean(x2 * x2, axis=-1, keepdims=True) + EPS) * g_ref[...]


def _final(x1, ys, dest, route, mod3, g_final, seq):
    n, d = x1.shape
    n_tiles = n // TM
    tiles_per_seq = seq // TM
    row = lambda i: (i, 0)
    return pl.pallas_call(
        _final_kernel,
        grid=(n_tiles,),
        in_specs=[pl.BlockSpec((1, 8, TM), _dest_block, memory_space=pltpu.SMEM),
                  pl.BlockSpec((1, 8, TM), lambda i: _dest_block(jnp.minimum(i + 1, n_tiles - 1)),
                               memory_space=pltpu.SMEM),
                  pl.BlockSpec((TM, d), row),
                  pl.BlockSpec((TM, LANES), row),
                  pl.BlockSpec((1, 1, d), lambda i: ((i // tiles_per_seq) * 6 + 5, 0, 0)),
                  pl.BlockSpec((1, d), lambda i: (0, 0)),
                  pl.BlockSpec(memory_space=pl.ANY)],
        out_specs=pl.BlockSpec((TM, d), row),
        out_shape=jax.ShapeDtypeStruct((n, d), jnp.float32),
        scratch_shapes=[pltpu.VMEM((2,) + _tiles_shape(2 * TM, d), jnp.float32),
                        pltpu.SemaphoreType.DMA((2,))],
        compiler_params=_cparams(("arbitrary",)),
        name="final",
    )(dest, dest, x1, route, mod3, g_final, ys)


def _block_layout(counts, n):
    cnt = counts[0, :N_EXPERTS].astype(jnp.int32)
    blocks = (cnt + BM - 1) // BM
    bends = jnp.cumsum(blocks)
    pstarts = (bends - blocks) * BM
    n_blk = (2 * n) // BM + N_EXPERTS
    pst_row = jnp.zeros((1, LANES), jnp.float32).at[0, :N_EXPERTS].set(pstarts.astype(jnp.float32))
    zrow = ((pstarts + cnt) // SUBLANES).astype(jnp.int32)
    n_used = bends[-1:].astype(jnp.int32)
    blk_e = jnp.sum(bends[None, :] <= jnp.arange(n_blk, dtype=jnp.int32)[:, None], axis=1)
    blk_e = jnp.minimum(blk_e, N_EXPERTS - 1).astype(jnp.int32)
    return pst_row, zrow, n_used, blk_e


def _layer(x2, c, pos2, w_ada, b_ada, g_mix, w_in, w_o, g_ffn, w_rg, w_re, w_up, w_gate, w_down, seq):
    n, d = x2.shape
    bsz = n // seq
    bf = jnp.bfloat16
    d_a = d // 2
    d_i = N_IDX_HEADS * IDX_DIM
    mod3 = _ada(c, w_ada, b_ada).reshape(bsz * 6, 1, d)

    c0 = 3 * d_a + d_i
    c1 = c0 + IDX_DIM + N_IDX_HEADS
    w_cat = jnp.concatenate([w_in[:, :c0], w_in[:, c0:c1],
                             jnp.zeros((d, LANES - (c1 - c0)), w_in.dtype), w_in[:, c1:]], axis=1).astype(bf)
    qat, ka, vat, qit, kiw, wit, qb, kb, vb, gb = _proj(x2, pos2, mod3, g_mix.reshape(1, d), w_cat, seq)

    n_kt = seq // TM
    ya = _dsa(qat, qit, wit, kiw.reshape(bsz, n_kt, TM, LANES), ka.reshape(bsz, n_kt, TM, d_a), vat, seq)
    yb = _ret(qb, kb, vb, gb, seq)

    w_route = jnp.concatenate([jnp.transpose(w_re, (1, 0, 2)).reshape(d, N_EXPERTS), w_rg,
                               jnp.zeros((d, LANES - N_EXPERTS - N_GROUPS), w_rg.dtype)], axis=1).astype(bf)
    x1, h2, route, counts = _mix_out(x2, ya, yb, w_o[:d_a].astype(bf), w_o[d_a:].astype(bf), mod3,
                                     g_ffn.reshape(1, d), w_route, seq)

    pst_row, zrow, n_used, blk_e = _block_layout(counts, n)
    dest = _plan(route, pst_row)
    xs = _dispatch(h2, dest, zrow, n_used, blk_e.shape[0])
    ys = _experts(xs, w_gate, w_up, w_down, blk_e, n_used)
    return x1, ys, dest, route, mod3


def kernel(x, c, positions, w_ada, b_ada, g_norm_mix, w_in, w_o, g_norm_ffn, w_router_group, w_router_expert,
           w_up, w_gate, w_down, g_norm_final):
    bsz, seq, d = x.shape
    depth = w_ada.shape[0]
    assert depth == 1, "the final norm is fused into the last layer's combine kernel"
    assert seq % (2 * TM) == 0 and seq % (RET_C * RET_CHUNKS) == 0 and seq % MIX_TM == 0 and (2 * bsz * seq) % BM == 0
    assert (bsz * seq) % PLAN_TM == 0 and PLAN_TM % TM == 0
    x2 = x.reshape(bsz * seq, d)
    pos2 = positions.astype(jnp.float32).reshape(bsz * seq, 1)
    x1, ys, dest, route, mod3 = _layer(x2, c, pos2, w_ada[0], b_ada[0], g_norm_mix[0], w_in[0], w_o[0],
                                       g_norm_ffn[0], w_router_group[0], w_router_expert[0], w_up[0], w_gate[0],
                                       w_down[0], seq)
    out = _final(x1, ys, dest, route, mod3, g_norm_final.reshape(1, d), seq)
    return out.reshape(bsz, seq, d)
```

```python
import functools

import jax
import jax.numpy as jnp
import numpy as np
from jax import lax
from jax.experimental import pallas as pl
from jax.experimental.pallas import tpu as pltpu

CHUNK = 64
HEAD_DIM = 64
N_IDX_HEADS = 16
IDX_DIM = 64
TOPK_MAX = 256
ROPE_THETA = 500000.0
ROT_DIM = HEAD_DIM // 4
RET_THETA = 10000.0
N_GROUPS = 4
EXPERTS_PER_GROUP = 8
N_EXPERTS = N_GROUPS * EXPERTS_PER_GROUP
EPS = 1e-6

LANES = 128
SUBLANES = 8
VMEM_LIMIT = 56 * 1024 * 1024

TM = 256
QB = TM
RET_C = 256
RET_CHUNKS = 2
BM = 256
PLAN_TM = 1024
MIX_TM = 512
CNT_ROWS = 64
ONES_ROWS = 16
HEAD_GROUP = 4
IDX_TILES = 8
ATT_TILES = 8
BISECT_PER_CHECK = 2
MAX_BISECT = 40

NEG_BIG = -1e30
LOG2E = 1.4426950408889634


def _cparams(sem):
    return pltpu.CompilerParams(dimension_semantics=sem, vmem_limit_bytes=VMEM_LIMIT)


def _ada_kernel(c_ref, w_ref, b_ref, o_ref):
    o_ref[...] = jnp.dot(c_ref[...], w_ref[...], preferred_element_type=jnp.float32) + b_ref[...]


def _ada(c, w_ada, b_ada):
    bsz, d = c.shape
    n_out = w_ada.shape[1]
    return pl.pallas_call(
        _ada_kernel,
        grid=(n_out // d,),
        in_specs=[pl.BlockSpec((bsz, d), lambda j: (0, 0)),
                  pl.BlockSpec((d, d), lambda j: (0, j)),
                  pl.BlockSpec((1, d), lambda j: (0, j))],
        out_specs=pl.BlockSpec((bsz, d), lambda j: (0, j)),
        out_shape=jax.ShapeDtypeStruct((bsz, n_out), jnp.float32),
        compiler_params=_cparams(("arbitrary",)),
        name="ada",
    )(c, w_ada, b_ada.reshape(1, n_out))


def _rmsnorm_mod(x, g, sc, sh):
    xn = x * lax.rsqrt(jnp.mean(x * x, axis=-1, keepdims=True) + EPS)
    return xn * g * (1.0 + sc) + sh


def _rope_lanes(x, cos, sin_lo, sin_hi, half):
    cols = []
    for k in range(x.shape[1] // LANES):
        xb = x[:, k * LANES:(k + 1) * LANES]
        cols.append(xb * cos + pltpu.roll(xb, LANES - half, 1) * sin_lo + pltpu.roll(xb, half, 1) * sin_hi)
    return cols[0] if len(cols) == 1 else jnp.concatenate(cols, axis=1)


def _proj_kernel(x_ref, pos_ref, sc_ref, sh_ref, g_ref, w_ref, tab_ref,
                 qat_ref, ka_ref, vat_ref, qit_ref, kiw_ref, wit_ref, qb_ref, kb_ref, vb_ref, gb_ref,
                 *, d_a, d_i, d_b):
    h = _rmsnorm_mod(x_ref[...], g_ref[...], sc_ref[0], sh_ref[0]).astype(jnp.bfloat16)
    pos = pos_ref[...]
    ang_a = pos * tab_ref[0:1, :]
    cos_a, sin_a = jnp.cos(ang_a), jnp.sin(ang_a)
    sa_lo, sa_hi = sin_a * tab_ref[1:2, :], sin_a * tab_ref[2:3, :]
    ang_b = pos * tab_ref[3:4, :]
    cos_b, sin_b = jnp.cos(ang_b), jnp.sin(ang_b)
    sb_lo, sb_hi = sin_b * tab_ref[4:5, :], sin_b * tab_ref[5:6, :]
    half_a, half_b = ROT_DIM // 2, HEAD_DIM // 2

    def seg(lo, width):
        return jnp.dot(h, w_ref[:, lo:lo + width], preferred_element_type=jnp.float32)

    o = 0
    qa = seg(o, d_a); o += d_a
    qat_ref[0, 0] = (_rope_lanes(qa, cos_a, sa_lo, sa_hi, half_a) * (HEAD_DIM ** -0.5 * LOG2E)).T.astype(qat_ref.dtype)
    ka = seg(o, d_a); o += d_a
    ka_ref[...] = _rope_lanes(ka, cos_a, sa_lo, sa_hi, half_a).astype(ka_ref.dtype)
    vat_ref[0, 0] = seg(o, d_a).T.astype(vat_ref.dtype); o += d_a
    qi = seg(o, d_i); o += d_i
    qit_ref[0, 0] = (_rope_lanes(qi, cos_a, sa_lo, sa_hi, half_a) * (IDX_DIM ** -0.5)).T.astype(qit_ref.dtype)
    kw = seg(o, LANES); o += LANES
    kiw_ref[...] = _rope_lanes(kw, cos_a, sa_lo, sa_hi, half_a).astype(kiw_ref.dtype)
    wit_ref[0, 0] = kw.T[IDX_DIM:IDX_DIM + N_IDX_HEADS, :] * (N_IDX_HEADS ** -0.5)
    qb = seg(o, d_b); o += d_b
    qb_ref[...] = _rope_lanes(qb, cos_b, sb_lo, sb_hi, half_b).astype(qb_ref.dtype)
    kb = seg(o, d_b); o += d_b
    kb_ref[...] = (_rope_lanes(kb, cos_b, sb_lo, sb_hi, half_b) * (HEAD_DIM ** -0.5)).astype(kb_ref.dtype)
    vb_ref[...] = seg(o, d_b).astype(vb_ref.dtype); o += d_b
    gb_ref[...] = seg(o, d_b)


def _rope_tables():
    lane = jnp.arange(LANES) % HEAD_DIM
    rows = []
    for rot, theta in ((ROT_DIM, ROPE_THETA), (HEAD_DIM, RET_THETA)):
        half = rot // 2
        inv_freq = theta ** (-jnp.arange(half, dtype=jnp.float32) / half)
        rows.append(jnp.where(lane < rot, inv_freq[lane % half], 0.0))
        rows.append(jnp.where(lane < half, -1.0, 0.0))
        rows.append(jnp.where((lane >= half) & (lane < rot), 1.0, 0.0))
    rows += [jnp.zeros((LANES,), jnp.float32)] * 2
    return jnp.stack(rows).astype(jnp.float32)


def _proj(x2, pos2, mod3, g_mix, w_cat, seq):
    n, d = x2.shape
    bsz = n // seq
    d_a = d // 2
    d_b = d // 2
    d_i = N_IDX_HEADS * IDX_DIM
    tiles_per_seq = seq // TM
    tab = _rope_tables()
    row = lambda i: (i, 0)
    const = lambda i: (0, 0)
    tile4 = lambda i: (i // tiles_per_seq, i % tiles_per_seq, 0, 0)
    bf = jnp.bfloat16
    out_shape = (
        jax.ShapeDtypeStruct((bsz, tiles_per_seq, d_a, TM), bf),
        jax.ShapeDtypeStruct((n, d_a), bf),
        jax.ShapeDtypeStruct((bsz, tiles_per_seq, d_a, TM), bf),
        jax.ShapeDtypeStruct((bsz, tiles_per_seq, d_i, TM), bf),
        jax.ShapeDtypeStruct((n, LANES), bf),
        jax.ShapeDtypeStruct((bsz, tiles_per_seq, N_IDX_HEADS, TM), jnp.float32),
        jax.ShapeDtypeStruct((n, d_b), bf),
        jax.ShapeDtypeStruct((n, d_b), bf),
        jax.ShapeDtypeStruct((n, d_b), bf),
        jax.ShapeDtypeStruct((n, d_b), jnp.float32),
    )
    out_specs = (
        pl.BlockSpec((1, 1, d_a, TM), tile4),
        pl.BlockSpec((TM, d_a), row),
        pl.BlockSpec((1, 1, d_a, TM), tile4),
        pl.BlockSpec((1, 1, d_i, TM), tile4),
        pl.BlockSpec((TM, LANES), row),
        pl.BlockSpec((1, 1, N_IDX_HEADS, TM), tile4),
        pl.BlockSpec((TM, d_b), row),
        pl.BlockSpec((TM, d_b), row),
        pl.BlockSpec((TM, d_b), row),
        pl.BlockSpec((TM, d_b), row),
    )
    return pl.pallas_call(
        functools.partial(_proj_kernel, d_a=d_a, d_i=d_i, d_b=d_b),
        grid=(n // TM,),
        in_specs=[pl.BlockSpec((TM, d), row),
                  pl.BlockSpec((TM, 1), row),
                  pl.BlockSpec((1, 1, d), lambda i: ((i // tiles_per_seq) * 6 + 1, 0, 0)),
                  pl.BlockSpec((1, 1, d), lambda i: ((i // tiles_per_seq) * 6 + 0, 0, 0)),
                  pl.BlockSpec((1, d), const),
                  pl.BlockSpec(w_cat.shape, const),
                  pl.BlockSpec(tab.shape, const)],
        out_specs=out_specs,
        out_shape=out_shape,
        compiler_params=_cparams(("arbitrary",)),
        name="proj",
    )(x2, pos2, mod3, mod3, g_mix, w_cat, tab)


def _row_blocks(x, rows):
    return [x[r * rows:(r + 1) * rows] for r in range(x.shape[0] // rows)]


def _dsa_kernel(qat_ref, qit_ref, wit_ref, kiw_ref, ka_ref, vat_ref, tri_ref, o_ref,
                qix_ref, qmx_ref, sc_ref, m_ref, l_ref, acc_ref, sa_ref, sb_ref, mxa_ref, mxb_ref, lohi_ref,
                *, k_top, n_heads):
    i = pl.program_id(1)
    n_grp = (i * QB + QB + 2 * TM - 1) // (2 * TM)
    n_kt = 2 * n_grp
    n_real = (i * QB + QB + TM - 1) // TM

    zero_rows = jnp.zeros((LANES - IDX_DIM, QB), qix_ref.dtype)
    for h in range(N_IDX_HEADS):
        qix_ref[h] = jnp.concatenate([qit_ref[0, 0, h * IDX_DIM:(h + 1) * IDX_DIM, :], zero_rows], axis=0)
    row_q = lax.broadcasted_iota(jnp.int32, (LANES, QB), 0)
    for h in range(n_heads):
        pair = qat_ref[0, 0, (h // 2) * LANES:(h // 2 + 1) * LANES, :]
        own = (row_q < HEAD_DIM) if h % 2 == 0 else (row_q >= HEAD_DIM)
        qmx_ref[h] = jnp.where(own, pair, jnp.zeros_like(pair))

    q_chunk = (i * QB + lax.broadcasted_iota(jnp.int32, (1, QB), 1)) // CHUNK
    key_chunk_in_tile = lax.broadcasted_iota(jnp.int32, (TM, QB), 0) // CHUNK
    w_all = wit_ref[0, 0]

    def idx_tiles(tiles):
        lo, hi = lohi_ref[0], lohi_ref[1]
        for j in tiles:
            kt = kiw_ref[0, j]
            acc = None
            for h in range(N_IDX_HEADS):
                d = jnp.dot(kt, qix_ref[h], preferred_element_type=jnp.float32)
                t = w_all[h:h + 1, :] * jnp.maximum(d, 0.0)
                acc = t if acc is None else acc + t
            adm = key_chunk_in_tile <= q_chunk - j * (TM // CHUNK)
            s = jnp.where(adm, acc, -jnp.inf)
            sc_ref[j] = s
            lo = jnp.minimum(lo, functools.reduce(jnp.minimum, _row_blocks(jnp.where(adm, acc, jnp.inf), SUBLANES)))
            hi = jnp.maximum(hi, functools.reduce(jnp.maximum, _row_blocks(s, SUBLANES)))
        lohi_ref[0], lohi_ref[1] = lo, hi

    lohi_ref[0] = jnp.full((SUBLANES, QB), jnp.inf, jnp.float32)
    lohi_ref[1] = jnp.full((SUBLANES, QB), -jnp.inf, jnp.float32)

    def idx_step(g, carry):
        idx_tiles([IDX_TILES * g + u for u in range(IDX_TILES)])
        return carry

    lax.fori_loop(0, n_kt // IDX_TILES, idx_step, 0)

    def idx_tail(p, carry):
        first = n_kt // IDX_TILES * IDX_TILES + 2 * p
        idx_tiles([first, first + 1])
        return carry

    lax.fori_loop(0, n_kt % IDX_TILES // 2, idx_tail, 0)

    lo = jnp.min(lohi_ref[0], axis=0, keepdims=True)
    hi = jnp.max(lohi_ref[1], axis=0, keepdims=True)
    n_adm = ((i * QB + lax.broadcasted_iota(jnp.int32, (1, QB), 1)) // CHUNK + 1) * CHUNK

    def bisect_cond(carry):
        it, _, _, _, unsettled = carry
        return (it < MAX_BISECT) & (unsettled > 0.0)

    def bisect(carry):
        it, lo, hi, cnt_lo, _ = carry
        for _ in range(BISECT_PER_CHECK):
            mid = lo + (hi - lo) * 0.5
            mid_b = jnp.broadcast_to(mid, (CNT_ROWS, QB))

            def count_tile(j, cnt, mid_b=mid_b):
                for blk in _row_blocks(sc_ref[j], CNT_ROWS):
                    cnt = cnt + jnp.where(blk >= mid_b, 1.0, 0.0)
                return cnt

            cnt = lax.fori_loop(0, n_real, count_tile, jnp.zeros((CNT_ROWS, QB), jnp.float32))
            c = jnp.sum(cnt, axis=0, keepdims=True)
            ge = c >= k_top
            lo = jnp.where(ge, mid, lo)
            hi = jnp.where(ge, hi, mid)
            cnt_lo = jnp.where(ge, c, cnt_lo)
        return it + BISECT_PER_CHECK, lo, hi, cnt_lo, jnp.max(jnp.where(cnt_lo > k_top, 1.0, 0.0))

    cnt0 = n_adm.astype(jnp.float32)
    _, thr, thr_hi, _, unsettled = lax.while_loop(bisect_cond, bisect,
                                                  (0, lo, hi, cnt0, jnp.max(jnp.where(cnt0 > k_top, 1.0, 0.0))))

    @pl.when(unsettled <= 0.0)
    def _():
        def bias_tile(j, carry):
            sc_ref[j] = jnp.where(sc_ref[j] >= thr, 0.0, NEG_BIG)
            return carry

        lax.fori_loop(0, n_kt, bias_tile, 0)

    @pl.when(unsettled > 0.0)
    def _():
        def count_hi(j, cnt):
            return cnt + functools.reduce(jnp.add, _row_blocks(jnp.where(sc_ref[j] >= thr_hi, 1.0, 0.0), SUBLANES))

        above = jnp.sum(lax.fori_loop(0, n_kt, count_hi, jnp.zeros((SUBLANES, QB), jnp.float32)),
                        axis=0, keepdims=True)
        top = jnp.where(above < k_top, thr_hi, jnp.inf)
        room = k_top - jnp.where(above < k_top, above, 0.0)

        def bias_tile(j, taken):
            s = sc_ref[j]
            tied = (s >= thr) & (s < top)
            before = jnp.dot(tri_ref[...], jnp.where(tied, 1.0, 0.0).astype(tri_ref.dtype),
                             preferred_element_type=jnp.float32)
            keep = (s >= top) | (tied & (taken + before <= room))
            sc_ref[j] = jnp.where(keep, 0.0, NEG_BIG)
            return taken + before[TM - 1:TM, :]

        lax.fori_loop(0, n_kt, bias_tile, jnp.zeros((1, QB), jnp.float32))

    m_ref[...] = jnp.full(m_ref.shape, NEG_BIG, jnp.float32)
    l_ref[...] = jnp.zeros(l_ref.shape, jnp.float32)
    acc_ref[...] = jnp.zeros(acc_ref.shape, jnp.float32)

    def pair(h):
        return slice((h // 2) * LANES, (h // 2 + 1) * LANES)

    def logits_into(s_ref, mx_ref, j, heads):
        bias = sc_ref[j]
        for h in heads:
            s = jnp.dot(ka_ref[0, j, :, pair(h)], qmx_ref[h], preferred_element_type=jnp.float32) + bias
            s_ref[h] = s
            mx_ref[h] = jnp.max(s, axis=0, keepdims=True)

    ones_rows = jnp.ones((ONES_ROWS, TM), vat_ref.dtype)

    def absorb(s_ref, mx_ref, j, heads):
        for h in heads:
            m_old = m_ref[h]
            m_new = jnp.maximum(m_old, mx_ref[h])
            alpha = jnp.exp2(m_old - m_new)
            p = jnp.exp2(s_ref[h] - m_new).astype(vat_ref.dtype)
            pv = jnp.dot(jnp.concatenate([vat_ref[0, j, pair(h), :], ones_rows], axis=0), p,
                         preferred_element_type=jnp.float32)
            acc_ref[h] = acc_ref[h] * alpha + pv[0:LANES]
            l_ref[h] = l_ref[h] * alpha + pv[LANES:LANES + SUBLANES]
            m_ref[h] = m_new

    for h0 in range(0, n_heads, HEAD_GROUP):
        heads = range(h0, h0 + HEAD_GROUP)
        logits_into(sa_ref, mxa_ref, 0, heads)

        def tile_pair(t, heads=heads):
            logits_into(sb_ref, mxb_ref, t + 1, heads)
            absorb(sa_ref, mxa_ref, t, heads)
            logits_into(sa_ref, mxa_ref, jnp.minimum(t + 2, n_kt - 1), heads)
            absorb(sb_ref, mxb_ref, t + 1, heads)

        def attn_step(g, carry):
            for u in range(0, ATT_TILES, 2):
                tile_pair(ATT_TILES * g + u)
            return carry

        lax.fori_loop(0, n_kt // ATT_TILES, attn_step, 0)

        def attn_tail(p, carry):
            tile_pair(n_kt // ATT_TILES * ATT_TILES + 2 * p)
            return carry

        lax.fori_loop(0, n_kt % ATT_TILES // 2, attn_tail, 0)

    for hp in range(n_heads // 2):
        even = acc_ref[2 * hp] / l_ref[2 * hp, 0:1, :]
        odd = acc_ref[2 * hp + 1] / l_ref[2 * hp + 1, 0:1, :]
        o_ref[:, hp * LANES:(hp + 1) * LANES] = jnp.where(row_q < HEAD_DIM, even, odd).T.astype(o_ref.dtype)


def _dsa(qat, qit, wit, kiw4, ka4, vat, seq):
    bsz, n_kt, d_a, _ = qat.shape
    n_heads = d_a // HEAD_DIM
    n_qb = seq // QB
    k_top = min(TOPK_MAX, seq // 4)
    qtile = lambda b, i: (b, i, 0, 0)
    per_b = lambda b, i: (b, 0, 0, 0)
    f32 = jnp.float32
    return pl.pallas_call(
        functools.partial(_dsa_kernel, k_top=float(k_top), n_heads=n_heads),
        grid=(bsz, n_qb),
        in_specs=[pl.BlockSpec((1, 1, d_a, QB), qtile),
                  pl.BlockSpec((1, 1, qit.shape[2], QB), qtile),
                  pl.BlockSpec((1, 1, N_IDX_HEADS, QB), qtile),
                  pl.BlockSpec((1, n_kt, TM, LANES), per_b, pipeline_mode=pl.Buffered(1)),
                  pl.BlockSpec((1, n_kt, TM, d_a), per_b, pipeline_mode=pl.Buffered(1)),
                  pl.BlockSpec((1, n_kt, d_a, TM), per_b, pipeline_mode=pl.Buffered(1)),
                  pl.BlockSpec((TM, TM), lambda b, i: (0, 0), pipeline_mode=pl.Buffered(1))],
        out_specs=pl.BlockSpec((QB, d_a), lambda b, i: (b * n_qb + i, 0)),
        out_shape=jax.ShapeDtypeStruct((bsz * seq, d_a), jnp.bfloat16),
        scratch_shapes=[pltpu.VMEM((N_IDX_HEADS, LANES, QB), jnp.bfloat16),
                        pltpu.VMEM((n_heads, LANES, QB), jnp.bfloat16),
                        pltpu.VMEM((n_kt, TM, QB), f32),
                        pltpu.VMEM((n_heads, 1, QB), f32),
                        pltpu.VMEM((n_heads, SUBLANES, QB), f32),
                        pltpu.VMEM((n_heads, LANES, QB), f32),
                        pltpu.VMEM((n_heads, TM, QB), f32),
                        pltpu.VMEM((n_heads, TM, QB), f32),
                        pltpu.VMEM((n_heads, 1, QB), f32),
                        pltpu.VMEM((n_heads, 1, QB), f32),
                        pltpu.VMEM((2, SUBLANES, QB), f32)],
        compiler_params=_cparams(("arbitrary", "arbitrary")),
        name="dsa",
    )(qat, qit, wit, kiw4, ka4, vat, jnp.asarray(np.tril(np.ones((TM, TM), np.float32)), jnp.bfloat16))


def _group_mean(y, avg):
    hi = y.astype(jnp.bfloat16)
    lo = (y - hi.astype(jnp.float32)).astype(jnp.bfloat16)
    return (jnp.dot(hi, avg, preferred_element_type=jnp.float32)
            + jnp.dot(lo, avg, preferred_element_type=jnp.float32))


def _ret_kernel(q_ref, k_ref, v_ref, g_ref, dec_ref, zt_ref, xi_ref, gc_ref, blk_ref, avg_ref, o_ref, st_ref,
                *, n_heads):
    @pl.when(pl.program_id(1) == 0)
    def _():
        st_ref[...] = jnp.zeros_like(st_ref)

    even = lax.broadcasted_iota(jnp.int32, (RET_C, LANES), 1) < HEAD_DIM
    avg = avg_ref[...]
    for p in range(n_heads // 2):
        sl = slice(p * LANES, (p + 1) * LANES)
        state = st_ref[p]
        for c in range(q_ref.shape[0] // RET_C):
            rows = slice(c * RET_C, (c + 1) * RET_C)
            qp, kp, vp = q_ref[rows, sl], k_ref[rows, sl], v_ref[rows, sl]
            kpt = kp.astype(jnp.float32).T
            kpt_b = kpt.astype(kp.dtype)
            inner = None
            for e in range(2):
                q_e = jnp.where(even if e == 0 else jnp.logical_not(even), qp, jnp.zeros_like(qp))
                s = jnp.dot(q_e, kpt_b, preferred_element_type=jnp.float32) * dec_ref[2 * p + e]
                t = jnp.dot(s.astype(vp.dtype), vp, preferred_element_type=jnp.float32)
                inner = t if e == 0 else jnp.where(even, inner, t)
            cross = jnp.dot(qp, state.astype(qp.dtype), preferred_element_type=jnp.float32) * xi_ref[p]
            y = inner + cross
            yc = y - _group_mean(y, avg)
            yn = yc * lax.rsqrt(_group_mean(yc * yc, avg) + EPS)
            g = g_ref[rows, sl]
            o_ref[rows, sl] = (g / (1.0 + jnp.exp(-g)) * yn).astype(o_ref.dtype)
            kz = (kpt * zt_ref[p]).astype(kp.dtype)
            kv = jnp.dot(kz, vp, preferred_element_type=jnp.float32)
            state = state * gc_ref[p] + kv * blk_ref[...]
        st_ref[p] = state


def _ret_consts(n_heads):
    log_gamma = jnp.log1p(-jnp.exp2(-5.0 - jnp.arange(n_heads, dtype=jnp.float32)))
    pos = jnp.arange(RET_C, dtype=jnp.float32)
    diff = pos[:, None] - pos[None, :]
    dec = jnp.where(diff[None] >= 0, jnp.exp(jnp.maximum(diff, 0.0)[None] * log_gamma[:, None, None]), 0.0)
    zeta = jnp.exp((RET_C - 1.0 - pos)[None, :] * log_gamma[:, None])
    xi = jnp.exp((pos + 1.0)[None, :] * log_gamma[:, None])
    gc = jnp.exp(RET_C * log_gamma)
    n_pairs = n_heads // 2
    lanes = lambda a: jnp.repeat(a.reshape(n_pairs, 2, -1), HEAD_DIM, axis=1)
    zt = lanes(zeta)
    xi_p = jnp.swapaxes(lanes(xi), 1, 2)
    gc_p = jnp.broadcast_to(lanes(gc[:, None]), (n_pairs, LANES, LANES))
    head_of = jnp.arange(LANES) // HEAD_DIM
    blk = (head_of[:, None] == head_of[None, :]).astype(jnp.float32)
    avg = (blk / HEAD_DIM).astype(jnp.bfloat16)
    f32 = lambda a: a.astype(jnp.float32)
    return dec, f32(zt), f32(xi_p), f32(gc_p), blk, avg


def _ret(qb, kb, vb, gb, seq):
    n, d_b = qb.shape
    bsz = n // seq
    n_heads = d_b // HEAD_DIM
    step = RET_C * RET_CHUNKS
    n_c = seq // step
    consts = _ret_consts(n_heads)
    row = lambda b, c: (b * n_c + c, 0)
    const_spec = lambda a: pl.BlockSpec(a.shape, lambda b, c: (0,) * a.ndim)
    return pl.pallas_call(
        functools.partial(_ret_kernel, n_heads=n_heads),
        grid=(bsz, n_c),
        in_specs=[pl.BlockSpec((step, d_b), row)] * 4 + [const_spec(a) for a in consts],
        out_specs=pl.BlockSpec((step, d_b), row),
        out_shape=jax.ShapeDtypeStruct((n, d_b), jnp.bfloat16),
        scratch_shapes=[pltpu.VMEM((n_heads // 2, LANES, LANES), jnp.float32)],
        compiler_params=_cparams(("arbitrary", "arbitrary")),
        name="ret",
    )(qb, kb, vb, gb, *consts)


def _tiles_shape(rows, d):
    return (rows // SUBLANES, d // LANES, SUBLANES, LANES)


def _to_tiles(ref, x):
    for s in range(ref.shape[1]):
        ref[:, s] = x[:, s * LANES:(s + 1) * LANES].reshape(ref.shape[0], SUBLANES, LANES)


def _from_tiles(ref):
    rows = ref.shape[0] * SUBLANES
    return jnp.concatenate([ref[:, s].reshape(rows, LANES) for s in range(ref.shape[1])], axis=1)


def _lane_first_eq(x, m, lane):
    return jnp.min(jnp.where(x == m, lane, float(LANES)), axis=1, keepdims=True)


def _mix_out_kernel(x_ref, ya_ref, yb_ref, woa_ref, wob_ref, gt_ref, sc_ref, sh_ref, g_ref, wr_ref, tri_ref,
                    x1_ref, h2_ref, route_ref, cnt_ref, carry_ref):
    @pl.when(pl.program_id(0) == 0)
    def _():
        carry_ref[...] = jnp.zeros_like(carry_ref)

    mix = (jnp.dot(ya_ref[...], woa_ref[...], preferred_element_type=jnp.float32)
           + jnp.dot(yb_ref[...], wob_ref[...], preferred_element_type=jnp.float32))
    x1 = x_ref[...] + gt_ref[0] * mix
    x1_ref[...] = x1
    h2 = _rmsnorm_mod(x1, g_ref[...], sc_ref[0], sh_ref[0])
    _to_tiles(h2_ref, h2)

    lg = jnp.dot(h2.astype(jnp.bfloat16), wr_ref[...], preferred_element_type=jnp.float32)
    lane = lax.broadcasted_iota(jnp.int32, lg.shape, 1).astype(jnp.float32)
    is_grp = (lane >= N_EXPERTS) & (lane < N_EXPERTS + N_GROUPS)
    gl = jnp.where(is_grp, lg, -jnp.inf)
    gmax = jnp.max(gl, axis=1, keepdims=True)
    grp = _lane_first_eq(gl, gmax, lane) - N_EXPERTS
    p_grp = 1.0 / jnp.sum(jnp.exp(gl - gmax), axis=1, keepdims=True)
    in_grp = jnp.floor(lane * (1.0 / EXPERTS_PER_GROUP)) == grp
    f = jnp.where(in_grp & (lane < N_EXPERTS), lg, -jnp.inf)
    f1 = jnp.max(f, axis=1, keepdims=True)
    e1 = _lane_first_eq(f, f1, lane)
    f = jnp.where(lane == e1, -jnp.inf, f)
    f2 = jnp.max(f, axis=1, keepdims=True)
    e2 = _lane_first_eq(f, f2, lane)
    a2 = jnp.exp(f2 - f1)
    w1 = p_grp / (1.0 + a2)
    w2 = p_grp * a2 / (1.0 + a2)

    oh1 = jnp.where(lane == e1, 1.0, 0.0)
    oh2 = jnp.where(lane == e2, 1.0, 0.0)
    both = oh1 + oh2
    before = jnp.dot(tri_ref[...], both.astype(jnp.bfloat16), preferred_element_type=jnp.float32) + carry_ref[...]
    r1 = jnp.sum(before * oh1, axis=1, keepdims=True)
    r2 = jnp.sum(before * oh2, axis=1, keepdims=True)
    carry = carry_ref[...] + jnp.sum(both, axis=0, keepdims=True)
    carry_ref[...] = carry
    cnt_ref[...] = carry

    out = jnp.zeros(lg.shape, jnp.float32)
    for col, val in enumerate((e1, e2, w1, w2, r1, r2)):
        out = jnp.where(lane == col, val, out)
    route_ref[...] = out


def _mix_out(x2, ya, yb, wo_a, wo_b, mod3, g_ffn, w_route, seq):
    n, d = x2.shape
    tm = MIX_TM
    tiles_per_seq = seq // tm
    tri = jnp.asarray(np.tril(np.ones((tm, tm), np.float32), -1), jnp.bfloat16)
    row = lambda i: (i, 0)
    const = lambda i: (0, 0)
    modk = lambda k: pl.BlockSpec((1, 1, d), lambda i: ((i // tiles_per_seq) * 6 + k, 0, 0))
    return pl.pallas_call(
        _mix_out_kernel,
        grid=(n // tm,),
        in_specs=[pl.BlockSpec((tm, d), row),
                  pl.BlockSpec((tm, ya.shape[1]), row),
                  pl.BlockSpec((tm, yb.shape[1]), row),
                  pl.BlockSpec(wo_a.shape, const),
                  pl.BlockSpec(wo_b.shape, const),
                  modk(2), modk(4), modk(3),
                  pl.BlockSpec((1, d), const),
                  pl.BlockSpec(w_route.shape, const),
                  pl.BlockSpec(tri.shape, const)],
        out_specs=(pl.BlockSpec((tm, d), row), pl.BlockSpec(_tiles_shape(tm, d), lambda i: (i, 0, 0, 0)),
                   pl.BlockSpec((tm, LANES), row), pl.BlockSpec((1, LANES), const)),
        out_shape=(jax.ShapeDtypeStruct((n, d), jnp.float32), jax.ShapeDtypeStruct(_tiles_shape(n, d), jnp.float32),
                   jax.ShapeDtypeStruct((n, LANES), jnp.float32), jax.ShapeDtypeStruct((1, LANES), jnp.float32)),
        scratch_shapes=[pltpu.VMEM((1, LANES), jnp.float32)],
        compiler_params=_cparams(("arbitrary",)),
        name="mix_out",
    )(x2, ya, yb, wo_a, wo_b, mod3, mod3, mod3, g_ffn, w_route, tri)


def _plan_kernel(route_ref, pst_ref, dest_ref):
    r = route_ref[...]
    lane = lax.broadcasted_iota(jnp.int32, r.shape, 1).astype(jnp.float32)
    pst = pst_ref[...]
    d1 = jnp.sum(jnp.where(lane == r[:, 0:1], pst, 0.0), axis=1, keepdims=True) + r[:, 4:5]
    d2 = jnp.sum(jnp.where(lane == r[:, 1:2], pst, 0.0), axis=1, keepdims=True) + r[:, 5:6]
    t1 = jnp.floor(d1 * (1.0 / SUBLANES))
    t2 = jnp.floor(d2 * (1.0 / SUBLANES))
    packed = jnp.zeros(r.shape, jnp.float32)
    for k, v in enumerate((t1, d1 - t1 * SUBLANES, t2, d2 - t2 * SUBLANES)):
        packed = jnp.where(lane == float(k), v, packed)
    dest_ref[0] = packed.T[0:8, :].astype(jnp.int32)


def _dest_block(i):
    return (i // (PLAN_TM // TM), 0, i % (PLAN_TM // TM))


def _plan(route, pst_row):
    n = route.shape[0]
    return pl.pallas_call(
        _plan_kernel,
        grid=(n // PLAN_TM,),
        in_specs=[pl.BlockSpec((PLAN_TM, LANES), lambda i: (i, 0)), pl.BlockSpec((1, LANES), lambda i: (0, 0))],
        out_specs=pl.BlockSpec((1, 8, PLAN_TM), lambda i: (i, 0, 0)),
        out_shape=jax.ShapeDtypeStruct((n // PLAN_TM, 8, PLAN_TM), jnp.int32),
        compiler_params=_cparams(("arbitrary",)),
        name="plan",
    )(route, pst_row)


def _dispatch_kernel(zrow_ref, n_used_ref, dest_ref, h2_ref, xs_hbm, zbuf, sem, zsem, *, n_blk):
    i = pl.program_id(0)

    @pl.when(i == 0)
    def _():
        zbuf[...] = jnp.zeros(zbuf.shape, zbuf.dtype)
        blk_tiles = BM // SUBLANES
        for e in range(N_EXPERTS):
            pltpu.make_async_copy(zbuf, xs_hbm.at[pl.ds(zrow_ref[e], blk_tiles)], zsem).start()
        for e in range(N_EXPERTS):
            pltpu.make_async_copy(zbuf, xs_hbm.at[pl.ds(0, blk_tiles)], zsem).wait()
        for b in range(N_EXPERTS + 1):
            @pl.when(n_used_ref[0] + b <= n_blk)
            def _():
                tail = pltpu.make_async_copy(zbuf, xs_hbm.at[pl.ds((n_used_ref[0] + b) * blk_tiles, blk_tiles)], zsem)
                tail.start()
                tail.wait()

    for r in range(TM):
        k, u = divmod(r, SUBLANES)
        row = h2_ref.at[k, :, u, :]
        pltpu.make_async_copy(row, xs_hbm.at[dest_ref[0, 0, r], :, dest_ref[0, 1, r], :], sem).start()
        pltpu.make_async_copy(row, xs_hbm.at[dest_ref[0, 2, r], :, dest_ref[0, 3, r], :], sem).start()
    for _ in range(2):
        pltpu.make_async_copy(h2_ref, xs_hbm.at[pl.ds(0, TM // SUBLANES)], sem).wait()


def _dispatch(h2, dest, zrow, n_used, n_blk):
    n_rows = (n_blk + 1) * BM
    d = h2.shape[1] * LANES
    grid_spec = pltpu.PrefetchScalarGridSpec(
        num_scalar_prefetch=2,
        grid=(h2.shape[0] * SUBLANES // TM,),
        in_specs=[pl.BlockSpec((1, 8, TM), lambda i, z, nu: _dest_block(i), memory_space=pltpu.SMEM),
                  pl.BlockSpec(_tiles_shape(TM, d), lambda i, z, nu: (i, 0, 0, 0))],
        out_specs=pl.BlockSpec(memory_space=pl.ANY),
        scratch_shapes=[pltpu.VMEM(_tiles_shape(BM, d), h2.dtype),
                        pltpu.SemaphoreType.DMA(()),
                        pltpu.SemaphoreType.DMA(())],
    )
    return pl.pallas_call(
        functools.partial(_dispatch_kernel, n_blk=n_blk),
        grid_spec=grid_spec,
        out_shape=jax.ShapeDtypeStruct(_tiles_shape(n_rows, d), h2.dtype),
        compiler_params=_cparams(("arbitrary",)),
        name="dispatch",
    )(zrow, n_used, dest, h2)


def _experts_kernel(blk_e_ref, n_used_ref, x_ref, wg_ref, wu_ref, wd_ref, y_ref, wg_bf, wu_bf, wd_bf):
    j = pl.program_id(0)

    @pl.when((j == 0) | (blk_e_ref[j] != blk_e_ref[jnp.maximum(j - 1, 0)]))
    def _():
        wg_bf[...] = wg_ref[0].astype(wg_bf.dtype)
        wu_bf[...] = wu_ref[0].astype(wu_bf.dtype)
        wd_bf[...] = wd_ref[0].astype(wd_bf.dtype)

    @pl.when(j < n_used_ref[0])
    def _():
        x = _from_tiles(x_ref).astype(wg_bf.dtype)
        a = jnp.dot(x, wg_bf[...], preferred_element_type=jnp.float32)
        b = jnp.dot(x, wu_bf[...], preferred_element_type=jnp.float32)
        hmid = (a / (1.0 + jnp.exp(-a)) * b).astype(x.dtype)
        _to_tiles(y_ref, jnp.dot(hmid, wd_bf[...], preferred_element_type=jnp.float32))

    @pl.when(j >= n_used_ref[0])
    def _():
        y_ref[...] = jnp.zeros(y_ref.shape, y_ref.dtype)


def _experts(xs, wg, wu, wd, blk_e, n_used):
    n_blk = blk_e.shape[0]
    d, d_e = wg.shape[1], wg.shape[2]
    blk = _tiles_shape(BM, d)
    grid_spec = pltpu.PrefetchScalarGridSpec(
        num_scalar_prefetch=2,
        grid=(n_blk,),
        in_specs=[pl.BlockSpec(blk, lambda j, be, nu: (jnp.minimum(j, nu[0] - 1), 0, 0, 0)),
                  pl.BlockSpec((1, d, d_e), lambda j, be, nu: (be[j], 0, 0)),
                  pl.BlockSpec((1, d, d_e), lambda j, be, nu: (be[j], 0, 0)),
                  pl.BlockSpec((1, d_e, d), lambda j, be, nu: (be[j], 0, 0))],
        out_specs=pl.BlockSpec(blk, lambda j, be, nu: (j, 0, 0, 0)),
        scratch_shapes=[pltpu.VMEM((d, d_e), jnp.bfloat16), pltpu.VMEM((d, d_e), jnp.bfloat16),
                        pltpu.VMEM((d_e, d), jnp.bfloat16)],
    )
    return pl.pallas_call(
        _experts_kernel,
        grid_spec=grid_spec,
        out_shape=jax.ShapeDtypeStruct(_tiles_shape(n_blk * BM, d), jnp.float32),
        compiler_params=_cparams(("arbitrary",)),
        name="experts",
    )(blk_e, n_used, xs, wg, wu, wd)


def _final_kernel(dest_ref, dest_next_ref, x1_ref, route_ref, gt_ref, g_ref, ys_hbm, o_ref, buf_a, buf_b, sem):
    i = pl.program_id(0)
    last = pl.num_programs(0) - 1
    tiles = TM // SUBLANES
    bufs = (buf_a, buf_b)

    def gather(d_ref, s):
        for r in range(TM):
            k, u = divmod(r, SUBLANES)
            pltpu.make_async_copy(ys_hbm.at[d_ref[0, 0, r], :, d_ref[0, 1, r], :],
                                  bufs[s].at[k, :, u, :], sem.at[s]).start()
            pltpu.make_async_copy(ys_hbm.at[d_ref[0, 2, r], :, d_ref[0, 3, r], :],
                                  bufs[s].at[tiles + k, :, u, :], sem.at[s]).start()

    def wait_gather(s):
        pltpu.make_async_copy(ys_hbm.at[pl.ds(0, 2 * tiles)], bufs[s], sem.at[s]).wait()

    @pl.when(i == 0)
    def _():
        gather(dest_ref, 0)

    for s in range(2):
        @pl.when(i % 2 == s)
        def _(s=s):
            wait_gather(s)
            gather(dest_next_ref, 1 - s)
            w1 = route_ref[:, 2:3]
            w2 = route_ref[:, 3:4]
            y0 = _from_tiles(bufs[s].at[0:tiles])
            y1 = _from_tiles(bufs[s].at[tiles:2 * tiles])
            x2 = x1_ref[...] + gt_ref[0] * (w1 * y0 + w2 * y1)
            o_ref[...] = x2 * lax.rsqrt(jnp.mean(x2 * x2, axis=-1, keepdims=True) + EPS) * g_ref[...]

            @pl.when(i == last)
            def _():
                wait_gather(1 - s)


def _final(x1, ys, dest, route, mod3, g_final, seq):
    n, d = x1.shape
    n_tiles = n // TM
    tiles_per_seq = seq // TM
    row = lambda i: (i, 0)
    return pl.pallas_call(
        _final_kernel,
        grid=(n_tiles,),
        in_specs=[pl.BlockSpec((1, 8, TM), _dest_block, memory_space=pltpu.SMEM),
                  pl.BlockSpec((1, 8, TM), lambda i: _dest_block(jnp.minimum(i + 1, n_tiles - 1)),
                               memory_space=pltpu.SMEM),
                  pl.BlockSpec((TM, d), row),
                  pl.BlockSpec((TM, LANES), row),
                  pl.BlockSpec((1, 1, d), lambda i: ((i // tiles_per_seq) * 6 + 5, 0, 0)),
                  pl.BlockSpec((1, d), lambda i: (0, 0)),
                  pl.BlockSpec(memory_space=pl.ANY)],
        out_specs=pl.BlockSpec((TM, d), row),
        out_shape=jax.ShapeDtypeStruct((n, d), jnp.float32),
        scratch_shapes=[pltpu.VMEM(_tiles_shape(2 * TM, d), jnp.float32),
                        pltpu.VMEM(_tiles_shape(2 * TM, d), jnp.float32),
                        pltpu.SemaphoreType.DMA((2,))],
        compiler_params=_cparams(("arbitrary",)),
        name="final",
    )(dest, dest, x1, route, mod3, g_final, ys)


def _block_layout(counts, n):
    cnt = counts[0, :N_EXPERTS].astype(jnp.int32)
    blocks = (cnt + BM - 1) // BM
    bends = jnp.cumsum(blocks)
    pstarts = (bends - blocks) * BM
    n_blk = (2 * n) // BM + N_EXPERTS
    pst_row = jnp.zeros((1, LANES), jnp.float32).at[0, :N_EXPERTS].set(pstarts.astype(jnp.float32))
    zrow = ((pstarts + cnt) // SUBLANES).astype(jnp.int32)
    n_used = bends[-1:].astype(jnp.int32)
    blk_e = jnp.sum(bends[None, :] <= jnp.arange(n_blk, dtype=jnp.int32)[:, None], axis=1)
    blk_e = jnp.minimum(blk_e, N_EXPERTS - 1).astype(jnp.int32)
    return pst_row, zrow, n_used, blk_e


def _layer(x2, c, pos2, w_ada, b_ada, g_mix, w_in, w_o, g_ffn, w_rg, w_re, w_up, w_gate, w_down, seq):
    n, d = x2.shape
    bsz = n // seq
    bf = jnp.bfloat16
    d_a = d // 2
    d_i = N_IDX_HEADS * IDX_DIM
    mod3 = _ada(c, w_ada, b_ada).reshape(bsz * 6, 1, d)

    c0 = 3 * d_a + d_i
    c1 = c0 + IDX_DIM + N_IDX_HEADS
    w_cat = jnp.concatenate([w_in[:, :c0], w_in[:, c0:c1],
                             jnp.zeros((d, LANES - (c1 - c0)), w_in.dtype), w_in[:, c1:]], axis=1).astype(bf)
    qat, ka, vat, qit, kiw, wit, qb, kb, vb, gb = _proj(x2, pos2, mod3, g_mix.reshape(1, d), w_cat, seq)

    n_kt = seq // TM
    ya = _dsa(qat, qit, wit, kiw.reshape(bsz, n_kt, TM, LANES), ka.reshape(bsz, n_kt, TM, d_a), vat, seq)
    yb = _ret(qb, kb, vb, gb, seq)

    w_route = jnp.concatenate([jnp.transpose(w_re, (1, 0, 2)).reshape(d, N_EXPERTS), w_rg,
                               jnp.zeros((d, LANES - N_EXPERTS - N_GROUPS), w_rg.dtype)], axis=1).astype(bf)
    x1, h2, route, counts = _mix_out(x2, ya, yb, w_o[:d_a].astype(bf), w_o[d_a:].astype(bf), mod3,
                                     g_ffn.reshape(1, d), w_route, seq)

    pst_row, zrow, n_used, blk_e = _block_layout(counts, n)
    dest = _plan(route, pst_row)
    xs = _dispatch(h2, dest, zrow, n_used, blk_e.shape[0])
    ys = _experts(xs, w_gate, w_up, w_down, blk_e, n_used)
    return x1, ys, dest, route, mod3


def kernel(x, c, positions, w_ada, b_ada, g_norm_mix, w_in, w_o, g_norm_ffn, w_router_group, w_router_expert,
           w_up, w_gate, w_down, g_norm_final):
    bsz, seq, d = x.shape
    depth = w_ada.shape[0]
    assert depth == 1, "the final norm is fused into the last layer's combine kernel"
    assert seq % (2 * TM) == 0 and seq % (RET_C * RET_CHUNKS) == 0 and seq % MIX_TM == 0 and (2 * bsz * seq) % BM == 0
    assert (bsz * seq) % PLAN_TM == 0 and PLAN_TM % TM == 0
    x2 = x.reshape(bsz * seq, d)
    pos2 = positions.astype(jnp.float32).reshape(bsz * seq, 1)
    x1, ys, dest, route, mod3 = _layer(x2, c, pos2, w_ada[0], b_ada[0], g_norm_mix[0], w_in[0], w_o[0],
                                       g_norm_ffn[0], w_router_group[0], w_router_expert[0], w_up[0], w_gate[0],
                                       w_down[0], seq)
    out = _final(x1, ys, dest, route, mod3, g_norm_final.reshape(1, d), seq)
    return out.reshape(bsz, seq, d)
```

```python
import functools

import jax
import jax.numpy as jnp
import numpy as np
from jax import lax
from jax.experimental import pallas as pl
from jax.experimental.pallas import tpu as pltpu

CHUNK = 64
HEAD_DIM = 64
N_IDX_HEADS = 16
IDX_DIM = 64
TOPK_MAX = 256
ROPE_THETA = 500000.0
ROT_DIM = HEAD_DIM // 4
RET_THETA = 10000.0
N_GROUPS = 4
EXPERTS_PER_GROUP = 8
N_EXPERTS = N_GROUPS * EXPERTS_PER_GROUP
EPS = 1e-6

LANES = 128
SUBLANES = 8
VMEM_LIMIT = 56 * 1024 * 1024

TM = 256
QB = TM
RET_C = 256
RET_CHUNKS = 2
BM = 256
PLAN_TM = 1024
MIX_TM = 512
CNT_ROWS = 64
ONES_ROWS = 16
HEAD_GROUP = 4
IDX_TILES = 8
ATT_TILES = 8
BISECT_PER_CHECK = 2
MAX_BISECT = 40

NEG_BIG = -1e30
LOG2E = 1.4426950408889634


def _cparams(sem):
    return pltpu.CompilerParams(dimension_semantics=sem, vmem_limit_bytes=VMEM_LIMIT)


def _ada_kernel(c_ref, w_ref, b_ref, o_ref):
    o_ref[...] = jnp.dot(c_ref[...], w_ref[...], preferred_element_type=jnp.float32) + b_ref[...]


def _ada(c, w_ada, b_ada):
    bsz, d = c.shape
    n_out = w_ada.shape[1]
    return pl.pallas_call(
        _ada_kernel,
        grid=(n_out // d,),
        in_specs=[pl.BlockSpec((bsz, d), lambda j: (0, 0)),
                  pl.BlockSpec((d, d), lambda j: (0, j)),
                  pl.BlockSpec((1, d), lambda j: (0, j))],
        out_specs=pl.BlockSpec((bsz, d), lambda j: (0, j)),
        out_shape=jax.ShapeDtypeStruct((bsz, n_out), jnp.float32),
        compiler_params=_cparams(("arbitrary",)),
        name="ada",
    )(c, w_ada, b_ada.reshape(1, n_out))


def _rmsnorm_mod(x, g, sc, sh):
    xn = x * lax.rsqrt(jnp.mean(x * x, axis=-1, keepdims=True) + EPS)
    return xn * g * (1.0 + sc) + sh


def _rope_lanes(x, cos, sin_lo, sin_hi, half):
    cols = []
    for k in range(x.shape[1] // LANES):
        xb = x[:, k * LANES:(k + 1) * LANES]
        cols.append(xb * cos + pltpu.roll(xb, LANES - half, 1) * sin_lo + pltpu.roll(xb, half, 1) * sin_hi)
    return cols[0] if len(cols) == 1 else jnp.concatenate(cols, axis=1)


def _proj_kernel(x_ref, pos_ref, sc_ref, sh_ref, g_ref, w_ref, tab_ref,
                 qat_ref, ka_ref, vat_ref, qit_ref, kiw_ref, wit_ref, qb_ref, kb_ref, vb_ref, gb_ref,
                 *, d_a, d_i, d_b):
    h = _rmsnorm_mod(x_ref[...], g_ref[...], sc_ref[0], sh_ref[0]).astype(jnp.bfloat16)
    pos = pos_ref[...]
    ang_a = pos * tab_ref[0:1, :]
    cos_a, sin_a = jnp.cos(ang_a), jnp.sin(ang_a)
    sa_lo, sa_hi = sin_a * tab_ref[1:2, :], sin_a * tab_ref[2:3, :]
    ang_b = pos * tab_ref[3:4, :]
    cos_b, sin_b = jnp.cos(ang_b), jnp.sin(ang_b)
    sb_lo, sb_hi = sin_b * tab_ref[4:5, :], sin_b * tab_ref[5:6, :]
    half_a, half_b = ROT_DIM // 2, HEAD_DIM // 2

    def seg(lo, width):
        return jnp.dot(h, w_ref[:, lo:lo + width], preferred_element_type=jnp.float32)

    o = 0
    qa = seg(o, d_a); o += d_a
    qat_ref[0, 0] = (_rope_lanes(qa, cos_a, sa_lo, sa_hi, half_a) * (HEAD_DIM ** -0.5 * LOG2E)).T.astype(qat_ref.dtype)
    ka = seg(o, d_a); o += d_a
    ka_ref[...] = _rope_lanes(ka, cos_a, sa_lo, sa_hi, half_a).astype(ka_ref.dtype)
    vat_ref[0, 0] = seg(o, d_a).T.astype(vat_ref.dtype); o += d_a
    qi = seg(o, d_i); o += d_i
    qit_ref[0, 0] = (_rope_lanes(qi, cos_a, sa_lo, sa_hi, half_a) * (IDX_DIM ** -0.5)).T.astype(qit_ref.dtype)
    kw = seg(o, LANES); o += LANES
    kiw_ref[...] = _rope_lanes(kw, cos_a, sa_lo, sa_hi, half_a).astype(kiw_ref.dtype)
    wit_ref[0, 0] = kw.T[IDX_DIM:IDX_DIM + N_IDX_HEADS, :] * (N_IDX_HEADS ** -0.5)
    qb = seg(o, d_b); o += d_b
    qb_ref[...] = _rope_lanes(qb, cos_b, sb_lo, sb_hi, half_b).astype(qb_ref.dtype)
    kb = seg(o, d_b); o += d_b
    kb_ref[...] = (_rope_lanes(kb, cos_b, sb_lo, sb_hi, half_b) * (HEAD_DIM ** -0.5)).astype(kb_ref.dtype)
    vb_ref[...] = seg(o, d_b).astype(vb_ref.dtype); o += d_b
    gb_ref[...] = seg(o, d_b)


def _rope_tables():
    lane = jnp.arange(LANES) % HEAD_DIM
    rows = []
    for rot, theta in ((ROT_DIM, ROPE_THETA), (HEAD_DIM, RET_THETA)):
        half = rot // 2
        inv_freq = theta ** (-jnp.arange(half, dtype=jnp.float32) / half)
        rows.append(jnp.where(lane < rot, inv_freq[lane % half], 0.0))
        rows.append(jnp.where(lane < half, -1.0, 0.0))
        rows.append(jnp.where((lane >= half) & (lane < rot), 1.0, 0.0))
    rows += [jnp.zeros((LANES,), jnp.float32)] * 2
    return jnp.stack(rows).astype(jnp.float32)


def _proj(x2, pos2, mod3, g_mix, w_cat, seq):
    n, d = x2.shape
    bsz = n // seq
    d_a = d // 2
    d_b = d // 2
    d_i = N_IDX_HEADS * IDX_DIM
    tiles_per_seq = seq // TM
    tab = _rope_tables()
    row = lambda i: (i, 0)
    const = lambda i: (0, 0)
    tile4 = lambda i: (i // tiles_per_seq, i % tiles_per_seq, 0, 0)
    bf = jnp.bfloat16
    out_shape = (
        jax.ShapeDtypeStruct((bsz, tiles_per_seq, d_a, TM), bf),
        jax.ShapeDtypeStruct((n, d_a), bf),
        jax.ShapeDtypeStruct((bsz, tiles_per_seq, d_a, TM), bf),
        jax.ShapeDtypeStruct((bsz, tiles_per_seq, d_i, TM), bf),
        jax.ShapeDtypeStruct((n, LANES), bf),
        jax.ShapeDtypeStruct((bsz, tiles_per_seq, N_IDX_HEADS, TM), jnp.float32),
        jax.ShapeDtypeStruct((n, d_b), bf),
        jax.ShapeDtypeStruct((n, d_b), bf),
        jax.ShapeDtypeStruct((n, d_b), bf),
        jax.ShapeDtypeStruct((n, d_b), jnp.float32),
    )
    out_specs = (
        pl.BlockSpec((1, 1, d_a, TM), tile4),
        pl.BlockSpec((TM, d_a), row),
        pl.BlockSpec((1, 1, d_a, TM), tile4),
        pl.BlockSpec((1, 1, d_i, TM), tile4),
        pl.BlockSpec((TM, LANES), row),
        pl.BlockSpec((1, 1, N_IDX_HEADS, TM), tile4),
        pl.BlockSpec((TM, d_b), row),
        pl.BlockSpec((TM, d_b), row),
        pl.BlockSpec((TM, d_b), row),
        pl.BlockSpec((TM, d_b), row),
    )
    return pl.pallas_call(
        functools.partial(_proj_kernel, d_a=d_a, d_i=d_i, d_b=d_b),
        grid=(n // TM,),
        in_specs=[pl.BlockSpec((TM, d), row),
                  pl.BlockSpec((TM, 1), row),
                  pl.BlockSpec((1, 1, d), lambda i: ((i // tiles_per_seq) * 6 + 1, 0, 0)),
                  pl.BlockSpec((1, 1, d), lambda i: ((i // tiles_per_seq) * 6 + 0, 0, 0)),
                  pl.BlockSpec((1, d), const),
                  pl.BlockSpec(w_cat.shape, const),
                  pl.BlockSpec(tab.shape, const)],
        out_specs=out_specs,
        out_shape=out_shape,
        compiler_params=_cparams(("arbitrary",)),
        name="proj",
    )(x2, pos2, mod3, mod3, g_mix, w_cat, tab)


def _row_blocks(x, rows):
    return [x[r * rows:(r + 1) * rows] for r in range(x.shape[0] // rows)]


def _dsa_kernel(qat_ref, qit_ref, wit_ref, kiw_ref, ka_ref, vat_ref, tri_ref, o_ref,
                qix_ref, qmx_ref, sc_ref, m_ref, l_ref, acc_ref, sa_ref, sb_ref, mxa_ref, mxb_ref, lohi_ref,
                *, k_top, n_heads):
    i = pl.program_id(1)
    n_grp = (i * QB + QB + 2 * TM - 1) // (2 * TM)
    n_kt = 2 * n_grp
    n_real = (i * QB + QB + TM - 1) // TM

    zero_rows = jnp.zeros((LANES - IDX_DIM, QB), qix_ref.dtype)
    for h in range(N_IDX_HEADS):
        qix_ref[h] = jnp.concatenate([qit_ref[0, 0, h * IDX_DIM:(h + 1) * IDX_DIM, :], zero_rows], axis=0)
    row_q = lax.broadcasted_iota(jnp.int32, (LANES, QB), 0)
    for h in range(n_heads):
        pair = qat_ref[0, 0, (h // 2) * LANES:(h // 2 + 1) * LANES, :]
        own = (row_q < HEAD_DIM) if h % 2 == 0 else (row_q >= HEAD_DIM)
        qmx_ref[h] = jnp.where(own, pair, jnp.zeros_like(pair))

    q_chunk = (i * QB + lax.broadcasted_iota(jnp.int32, (1, QB), 1)) // CHUNK
    key_chunk_in_tile = lax.broadcasted_iota(jnp.int32, (TM, QB), 0) // CHUNK
    w_all = wit_ref[0, 0]

    def idx_tiles(tiles):
        lo, hi = lohi_ref[0], lohi_ref[1]
        for j in tiles:
            kt = kiw_ref[0, j]
            acc = None
            for h in range(N_IDX_HEADS):
                d = jnp.dot(kt, qix_ref[h], preferred_element_type=jnp.float32)
                t = w_all[h:h + 1, :] * jnp.maximum(d, 0.0)
                acc = t if acc is None else acc + t
            adm = key_chunk_in_tile <= q_chunk - j * (TM // CHUNK)
            s = jnp.where(adm, acc, -jnp.inf)
            sc_ref[j] = s
            lo = jnp.minimum(lo, functools.reduce(jnp.minimum, _row_blocks(jnp.where(adm, acc, jnp.inf), SUBLANES)))
            hi = jnp.maximum(hi, functools.reduce(jnp.maximum, _row_blocks(s, SUBLANES)))
        lohi_ref[0], lohi_ref[1] = lo, hi

    lohi_ref[0] = jnp.full((SUBLANES, QB), jnp.inf, jnp.float32)
    lohi_ref[1] = jnp.full((SUBLANES, QB), -jnp.inf, jnp.float32)

    def idx_step(g, carry):
        idx_tiles([IDX_TILES * g + u for u in range(IDX_TILES)])
        return carry

    lax.fori_loop(0, n_kt // IDX_TILES, idx_step, 0)

    def idx_tail(p, carry):
        first = n_kt // IDX_TILES * IDX_TILES + 2 * p
        idx_tiles([first, first + 1])
        return carry

    lax.fori_loop(0, n_kt % IDX_TILES // 2, idx_tail, 0)

    lo = jnp.min(lohi_ref[0], axis=0, keepdims=True)
    hi = jnp.max(lohi_ref[1], axis=0, keepdims=True)
    n_adm = ((i * QB + lax.broadcasted_iota(jnp.int32, (1, QB), 1)) // CHUNK + 1) * CHUNK

    def bisect_cond(carry):
        it, _, _, _, unsettled = carry
        return (it < MAX_BISECT) & (unsettled > 0.0)

    def bisect(carry):
        it, lo, hi, cnt_lo, _ = carry
        for _ in range(BISECT_PER_CHECK):
            mid = lo + (hi - lo) * 0.5
            mid_b = jnp.broadcast_to(mid, (CNT_ROWS, QB))

            def count_tile(j, cnt, mid_b=mid_b):
                for blk in _row_blocks(sc_ref[j], CNT_ROWS):
                    cnt = cnt + jnp.where(blk >= mid_b, 1.0, 0.0)
                return cnt

            cnt = lax.fori_loop(0, n_real, count_tile, jnp.zeros((CNT_ROWS, QB), jnp.float32))
            c = jnp.sum(cnt, axis=0, keepdims=True)
            ge = c >= k_top
            lo = jnp.where(ge, mid, lo)
            hi = jnp.where(ge, hi, mid)
            cnt_lo = jnp.where(ge, c, cnt_lo)
        return it + BISECT_PER_CHECK, lo, hi, cnt_lo, jnp.max(jnp.where(cnt_lo > k_top, 1.0, 0.0))

    cnt0 = n_adm.astype(jnp.float32)
    _, thr, thr_hi, _, unsettled = lax.while_loop(bisect_cond, bisect,
                                                  (0, lo, hi, cnt0, jnp.max(jnp.where(cnt0 > k_top, 1.0, 0.0))))

    @pl.when(unsettled <= 0.0)
    def _():
        def bias_tile(j, carry):
            sc_ref[j] = jnp.where(sc_ref[j] >= thr, 0.0, NEG_BIG)
            return carry

        lax.fori_loop(0, n_kt, bias_tile, 0)

    @pl.when(unsettled > 0.0)
    def _():
        def count_hi(j, cnt):
            return cnt + functools.reduce(jnp.add, _row_blocks(jnp.where(sc_ref[j] >= thr_hi, 1.0, 0.0), SUBLANES))

        above = jnp.sum(lax.fori_loop(0, n_kt, count_hi, jnp.zeros((SUBLANES, QB), jnp.float32)),
                        axis=0, keepdims=True)
        top = jnp.where(above < k_top, thr_hi, jnp.inf)
        room = k_top - jnp.where(above < k_top, above, 0.0)

        def bias_tile(j, taken):
            s = sc_ref[j]
            tied = (s >= thr) & (s < top)
            before = jnp.dot(tri_ref[...], jnp.where(tied, 1.0, 0.0).astype(tri_ref.dtype),
                             preferred_element_type=jnp.float32)
            keep = (s >= top) | (tied & (taken + before <= room))
            sc_ref[j] = jnp.where(keep, 0.0, NEG_BIG)
            return taken + before[TM - 1:TM, :]

        lax.fori_loop(0, n_kt, bias_tile, jnp.zeros((1, QB), jnp.float32))

    m_ref[...] = jnp.full(m_ref.shape, NEG_BIG, jnp.float32)
    l_ref[...] = jnp.zeros(l_ref.shape, jnp.float32)
    acc_ref[...] = jnp.zeros(acc_ref.shape, jnp.float32)

    def pair(h):
        return slice((h // 2) * LANES, (h // 2 + 1) * LANES)

    def logits_into(s_ref, mx_ref, j, heads):
        bias = sc_ref[j]
        for h in heads:
            s = jnp.dot(ka_ref[0, j, :, pair(h)], qmx_ref[h], preferred_element_type=jnp.float32) + bias
            s_ref[h] = s
            mx_ref[h] = jnp.max(s, axis=0, keepdims=True)

    ones_rows = jnp.ones((ONES_ROWS, TM), vat_ref.dtype)

    def absorb(s_ref, mx_ref, j, heads):
        for h in heads:
            m_old = m_ref[h]
            m_new = jnp.maximum(m_old, mx_ref[h])
            alpha = jnp.exp2(m_old - m_new)
            p = jnp.exp2(s_ref[h] - m_new).astype(vat_ref.dtype)
            pv = jnp.dot(jnp.concatenate([vat_ref[0, j, pair(h), :], ones_rows], axis=0), p,
                         preferred_element_type=jnp.float32)
            acc_ref[h] = acc_ref[h] * alpha + pv[0:LANES]
            l_ref[h] = l_ref[h] * alpha + pv[LANES:LANES + SUBLANES]
            m_ref[h] = m_new

    for h0 in range(0, n_heads, HEAD_GROUP):
        heads = range(h0, h0 + HEAD_GROUP)
        logits_into(sa_ref, mxa_ref, 0, heads)

        def tile_pair(t, heads=heads):
            logits_into(sb_ref, mxb_ref, t + 1, heads)
            absorb(sa_ref, mxa_ref, t, heads)
            logits_into(sa_ref, mxa_ref, jnp.minimum(t + 2, n_kt - 1), heads)
            absorb(sb_ref, mxb_ref, t + 1, heads)

        def attn_step(g, carry):
            for u in range(0, ATT_TILES, 2):
                tile_pair(ATT_TILES * g + u)
            return carry

        lax.fori_loop(0, n_kt // ATT_TILES, attn_step, 0)

        def attn_tail(p, carry):
            tile_pair(n_kt // ATT_TILES * ATT_TILES + 2 * p)
            return carry

        lax.fori_loop(0, n_kt % ATT_TILES // 2, attn_tail, 0)

    for hp in range(n_heads // 2):
        even = acc_ref[2 * hp] / l_ref[2 * hp, 0:1, :]
        odd = acc_ref[2 * hp + 1] / l_ref[2 * hp + 1, 0:1, :]
        o_ref[:, hp * LANES:(hp + 1) * LANES] = jnp.where(row_q < HEAD_DIM, even, odd).T.astype(o_ref.dtype)


def _dsa(qat, qit, wit, kiw4, ka4, vat, seq):
    bsz, n_kt, d_a, _ = qat.shape
    n_heads = d_a // HEAD_DIM
    n_qb = seq // QB
    k_top = min(TOPK_MAX, seq // 4)
    qtile = lambda b, i: (b, i, 0, 0)
    per_b = lambda b, i: (b, 0, 0, 0)
    f32 = jnp.float32
    return pl.pallas_call(
        functools.partial(_dsa_kernel, k_top=float(k_top), n_heads=n_heads),
        grid=(bsz, n_qb),
        in_specs=[pl.BlockSpec((1, 1, d_a, QB), qtile),
                  pl.BlockSpec((1, 1, qit.shape[2], QB), qtile),
                  pl.BlockSpec((1, 1, N_IDX_HEADS, QB), qtile),
                  pl.BlockSpec((1, n_kt, TM, LANES), per_b, pipeline_mode=pl.Buffered(1)),
                  pl.BlockSpec((1, n_kt, TM, d_a), per_b, pipeline_mode=pl.Buffered(1)),
                  pl.BlockSpec((1, n_kt, d_a, TM), per_b, pipeline_mode=pl.Buffered(1)),
                  pl.BlockSpec((TM, TM), lambda b, i: (0, 0), pipeline_mode=pl.Buffered(1))],
        out_specs=pl.BlockSpec((QB, d_a), lambda b, i: (b * n_qb + i, 0)),
        out_shape=jax.ShapeDtypeStruct((bsz * seq, d_a), jnp.bfloat16),
        scratch_shapes=[pltpu.VMEM((N_IDX_HEADS, LANES, QB), jnp.bfloat16),
                        pltpu.VMEM((n_heads, LANES, QB), jnp.bfloat16),
                        pltpu.VMEM((n_kt, TM, QB), f32),
                        pltpu.VMEM((n_heads, 1, QB), f32),
                        pltpu.VMEM((n_heads, SUBLANES, QB), f32),
                        pltpu.VMEM((n_heads, LANES, QB), f32),
                        pltpu.VMEM((n_heads, TM, QB), f32),
                        pltpu.VMEM((n_heads, TM, QB), f32),
                        pltpu.VMEM((n_heads, 1, QB), f32),
                        pltpu.VMEM((n_heads, 1, QB), f32),
                        pltpu.VMEM((2, SUBLANES, QB), f32)],
        compiler_params=_cparams(("arbitrary", "arbitrary")),
        name="dsa",
    )(qat, qit, wit, kiw4, ka4, vat, jnp.asarray(np.tril(np.ones((TM, TM), np.float32)), jnp.bfloat16))


def _group_mean(y, avg):
    hi = y.astype(jnp.bfloat16)
    lo = (y - hi.astype(jnp.float32)).astype(jnp.bfloat16)
    return (jnp.dot(hi, avg, preferred_element_type=jnp.float32)
            + jnp.dot(lo, avg, preferred_element_type=jnp.float32))


def _ret_kernel(q_ref, k_ref, v_ref, g_ref, dec_ref, zt_ref, xi_ref, gc_ref, blk_ref, avg_ref, o_ref, st_ref,
                *, n_heads):
    @pl.when(pl.program_id(1) == 0)
    def _():
        st_ref[...] = jnp.zeros_like(st_ref)

    even = lax.broadcasted_iota(jnp.int32, (RET_C, LANES), 1) < HEAD_DIM
    avg = avg_ref[...]
    for p in range(n_heads // 2):
        sl = slice(p * LANES, (p + 1) * LANES)
        state = st_ref[p]
        for c in range(q_ref.shape[0] // RET_C):
            rows = slice(c * RET_C, (c + 1) * RET_C)
            qp, kp, vp = q_ref[rows, sl], k_ref[rows, sl], v_ref[rows, sl]
            kpt = kp.astype(jnp.float32).T
            kpt_b = kpt.astype(kp.dtype)
            inner = None
            for e in range(2):
                q_e = jnp.where(even if e == 0 else jnp.logical_not(even), qp, jnp.zeros_like(qp))
                s = jnp.dot(q_e, kpt_b, preferred_element_type=jnp.float32) * dec_ref[2 * p + e]
                t = jnp.dot(s.astype(vp.dtype), vp, preferred_element_type=jnp.float32)
                inner = t if e == 0 else jnp.where(even, inner, t)
            cross = jnp.dot(qp, state.astype(qp.dtype), preferred_element_type=jnp.float32) * xi_ref[p]
            y = inner + cross
            yc = y - _group_mean(y, avg)
            yn = yc * lax.rsqrt(_group_mean(yc * yc, avg) + EPS)
            g = g_ref[rows, sl]
            o_ref[rows, sl] = (g / (1.0 + jnp.exp(-g)) * yn).astype(o_ref.dtype)
            kz = (kpt * zt_ref[p]).astype(kp.dtype)
            kv = jnp.dot(kz, vp, preferred_element_type=jnp.float32)
            state = state * gc_ref[p] + kv * blk_ref[...]
        st_ref[p] = state


def _ret_consts(n_heads):
    log_gamma = jnp.log1p(-jnp.exp2(-5.0 - jnp.arange(n_heads, dtype=jnp.float32)))
    pos = jnp.arange(RET_C, dtype=jnp.float32)
    diff = pos[:, None] - pos[None, :]
    dec = jnp.where(diff[None] >= 0, jnp.exp(jnp.maximum(diff, 0.0)[None] * log_gamma[:, None, None]), 0.0)
    zeta = jnp.exp((RET_C - 1.0 - pos)[None, :] * log_gamma[:, None])
    xi = jnp.exp((pos + 1.0)[None, :] * log_gamma[:, None])
    gc = jnp.exp(RET_C * log_gamma)
    n_pairs = n_heads // 2
    lanes = lambda a: jnp.repeat(a.reshape(n_pairs, 2, -1), HEAD_DIM, axis=1)
    zt = lanes(zeta)
    xi_p = jnp.swapaxes(lanes(xi), 1, 2)
    gc_p = jnp.broadcast_to(lanes(gc[:, None]), (n_pairs, LANES, LANES))
    head_of = jnp.arange(LANES) // HEAD_DIM
    blk = (head_of[:, None] == head_of[None, :]).astype(jnp.float32)
    avg = (blk / HEAD_DIM).astype(jnp.bfloat16)
    f32 = lambda a: a.astype(jnp.float32)
    return dec, f32(zt), f32(xi_p), f32(gc_p), blk, avg


def _ret(qb, kb, vb, gb, seq):
    n, d_b = qb.shape
    bsz = n // seq
    n_heads = d_b // HEAD_DIM
    step = RET_C * RET_CHUNKS
    n_c = seq // step
    consts = _ret_consts(n_heads)
    row = lambda b, c: (b * n_c + c, 0)
    const_spec = lambda a: pl.BlockSpec(a.shape, lambda b, c: (0,) * a.ndim)
    return pl.pallas_call(
        functools.partial(_ret_kernel, n_heads=n_heads),
        grid=(bsz, n_c),
        in_specs=[pl.BlockSpec((step, d_b), row)] * 4 + [const_spec(a) for a in consts],
        out_specs=pl.BlockSpec((step, d_b), row),
        out_shape=jax.ShapeDtypeStruct((n, d_b), jnp.bfloat16),
        scratch_shapes=[pltpu.VMEM((n_heads // 2, LANES, LANES), jnp.float32)],
        compiler_params=_cparams(("arbitrary", "arbitrary")),
        name="ret",
    )(qb, kb, vb, gb, *consts)


def _tiles_shape(rows, d):
    return (rows // SUBLANES, d // LANES, SUBLANES, LANES)


def _to_tiles(ref, x):
    for s in range(ref.shape[1]):
        ref[:, s] = x[:, s * LANES:(s + 1) * LANES].reshape(ref.shape[0], SUBLANES, LANES)


def _from_tiles(ref):
    rows = ref.shape[0] * SUBLANES
    return jnp.concatenate([ref[:, s].reshape(rows, LANES) for s in range(ref.shape[1])], axis=1)


def _lane_first_eq(x, m, lane):
    return jnp.min(jnp.where(x == m, lane, float(LANES)), axis=1, keepdims=True)


def _mix_out_kernel(x_ref, ya_ref, yb_ref, woa_ref, wob_ref, gt_ref, sc_ref, sh_ref, g_ref, wr_ref, tri_ref,
                    x1_ref, h2_ref, route_ref, cnt_ref, carry_ref):
    @pl.when(pl.program_id(0) == 0)
    def _():
        carry_ref[...] = jnp.zeros_like(carry_ref)

    mix = (jnp.dot(ya_ref[...], woa_ref[...], preferred_element_type=jnp.float32)
           + jnp.dot(yb_ref[...], wob_ref[...], preferred_element_type=jnp.float32))
    x1 = x_ref[...] + gt_ref[0] * mix
    x1_ref[...] = x1
    h2 = _rmsnorm_mod(x1, g_ref[...], sc_ref[0], sh_ref[0])
    _to_tiles(h2_ref, h2)

    lg = jnp.dot(h2.astype(jnp.bfloat16), wr_ref[...], preferred_element_type=jnp.float32)
    lane = lax.broadcasted_iota(jnp.int32, lg.shape, 1).astype(jnp.float32)
    is_grp = (lane >= N_EXPERTS) & (lane < N_EXPERTS + N_GROUPS)
    gl = jnp.where(is_grp, lg, -jnp.inf)
    gmax = jnp.max(gl, axis=1, keepdims=True)
    grp = _lane_first_eq(gl, gmax, lane) - N_EXPERTS
    p_grp = 1.0 / jnp.sum(jnp.exp(gl - gmax), axis=1, keepdims=True)
    in_grp = jnp.floor(lane * (1.0 / EXPERTS_PER_GROUP)) == grp
    f = jnp.where(in_grp & (lane < N_EXPERTS), lg, -jnp.inf)
    f1 = jnp.max(f, axis=1, keepdims=True)
    e1 = _lane_first_eq(f, f1, lane)
    f = jnp.where(lane == e1, -jnp.inf, f)
    f2 = jnp.max(f, axis=1, keepdims=True)
    e2 = _lane_first_eq(f, f2, lane)
    a2 = jnp.exp(f2 - f1)
    w1 = p_grp / (1.0 + a2)
    w2 = p_grp * a2 / (1.0 + a2)

    oh1 = jnp.where(lane == e1, 1.0, 0.0)
    oh2 = jnp.where(lane == e2, 1.0, 0.0)
    both = oh1 + oh2
    before = jnp.dot(tri_ref[...], both.astype(jnp.bfloat16), preferred_element_type=jnp.float32) + carry_ref[...]
    r1 = jnp.sum(before * oh1, axis=1, keepdims=True)
    r2 = jnp.sum(before * oh2, axis=1, keepdims=True)
    carry = carry_ref[...] + jnp.sum(both, axis=0, keepdims=True)
    carry_ref[...] = carry
    cnt_ref[...] = carry

    out = jnp.zeros(lg.shape, jnp.float32)
    for col, val in enumerate((e1, e2, w1, w2, r1, r2)):
        out = jnp.where(lane == col, val, out)
    route_ref[...] = out


def _mix_out(x2, ya, yb, wo_a, wo_b, mod3, g_ffn, w_route, seq):
    n, d = x2.shape
    tm = MIX_TM
    tiles_per_seq = seq // tm
    tri = jnp.asarray(np.tril(np.ones((tm, tm), np.float32), -1), jnp.bfloat16)
    row = lambda i: (i, 0)
    const = lambda i: (0, 0)
    modk = lambda k: pl.BlockSpec((1, 1, d), lambda i: ((i // tiles_per_seq) * 6 + k, 0, 0))
    return pl.pallas_call(
        _mix_out_kernel,
        grid=(n // tm,),
        in_specs=[pl.BlockSpec((tm, d), row),
                  pl.BlockSpec((tm, ya.shape[1]), row),
                  pl.BlockSpec((tm, yb.shape[1]), row),
                  pl.BlockSpec(wo_a.shape, const),
                  pl.BlockSpec(wo_b.shape, const),
                  modk(2), modk(4), modk(3),
                  pl.BlockSpec((1, d), const),
                  pl.BlockSpec(w_route.shape, const),
                  pl.BlockSpec(tri.shape, const)],
        out_specs=(pl.BlockSpec((tm, d), row), pl.BlockSpec(_tiles_shape(tm, d), lambda i: (i, 0, 0, 0)),
                   pl.BlockSpec((tm, LANES), row), pl.BlockSpec((1, LANES), const)),
        out_shape=(jax.ShapeDtypeStruct((n, d), jnp.float32), jax.ShapeDtypeStruct(_tiles_shape(n, d), jnp.float32),
                   jax.ShapeDtypeStruct((n, LANES), jnp.float32), jax.ShapeDtypeStruct((1, LANES), jnp.float32)),
        scratch_shapes=[pltpu.VMEM((1, LANES), jnp.float32)],
        compiler_params=_cparams(("arbitrary",)),
        name="mix_out",
    )(x2, ya, yb, wo_a, wo_b, mod3, mod3, mod3, g_ffn, w_route, tri)


def _plan_kernel(route_ref, pst_ref, dest_ref):
    r = route_ref[...]
    lane = lax.broadcasted_iota(jnp.int32, r.shape, 1).astype(jnp.float32)
    pst = pst_ref[...]
    d1 = jnp.sum(jnp.where(lane == r[:, 0:1], pst, 0.0), axis=1, keepdims=True) + r[:, 4:5]
    d2 = jnp.sum(jnp.where(lane == r[:, 1:2], pst, 0.0), axis=1, keepdims=True) + r[:, 5:6]
    t1 = jnp.floor(d1 * (1.0 / SUBLANES))
    t2 = jnp.floor(d2 * (1.0 / SUBLANES))
    packed = jnp.zeros(r.shape, jnp.float32)
    for k, v in enumerate((t1, d1 - t1 * SUBLANES, t2, d2 - t2 * SUBLANES)):
        packed = jnp.where(lane == float(k), v, packed)
    dest_ref[0] = packed.T[0:8, :].astype(jnp.int32)


def _dest_block(i):
    return (i // (PLAN_TM // TM), 0, i % (PLAN_TM // TM))


def _plan(route, pst_row):
    n = route.shape[0]
    return pl.pallas_call(
        _plan_kernel,
        grid=(n // PLAN_TM,),
        in_specs=[pl.BlockSpec((PLAN_TM, LANES), lambda i: (i, 0)), pl.BlockSpec((1, LANES), lambda i: (0, 0))],
        out_specs=pl.BlockSpec((1, 8, PLAN_TM), lambda i: (i, 0, 0)),
        out_shape=jax.ShapeDtypeStruct((n // PLAN_TM, 8, PLAN_TM), jnp.int32),
        compiler_params=_cparams(("arbitrary",)),
        name="plan",
    )(route, pst_row)


def _dispatch_kernel(zrow_ref, n_used_ref, dest_ref, h2_ref, xs_hbm, zbuf, sem, zsem, *, n_blk):
    i = pl.program_id(0)

    @pl.when(i == 0)
    def _():
        zbuf[...] = jnp.zeros(zbuf.shape, zbuf.dtype)
        blk_tiles = BM // SUBLANES
        for e in range(N_EXPERTS):
            pltpu.make_async_copy(zbuf, xs_hbm.at[pl.ds(zrow_ref[e], blk_tiles)], zsem).start()
        for e in range(N_EXPERTS):
            pltpu.make_async_copy(zbuf, xs_hbm.at[pl.ds(0, blk_tiles)], zsem).wait()
        for b in range(N_EXPERTS + 1):
            @pl.when(n_used_ref[0] + b <= n_blk)
            def _():
                tail = pltpu.make_async_copy(zbuf, xs_hbm.at[pl.ds((n_used_ref[0] + b) * blk_tiles, blk_tiles)], zsem)
                tail.start()
                tail.wait()

    def issue(k, c):
        for u in range(SUBLANES):
            r = k * SUBLANES + u
            row = h2_ref.at[k, :, u, :]
            pltpu.make_async_copy(row, xs_hbm.at[dest_ref[0, 0, r], :, dest_ref[0, 1, r], :], sem).start(priority=0)
            pltpu.make_async_copy(row, xs_hbm.at[dest_ref[0, 2, r], :, dest_ref[0, 3, r], :], sem).start(priority=1)
        return c

    lax.fori_loop(0, TM // SUBLANES, issue, 0)
    for _ in range(2):
        pltpu.make_async_copy(h2_ref, xs_hbm.at[pl.ds(0, TM // SUBLANES)], sem).wait()


def _dispatch(h2, dest, zrow, n_used, n_blk):
    n_rows = (n_blk + 1) * BM
    d = h2.shape[1] * LANES
    grid_spec = pltpu.PrefetchScalarGridSpec(
        num_scalar_prefetch=2,
        grid=(h2.shape[0] * SUBLANES // TM,),
        in_specs=[pl.BlockSpec((1, 8, TM), lambda i, z, nu: _dest_block(i), memory_space=pltpu.SMEM),
                  pl.BlockSpec(_tiles_shape(TM, d), lambda i, z, nu: (i, 0, 0, 0))],
        out_specs=pl.BlockSpec(memory_space=pl.ANY),
        scratch_shapes=[pltpu.VMEM(_tiles_shape(BM, d), h2.dtype),
                        pltpu.SemaphoreType.DMA(()),
                        pltpu.SemaphoreType.DMA(())],
    )
    return pl.pallas_call(
        functools.partial(_dispatch_kernel, n_blk=n_blk),
        grid_spec=grid_spec,
        out_shape=jax.ShapeDtypeStruct(_tiles_shape(n_rows, d), h2.dtype),
        compiler_params=_cparams(("arbitrary",)),
        name="dispatch",
    )(zrow, n_used, dest, h2)


def _experts_kernel(blk_e_ref, n_used_ref, x_ref, wg_ref, wu_ref, wd_ref, y_ref, wg_bf, wu_bf, wd_bf):
    j = pl.program_id(0)

    @pl.when((j == 0) | (blk_e_ref[j] != blk_e_ref[jnp.maximum(j - 1, 0)]))
    def _():
        wg_bf[...] = wg_ref[0].astype(wg_bf.dtype)
        wu_bf[...] = wu_ref[0].astype(wu_bf.dtype)
        wd_bf[...] = wd_ref[0].astype(wd_bf.dtype)

    @pl.when(j < n_used_ref[0])
    def _():
        x = _from_tiles(x_ref).astype(wg_bf.dtype)
        a = jnp.dot(x, wg_bf[...], preferred_element_type=jnp.float32)
        b = jnp.dot(x, wu_bf[...], preferred_element_type=jnp.float32)
        hmid = (a / (1.0 + jnp.exp(-a)) * b).astype(x.dtype)
        _to_tiles(y_ref, jnp.dot(hmid, wd_bf[...], preferred_element_type=jnp.float32))

    @pl.when(j >= n_used_ref[0])
    def _():
        y_ref[...] = jnp.zeros(y_ref.shape, y_ref.dtype)


def _experts(xs, wg, wu, wd, blk_e, n_used):
    n_blk = blk_e.shape[0]
    d, d_e = wg.shape[1], wg.shape[2]
    blk = _tiles_shape(BM, d)
    grid_spec = pltpu.PrefetchScalarGridSpec(
        num_scalar_prefetch=2,
        grid=(n_blk,),
        in_specs=[pl.BlockSpec(blk, lambda j, be, nu: (jnp.minimum(j, nu[0] - 1), 0, 0, 0)),
                  pl.BlockSpec((1, d, d_e), lambda j, be, nu: (be[j], 0, 0)),
                  pl.BlockSpec((1, d, d_e), lambda j, be, nu: (be[j], 0, 0)),
                  pl.BlockSpec((1, d_e, d), lambda j, be, nu: (be[j], 0, 0))],
        out_specs=pl.BlockSpec(blk, lambda j, be, nu: (j, 0, 0, 0)),
        scratch_shapes=[pltpu.VMEM((d, d_e), jnp.bfloat16), pltpu.VMEM((d, d_e), jnp.bfloat16),
                        pltpu.VMEM((d_e, d), jnp.bfloat16)],
    )
    return pl.pallas_call(
        _experts_kernel,
        grid_spec=grid_spec,
        out_shape=jax.ShapeDtypeStruct(_tiles_shape(n_blk * BM, d), jnp.float32),
        compiler_params=_cparams(("arbitrary",)),
        name="experts",
    )(blk_e, n_used, xs, wg, wu, wd)


def _final_kernel(dest_ref, dest_next_ref, x1_ref, route_ref, gt_ref, g_ref, ys_hbm, o_ref, ybuf, sem):
    i = pl.program_id(0)
    slot = i % 2

    tiles = TM // SUBLANES

    def gather(d_ref, s):
        def issue(k, c):
            for u in range(SUBLANES):
                r = k * SUBLANES + u
                pltpu.make_async_copy(ys_hbm.at[d_ref[0, 0, r], :, d_ref[0, 1, r], :],
                                      ybuf.at[s, k, :, u, :], sem.at[s]).start(priority=0)
                pltpu.make_async_copy(ys_hbm.at[d_ref[0, 2, r], :, d_ref[0, 3, r], :],
                                      ybuf.at[s, tiles + k, :, u, :], sem.at[s]).start(priority=1)
            return c
        lax.fori_loop(0, tiles, issue, 0)

    @pl.when(i == 0)
    def _():
        gather(dest_ref, 0)

    @pl.when(i + 1 < pl.num_programs(0))
    def _():
        gather(dest_next_ref, 1 - slot)

    pltpu.make_async_copy(ys_hbm.at[pl.ds(0, 2 * tiles)], ybuf.at[slot], sem.at[slot]).wait()
    w1 = route_ref[:, 2:3]
    w2 = route_ref[:, 3:4]
    y0 = _from_tiles(ybuf.at[slot, 0:tiles])
    y1 = _from_tiles(ybuf.at[slot, tiles:2 * tiles])
    x2 = x1_ref[...] + gt_ref[0] * (w1 * y0 + w2 * y1)
    o_ref[...] = x2 * lax.rsqrt(jnp.mean(x2 * x2, axis=-1, keepdims=True) + EPS) * g_ref[...]


def _final(x1, ys, dest, route, mod3, g_final, seq):
    n, d = x1.shape
    n_tiles = n // TM
    tiles_per_seq = seq // TM
    row = lambda i: (i, 0)
    return pl.pallas_call(
        _final_kernel,
        grid=(n_tiles,),
        in_specs=[pl.BlockSpec((1, 8, TM), _dest_block, memory_space=pltpu.SMEM),
                  pl.BlockSpec((1, 8, TM), lambda i: _dest_block(jnp.minimum(i + 1, n_tiles - 1)),
                               memory_space=pltpu.SMEM),
                  pl.BlockSpec((TM, d), row),
                  pl.BlockSpec((TM, LANES), row),
                  pl.BlockSpec((1, 1, d), lambda i: ((i // tiles_per_seq) * 6 + 5, 0, 0)),
                  pl.BlockSpec((1, d), lambda i: (0, 0)),
                  pl.BlockSpec(memory_space=pl.ANY)],
        out_specs=pl.BlockSpec((TM, d), row),
        out_shape=jax.ShapeDtypeStruct((n, d), jnp.float32),
        scratch_shapes=[pltpu.VMEM((2,) + _tiles_shape(2 * TM, d), jnp.float32),
                        pltpu.SemaphoreType.DMA((2,))],
        compiler_params=_cparams(("arbitrary",)),
        name="final",
    )(dest, dest, x1, route, mod3, g_final, ys)


def _block_layout(counts, n):
    cnt = counts[0, :N_EXPERTS].astype(jnp.int32)
    blocks = (cnt + BM - 1) // BM
    bends = jnp.cumsum(blocks)
    pstarts = (bends - blocks) * BM
    n_blk = (2 * n) // BM + N_EXPERTS
    pst_row = jnp.zeros((1, LANES), jnp.float32).at[0, :N_EXPERTS].set(pstarts.astype(jnp.float32))
    zrow = ((pstarts + cnt) // SUBLANES).astype(jnp.int32)
    n_used = bends[-1:].astype(jnp.int32)
    blk_e = jnp.sum(bends[None, :] <= jnp.arange(n_blk, dtype=jnp.int32)[:, None], axis=1)
    blk_e = jnp.minimum(blk_e, N_EXPERTS - 1).astype(jnp.int32)
    return pst_row, zrow, n_used, blk_e


def _layer(x2, c, pos2, w_ada, b_ada, g_mix, w_in, w_o, g_ffn, w_rg, w_re, w_up, w_gate, w_down, seq):
    n, d = x2.shape
    bsz = n // seq
    bf = jnp.bfloat16
    d_a = d // 2
    d_i = N_IDX_HEADS * IDX_DIM
    mod3 = _ada(c, w_ada, b_ada).reshape(bsz * 6, 1, d)

    c0 = 3 * d_a + d_i
    c1 = c0 + IDX_DIM + N_IDX_HEADS
    w_cat = jnp.concatenate([w_in[:, :c0], w_in[:, c0:c1],
                             jnp.zeros((d, LANES - (c1 - c0)), w_in.dtype), w_in[:, c1:]], axis=1).astype(bf)
    qat, ka, vat, qit, kiw, wit, qb, kb, vb, gb = _proj(x2, pos2, mod3, g_mix.reshape(1, d), w_cat, seq)

    n_kt = seq // TM
    ya = _dsa(qat, qit, wit, kiw.reshape(bsz, n_kt, TM, LANES), ka.reshape(bsz, n_kt, TM, d_a), vat, seq)
    yb = _ret(qb, kb, vb, gb, seq)

    w_route = jnp.concatenate([jnp.transpose(w_re, (1, 0, 2)).reshape(d, N_EXPERTS), w_rg,
                               jnp.zeros((d, LANES - N_EXPERTS - N_GROUPS), w_rg.dtype)], axis=1).astype(bf)
    x1, h2, route, counts = _mix_out(x2, ya, yb, w_o[:d_a].astype(bf), w_o[d_a:].astype(bf), mod3,
                                     g_ffn.reshape(1, d), w_route, seq)

    pst_row, zrow, n_used, blk_e = _block_layout(counts, n)
    dest = _plan(route, pst_row)
    xs = _dispatch(h2, dest, zrow, n_used, blk_e.shape[0])
    ys = _experts(xs, w_gate, w_up, w_down, blk_e, n_used)
    return x1, ys, dest, route, mod3


def kernel(x, c, positions, w_ada, b_ada, g_norm_mix, w_in, w_o, g_norm_ffn, w_router_group, w_router_expert,
           w_up, w_gate, w_down, g_norm_final):
    bsz, seq, d = x.shape
    depth = w_ada.shape[0]
    assert depth == 1, "the final norm is fused into the last layer's combine kernel"
    assert seq % (2 * TM) == 0 and seq % (RET_C * RET_CHUNKS) == 0 and seq % MIX_TM == 0 and (2 * bsz * seq) % BM == 0
    assert (bsz * seq) % PLAN_TM == 0 and PLAN_TM % TM == 0
    x2 = x.reshape(bsz * seq, d)
    pos2 = positions.astype(jnp.float32).reshape(bsz * seq, 1)
    x1, ys, dest, route, mod3 = _layer(x2, c, pos2, w_ada[0], b_ada[0], g_norm_mix[0], w_in[0], w_o[0],
                                       g_norm_ffn[0], w_router_group[0], w_router_expert[0], w_up[0], w_gate[0],
                                       w_down[0], seq)
    out = _final(x1, ys, dest, route, mod3, g_norm_final.reshape(1, d), seq)
    return out.reshape(bsz, seq, d)
```

```python
import functools

import jax
import jax.numpy as jnp
import numpy as np
from jax import lax
from jax.experimental import pallas as pl
from jax.experimental.pallas import tpu as pltpu

CHUNK = 64
HEAD_DIM = 64
N_IDX_HEADS = 16
IDX_DIM = 64
TOPK_MAX = 256
ROPE_THETA = 500000.0
ROT_DIM = HEAD_DIM // 4
RET_THETA = 10000.0
N_GROUPS = 4
EXPERTS_PER_GROUP = 8
N_EXPERTS = N_GROUPS * EXPERTS_PER_GROUP
EPS = 1e-6

LANES = 128
SUBLANES = 8
VMEM_LIMIT = 56 * 1024 * 1024

TM = 256
QB = TM
RET_C = 256
RET_CHUNKS = 2
BM = 256
PLAN_TM = 1024
MIX_TM = 512
CNT_ROWS = 64
ONES_ROWS = 16
HEAD_GROUP = 4
IDX_TILES = 8
ATT_TILES = 8
BISECT_PER_CHECK = 2
MAX_BISECT = 40

NEG_BIG = -1e30
LOG2E = 1.4426950408889634


def _cparams(sem):
    return pltpu.CompilerParams(dimension_semantics=sem, vmem_limit_bytes=VMEM_LIMIT)


def _ada_kernel(c_ref, w_ref, b_ref, o_ref):
    o_ref[...] = jnp.dot(c_ref[...], w_ref[...], preferred_element_type=jnp.float32) + b_ref[...]


def _ada(c, w_ada, b_ada):
    bsz, d = c.shape
    n_out = w_ada.shape[1]
    return pl.pallas_call(
        _ada_kernel,
        grid=(n_out // d,),
        in_specs=[pl.BlockSpec((bsz, d), lambda j: (0, 0)),
                  pl.BlockSpec((d, d), lambda j: (0, j)),
                  pl.BlockSpec((1, d), lambda j: (0, j))],
        out_specs=pl.BlockSpec((bsz, d), lambda j: (0, j)),
        out_shape=jax.ShapeDtypeStruct((bsz, n_out), jnp.float32),
        compiler_params=_cparams(("arbitrary",)),
        name="ada",
    )(c, w_ada, b_ada.reshape(1, n_out))


def _rmsnorm_mod(x, g, sc, sh):
    xn = x * lax.rsqrt(jnp.mean(x * x, axis=-1, keepdims=True) + EPS)
    return xn * g * (1.0 + sc) + sh


def _rope_lanes(x, cos, sin_lo, sin_hi, half):
    cols = []
    for k in range(x.shape[1] // LANES):
        xb = x[:, k * LANES:(k + 1) * LANES]
        cols.append(xb * cos + pltpu.roll(xb, LANES - half, 1) * sin_lo + pltpu.roll(xb, half, 1) * sin_hi)
    return cols[0] if len(cols) == 1 else jnp.concatenate(cols, axis=1)


def _proj_kernel(x_ref, pos_ref, sc_ref, sh_ref, g_ref, w_ref, tab_ref,
                 qat_ref, ka_ref, vat_ref, qit_ref, kiw_ref, wit_ref, qb_ref, kb_ref, vb_ref, gb_ref,
                 *, d_a, d_i, d_b):
    h = _rmsnorm_mod(x_ref[...], g_ref[...], sc_ref[0], sh_ref[0]).astype(jnp.bfloat16)
    pos = pos_ref[...]
    ang_a = pos * tab_ref[0:1, :]
    cos_a, sin_a = jnp.cos(ang_a), jnp.sin(ang_a)
    sa_lo, sa_hi = sin_a * tab_ref[1:2, :], sin_a * tab_ref[2:3, :]
    ang_b = pos * tab_ref[3:4, :]
    cos_b, sin_b = jnp.cos(ang_b), jnp.sin(ang_b)
    sb_lo, sb_hi = sin_b * tab_ref[4:5, :], sin_b * tab_ref[5:6, :]
    half_a, half_b = ROT_DIM // 2, HEAD_DIM // 2

    def seg(lo, width):
        return jnp.dot(h, w_ref[:, lo:lo + width], preferred_element_type=jnp.float32)

    o = 0
    qa = seg(o, d_a); o += d_a
    qat_ref[0, 0] = (_rope_lanes(qa, cos_a, sa_lo, sa_hi, half_a) * (HEAD_DIM ** -0.5 * LOG2E)).T.astype(qat_ref.dtype)
    ka = seg(o, d_a); o += d_a
    ka_ref[...] = _rope_lanes(ka, cos_a, sa_lo, sa_hi, half_a).astype(ka_ref.dtype)
    vat_ref[0, 0] = seg(o, d_a).T.astype(vat_ref.dtype); o += d_a
    qi = seg(o, d_i); o += d_i
    qit_ref[0, 0] = (_rope_lanes(qi, cos_a, sa_lo, sa_hi, half_a) * (IDX_DIM ** -0.5)).T.astype(qit_ref.dtype)
    kw = seg(o, LANES); o += LANES
    kiw_ref[...] = _rope_lanes(kw, cos_a, sa_lo, sa_hi, half_a).astype(kiw_ref.dtype)
    wit_ref[0, 0] = kw.T[IDX_DIM:IDX_DIM + N_IDX_HEADS, :] * (N_IDX_HEADS ** -0.5)
    qb = seg(o, d_b); o += d_b
    qb_ref[...] = _rope_lanes(qb, cos_b, sb_lo, sb_hi, half_b).astype(qb_ref.dtype)
    kb = seg(o, d_b); o += d_b
    kb_ref[...] = (_rope_lanes(kb, cos_b, sb_lo, sb_hi, half_b) * (HEAD_DIM ** -0.5)).astype(kb_ref.dtype)
    vb_ref[...] = seg(o, d_b).astype(vb_ref.dtype); o += d_b
    gb_ref[...] = seg(o, d_b)


def _rope_tables():
    lane = jnp.arange(LANES) % HEAD_DIM
    rows = []
    for rot, theta in ((ROT_DIM, ROPE_THETA), (HEAD_DIM, RET_THETA)):
        half = rot // 2
        inv_freq = theta ** (-jnp.arange(half, dtype=jnp.float32) / half)
        rows.append(jnp.where(lane < rot, inv_freq[lane % half], 0.0))
        rows.append(jnp.where(lane < half, -1.0, 0.0))
        rows.append(jnp.where((lane >= half) & (lane < rot), 1.0, 0.0))
    rows += [jnp.zeros((LANES,), jnp.float32)] * 2
    return jnp.stack(rows).astype(jnp.float32)


def _proj(x2, pos2, mod3, g_mix, w_cat, seq):
    n, d = x2.shape
    bsz = n // seq
    d_a = d // 2
    d_b = d // 2
    d_i = N_IDX_HEADS * IDX_DIM
    tiles_per_seq = seq // TM
    tab = _rope_tables()
    row = lambda i: (i, 0)
    const = lambda i: (0, 0)
    tile4 = lambda i: (i // tiles_per_seq, i % tiles_per_seq, 0, 0)
    bf = jnp.bfloat16
    out_shape = (
        jax.ShapeDtypeStruct((bsz, tiles_per_seq, d_a, TM), bf),
        jax.ShapeDtypeStruct((n, d_a), bf),
        jax.ShapeDtypeStruct((bsz, tiles_per_seq, d_a, TM), bf),
        jax.ShapeDtypeStruct((bsz, tiles_per_seq, d_i, TM), bf),
        jax.ShapeDtypeStruct((n, LANES), bf),
        jax.ShapeDtypeStruct((bsz, tiles_per_seq, N_IDX_HEADS, TM), jnp.float32),
        jax.ShapeDtypeStruct((n, d_b), bf),
        jax.ShapeDtypeStruct((n, d_b), bf),
        jax.ShapeDtypeStruct((n, d_b), bf),
        jax.ShapeDtypeStruct((n, d_b), jnp.float32),
    )
    out_specs = (
        pl.BlockSpec((1, 1, d_a, TM), tile4),
        pl.BlockSpec((TM, d_a), row),
        pl.BlockSpec((1, 1, d_a, TM), tile4),
        pl.BlockSpec((1, 1, d_i, TM), tile4),
        pl.BlockSpec((TM, LANES), row),
        pl.BlockSpec((1, 1, N_IDX_HEADS, TM), tile4),
        pl.BlockSpec((TM, d_b), row),
        pl.BlockSpec((TM, d_b), row),
        pl.BlockSpec((TM, d_b), row),
        pl.BlockSpec((TM, d_b), row),
    )
    return pl.pallas_call(
        functools.partial(_proj_kernel, d_a=d_a, d_i=d_i, d_b=d_b),
        grid=(n // TM,),
        in_specs=[pl.BlockSpec((TM, d), row),
                  pl.BlockSpec((TM, 1), row),
                  pl.BlockSpec((1, 1, d), lambda i: ((i // tiles_per_seq) * 6 + 1, 0, 0)),
                  pl.BlockSpec((1, 1, d), lambda i: ((i // tiles_per_seq) * 6 + 0, 0, 0)),
                  pl.BlockSpec((1, d), const),
                  pl.BlockSpec(w_cat.shape, const),
                  pl.BlockSpec(tab.shape, const)],
        out_specs=out_specs,
        out_shape=out_shape,
        compiler_params=_cparams(("arbitrary",)),
        name="proj",
    )(x2, pos2, mod3, mod3, g_mix, w_cat, tab)


def _row_blocks(x, rows):
    return [x[r * rows:(r + 1) * rows] for r in range(x.shape[0] // rows)]


def _dsa_kernel(qat_ref, qit_ref, wit_ref, kiw_ref, ka_ref, vat_ref, tri_ref, o_ref,
                qix_ref, qmx_ref, sc_ref, m_ref, l_ref, acc_ref, sa_ref, sb_ref, mxa_ref, mxb_ref, lohi_ref,
                *, k_top, n_heads):
    i = pl.program_id(1)
    n_grp = (i * QB + QB + 2 * TM - 1) // (2 * TM)
    n_kt = 2 * n_grp
    n_real = (i * QB + QB + TM - 1) // TM

    zero_rows = jnp.zeros((LANES - IDX_DIM, QB), qix_ref.dtype)
    for h in range(N_IDX_HEADS):
        qix_ref[h] = jnp.concatenate([qit_ref[0, 0, h * IDX_DIM:(h + 1) * IDX_DIM, :], zero_rows], axis=0)
    row_q = lax.broadcasted_iota(jnp.int32, (LANES, QB), 0)
    for h in range(n_heads):
        pair = qat_ref[0, 0, (h // 2) * LANES:(h // 2 + 1) * LANES, :]
        own = (row_q < HEAD_DIM) if h % 2 == 0 else (row_q >= HEAD_DIM)
        qmx_ref[h] = jnp.where(own, pair, jnp.zeros_like(pair))

    q_chunk = (i * QB + lax.broadcasted_iota(jnp.int32, (1, QB), 1)) // CHUNK
    key_chunk_in_tile = lax.broadcasted_iota(jnp.int32, (TM, QB), 0) // CHUNK
    w_all = wit_ref[0, 0]

    def idx_tiles(tiles):
        lo, hi = lohi_ref[0], lohi_ref[1]
        for j in tiles:
            kt = kiw_ref[0, j]
            acc = None
            for h in range(N_IDX_HEADS):
                d = jnp.dot(kt, qix_ref[h], preferred_element_type=jnp.float32)
                t = w_all[h:h + 1, :] * jnp.maximum(d, 0.0)
                acc = t if acc is None else acc + t
            adm = key_chunk_in_tile <= q_chunk - j * (TM // CHUNK)
            s = jnp.where(adm, acc, -jnp.inf)
            sc_ref[j] = s
            lo = jnp.minimum(lo, functools.reduce(jnp.minimum, _row_blocks(jnp.where(adm, acc, jnp.inf), SUBLANES)))
            hi = jnp.maximum(hi, functools.reduce(jnp.maximum, _row_blocks(s, SUBLANES)))
        lohi_ref[0], lohi_ref[1] = lo, hi

    lohi_ref[0] = jnp.full((SUBLANES, QB), jnp.inf, jnp.float32)
    lohi_ref[1] = jnp.full((SUBLANES, QB), -jnp.inf, jnp.float32)

    def idx_step(g, carry):
        idx_tiles([IDX_TILES * g + u for u in range(IDX_TILES)])
        return carry

    lax.fori_loop(0, n_kt // IDX_TILES, idx_step, 0)

    def idx_tail(p, carry):
        first = n_kt // IDX_TILES * IDX_TILES + 2 * p
        idx_tiles([first, first + 1])
        return carry

    lax.fori_loop(0, n_kt % IDX_TILES // 2, idx_tail, 0)

    lo = jnp.min(lohi_ref[0], axis=0, keepdims=True)
    hi = jnp.max(lohi_ref[1], axis=0, keepdims=True)
    n_adm = ((i * QB + lax.broadcasted_iota(jnp.int32, (1, QB), 1)) // CHUNK + 1) * CHUNK

    def bisect_cond(carry):
        it, _, _, _, unsettled = carry
        return (it < MAX_BISECT) & (unsettled > 0.0)

    def bisect(carry):
        it, lo, hi, cnt_lo, _ = carry
        for _ in range(BISECT_PER_CHECK):
            mid = lo + (hi - lo) * 0.5
            mid_b = jnp.broadcast_to(mid, (CNT_ROWS, QB))

            def count_tile(j, cnt, mid_b=mid_b):
                for blk in _row_blocks(sc_ref[j], CNT_ROWS):
                    cnt = cnt + jnp.where(blk >= mid_b, 1.0, 0.0)
                return cnt

            cnt = lax.fori_loop(0, n_real, count_tile, jnp.zeros((CNT_ROWS, QB), jnp.float32))
            c = jnp.sum(cnt, axis=0, keepdims=True)
            ge = c >= k_top
            lo = jnp.where(ge, mid, lo)
            hi = jnp.where(ge, hi, mid)
            cnt_lo = jnp.where(ge, c, cnt_lo)
        return it + BISECT_PER_CHECK, lo, hi, cnt_lo, jnp.max(jnp.where(cnt_lo > k_top, 1.0, 0.0))

    cnt0 = n_adm.astype(jnp.float32)
    _, thr, thr_hi, _, unsettled = lax.while_loop(bisect_cond, bisect,
                                                  (0, lo, hi, cnt0, jnp.max(jnp.where(cnt0 > k_top, 1.0, 0.0))))

    @pl.when(unsettled <= 0.0)
    def _():
        def bias_tile(j, carry):
            sc_ref[j] = jnp.where(sc_ref[j] >= thr, 0.0, NEG_BIG)
            return carry

        lax.fori_loop(0, n_kt, bias_tile, 0)

    @pl.when(unsettled > 0.0)
    def _():
        def count_hi(j, cnt):
            return cnt + functools.reduce(jnp.add, _row_blocks(jnp.where(sc_ref[j] >= thr_hi, 1.0, 0.0), SUBLANES))

        above = jnp.sum(lax.fori_loop(0, n_kt, count_hi, jnp.zeros((SUBLANES, QB), jnp.float32)),
                        axis=0, keepdims=True)
        top = jnp.where(above < k_top, thr_hi, jnp.inf)
        room = k_top - jnp.where(above < k_top, above, 0.0)

        def bias_tile(j, taken):
            s = sc_ref[j]
            tied = (s >= thr) & (s < top)
            before = jnp.dot(tri_ref[...], jnp.where(tied, 1.0, 0.0).astype(tri_ref.dtype),
                             preferred_element_type=jnp.float32)
            keep = (s >= top) | (tied & (taken + before <= room))
            sc_ref[j] = jnp.where(keep, 0.0, NEG_BIG)
            return taken + before[TM - 1:TM, :]

        lax.fori_loop(0, n_kt, bias_tile, jnp.zeros((1, QB), jnp.float32))

    m_ref[...] = jnp.full(m_ref.shape, NEG_BIG, jnp.float32)
    l_ref[...] = jnp.zeros(l_ref.shape, jnp.float32)
    acc_ref[...] = jnp.zeros(acc_ref.shape, jnp.float32)

    def pair(h):
        return slice((h // 2) * LANES, (h // 2 + 1) * LANES)

    def logits_into(s_ref, mx_ref, j, heads):
        bias = sc_ref[j]
        for h in heads:
            s = jnp.dot(ka_ref[0, j, :, pair(h)], qmx_ref[h], preferred_element_type=jnp.float32) + bias
            s_ref[h] = s
            mx_ref[h] = jnp.max(s, axis=0, keepdims=True)

    ones_rows = jnp.ones((ONES_ROWS, TM), vat_ref.dtype)

    def absorb(s_ref, mx_ref, j, heads):
        for h in heads:
            m_old = m_ref[h]
            m_new = jnp.maximum(m_old, mx_ref[h])
            alpha = jnp.exp2(m_old - m_new)
            p = jnp.exp2(s_ref[h] - m_new).astype(vat_ref.dtype)
            pv = jnp.dot(jnp.concatenate([vat_ref[0, j, pair(h), :], ones_rows], axis=0), p,
                         preferred_element_type=jnp.float32)
            acc_ref[h] = acc_ref[h] * alpha + pv[0:LANES]
            l_ref[h] = l_ref[h] * alpha + pv[LANES:LANES + SUBLANES]
            m_ref[h] = m_new

    for h0 in range(0, n_heads, HEAD_GROUP):
        heads = range(h0, h0 + HEAD_GROUP)
        logits_into(sa_ref, mxa_ref, 0, heads)

        def tile_pair(t, heads=heads):
            logits_into(sb_ref, mxb_ref, t + 1, heads)
            absorb(sa_ref, mxa_ref, t, heads)
            logits_into(sa_ref, mxa_ref, jnp.minimum(t + 2, n_kt - 1), heads)
            absorb(sb_ref, mxb_ref, t + 1, heads)

        def attn_step(g, carry):
            for u in range(0, ATT_TILES, 2):
                tile_pair(ATT_TILES * g + u)
            return carry

        lax.fori_loop(0, n_kt // ATT_TILES, attn_step, 0)

        def attn_tail(p, carry):
            tile_pair(n_kt // ATT_TILES * ATT_TILES + 2 * p)
            return carry

        lax.fori_loop(0, n_kt % ATT_TILES // 2, attn_tail, 0)

    for hp in range(n_heads // 2):
        even = acc_ref[2 * hp] / l_ref[2 * hp, 0:1, :]
        odd = acc_ref[2 * hp + 1] / l_ref[2 * hp + 1, 0:1, :]
        o_ref[:, hp * LANES:(hp + 1) * LANES] = jnp.where(row_q < HEAD_DIM, even, odd).T.astype(o_ref.dtype)


def _dsa(qat, qit, wit, kiw4, ka4, vat, seq):
    bsz, n_kt, d_a, _ = qat.shape
    n_heads = d_a // HEAD_DIM
    n_qb = seq // QB
    k_top = min(TOPK_MAX, seq // 4)
    qtile = lambda b, i: (b, i, 0, 0)
    per_b = lambda b, i: (b, 0, 0, 0)
    f32 = jnp.float32
    return pl.pallas_call(
        functools.partial(_dsa_kernel, k_top=float(k_top), n_heads=n_heads),
        grid=(bsz, n_qb),
        in_specs=[pl.BlockSpec((1, 1, d_a, QB), qtile),
                  pl.BlockSpec((1, 1, qit.shape[2], QB), qtile),
                  pl.BlockSpec((1, 1, N_IDX_HEADS, QB), qtile),
                  pl.BlockSpec((1, n_kt, TM, LANES), per_b, pipeline_mode=pl.Buffered(1)),
                  pl.BlockSpec((1, n_kt, TM, d_a), per_b, pipeline_mode=pl.Buffered(1)),
                  pl.BlockSpec((1, n_kt, d_a, TM), per_b, pipeline_mode=pl.Buffered(1)),
                  pl.BlockSpec((TM, TM), lambda b, i: (0, 0), pipeline_mode=pl.Buffered(1))],
        out_specs=pl.BlockSpec((QB, d_a), lambda b, i: (b * n_qb + i, 0)),
        out_shape=jax.ShapeDtypeStruct((bsz * seq, d_a), jnp.bfloat16),
        scratch_shapes=[pltpu.VMEM((N_IDX_HEADS, LANES, QB), jnp.bfloat16),
                        pltpu.VMEM((n_heads, LANES, QB), jnp.bfloat16),
                        pltpu.VMEM((n_kt, TM, QB), f32),
                        pltpu.VMEM((n_heads, 1, QB), f32),
                        pltpu.VMEM((n_heads, SUBLANES, QB), f32),
                        pltpu.VMEM((n_heads, LANES, QB), f32),
                        pltpu.VMEM((n_heads, TM, QB), f32),
                        pltpu.VMEM((n_heads, TM, QB), f32),
                        pltpu.VMEM((n_heads, 1, QB), f32),
                        pltpu.VMEM((n_heads, 1, QB), f32),
                        pltpu.VMEM((2, SUBLANES, QB), f32)],
        compiler_params=_cparams(("arbitrary", "arbitrary")),
        name="dsa",
    )(qat, qit, wit, kiw4, ka4, vat, jnp.asarray(np.tril(np.ones((TM, TM), np.float32)), jnp.bfloat16))


def _group_mean(y, avg):
    hi = y.astype(jnp.bfloat16)
    lo = (y - hi.astype(jnp.float32)).astype(jnp.bfloat16)
    return (jnp.dot(hi, avg, preferred_element_type=jnp.float32)
            + jnp.dot(lo, avg, preferred_element_type=jnp.float32))


def _ret_kernel(q_ref, k_ref, v_ref, g_ref, dec_ref, zt_ref, xi_ref, gc_ref, blk_ref, avg_ref, o_ref, st_ref,
                *, n_heads):
    @pl.when(pl.program_id(1) == 0)
    def _():
        st_ref[...] = jnp.zeros_like(st_ref)

    even = lax.broadcasted_iota(jnp.int32, (RET_C, LANES), 1) < HEAD_DIM
    avg = avg_ref[...]
    for p in range(n_heads // 2):
        sl = slice(p * LANES, (p + 1) * LANES)
        state = st_ref[p]
        for c in range(q_ref.shape[0] // RET_C):
            rows = slice(c * RET_C, (c + 1) * RET_C)
            qp, kp, vp = q_ref[rows, sl], k_ref[rows, sl], v_ref[rows, sl]
            kpt = kp.astype(jnp.float32).T
            kpt_b = kpt.astype(kp.dtype)
            inner = None
            for e in range(2):
                q_e = jnp.where(even if e == 0 else jnp.logical_not(even), qp, jnp.zeros_like(qp))
                s = jnp.dot(q_e, kpt_b, preferred_element_type=jnp.float32) * dec_ref[2 * p + e]
                t = jnp.dot(s.astype(vp.dtype), vp, preferred_element_type=jnp.float32)
                inner = t if e == 0 else jnp.where(even, inner, t)
            cross = jnp.dot(qp, state.astype(qp.dtype), preferred_element_type=jnp.float32) * xi_ref[p]
            y = inner + cross
            yc = y - _group_mean(y, avg)
            yn = yc * lax.rsqrt(_group_mean(yc * yc, avg) + EPS)
            g = g_ref[rows, sl]
            o_ref[rows, sl] = (g / (1.0 + jnp.exp(-g)) * yn).astype(o_ref.dtype)
            kz = (kpt * zt_ref[p]).astype(kp.dtype)
            kv = jnp.dot(kz, vp, preferred_element_type=jnp.float32)
            state = state * gc_ref[p] + kv * blk_ref[...]
        st_ref[p] = state


def _ret_consts(n_heads):
    log_gamma = jnp.log1p(-jnp.exp2(-5.0 - jnp.arange(n_heads, dtype=jnp.float32)))
    pos = jnp.arange(RET_C, dtype=jnp.float32)
    diff = pos[:, None] - pos[None, :]
    dec = jnp.where(diff[None] >= 0, jnp.exp(jnp.maximum(diff, 0.0)[None] * log_gamma[:, None, None]), 0.0)
    zeta = jnp.exp((RET_C - 1.0 - pos)[None, :] * log_gamma[:, None])
    xi = jnp.exp((pos + 1.0)[None, :] * log_gamma[:, None])
    gc = jnp.exp(RET_C * log_gamma)
    n_pairs = n_heads // 2
    lanes = lambda a: jnp.repeat(a.reshape(n_pairs, 2, -1), HEAD_DIM, axis=1)
    zt = lanes(zeta)
    xi_p = jnp.swapaxes(lanes(xi), 1, 2)
    gc_p = jnp.broadcast_to(lanes(gc[:, None]), (n_pairs, LANES, LANES))
    head_of = jnp.arange(LANES) // HEAD_DIM
    blk = (head_of[:, None] == head_of[None, :]).astype(jnp.float32)
    avg = (blk / HEAD_DIM).astype(jnp.bfloat16)
    f32 = lambda a: a.astype(jnp.float32)
    return dec, f32(zt), f32(xi_p), f32(gc_p), blk, avg


def _ret(qb, kb, vb, gb, seq):
    n, d_b = qb.shape
    bsz = n // seq
    n_heads = d_b // HEAD_DIM
    step = RET_C * RET_CHUNKS
    n_c = seq // step
    consts = _ret_consts(n_heads)
    row = lambda b, c: (b * n_c + c, 0)
    const_spec = lambda a: pl.BlockSpec(a.shape, lambda b, c: (0,) * a.ndim)
    return pl.pallas_call(
        functools.partial(_ret_kernel, n_heads=n_heads),
        grid=(bsz, n_c),
        in_specs=[pl.BlockSpec((step, d_b), row)] * 4 + [const_spec(a) for a in consts],
        out_specs=pl.BlockSpec((step, d_b), row),
        out_shape=jax.ShapeDtypeStruct((n, d_b), jnp.bfloat16),
        scratch_shapes=[pltpu.VMEM((n_heads // 2, LANES, LANES), jnp.float32)],
        compiler_params=_cparams(("arbitrary", "arbitrary")),
        name="ret",
    )(qb, kb, vb, gb, *consts)


def _tiles_shape(rows, d):
    return (rows // SUBLANES, d // LANES, SUBLANES, LANES)


def _rows_shape(rows, d):
    return (rows // SUBLANES, SUBLANES, d // LANES, LANES)


def _to_tiles(ref, x):
    for s in range(ref.shape[1]):
        ref[:, s] = x[:, s * LANES:(s + 1) * LANES].reshape(ref.shape[0], SUBLANES, LANES)


def _from_tiles(ref):
    rows = ref.shape[0] * SUBLANES
    return jnp.concatenate([ref[:, s].reshape(rows, LANES) for s in range(ref.shape[1])], axis=1)


def _lane_first_eq(x, m, lane):
    return jnp.min(jnp.where(x == m, lane, float(LANES)), axis=1, keepdims=True)


def _mix_out_kernel(x_ref, ya_ref, yb_ref, woa_ref, wob_ref, gt_ref, sc_ref, sh_ref, g_ref, wr_ref, tri_ref,
                    x1_ref, h2_ref, route_ref, cnt_ref, carry_ref):
    @pl.when(pl.program_id(0) == 0)
    def _():
        carry_ref[...] = jnp.zeros_like(carry_ref)

    mix = (jnp.dot(ya_ref[...], woa_ref[...], preferred_element_type=jnp.float32)
           + jnp.dot(yb_ref[...], wob_ref[...], preferred_element_type=jnp.float32))
    x1 = x_ref[...] + gt_ref[0] * mix
    x1_ref[...] = x1
    h2 = _rmsnorm_mod(x1, g_ref[...], sc_ref[0], sh_ref[0])
    _to_tiles(h2_ref, h2)

    lg = jnp.dot(h2.astype(jnp.bfloat16), wr_ref[...], preferred_element_type=jnp.float32)
    lane = lax.broadcasted_iota(jnp.int32, lg.shape, 1).astype(jnp.float32)
    is_grp = (lane >= N_EXPERTS) & (lane < N_EXPERTS + N_GROUPS)
    gl = jnp.where(is_grp, lg, -jnp.inf)
    gmax = jnp.max(gl, axis=1, keepdims=True)
    grp = _lane_first_eq(gl, gmax, lane) - N_EXPERTS
    p_grp = 1.0 / jnp.sum(jnp.exp(gl - gmax), axis=1, keepdims=True)
    in_grp = jnp.floor(lane * (1.0 / EXPERTS_PER_GROUP)) == grp
    f = jnp.where(in_grp & (lane < N_EXPERTS), lg, -jnp.inf)
    f1 = jnp.max(f, axis=1, keepdims=True)
    e1 = _lane_first_eq(f, f1, lane)
    f = jnp.where(lane == e1, -jnp.inf, f)
    f2 = jnp.max(f, axis=1, keepdims=True)
    e2 = _lane_first_eq(f, f2, lane)
    a2 = jnp.exp(f2 - f1)
    w1 = p_grp / (1.0 + a2)
    w2 = p_grp * a2 / (1.0 + a2)

    oh1 = jnp.where(lane == e1, 1.0, 0.0)
    oh2 = jnp.where(lane == e2, 1.0, 0.0)
    both = oh1 + oh2
    before = jnp.dot(tri_ref[...], both.astype(jnp.bfloat16), preferred_element_type=jnp.float32) + carry_ref[...]
    r1 = jnp.sum(before * oh1, axis=1, keepdims=True)
    r2 = jnp.sum(before * oh2, axis=1, keepdims=True)
    carry = carry_ref[...] + jnp.sum(both, axis=0, keepdims=True)
    carry_ref[...] = carry
    cnt_ref[...] = carry

    out = jnp.zeros(lg.shape, jnp.float32)
    for col, val in enumerate((e1, e2, w1, w2, r1, r2)):
        out = jnp.where(lane == col, val, out)
    route_ref[...] = out


def _mix_out(x2, ya, yb, wo_a, wo_b, mod3, g_ffn, w_route, seq):
    n, d = x2.shape
    tm = MIX_TM
    tiles_per_seq = seq // tm
    tri = jnp.asarray(np.tril(np.ones((tm, tm), np.float32), -1), jnp.bfloat16)
    row = lambda i: (i, 0)
    const = lambda i: (0, 0)
    modk = lambda k: pl.BlockSpec((1, 1, d), lambda i: ((i // tiles_per_seq) * 6 + k, 0, 0))
    return pl.pallas_call(
        _mix_out_kernel,
        grid=(n // tm,),
        in_specs=[pl.BlockSpec((tm, d), row),
                  pl.BlockSpec((tm, ya.shape[1]), row),
                  pl.BlockSpec((tm, yb.shape[1]), row),
                  pl.BlockSpec(wo_a.shape, const),
                  pl.BlockSpec(wo_b.shape, const),
                  modk(2), modk(4), modk(3),
                  pl.BlockSpec((1, d), const),
                  pl.BlockSpec(w_route.shape, const),
                  pl.BlockSpec(tri.shape, const)],
        out_specs=(pl.BlockSpec((tm, d), row), pl.BlockSpec(_tiles_shape(tm, d), lambda i: (i, 0, 0, 0)),
                   pl.BlockSpec((tm, LANES), row), pl.BlockSpec((1, LANES), const)),
        out_shape=(jax.ShapeDtypeStruct((n, d), jnp.float32), jax.ShapeDtypeStruct(_tiles_shape(n, d), jnp.float32),
                   jax.ShapeDtypeStruct((n, LANES), jnp.float32), jax.ShapeDtypeStruct((1, LANES), jnp.float32)),
        scratch_shapes=[pltpu.VMEM((1, LANES), jnp.float32)],
        compiler_params=_cparams(("arbitrary",)),
        name="mix_out",
    )(x2, ya, yb, wo_a, wo_b, mod3, mod3, mod3, g_ffn, w_route, tri)


def _plan_kernel(route_ref, pst_ref, dest_ref):
    r = route_ref[...]
    lane = lax.broadcasted_iota(jnp.int32, r.shape, 1).astype(jnp.float32)
    pst = pst_ref[...]
    d1 = jnp.sum(jnp.where(lane == r[:, 0:1], pst, 0.0), axis=1, keepdims=True) + r[:, 4:5]
    d2 = jnp.sum(jnp.where(lane == r[:, 1:2], pst, 0.0), axis=1, keepdims=True) + r[:, 5:6]
    t1 = jnp.floor(d1 * (1.0 / SUBLANES))
    t2 = jnp.floor(d2 * (1.0 / SUBLANES))
    packed = jnp.zeros(r.shape, jnp.float32)
    for k, v in enumerate((t1, d1 - t1 * SUBLANES, t2, d2 - t2 * SUBLANES)):
        packed = jnp.where(lane == float(k), v, packed)
    dest_ref[0] = packed.T[0:8, :].astype(jnp.int32)


def _dest_block(i):
    return (i // (PLAN_TM // TM), 0, i % (PLAN_TM // TM))


def _plan(route, pst_row):
    n = route.shape[0]
    return pl.pallas_call(
        _plan_kernel,
        grid=(n // PLAN_TM,),
        in_specs=[pl.BlockSpec((PLAN_TM, LANES), lambda i: (i, 0)), pl.BlockSpec((1, LANES), lambda i: (0, 0))],
        out_specs=pl.BlockSpec((1, 8, PLAN_TM), lambda i: (i, 0, 0)),
        out_shape=jax.ShapeDtypeStruct((n // PLAN_TM, 8, PLAN_TM), jnp.int32),
        compiler_params=_cparams(("arbitrary",)),
        name="plan",
    )(route, pst_row)


def _dispatch_kernel(zrow_ref, n_used_ref, dest_ref, h2_ref, xs_hbm, zbuf, sem, zsem, *, n_blk):
    i = pl.program_id(0)

    @pl.when(i == 0)
    def _():
        zbuf[...] = jnp.zeros(zbuf.shape, zbuf.dtype)
        blk_tiles = BM // SUBLANES
        for e in range(N_EXPERTS):
            pltpu.make_async_copy(zbuf, xs_hbm.at[pl.ds(zrow_ref[e], blk_tiles)], zsem).start()
        for e in range(N_EXPERTS):
            pltpu.make_async_copy(zbuf, xs_hbm.at[pl.ds(0, blk_tiles)], zsem).wait()
        for b in range(N_EXPERTS + 1):
            @pl.when(n_used_ref[0] + b <= n_blk)
            def _():
                tail = pltpu.make_async_copy(zbuf, xs_hbm.at[pl.ds((n_used_ref[0] + b) * blk_tiles, blk_tiles)], zsem)
                tail.start()
                tail.wait()

    def issue(k, c):
        for u in range(SUBLANES):
            r = k * SUBLANES + u
            row = h2_ref.at[k, :, u, :]
            pltpu.make_async_copy(row, xs_hbm.at[dest_ref[0, 0, r], dest_ref[0, 1, r]], sem).start(priority=0)
            pltpu.make_async_copy(row, xs_hbm.at[dest_ref[0, 2, r], dest_ref[0, 3, r]], sem).start(priority=1)
        return c

    lax.fori_loop(0, TM // SUBLANES, issue, 0)
    for _ in range(2 * TM // BM):
        pltpu.make_async_copy(zbuf, xs_hbm.at[pl.ds(0, BM // SUBLANES)], sem).wait()


def _dispatch(h2, dest, zrow, n_used, n_blk):
    n_rows = (n_blk + 1) * BM
    d = h2.shape[1] * LANES
    grid_spec = pltpu.PrefetchScalarGridSpec(
        num_scalar_prefetch=2,
        grid=(h2.shape[0] * SUBLANES // TM,),
        in_specs=[pl.BlockSpec((1, 8, TM), lambda i, z, nu: _dest_block(i), memory_space=pltpu.SMEM),
                  pl.BlockSpec(_tiles_shape(TM, d), lambda i, z, nu: (i, 0, 0, 0))],
        out_specs=pl.BlockSpec(memory_space=pl.ANY),
        scratch_shapes=[pltpu.VMEM(_rows_shape(BM, d), h2.dtype),
                        pltpu.SemaphoreType.DMA(()),
                        pltpu.SemaphoreType.DMA(())],
    )
    return pl.pallas_call(
        functools.partial(_dispatch_kernel, n_blk=n_blk),
        grid_spec=grid_spec,
        out_shape=jax.ShapeDtypeStruct(_rows_shape(n_rows, d), h2.dtype),
        compiler_params=_cparams(("arbitrary",)),
        name="dispatch",
    )(zrow, n_used, dest, h2)


def _experts_kernel(blk_e_ref, n_used_ref, xs_hbm, wg_ref, wu_ref, wd_ref, ys_hbm, wg_bf, wu_bf, wd_bf,
                    xbuf, ybuf, isem, osem):
    j = pl.program_id(0)
    last = pl.num_programs(0) - 1
    slot = j % 2
    blk_tiles = BM // SUBLANES

    def block_in(b, s):
        return [pltpu.make_async_copy(xs_hbm.at[pl.ds(b * blk_tiles, blk_tiles), u], xbuf.at[s, :, :, u, :], isem.at[s])
                for u in range(SUBLANES)]

    def block_out(b, s):
        return [pltpu.make_async_copy(ybuf.at[s, :, :, u, :], ys_hbm.at[pl.ds(b * blk_tiles, blk_tiles), u], osem.at[s])
                for u in range(SUBLANES)]

    def read_block(b):
        return jnp.minimum(b, n_used_ref[0] - 1)

    @pl.when(j == 0)
    def _():
        for cp in block_in(read_block(0), 0):
            cp.start()

    for cp in block_in(read_block(j + 1), 1 - slot):
        cp.start()
    for cp in block_in(0, slot):
        cp.wait()

    @pl.when(j >= 2)
    def _():
        for cp in block_out(0, slot):
            cp.wait()

    @pl.when((j == 0) | (blk_e_ref[j] != blk_e_ref[jnp.maximum(j - 1, 0)]))
    def _():
        wg_bf[...] = wg_ref[0].astype(wg_bf.dtype)
        wu_bf[...] = wu_ref[0].astype(wu_bf.dtype)
        wd_bf[...] = wd_ref[0].astype(wd_bf.dtype)

    @pl.when(j < n_used_ref[0])
    def _():
        x = _from_tiles(xbuf.at[slot]).astype(wg_bf.dtype)
        a = jnp.dot(x, wg_bf[...], preferred_element_type=jnp.float32)
        b = jnp.dot(x, wu_bf[...], preferred_element_type=jnp.float32)
        hmid = (a / (1.0 + jnp.exp(-a)) * b).astype(x.dtype)
        _to_tiles(ybuf.at[slot], jnp.dot(hmid, wd_bf[...], preferred_element_type=jnp.float32))

    @pl.when(j >= n_used_ref[0])
    def _():
        ybuf[slot] = jnp.zeros(ybuf.shape[1:], ybuf.dtype)

    for cp in block_out(j, slot):
        cp.start()

    @pl.when(j == last)
    def _():
        for cp in block_out(0, slot) + block_in(0, 1 - slot):
            cp.wait()

        @pl.when(j >= 1)
        def _():
            for cp in block_out(0, 1 - slot):
                cp.wait()


def _experts(xs, wg, wu, wd, blk_e, n_used):
    n_blk = blk_e.shape[0]
    d, d_e = wg.shape[1], wg.shape[2]
    grid_spec = pltpu.PrefetchScalarGridSpec(
        num_scalar_prefetch=2,
        grid=(n_blk,),
        in_specs=[pl.BlockSpec(memory_space=pl.ANY),
                  pl.BlockSpec((1, d, d_e), lambda j, be, nu: (be[j], 0, 0)),
                  pl.BlockSpec((1, d, d_e), lambda j, be, nu: (be[j], 0, 0)),
                  pl.BlockSpec((1, d_e, d), lambda j, be, nu: (be[j], 0, 0))],
        out_specs=pl.BlockSpec(memory_space=pl.ANY),
        scratch_shapes=[pltpu.VMEM((d, d_e), jnp.bfloat16), pltpu.VMEM((d, d_e), jnp.bfloat16),
                        pltpu.VMEM((d_e, d), jnp.bfloat16),
                        pltpu.VMEM((2,) + _tiles_shape(BM, d), jnp.float32),
                        pltpu.VMEM((2,) + _tiles_shape(BM, d), jnp.float32),
                        pltpu.SemaphoreType.DMA((2,)), pltpu.SemaphoreType.DMA((2,))],
    )
    return pl.pallas_call(
        _experts_kernel,
        grid_spec=grid_spec,
        out_shape=jax.ShapeDtypeStruct(_rows_shape(n_blk * BM, d), jnp.float32),
        compiler_params=_cparams(("arbitrary",)),
        name="experts",
    )(blk_e, n_used, xs, wg, wu, wd)


def _final_kernel(dest_ref, dest_next_ref, x1_ref, route_ref, gt_ref, g_ref, ys_hbm, o_ref, ybuf, sem):
    i = pl.program_id(0)
    slot = i % 2

    tiles = TM // SUBLANES

    def gather(d_ref, s):
        def issue(k, c):
            for u in range(SUBLANES):
                r = k * SUBLANES + u
                pltpu.make_async_copy(ys_hbm.at[d_ref[0, 0, r], d_ref[0, 1, r]],
                                      ybuf.at[s, k, :, u, :], sem.at[s]).start(priority=0)
                pltpu.make_async_copy(ys_hbm.at[d_ref[0, 2, r], d_ref[0, 3, r]],
                                      ybuf.at[s, tiles + k, :, u, :], sem.at[s]).start(priority=1)
            return c
        lax.fori_loop(0, tiles, issue, 0)

    @pl.when(i == 0)
    def _():
        gather(dest_ref, 0)

    @pl.when(i + 1 < pl.num_programs(0))
    def _():
        gather(dest_next_ref, 1 - slot)

    pltpu.make_async_copy(ybuf.at[slot], ybuf.at[slot], sem.at[slot]).wait()
    w1 = route_ref[:, 2:3]
    w2 = route_ref[:, 3:4]
    y0 = _from_tiles(ybuf.at[slot, 0:tiles])
    y1 = _from_tiles(ybuf.at[slot, tiles:2 * tiles])
    x2 = x1_ref[...] + gt_ref[0] * (w1 * y0 + w2 * y1)
    o_ref[...] = x2 * lax.rsqrt(jnp.mean(x2 * x2, axis=-1, keepdims=True) + EPS) * g_ref[...]


def _final(x1, ys, dest, route, mod3, g_final, seq):
    n, d = x1.shape
    n_tiles = n // TM
    tiles_per_seq = seq // TM
    row = lambda i: (i, 0)
    return pl.pallas_call(
        _final_kernel,
        grid=(n_tiles,),
        in_specs=[pl.BlockSpec((1, 8, TM), _dest_block, memory_space=pltpu.SMEM),
                  pl.BlockSpec((1, 8, TM), lambda i: _dest_block(jnp.minimum(i + 1, n_tiles - 1)),
                               memory_space=pltpu.SMEM),
                  pl.BlockSpec((TM, d), row),
                  pl.BlockSpec((TM, LANES), row),
                  pl.BlockSpec((1, 1, d), lambda i: ((i // tiles_per_seq) * 6 + 5, 0, 0)),
                  pl.BlockSpec((1, d), lambda i: (0, 0)),
                  pl.BlockSpec(memory_space=pl.ANY)],
        out_specs=pl.BlockSpec((TM, d), row),
        out_shape=jax.ShapeDtypeStruct((n, d), jnp.float32),
        scratch_shapes=[pltpu.VMEM((2,) + _tiles_shape(2 * TM, d), jnp.float32),
                        pltpu.SemaphoreType.DMA((2,))],
        compiler_params=_cparams(("arbitrary",)),
        name="final",
    )(dest, dest, x1, route, mod3, g_final, ys)


def _block_layout(counts, n):
    cnt = counts[0, :N_EXPERTS].astype(jnp.int32)
    blocks = (cnt + BM - 1) // BM
    bends = jnp.cumsum(blocks)
    pstarts = (bends - blocks) * BM
    n_blk = (2 * n) // BM + N_EXPERTS
    pst_row = jnp.zeros((1, LANES), jnp.float32).at[0, :N_EXPERTS].set(pstarts.astype(jnp.float32))
    zrow = ((pstarts + cnt) // SUBLANES).astype(jnp.int32)
    n_used = bends[-1:].astype(jnp.int32)
    blk_e = jnp.sum(bends[None, :] <= jnp.arange(n_blk, dtype=jnp.int32)[:, None], axis=1)
    blk_e = jnp.minimum(blk_e, N_EXPERTS - 1).astype(jnp.int32)
    return pst_row, zrow, n_used, blk_e


def _layer(x2, c, pos2, w_ada, b_ada, g_mix, w_in, w_o, g_ffn, w_rg, w_re, w_up, w_gate, w_down, seq):
    n, d = x2.shape
    bsz = n // seq
    bf = jnp.bfloat16
    d_a = d // 2
    d_i = N_IDX_HEADS * IDX_DIM
    mod3 = _ada(c, w_ada, b_ada).reshape(bsz * 6, 1, d)

    c0 = 3 * d_a + d_i
    c1 = c0 + IDX_DIM + N_IDX_HEADS
    w_cat = jnp.concatenate([w_in[:, :c0], w_in[:, c0:c1],
                             jnp.zeros((d, LANES - (c1 - c0)), w_in.dtype), w_in[:, c1:]], axis=1).astype(bf)
    qat, ka, vat, qit, kiw, wit, qb, kb, vb, gb = _proj(x2, pos2, mod3, g_mix.reshape(1, d), w_cat, seq)

    n_kt = seq // TM
    ya = _dsa(qat, qit, wit, kiw.reshape(bsz, n_kt, TM, LANES), ka.reshape(bsz, n_kt, TM, d_a), vat, seq)
    yb = _ret(qb, kb, vb, gb, seq)

    w_route = jnp.concatenate([jnp.transpose(w_re, (1, 0, 2)).reshape(d, N_EXPERTS), w_rg,
                               jnp.zeros((d, LANES - N_EXPERTS - N_GROUPS), w_rg.dtype)], axis=1).astype(bf)
    x1, h2, route, counts = _mix_out(x2, ya, yb, w_o[:d_a].astype(bf), w_o[d_a:].astype(bf), mod3,
                                     g_ffn.reshape(1, d), w_route, seq)

    pst_row, zrow, n_used, blk_e = _block_layout(counts, n)
    dest = _plan(route, pst_row)
    xs = _dispatch(h2, dest, zrow, n_used, blk_e.shape[0])
    ys = _experts(xs, w_gate, w_up, w_down, blk_e, n_used)
    return x1, ys, dest, route, mod3


def kernel(x, c, positions, w_ada, b_ada, g_norm_mix, w_in, w_o, g_norm_ffn, w_router_group, w_router_expert,
           w_up, w_gate, w_down, g_norm_final):
    bsz, seq, d = x.shape
    depth = w_ada.shape[0]
    assert depth == 1, "the final norm is fused into the last layer's combine kernel"
    assert seq % (2 * TM) == 0 and seq % (RET_C * RET_CHUNKS) == 0 and seq % MIX_TM == 0 and (2 * bsz * seq) % BM == 0
    assert (bsz * seq) % PLAN_TM == 0 and PLAN_TM % TM == 0
    x2 = x.reshape(bsz * seq, d)
    pos2 = positions.astype(jnp.float32).reshape(bsz * seq, 1)
    x1, ys, dest, route, mod3 = _layer(x2, c, pos2, w_ada[0], b_ada[0], g_norm_mix[0], w_in[0], w_o[0],
                                       g_norm_ffn[0], w_router_group[0], w_router_expert[0], w_up[0], w_gate[0],
                                       w_down[0], seq)
    out = _final(x1, ys, dest, route, mod3, g_norm_final.reshape(1, d), seq)
    return out.reshape(bsz, seq, d)
```

```python
import functools

import jax
import jax.numpy as jnp
import numpy as np
from jax import lax
from jax.experimental import pallas as pl
from jax.experimental.pallas import tpu as pltpu

CHUNK = 64
HEAD_DIM = 64
N_IDX_HEADS = 16
IDX_DIM = 64
TOPK_MAX = 256
ROPE_THETA = 500000.0
ROT_DIM = HEAD_DIM // 4
RET_THETA = 10000.0
N_GROUPS = 4
EXPERTS_PER_GROUP = 8
N_EXPERTS = N_GROUPS * EXPERTS_PER_GROUP
EPS = 1e-6

LANES = 128
SUBLANES = 8
VMEM_LIMIT = 56 * 1024 * 1024

TM = 256
QB = TM
RET_C = 256
RET_CHUNKS = 2
BM = 256
PLAN_TM = 1024
MIX_TM = 512
CNT_ROWS = 64
ONES_ROWS = 16
HEAD_GROUP = 4
IDX_TILES = 8
ATT_TILES = 8
BISECT_PER_CHECK = 2
MAX_BISECT = 40

NEG_BIG = -1e30
LOG2E = 1.4426950408889634


def _cparams(sem):
    return pltpu.CompilerParams(dimension_semantics=sem, vmem_limit_bytes=VMEM_LIMIT)


def _ada_kernel(c_ref, w_ref, b_ref, o_ref):
    o_ref[...] = jnp.dot(c_ref[...], w_ref[...], preferred_element_type=jnp.float32) + b_ref[...]


def _ada(c, w_ada, b_ada):
    bsz, d = c.shape
    n_out = w_ada.shape[1]
    return pl.pallas_call(
        _ada_kernel,
        grid=(n_out // d,),
        in_specs=[pl.BlockSpec((bsz, d), lambda j: (0, 0)),
                  pl.BlockSpec((d, d), lambda j: (0, j)),
                  pl.BlockSpec((1, d), lambda j: (0, j))],
        out_specs=pl.BlockSpec((bsz, d), lambda j: (0, j)),
        out_shape=jax.ShapeDtypeStruct((bsz, n_out), jnp.float32),
        compiler_params=_cparams(("arbitrary",)),
        name="ada",
    )(c, w_ada, b_ada.reshape(1, n_out))


def _rmsnorm_mod(x, g, sc, sh):
    xn = x * lax.rsqrt(jnp.mean(x * x, axis=-1, keepdims=True) + EPS)
    return xn * g * (1.0 + sc) + sh


def _rope_lanes(x, cos, sin_lo, sin_hi, half):
    cols = []
    for k in range(x.shape[1] // LANES):
        xb = x[:, k * LANES:(k + 1) * LANES]
        cols.append(xb * cos + pltpu.roll(xb, LANES - half, 1) * sin_lo + pltpu.roll(xb, half, 1) * sin_hi)
    return cols[0] if len(cols) == 1 else jnp.concatenate(cols, axis=1)


def _proj_kernel(x_ref, pos_ref, sc_ref, sh_ref, g_ref, w_ref, tab_ref,
                 qat_ref, ka_ref, vat_ref, qit_ref, kiw_ref, wit_ref, qb_ref, kb_ref, vb_ref, gb_ref,
                 *, d_a, d_i, d_b):
    h = _rmsnorm_mod(x_ref[...], g_ref[...], sc_ref[0], sh_ref[0]).astype(jnp.bfloat16)
    pos = pos_ref[...]
    ang_a = pos * tab_ref[0:1, :]
    cos_a, sin_a = jnp.cos(ang_a), jnp.sin(ang_a)
    sa_lo, sa_hi = sin_a * tab_ref[1:2, :], sin_a * tab_ref[2:3, :]
    ang_b = pos * tab_ref[3:4, :]
    cos_b, sin_b = jnp.cos(ang_b), jnp.sin(ang_b)
    sb_lo, sb_hi = sin_b * tab_ref[4:5, :], sin_b * tab_ref[5:6, :]
    half_a, half_b = ROT_DIM // 2, HEAD_DIM // 2

    def seg(lo, width):
        return jnp.dot(h, w_ref[:, lo:lo + width], preferred_element_type=jnp.float32)

    o = 0
    qa = seg(o, d_a); o += d_a
    qat_ref[0, 0] = (_rope_lanes(qa, cos_a, sa_lo, sa_hi, half_a) * (HEAD_DIM ** -0.5 * LOG2E)).T.astype(qat_ref.dtype)
    ka = seg(o, d_a); o += d_a
    ka_ref[...] = _rope_lanes(ka, cos_a, sa_lo, sa_hi, half_a).astype(ka_ref.dtype)
    vat_ref[0, 0] = seg(o, d_a).T.astype(vat_ref.dtype); o += d_a
    qi = seg(o, d_i); o += d_i
    qit_ref[0, 0] = (_rope_lanes(qi, cos_a, sa_lo, sa_hi, half_a) * (IDX_DIM ** -0.5)).T.astype(qit_ref.dtype)
    kw = seg(o, LANES); o += LANES
    kiw_ref[...] = _rope_lanes(kw, cos_a, sa_lo, sa_hi, half_a).astype(kiw_ref.dtype)
    wit_ref[0, 0] = kw.T[IDX_DIM:IDX_DIM + N_IDX_HEADS, :] * (N_IDX_HEADS ** -0.5)
    qb = seg(o, d_b); o += d_b
    qb_ref[...] = _rope_lanes(qb, cos_b, sb_lo, sb_hi, half_b).astype(qb_ref.dtype)
    kb = seg(o, d_b); o += d_b
    kb_ref[...] = (_rope_lanes(kb, cos_b, sb_lo, sb_hi, half_b) * (HEAD_DIM ** -0.5)).astype(kb_ref.dtype)
    vb_ref[...] = seg(o, d_b).astype(vb_ref.dtype); o += d_b
    gb_ref[...] = seg(o, d_b)


def _rope_tables():
    lane = jnp.arange(LANES) % HEAD_DIM
    rows = []
    for rot, theta in ((ROT_DIM, ROPE_THETA), (HEAD_DIM, RET_THETA)):
        half = rot // 2
        inv_freq = theta ** (-jnp.arange(half, dtype=jnp.float32) / half)
        rows.append(jnp.where(lane < rot, inv_freq[lane % half], 0.0))
        rows.append(jnp.where(lane < half, -1.0, 0.0))
        rows.append(jnp.where((lane >= half) & (lane < rot), 1.0, 0.0))
    rows += [jnp.zeros((LANES,), jnp.float32)] * 2
    return jnp.stack(rows).astype(jnp.float32)


def _proj(x2, pos2, mod3, g_mix, w_cat, seq):
    n, d = x2.shape
    bsz = n // seq
    d_a = d // 2
    d_b = d // 2
    d_i = N_IDX_HEADS * IDX_DIM
    tiles_per_seq = seq // TM
    tab = _rope_tables()
    row = lambda i: (i, 0)
    const = lambda i: (0, 0)
    tile4 = lambda i: (i // tiles_per_seq, i % tiles_per_seq, 0, 0)
    bf = jnp.bfloat16
    out_shape = (
        jax.ShapeDtypeStruct((bsz, tiles_per_seq, d_a, TM), bf),
        jax.ShapeDtypeStruct((n, d_a), bf),
        jax.ShapeDtypeStruct((bsz, tiles_per_seq, d_a, TM), bf),
        jax.ShapeDtypeStruct((bsz, tiles_per_seq, d_i, TM), bf),
        jax.ShapeDtypeStruct((n, LANES), bf),
        jax.ShapeDtypeStruct((bsz, tiles_per_seq, N_IDX_HEADS, TM), jnp.float32),
        jax.ShapeDtypeStruct((n, d_b), bf),
        jax.ShapeDtypeStruct((n, d_b), bf),
        jax.ShapeDtypeStruct((n, d_b), bf),
        jax.ShapeDtypeStruct((n, d_b), jnp.float32),
    )
    out_specs = (
        pl.BlockSpec((1, 1, d_a, TM), tile4),
        pl.BlockSpec((TM, d_a), row),
        pl.BlockSpec((1, 1, d_a, TM), tile4),
        pl.BlockSpec((1, 1, d_i, TM), tile4),
        pl.BlockSpec((TM, LANES), row),
        pl.BlockSpec((1, 1, N_IDX_HEADS, TM), tile4),
        pl.BlockSpec((TM, d_b), row),
        pl.BlockSpec((TM, d_b), row),
        pl.BlockSpec((TM, d_b), row),
        pl.BlockSpec((TM, d_b), row),
    )
    return pl.pallas_call(
        functools.partial(_proj_kernel, d_a=d_a, d_i=d_i, d_b=d_b),
        grid=(n // TM,),
        in_specs=[pl.BlockSpec((TM, d), row),
                  pl.BlockSpec((TM, 1), row),
                  pl.BlockSpec((1, 1, d), lambda i: ((i // tiles_per_seq) * 6 + 1, 0, 0)),
                  pl.BlockSpec((1, 1, d), lambda i: ((i // tiles_per_seq) * 6 + 0, 0, 0)),
                  pl.BlockSpec((1, d), const),
                  pl.BlockSpec(w_cat.shape, const),
                  pl.BlockSpec(tab.shape, const)],
        out_specs=out_specs,
        out_shape=out_shape,
        compiler_params=_cparams(("arbitrary",)),
        name="proj",
    )(x2, pos2, mod3, mod3, g_mix, w_cat, tab)


def _row_blocks(x, rows):
    return [x[r * rows:(r + 1) * rows] for r in range(x.shape[0] // rows)]


def _dsa_kernel(qat_ref, qit_ref, wit_ref, kiw_ref, ka_ref, vat_ref, tri_ref, o_ref,
                qix_ref, qmx_ref, sc_ref, m_ref, l_ref, acc_ref, sa_ref, sb_ref, mxa_ref, mxb_ref, lohi_ref,
                *, k_top, n_heads):
    i = pl.program_id(1)
    n_real = (i * QB + QB + TM - 1) // TM
    n_kt = n_real // 2 * 2
    odd_tile = n_real % 2 == 1

    zero_rows = jnp.zeros((LANES - IDX_DIM, QB), qix_ref.dtype)
    for h in range(N_IDX_HEADS):
        qix_ref[h] = jnp.concatenate([qit_ref[0, 0, h * IDX_DIM:(h + 1) * IDX_DIM, :], zero_rows], axis=0)
    row_q = lax.broadcasted_iota(jnp.int32, (LANES, QB), 0)
    for h in range(n_heads):
        pair = qat_ref[0, 0, (h // 2) * LANES:(h // 2 + 1) * LANES, :]
        own = (row_q < HEAD_DIM) if h % 2 == 0 else (row_q >= HEAD_DIM)
        qmx_ref[h] = jnp.where(own, pair, jnp.zeros_like(pair))

    q_chunk = (i * QB + lax.broadcasted_iota(jnp.int32, (1, QB), 1)) // CHUNK
    key_chunk_in_tile = lax.broadcasted_iota(jnp.int32, (TM, QB), 0) // CHUNK
    w_all = wit_ref[0, 0]

    def idx_tiles(tiles):
        lo, hi = lohi_ref[0], lohi_ref[1]
        for j in tiles:
            kt = kiw_ref[0, j]
            acc = None
            for h in range(N_IDX_HEADS):
                d = jnp.dot(kt, qix_ref[h], preferred_element_type=jnp.float32)
                t = w_all[h:h + 1, :] * jnp.maximum(d, 0.0)
                acc = t if acc is None else acc + t
            adm = key_chunk_in_tile <= q_chunk - j * (TM // CHUNK)
            s = jnp.where(adm, acc, -jnp.inf)
            sc_ref[j] = s
            lo = jnp.minimum(lo, functools.reduce(jnp.minimum, _row_blocks(jnp.where(adm, acc, jnp.inf), SUBLANES)))
            hi = jnp.maximum(hi, functools.reduce(jnp.maximum, _row_blocks(s, SUBLANES)))
        lohi_ref[0], lohi_ref[1] = lo, hi

    lohi_ref[0] = jnp.full((SUBLANES, QB), jnp.inf, jnp.float32)
    lohi_ref[1] = jnp.full((SUBLANES, QB), -jnp.inf, jnp.float32)

    def idx_step(g, carry):
        idx_tiles([IDX_TILES * g + u for u in range(IDX_TILES)])
        return carry

    lax.fori_loop(0, n_kt // IDX_TILES, idx_step, 0)

    def idx_tail(p, carry):
        first = n_kt // IDX_TILES * IDX_TILES + 2 * p
        idx_tiles([first, first + 1])
        return carry

    lax.fori_loop(0, n_kt % IDX_TILES // 2, idx_tail, 0)

    @pl.when(odd_tile)
    def _():
        idx_tiles([n_real - 1])

    lo = jnp.min(lohi_ref[0], axis=0, keepdims=True)
    hi = jnp.max(lohi_ref[1], axis=0, keepdims=True)
    n_adm = ((i * QB + lax.broadcasted_iota(jnp.int32, (1, QB), 1)) // CHUNK + 1) * CHUNK

    def bisect_cond(carry):
        it, _, _, _, unsettled = carry
        return (it < MAX_BISECT) & (unsettled > 0.0)

    def bisect(carry):
        it, lo, hi, cnt_lo, _ = carry
        for _ in range(BISECT_PER_CHECK):
            mid = lo + (hi - lo) * 0.5
            mid_b = jnp.broadcast_to(mid, (CNT_ROWS, QB))

            def count_tile(j, cnt, mid_b=mid_b):
                for blk in _row_blocks(sc_ref[j], CNT_ROWS):
                    cnt = cnt + jnp.where(blk >= mid_b, 1.0, 0.0)
                return cnt

            cnt = lax.fori_loop(0, n_real, count_tile, jnp.zeros((CNT_ROWS, QB), jnp.float32))
            c = jnp.sum(cnt, axis=0, keepdims=True)
            ge = c >= k_top
            lo = jnp.where(ge, mid, lo)
            hi = jnp.where(ge, hi, mid)
            cnt_lo = jnp.where(ge, c, cnt_lo)
        return it + BISECT_PER_CHECK, lo, hi, cnt_lo, jnp.max(jnp.where(cnt_lo > k_top, 1.0, 0.0))

    cnt0 = n_adm.astype(jnp.float32)
    _, thr, thr_hi, _, unsettled = lax.while_loop(bisect_cond, bisect,
                                                  (0, lo, hi, cnt0, jnp.max(jnp.where(cnt0 > k_top, 1.0, 0.0))))

    @pl.when(unsettled <= 0.0)
    def _():
        def bias_tile(j, carry):
            sc_ref[j] = jnp.where(sc_ref[j] >= thr, 0.0, NEG_BIG)
            return carry

        lax.fori_loop(0, n_real, bias_tile, 0)

    @pl.when(unsettled > 0.0)
    def _():
        def count_hi(j, cnt):
            return cnt + functools.reduce(jnp.add, _row_blocks(jnp.where(sc_ref[j] >= thr_hi, 1.0, 0.0), SUBLANES))

        above = jnp.sum(lax.fori_loop(0, n_real, count_hi, jnp.zeros((SUBLANES, QB), jnp.float32)),
                        axis=0, keepdims=True)
        top = jnp.where(above < k_top, thr_hi, jnp.inf)
        room = k_top - jnp.where(above < k_top, above, 0.0)

        def bias_tile(j, taken):
            s = sc_ref[j]
            tied = (s >= thr) & (s < top)
            before = jnp.dot(tri_ref[...], jnp.where(tied, 1.0, 0.0).astype(tri_ref.dtype),
                             preferred_element_type=jnp.float32)
            keep = (s >= top) | (tied & (taken + before <= room))
            sc_ref[j] = jnp.where(keep, 0.0, NEG_BIG)
            return taken + before[TM - 1:TM, :]

        lax.fori_loop(0, n_real, bias_tile, jnp.zeros((1, QB), jnp.float32))

    m_ref[...] = jnp.full(m_ref.shape, NEG_BIG, jnp.float32)
    l_ref[...] = jnp.zeros(l_ref.shape, jnp.float32)
    acc_ref[...] = jnp.zeros(acc_ref.shape, jnp.float32)

    def pair(h):
        return slice((h // 2) * LANES, (h // 2 + 1) * LANES)

    def logits_into(s_ref, mx_ref, j, heads):
        bias = sc_ref[j]
        for h in heads:
            s = jnp.dot(ka_ref[0, j, :, pair(h)], qmx_ref[h], preferred_element_type=jnp.float32) + bias
            s_ref[h] = s
            mx_ref[h] = jnp.max(s, axis=0, keepdims=True)

    ones_rows = jnp.ones((ONES_ROWS, TM), vat_ref.dtype)

    def absorb(s_ref, mx_ref, j, heads):
        for h in heads:
            m_old = m_ref[h]
            m_new = jnp.maximum(m_old, mx_ref[h])
            alpha = jnp.exp2(m_old - m_new)
            p = jnp.exp2(s_ref[h] - m_new).astype(vat_ref.dtype)
            pv = jnp.dot(jnp.concatenate([vat_ref[0, j, pair(h), :], ones_rows], axis=0), p,
                         preferred_element_type=jnp.float32)
            acc_ref[h] = acc_ref[h] * alpha + pv[0:LANES]
            l_ref[h] = l_ref[h] * alpha + pv[LANES:LANES + SUBLANES]
            m_ref[h] = m_new

    for h0 in range(0, n_heads, HEAD_GROUP):
        heads = range(h0, h0 + HEAD_GROUP)
        logits_into(sa_ref, mxa_ref, 0, heads)

        def tile_pair(t, heads=heads):
            logits_into(sb_ref, mxb_ref, t + 1, heads)
            absorb(sa_ref, mxa_ref, t, heads)
            logits_into(sa_ref, mxa_ref, jnp.minimum(t + 2, n_real - 1), heads)
            absorb(sb_ref, mxb_ref, t + 1, heads)

        def attn_step(g, carry):
            for u in range(0, ATT_TILES, 2):
                tile_pair(ATT_TILES * g + u)
            return carry

        lax.fori_loop(0, n_kt // ATT_TILES, attn_step, 0)

        def attn_tail(p, carry):
            tile_pair(n_kt // ATT_TILES * ATT_TILES + 2 * p)
            return carry

        lax.fori_loop(0, n_kt % ATT_TILES // 2, attn_tail, 0)

        @pl.when(odd_tile)
        def _(heads=heads):
            absorb(sa_ref, mxa_ref, n_real - 1, heads)

    for hp in range(n_heads // 2):
        even = acc_ref[2 * hp] / l_ref[2 * hp, 0:1, :]
        odd = acc_ref[2 * hp + 1] / l_ref[2 * hp + 1, 0:1, :]
        o_ref[:, hp * LANES:(hp + 1) * LANES] = jnp.where(row_q < HEAD_DIM, even, odd).T.astype(o_ref.dtype)


def _dsa(qat, qit, wit, kiw4, ka4, vat, seq):
    bsz, n_kt, d_a, _ = qat.shape
    n_heads = d_a // HEAD_DIM
    n_qb = seq // QB
    k_top = min(TOPK_MAX, seq // 4)
    qtile = lambda b, i: (b, i, 0, 0)
    per_b = lambda b, i: (b, 0, 0, 0)
    f32 = jnp.float32
    return pl.pallas_call(
        functools.partial(_dsa_kernel, k_top=float(k_top), n_heads=n_heads),
        grid=(bsz, n_qb),
        in_specs=[pl.BlockSpec((1, 1, d_a, QB), qtile),
                  pl.BlockSpec((1, 1, qit.shape[2], QB), qtile),
                  pl.BlockSpec((1, 1, N_IDX_HEADS, QB), qtile),
                  pl.BlockSpec((1, n_kt, TM, LANES), per_b, pipeline_mode=pl.Buffered(1)),
                  pl.BlockSpec((1, n_kt, TM, d_a), per_b, pipeline_mode=pl.Buffered(1)),
                  pl.BlockSpec((1, n_kt, d_a, TM), per_b, pipeline_mode=pl.Buffered(1)),
                  pl.BlockSpec((TM, TM), lambda b, i: (0, 0), pipeline_mode=pl.Buffered(1))],
        out_specs=pl.BlockSpec((QB, d_a), lambda b, i: (b * n_qb + i, 0)),
        out_shape=jax.ShapeDtypeStruct((bsz * seq, d_a), jnp.bfloat16),
        scratch_shapes=[pltpu.VMEM((N_IDX_HEADS, LANES, QB), jnp.bfloat16),
                        pltpu.VMEM((n_heads, LANES, QB), jnp.bfloat16),
                        pltpu.VMEM((n_kt, TM, QB), f32),
                        pltpu.VMEM((n_heads, 1, QB), f32),
                        pltpu.VMEM((n_heads, SUBLANES, QB), f32),
                        pltpu.VMEM((n_heads, LANES, QB), f32),
                        pltpu.VMEM((n_heads, TM, QB), f32),
                        pltpu.VMEM((n_heads, TM, QB), f32),
                        pltpu.VMEM((n_heads, 1, QB), f32),
                        pltpu.VMEM((n_heads, 1, QB), f32),
                        pltpu.VMEM((2, SUBLANES, QB), f32)],
        compiler_params=_cparams(("arbitrary", "arbitrary")),
        name="dsa",
    )(qat, qit, wit, kiw4, ka4, vat, jnp.asarray(np.tril(np.ones((TM, TM), np.float32)), jnp.bfloat16))


def _group_mean(y, avg):
    hi = y.astype(jnp.bfloat16)
    lo = (y - hi.astype(jnp.float32)).astype(jnp.bfloat16)
    return (jnp.dot(hi, avg, preferred_element_type=jnp.float32)
            + jnp.dot(lo, avg, preferred_element_type=jnp.float32))


def _ret_kernel(q_ref, k_ref, v_ref, g_ref, dec_ref, zt_ref, xi_ref, gc_ref, blk_ref, avg_ref, o_ref, st_ref,
                *, n_heads):
    @pl.when(pl.program_id(1) == 0)
    def _():
        st_ref[...] = jnp.zeros_like(st_ref)

    even = lax.broadcasted_iota(jnp.int32, (RET_C, LANES), 1) < HEAD_DIM
    avg = avg_ref[...]
    for p in range(n_heads // 2):
        sl = slice(p * LANES, (p + 1) * LANES)
        state = st_ref[p]
        for c in range(q_ref.shape[0] // RET_C):
            rows = slice(c * RET_C, (c + 1) * RET_C)
            qp, kp, vp = q_ref[rows, sl], k_ref[rows, sl], v_ref[rows, sl]
            kpt = kp.astype(jnp.float32).T
            kpt_b = kpt.astype(kp.dtype)
            inner = None
            for e in range(2):
                q_e = jnp.where(even if e == 0 else jnp.logical_not(even), qp, jnp.zeros_like(qp))
                s = jnp.dot(q_e, kpt_b, preferred_element_type=jnp.float32) * dec_ref[2 * p + e]
                t = jnp.dot(s.astype(vp.dtype), vp, preferred_element_type=jnp.float32)
                inner = t if e == 0 else jnp.where(even, inner, t)
            cross = jnp.dot(qp, state.astype(qp.dtype), preferred_element_type=jnp.float32) * xi_ref[p]
            y = inner + cross
            yc = y - _group_mean(y, avg)
            yn = yc * lax.rsqrt(_group_mean(yc * yc, avg) + EPS)
            g = g_ref[rows, sl]
            o_ref[rows, sl] = (g / (1.0 + jnp.exp(-g)) * yn).astype(o_ref.dtype)
            kz = (kpt * zt_ref[p]).astype(kp.dtype)
            kv = jnp.dot(kz, vp, preferred_element_type=jnp.float32)
            state = state * gc_ref[p] + kv * blk_ref[...]
        st_ref[p] = state


def _ret_consts(n_heads):
    log_gamma = jnp.log1p(-jnp.exp2(-5.0 - jnp.arange(n_heads, dtype=jnp.float32)))
    pos = jnp.arange(RET_C, dtype=jnp.float32)
    diff = pos[:, None] - pos[None, :]
    dec = jnp.where(diff[None] >= 0, jnp.exp(jnp.maximum(diff, 0.0)[None] * log_gamma[:, None, None]), 0.0)
    zeta = jnp.exp((RET_C - 1.0 - pos)[None, :] * log_gamma[:, None])
    xi = jnp.exp((pos + 1.0)[None, :] * log_gamma[:, None])
    gc = jnp.exp(RET_C * log_gamma)
    n_pairs = n_heads // 2
    lanes = lambda a: jnp.repeat(a.reshape(n_pairs, 2, -1), HEAD_DIM, axis=1)
    zt = lanes(zeta)
    xi_p = jnp.swapaxes(lanes(xi), 1, 2)
    gc_p = jnp.broadcast_to(lanes(gc[:, None]), (n_pairs, LANES, LANES))
    head_of = jnp.arange(LANES) // HEAD_DIM
    blk = (head_of[:, None] == head_of[None, :]).astype(jnp.float32)
    avg = (blk / HEAD_DIM).astype(jnp.bfloat16)
    f32 = lambda a: a.astype(jnp.float32)
    return dec, f32(zt), f32(xi_p), f32(gc_p), blk, avg


def _ret(qb, kb, vb, gb, seq):
    n, d_b = qb.shape
    bsz = n // seq
    n_heads = d_b // HEAD_DIM
    step = RET_C * RET_CHUNKS
    n_c = seq // step
    consts = _ret_consts(n_heads)
    row = lambda b, c: (b * n_c + c, 0)
    const_spec = lambda a: pl.BlockSpec(a.shape, lambda b, c: (0,) * a.ndim)
    return pl.pallas_call(
        functools.partial(_ret_kernel, n_heads=n_heads),
        grid=(bsz, n_c),
        in_specs=[pl.BlockSpec((step, d_b), row)] * 4 + [const_spec(a) for a in consts],
        out_specs=pl.BlockSpec((step, d_b), row),
        out_shape=jax.ShapeDtypeStruct((n, d_b), jnp.bfloat16),
        scratch_shapes=[pltpu.VMEM((n_heads // 2, LANES, LANES), jnp.float32)],
        compiler_params=_cparams(("arbitrary", "arbitrary")),
        name="ret",
    )(qb, kb, vb, gb, *consts)


def _tiles_shape(rows, d):
    return (rows // SUBLANES, d // LANES, SUBLANES, LANES)


def _to_tiles(ref, x):
    for s in range(ref.shape[1]):
        ref[:, s] = x[:, s * LANES:(s + 1) * LANES].reshape(ref.shape[0], SUBLANES, LANES)


def _from_tiles(ref):
    rows = ref.shape[0] * SUBLANES
    return jnp.concatenate([ref[:, s].reshape(rows, LANES) for s in range(ref.shape[1])], axis=1)


def _lane_first_eq(x, m, lane):
    return jnp.min(jnp.where(x == m, lane, float(LANES)), axis=1, keepdims=True)


def _mix_out_kernel(x_ref, ya_ref, yb_ref, woa_ref, wob_ref, gt_ref, sc_ref, sh_ref, g_ref, wr_ref, tri_ref,
                    x1_ref, h2_ref, route_ref, cnt_ref, carry_ref):
    @pl.when(pl.program_id(0) == 0)
    def _():
        carry_ref[...] = jnp.zeros_like(carry_ref)

    mix = (jnp.dot(ya_ref[...], woa_ref[...], preferred_element_type=jnp.float32)
           + jnp.dot(yb_ref[...], wob_ref[...], preferred_element_type=jnp.float32))
    x1 = x_ref[...] + gt_ref[0] * mix
    x1_ref[...] = x1
    h2 = _rmsnorm_mod(x1, g_ref[...], sc_ref[0], sh_ref[0])
    _to_tiles(h2_ref, h2)

    lg = jnp.dot(h2.astype(jnp.bfloat16), wr_ref[...], preferred_element_type=jnp.float32)
    lane = lax.broadcasted_iota(jnp.int32, lg.shape, 1).astype(jnp.float32)
    is_grp = (lane >= N_EXPERTS) & (lane < N_EXPERTS + N_GROUPS)
    gl = jnp.where(is_grp, lg, -jnp.inf)
    gmax = jnp.max(gl, axis=1, keepdims=True)
    grp = _lane_first_eq(gl, gmax, lane) - N_EXPERTS
    p_grp = 1.0 / jnp.sum(jnp.exp(gl - gmax), axis=1, keepdims=True)
    in_grp = jnp.floor(lane * (1.0 / EXPERTS_PER_GROUP)) == grp
    f = jnp.where(in_grp & (lane < N_EXPERTS), lg, -jnp.inf)
    f1 = jnp.max(f, axis=1, keepdims=True)
    e1 = _lane_first_eq(f, f1, lane)
    f = jnp.where(lane == e1, -jnp.inf, f)
    f2 = jnp.max(f, axis=1, keepdims=True)
    e2 = _lane_first_eq(f, f2, lane)
    a2 = jnp.exp(f2 - f1)
    w1 = p_grp / (1.0 + a2)
    w2 = p_grp * a2 / (1.0 + a2)

    oh1 = jnp.where(lane == e1, 1.0, 0.0)
    oh2 = jnp.where(lane == e2, 1.0, 0.0)
    both = oh1 + oh2
    before = jnp.dot(tri_ref[...], both.astype(jnp.bfloat16), preferred_element_type=jnp.float32) + carry_ref[...]
    r1 = jnp.sum(before * oh1, axis=1, keepdims=True)
    r2 = jnp.sum(before * oh2, axis=1, keepdims=True)
    carry = carry_ref[...] + jnp.sum(both, axis=0, keepdims=True)
    carry_ref[...] = carry
    cnt_ref[...] = carry

    out = jnp.zeros(lg.shape, jnp.float32)
    for col, val in enumerate((e1, e2, w1, w2, r1, r2)):
        out = jnp.where(lane == col, val, out)
    route_ref[...] = out


def _mix_out(x2, ya, yb, wo_a, wo_b, mod3, g_ffn, w_route, seq):
    n, d = x2.shape
    tm = MIX_TM
    tiles_per_seq = seq // tm
    tri = jnp.asarray(np.tril(np.ones((tm, tm), np.float32), -1), jnp.bfloat16)
    row = lambda i: (i, 0)
    const = lambda i: (0, 0)
    modk = lambda k: pl.BlockSpec((1, 1, d), lambda i: ((i // tiles_per_seq) * 6 + k, 0, 0))
    return pl.pallas_call(
        _mix_out_kernel,
        grid=(n // tm,),
        in_specs=[pl.BlockSpec((tm, d), row),
                  pl.BlockSpec((tm, ya.shape[1]), row),
                  pl.BlockSpec((tm, yb.shape[1]), row),
                  pl.BlockSpec(wo_a.shape, const),
                  pl.BlockSpec(wo_b.shape, const),
                  modk(2), modk(4), modk(3),
                  pl.BlockSpec((1, d), const),
                  pl.BlockSpec(w_route.shape, const),
                  pl.BlockSpec(tri.shape, const)],
        out_specs=(pl.BlockSpec((tm, d), row), pl.BlockSpec(_tiles_shape(tm, d), lambda i: (i, 0, 0, 0)),
                   pl.BlockSpec((tm, LANES), row), pl.BlockSpec((1, LANES), const)),
        out_shape=(jax.ShapeDtypeStruct((n, d), jnp.float32), jax.ShapeDtypeStruct(_tiles_shape(n, d), jnp.float32),
                   jax.ShapeDtypeStruct((n, LANES), jnp.float32), jax.ShapeDtypeStruct((1, LANES), jnp.float32)),
        scratch_shapes=[pltpu.VMEM((1, LANES), jnp.float32)],
        compiler_params=_cparams(("arbitrary",)),
        name="mix_out",
    )(x2, ya, yb, wo_a, wo_b, mod3, mod3, mod3, g_ffn, w_route, tri)


def _plan_kernel(route_ref, pst_ref, dest_ref):
    r = route_ref[...]
    lane = lax.broadcasted_iota(jnp.int32, r.shape, 1).astype(jnp.float32)
    pst = pst_ref[...]
    d1 = jnp.sum(jnp.where(lane == r[:, 0:1], pst, 0.0), axis=1, keepdims=True) + r[:, 4:5]
    d2 = jnp.sum(jnp.where(lane == r[:, 1:2], pst, 0.0), axis=1, keepdims=True) + r[:, 5:6]
    t1 = jnp.floor(d1 * (1.0 / SUBLANES))
    t2 = jnp.floor(d2 * (1.0 / SUBLANES))
    packed = jnp.zeros(r.shape, jnp.float32)
    for k, v in enumerate((t1, d1 - t1 * SUBLANES, t2, d2 - t2 * SUBLANES)):
        packed = jnp.where(lane == float(k), v, packed)
    dest_ref[0] = packed.T[0:8, :].astype(jnp.int32)


def _dest_block(i):
    return (i // (PLAN_TM // TM), 0, i % (PLAN_TM // TM))


def _plan(route, pst_row):
    n = route.shape[0]
    return pl.pallas_call(
        _plan_kernel,
        grid=(n // PLAN_TM,),
        in_specs=[pl.BlockSpec((PLAN_TM, LANES), lambda i: (i, 0)), pl.BlockSpec((1, LANES), lambda i: (0, 0))],
        out_specs=pl.BlockSpec((1, 8, PLAN_TM), lambda i: (i, 0, 0)),
        out_shape=jax.ShapeDtypeStruct((n // PLAN_TM, 8, PLAN_TM), jnp.int32),
        compiler_params=_cparams(("arbitrary",)),
        name="plan",
    )(route, pst_row)


def _dispatch_kernel(zrow_ref, n_used_ref, dest_ref, h2_ref, xs_hbm, zbuf, sem, zsem, *, n_blk):
    i = pl.program_id(0)

    @pl.when(i == 0)
    def _():
        zbuf[...] = jnp.zeros(zbuf.shape, zbuf.dtype)
        blk_tiles = BM // SUBLANES
        for e in range(N_EXPERTS):
            pltpu.make_async_copy(zbuf, xs_hbm.at[pl.ds(zrow_ref[e], blk_tiles)], zsem).start()
        for e in range(N_EXPERTS):
            pltpu.make_async_copy(zbuf, xs_hbm.at[pl.ds(0, blk_tiles)], zsem).wait()
        for b in range(N_EXPERTS + 1):
            @pl.when(n_used_ref[0] + b <= n_blk)
            def _():
                tail = pltpu.make_async_copy(zbuf, xs_hbm.at[pl.ds((n_used_ref[0] + b) * blk_tiles, blk_tiles)], zsem)
                tail.start()
                tail.wait()

    def issue(k, c):
        for u in range(SUBLANES):
            r = k * SUBLANES + u
            row = h2_ref.at[k, :, u, :]
            pltpu.make_async_copy(row, xs_hbm.at[dest_ref[0, 0, r], :, dest_ref[0, 1, r], :], sem).start()
            pltpu.make_async_copy(row, xs_hbm.at[dest_ref[0, 2, r], :, dest_ref[0, 3, r], :], sem).start()
        return c

    lax.fori_loop(0, TM // SUBLANES, issue, 0)
    for _ in range(2):
        pltpu.make_async_copy(h2_ref, xs_hbm.at[pl.ds(0, TM // SUBLANES)], sem).wait()


def _dispatch(h2, dest, zrow, n_used, n_blk):
    n_rows = (n_blk + 1) * BM
    d = h2.shape[1] * LANES
    grid_spec = pltpu.PrefetchScalarGridSpec(
        num_scalar_prefetch=2,
        grid=(h2.shape[0] * SUBLANES // TM,),
        in_specs=[pl.BlockSpec((1, 8, TM), lambda i, z, nu: _dest_block(i), memory_space=pltpu.SMEM),
                  pl.BlockSpec(_tiles_shape(TM, d), lambda i, z, nu: (i, 0, 0, 0))],
        out_specs=pl.BlockSpec(memory_space=pl.ANY),
        scratch_shapes=[pltpu.VMEM(_tiles_shape(BM, d), h2.dtype),
                        pltpu.SemaphoreType.DMA(()),
                        pltpu.SemaphoreType.DMA(())],
    )
    return pl.pallas_call(
        functools.partial(_dispatch_kernel, n_blk=n_blk),
        grid_spec=grid_spec,
        out_shape=jax.ShapeDtypeStruct(_tiles_shape(n_rows, d), h2.dtype),
        compiler_params=_cparams(("arbitrary",)),
        name="dispatch",
    )(zrow, n_used, dest, h2)


def _experts_kernel(blk_e_ref, n_used_ref, x_ref, wg_ref, wu_ref, wd_ref, y_ref, wg_bf, wu_bf, wd_bf):
    j = pl.program_id(0)

    @pl.when((j == 0) | (blk_e_ref[j] != blk_e_ref[jnp.maximum(j - 1, 0)]))
    def _():
        wg_bf[...] = wg_ref[0].astype(wg_bf.dtype)
        wu_bf[...] = wu_ref[0].astype(wu_bf.dtype)
        wd_bf[...] = wd_ref[0].astype(wd_bf.dtype)

    @pl.when(j < n_used_ref[0])
    def _():
        x = _from_tiles(x_ref).astype(wg_bf.dtype)
        a = jnp.dot(x, wg_bf[...], preferred_element_type=jnp.float32)
        b = jnp.dot(x, wu_bf[...], preferred_element_type=jnp.float32)
        hmid = (a / (1.0 + jnp.exp(-a)) * b).astype(x.dtype)
        _to_tiles(y_ref, jnp.dot(hmid, wd_bf[...], preferred_element_type=jnp.float32))

    @pl.when(j >= n_used_ref[0])
    def _():
        y_ref[...] = jnp.zeros(y_ref.shape, y_ref.dtype)


def _experts(xs, wg, wu, wd, blk_e, n_used):
    n_blk = blk_e.shape[0]
    d, d_e = wg.shape[1], wg.shape[2]
    blk = _tiles_shape(BM, d)
    grid_spec = pltpu.PrefetchScalarGridSpec(
        num_scalar_prefetch=2,
        grid=(n_blk,),
        in_specs=[pl.BlockSpec(blk, lambda j, be, nu: (jnp.minimum(j, nu[0] - 1), 0, 0, 0)),
                  pl.BlockSpec((1, d, d_e), lambda j, be, nu: (be[j], 0, 0)),
                  pl.BlockSpec((1, d, d_e), lambda j, be, nu: (be[j], 0, 0)),
                  pl.BlockSpec((1, d_e, d), lambda j, be, nu: (be[j], 0, 0))],
        out_specs=pl.BlockSpec(blk, lambda j, be, nu: (j, 0, 0, 0)),
        scratch_shapes=[pltpu.VMEM((d, d_e), jnp.bfloat16), pltpu.VMEM((d, d_e), jnp.bfloat16),
                        pltpu.VMEM((d_e, d), jnp.bfloat16)],
    )
    return pl.pallas_call(
        _experts_kernel,
        grid_spec=grid_spec,
        out_shape=jax.ShapeDtypeStruct(_tiles_shape(n_blk * BM, d), jnp.float32),
        compiler_params=_cparams(("arbitrary",)),
        name="experts",
    )(blk_e, n_used, xs, wg, wu, wd)


def _final_kernel(dest_ref, dest_next_ref, x1_ref, route_ref, gt_ref, g_ref, ys_hbm, o_ref, ybuf, sem):
    i = pl.program_id(0)
    slot = i % 2

    tiles = TM // SUBLANES

    def gather(d_ref, s):
        def issue(k, c):
            for u in range(SUBLANES):
                r = k * SUBLANES + u
                pltpu.make_async_copy(ys_hbm.at[d_ref[0, 0, r], :, d_ref[0, 1, r], :],
                                      ybuf.at[s, k, :, u, :], sem.at[s]).start()
                pltpu.make_async_copy(ys_hbm.at[d_ref[0, 2, r], :, d_ref[0, 3, r], :],
                                      ybuf.at[s, tiles + k, :, u, :], sem.at[s]).start()
            return c
        lax.fori_loop(0, tiles, issue, 0)

    @pl.when(i == 0)
    def _():
        gather(dest_ref, 0)

    @pl.when(i + 1 < pl.num_programs(0))
    def _():
        gather(dest_next_ref, 1 - slot)

    pltpu.make_async_copy(ys_hbm.at[pl.ds(0, 2 * tiles)], ybuf.at[slot], sem.at[slot]).wait()
    w1 = route_ref[:, 2:3]
    w2 = route_ref[:, 3:4]
    y0 = _from_tiles(ybuf.at[slot, 0:tiles])
    y1 = _from_tiles(ybuf.at[slot, tiles:2 * tiles])
    x2 = x1_ref[...] + gt_ref[0] * (w1 * y0 + w2 * y1)
    o_ref[...] = x2 * lax.rsqrt(jnp.mean(x2 * x2, axis=-1, keepdims=True) + EPS) * g_ref[...]


def _final(x1, ys, dest, route, mod3, g_final, seq):
    n, d = x1.shape
    n_tiles = n // TM
    tiles_per_seq = seq // TM
    row = lambda i: (i, 0)
    return pl.pallas_call(
        _final_kernel,
        grid=(n_tiles,),
        in_specs=[pl.BlockSpec((1, 8, TM), _dest_block, memory_space=pltpu.SMEM),
                  pl.BlockSpec((1, 8, TM), lambda i: _dest_block(jnp.minimum(i + 1, n_tiles - 1)),
                               memory_space=pltpu.SMEM),
                  pl.BlockSpec((TM, d), row),
                  pl.BlockSpec((TM, LANES), row),
                  pl.BlockSpec((1, 1, d), lambda i: ((i // tiles_per_seq) * 6 + 5, 0, 0)),
                  pl.BlockSpec((1, d), lambda i: (0, 0)),
                  pl.BlockSpec(memory_space=pl.ANY)],
        out_specs=pl.BlockSpec((TM, d), row),
        out_shape=jax.ShapeDtypeStruct((n, d), jnp.float32),
        scratch_shapes=[pltpu.VMEM((2,) + _tiles_shape(2 * TM, d), jnp.float32),
                        pltpu.SemaphoreType.DMA((2,))],
        compiler_params=_cparams(("arbitrary",)),
        name="final",
    )(dest, dest, x1, route, mod3, g_final, ys)


def _block_layout(counts, n):
    cnt = counts[0, :N_EXPERTS].astype(jnp.int32)
    blocks = (cnt + BM - 1) // BM
    bends = jnp.cumsum(blocks)
    pstarts = (bends - blocks) * BM
    n_blk = (2 * n) // BM + N_EXPERTS
    pst_row = jnp.zeros((1, LANES), jnp.float32).at[0, :N_EXPERTS].set(pstarts.astype(jnp.float32))
    zrow = ((pstarts + cnt) // SUBLANES).astype(jnp.int32)
    n_used = bends[-1:].astype(jnp.int32)
    blk_e = jnp.sum(bends[None, :] <= jnp.arange(n_blk, dtype=jnp.int32)[:, None], axis=1)
    blk_e = jnp.minimum(blk_e, N_EXPERTS - 1).astype(jnp.int32)
    return pst_row, zrow, n_used, blk_e


def _layer(x2, c, pos2, w_ada, b_ada, g_mix, w_in, w_o, g_ffn, w_rg, w_re, w_up, w_gate, w_down, seq):
    n, d = x2.shape
    bsz = n // seq
    bf = jnp.bfloat16
    d_a = d // 2
    d_i = N_IDX_HEADS * IDX_DIM
    mod3 = _ada(c, w_ada, b_ada).reshape(bsz * 6, 1, d)

    c0 = 3 * d_a + d_i
    c1 = c0 + IDX_DIM + N_IDX_HEADS
    w_cat = jnp.concatenate([w_in[:, :c0], w_in[:, c0:c1],
                             jnp.zeros((d, LANES - (c1 - c0)), w_in.dtype), w_in[:, c1:]], axis=1).astype(bf)
    qat, ka, vat, qit, kiw, wit, qb, kb, vb, gb = _proj(x2, pos2, mod3, g_mix.reshape(1, d), w_cat, seq)

    n_kt = seq // TM
    ya = _dsa(qat, qit, wit, kiw.reshape(bsz, n_kt, TM, LANES), ka.reshape(bsz, n_kt, TM, d_a), vat, seq)
    yb = _ret(qb, kb, vb, gb, seq)

    w_route = jnp.concatenate([jnp.transpose(w_re, (1, 0, 2)).reshape(d, N_EXPERTS), w_rg,
                               jnp.zeros((d, LANES - N_EXPERTS - N_GROUPS), w_rg.dtype)], axis=1).astype(bf)
    x1, h2, route, counts = _mix_out(x2, ya, yb, w_o[:d_a].astype(bf), w_o[d_a:].astype(bf), mod3,
                                     g_ffn.reshape(1, d), w_route, seq)

    pst_row, zrow, n_used, blk_e = _block_layout(counts, n)
    dest = _plan(route, pst_row)
    xs = _dispatch(h2, dest, zrow, n_used, blk_e.shape[0])
    ys = _experts(xs, w_gate, w_up, w_down, blk_e, n_used)
    return x1, ys, dest, route, mod3


def kernel(x, c, positions, w_ada, b_ada, g_norm_mix, w_in, w_o, g_norm_ffn, w_router_group, w_router_expert,
           w_up, w_gate, w_down, g_norm_final):
    bsz, seq, d = x.shape
    depth = w_ada.shape[0]
    assert depth == 1, "the final norm is fused into the last layer's combine kernel"
    assert seq % (2 * TM) == 0 and seq % (RET_C * RET_CHUNKS) == 0 and seq % MIX_TM == 0 and (2 * bsz * seq) % BM == 0
    assert (bsz * seq) % PLAN_TM == 0 and PLAN_TM % TM == 0
    x2 = x.reshape(bsz * seq, d)
    pos2 = positions.astype(jnp.float32).reshape(bsz * seq, 1)
    x1, ys, dest, route, mod3 = _layer(x2, c, pos2, w_ada[0], b_ada[0], g_norm_mix[0], w_in[0], w_o[0],
                                       g_norm_ffn[0], w_router_group[0], w_router_expert[0], w_up[0], w_gate[0],
                                       w_down[0], seq)
    out = _final(x1, ys, dest, route, mod3, g_norm_final.reshape(1, d), seq)
    return out.reshape(bsz, seq, d)
```

```python
import functools

import jax
import jax.numpy as jnp
import numpy as np
from jax import lax
from jax.experimental import pallas as pl
from jax.experimental.pallas import tpu as pltpu

CHUNK = 64
HEAD_DIM = 64
N_IDX_HEADS = 16
IDX_DIM = 64
TOPK_MAX = 256
ROPE_THETA = 500000.0
ROT_DIM = HEAD_DIM // 4
RET_THETA = 10000.0
N_GROUPS = 4
EXPERTS_PER_GROUP = 8
N_EXPERTS = N_GROUPS * EXPERTS_PER_GROUP
EPS = 1e-6

LANES = 128
SUBLANES = 8
VMEM_LIMIT = 56 * 1024 * 1024

TM = 256
QB = TM
RET_C = 256
RET_CHUNKS = 2
BM = 256
PLAN_TM = 1024
MIX_TM = 512
CNT_ROWS = 16
ONES_ROWS = 16
HEAD_GROUP = 4
IDX_TILES = 8
ATT_TILES = 8
BISECT_PER_CHECK = 2
MAX_BISECT = 40

NEG_BIG = -1e30
LOG2E = 1.4426950408889634


def _cparams(sem):
    return pltpu.CompilerParams(dimension_semantics=sem, vmem_limit_bytes=VMEM_LIMIT)


def _ada_kernel(c_ref, w_ref, b_ref, o_ref):
    o_ref[...] = jnp.dot(c_ref[...], w_ref[...], preferred_element_type=jnp.float32) + b_ref[...]


def _ada(c, w_ada, b_ada):
    bsz, d = c.shape
    n_out = w_ada.shape[1]
    return pl.pallas_call(
        _ada_kernel,
        grid=(n_out // d,),
        in_specs=[pl.BlockSpec((bsz, d), lambda j: (0, 0)),
                  pl.BlockSpec((d, d), lambda j: (0, j)),
                  pl.BlockSpec((1, d), lambda j: (0, j))],
        out_specs=pl.BlockSpec((bsz, d), lambda j: (0, j)),
        out_shape=jax.ShapeDtypeStruct((bsz, n_out), jnp.float32),
        compiler_params=_cparams(("arbitrary",)),
        name="ada",
    )(c, w_ada, b_ada.reshape(1, n_out))


def _rmsnorm_mod(x, g, sc, sh):
    xn = x * lax.rsqrt(jnp.mean(x * x, axis=-1, keepdims=True) + EPS)
    return xn * g * (1.0 + sc) + sh


def _rope_lanes(x, cos, sin_lo, sin_hi, half):
    cols = []
    for k in range(x.shape[1] // LANES):
        xb = x[:, k * LANES:(k + 1) * LANES]
        cols.append(xb * cos + pltpu.roll(xb, LANES - half, 1) * sin_lo + pltpu.roll(xb, half, 1) * sin_hi)
    return cols[0] if len(cols) == 1 else jnp.concatenate(cols, axis=1)


def _proj_kernel(x_ref, pos_ref, sc_ref, sh_ref, g_ref, w_ref, tab_ref,
                 qat_ref, ka_ref, vat_ref, qit_ref, kiw_ref, wit_ref, qb_ref, kb_ref, vb_ref, gb_ref,
                 *, d_a, d_i, d_b):
    h = _rmsnorm_mod(x_ref[...], g_ref[...], sc_ref[0], sh_ref[0]).astype(jnp.bfloat16)
    pos = pos_ref[...]
    ang_a = pos * tab_ref[0:1, :]
    cos_a, sin_a = jnp.cos(ang_a), jnp.sin(ang_a)
    sa_lo, sa_hi = sin_a * tab_ref[1:2, :], sin_a * tab_ref[2:3, :]
    ang_b = pos * tab_ref[3:4, :]
    cos_b, sin_b = jnp.cos(ang_b), jnp.sin(ang_b)
    sb_lo, sb_hi = sin_b * tab_ref[4:5, :], sin_b * tab_ref[5:6, :]
    half_a, half_b = ROT_DIM // 2, HEAD_DIM // 2

    def seg(lo, width):
        return jnp.dot(h, w_ref[:, lo:lo + width], preferred_element_type=jnp.float32)

    o = 0
    qa = seg(o, d_a); o += d_a
    qat_ref[0, 0] = (_rope_lanes(qa, cos_a, sa_lo, sa_hi, half_a) * (HEAD_DIM ** -0.5 * LOG2E)).T.astype(qat_ref.dtype)
    ka = seg(o, d_a); o += d_a
    ka_ref[...] = _rope_lanes(ka, cos_a, sa_lo, sa_hi, half_a).astype(ka_ref.dtype)
    vat_ref[0, 0] = seg(o, d_a).T.astype(vat_ref.dtype); o += d_a
    qi = seg(o, d_i); o += d_i
    qit_ref[0, 0] = (_rope_lanes(qi, cos_a, sa_lo, sa_hi, half_a) * (IDX_DIM ** -0.5)).T.astype(qit_ref.dtype)
    kw = seg(o, LANES); o += LANES
    kiw_ref[...] = _rope_lanes(kw, cos_a, sa_lo, sa_hi, half_a).astype(kiw_ref.dtype)
    wit_ref[0, 0] = kw.T[IDX_DIM:IDX_DIM + N_IDX_HEADS, :] * (N_IDX_HEADS ** -0.5)
    qb = seg(o, d_b); o += d_b
    qb_ref[...] = _rope_lanes(qb, cos_b, sb_lo, sb_hi, half_b).astype(qb_ref.dtype)
    kb = seg(o, d_b); o += d_b
    kb_ref[...] = (_rope_lanes(kb, cos_b, sb_lo, sb_hi, half_b) * (HEAD_DIM ** -0.5)).astype(kb_ref.dtype)
    vb_ref[...] = seg(o, d_b).astype(vb_ref.dtype); o += d_b
    gb_ref[...] = seg(o, d_b)


def _rope_tables():
    lane = jnp.arange(LANES) % HEAD_DIM
    rows = []
    for rot, theta in ((ROT_DIM, ROPE_THETA), (HEAD_DIM, RET_THETA)):
        half = rot // 2
        inv_freq = theta ** (-jnp.arange(half, dtype=jnp.float32) / half)
        rows.append(jnp.where(lane < rot, inv_freq[lane % half], 0.0))
        rows.append(jnp.where(lane < half, -1.0, 0.0))
        rows.append(jnp.where((lane >= half) & (lane < rot), 1.0, 0.0))
    rows += [jnp.zeros((LANES,), jnp.float32)] * 2
    return jnp.stack(rows).astype(jnp.float32)


def _proj(x2, pos2, mod3, g_mix, w_cat, seq):
    n, d = x2.shape
    bsz = n // seq
    d_a = d // 2
    d_b = d // 2
    d_i = N_IDX_HEADS * IDX_DIM
    tiles_per_seq = seq // TM
    tab = _rope_tables()
    row = lambda i: (i, 0)
    const = lambda i: (0, 0)
    tile4 = lambda i: (i // tiles_per_seq, i % tiles_per_seq, 0, 0)
    bf = jnp.bfloat16
    out_shape = (
        jax.ShapeDtypeStruct((bsz, tiles_per_seq, d_a, TM), bf),
        jax.ShapeDtypeStruct((n, d_a), bf),
        jax.ShapeDtypeStruct((bsz, tiles_per_seq, d_a, TM), bf),
        jax.ShapeDtypeStruct((bsz, tiles_per_seq, d_i, TM), bf),
        jax.ShapeDtypeStruct((n, LANES), bf),
        jax.ShapeDtypeStruct((bsz, tiles_per_seq, N_IDX_HEADS, TM), jnp.float32),
        jax.ShapeDtypeStruct((n, d_b), bf),
        jax.ShapeDtypeStruct((n, d_b), bf),
        jax.ShapeDtypeStruct((n, d_b), bf),
        jax.ShapeDtypeStruct((n, d_b), jnp.float32),
    )
    out_specs = (
        pl.BlockSpec((1, 1, d_a, TM), tile4),
        pl.BlockSpec((TM, d_a), row),
        pl.BlockSpec((1, 1, d_a, TM), tile4),
        pl.BlockSpec((1, 1, d_i, TM), tile4),
        pl.BlockSpec((TM, LANES), row),
        pl.BlockSpec((1, 1, N_IDX_HEADS, TM), tile4),
        pl.BlockSpec((TM, d_b), row),
        pl.BlockSpec((TM, d_b), row),
        pl.BlockSpec((TM, d_b), row),
        pl.BlockSpec((TM, d_b), row),
    )
    return pl.pallas_call(
        functools.partial(_proj_kernel, d_a=d_a, d_i=d_i, d_b=d_b),
        grid=(n // TM,),
        in_specs=[pl.BlockSpec((TM, d), row),
                  pl.BlockSpec((TM, 1), row),
                  pl.BlockSpec((1, 1, d), lambda i: ((i // tiles_per_seq) * 6 + 1, 0, 0)),
                  pl.BlockSpec((1, 1, d), lambda i: ((i // tiles_per_seq) * 6 + 0, 0, 0)),
                  pl.BlockSpec((1, d), const),
                  pl.BlockSpec(w_cat.shape, const),
                  pl.BlockSpec(tab.shape, const)],
        out_specs=out_specs,
        out_shape=out_shape,
        compiler_params=_cparams(("arbitrary",)),
        name="proj",
    )(x2, pos2, mod3, mod3, g_mix, w_cat, tab)


def _row_blocks(x, rows):
    return [x[r * rows:(r + 1) * rows] for r in range(x.shape[0] // rows)]


def _dsa_kernel(qat_ref, qit_ref, wit_ref, kiw_ref, ka_ref, vat_ref, tri_ref, o_ref,
                qix_ref, qmx_ref, sc_ref, m_ref, l_ref, acc_ref, sa_ref, sb_ref, mxa_ref, mxb_ref, lohi_ref,
                *, k_top, n_heads):
    i = pl.program_id(1)
    n_real = (i * QB + QB + TM - 1) // TM
    n_kt = n_real // 2 * 2
    odd_tile = n_real % 2 == 1

    zero_rows = jnp.zeros((LANES - IDX_DIM, QB), qix_ref.dtype)
    for h in range(N_IDX_HEADS):
        qix_ref[h] = jnp.concatenate([qit_ref[0, 0, h * IDX_DIM:(h + 1) * IDX_DIM, :], zero_rows], axis=0)
    row_q = lax.broadcasted_iota(jnp.int32, (LANES, QB), 0)
    for h in range(n_heads):
        pair = qat_ref[0, 0, (h // 2) * LANES:(h // 2 + 1) * LANES, :]
        own = (row_q < HEAD_DIM) if h % 2 == 0 else (row_q >= HEAD_DIM)
        qmx_ref[h] = jnp.where(own, pair, jnp.zeros_like(pair))

    q_chunk = (i * QB + lax.broadcasted_iota(jnp.int32, (1, QB), 1)) // CHUNK
    key_chunk_in_tile = lax.broadcasted_iota(jnp.int32, (TM, QB), 0) // CHUNK
    w_all = wit_ref[0, 0]

    def idx_tiles(tiles):
        lo, hi = lohi_ref[0], lohi_ref[1]
        for j in tiles:
            kt = kiw_ref[0, j]
            acc = None
            for h in range(N_IDX_HEADS):
                d = jnp.dot(kt, qix_ref[h], preferred_element_type=jnp.float32)
                t = w_all[h:h + 1, :] * jnp.maximum(d, 0.0)
                acc = t if acc is None else acc + t
            adm = key_chunk_in_tile <= q_chunk - j * (TM // CHUNK)
            s = jnp.where(adm, acc, -jnp.inf)
            sc_ref[j] = s
            lo = jnp.minimum(lo, functools.reduce(jnp.minimum, _row_blocks(jnp.where(adm, acc, jnp.inf), SUBLANES)))
            hi = jnp.maximum(hi, functools.reduce(jnp.maximum, _row_blocks(s, SUBLANES)))
        lohi_ref[0], lohi_ref[1] = lo, hi

    lohi_ref[0] = jnp.full((SUBLANES, QB), jnp.inf, jnp.float32)
    lohi_ref[1] = jnp.full((SUBLANES, QB), -jnp.inf, jnp.float32)

    def idx_step(g, carry):
        idx_tiles([IDX_TILES * g + u for u in range(IDX_TILES)])
        return carry

    lax.fori_loop(0, n_kt // IDX_TILES, idx_step, 0)

    def idx_tail(p, carry):
        first = n_kt // IDX_TILES * IDX_TILES + 2 * p
        idx_tiles([first, first + 1])
        return carry

    lax.fori_loop(0, n_kt % IDX_TILES // 2, idx_tail, 0)

    @pl.when(odd_tile)
    def _():
        idx_tiles([n_real - 1])

    lo = jnp.min(lohi_ref[0], axis=0, keepdims=True)
    hi = jnp.max(lohi_ref[1], axis=0, keepdims=True)
    n_adm = ((i * QB + lax.broadcasted_iota(jnp.int32, (1, QB), 1)) // CHUNK + 1) * CHUNK

    def bisect_cond(carry):
        it, _, _, _, unsettled = carry
        return (it < MAX_BISECT) & (unsettled > 0.0)

    def bisect(carry):
        it, lo, hi, cnt_lo, _ = carry
        for _ in range(BISECT_PER_CHECK):
            mid = lo + (hi - lo) * 0.5
            mid_b = jnp.broadcast_to(mid, (CNT_ROWS, QB))

            def count_tile(j, cnt, mid_b=mid_b):
                for blk in _row_blocks(sc_ref[j], CNT_ROWS):
                    cnt = cnt + jnp.where(blk >= mid_b, 1.0, 0.0)
                return cnt

            cnt = lax.fori_loop(0, n_real, count_tile, jnp.zeros((CNT_ROWS, QB), jnp.float32))
            c = jnp.sum(cnt, axis=0, keepdims=True)
            ge = c >= k_top
            lo = jnp.where(ge, mid, lo)
            hi = jnp.where(ge, hi, mid)
            cnt_lo = jnp.where(ge, c, cnt_lo)
        return it + BISECT_PER_CHECK, lo, hi, cnt_lo, jnp.max(jnp.where(cnt_lo > k_top, 1.0, 0.0))

    cnt0 = n_adm.astype(jnp.float32)
    _, thr, thr_hi, _, unsettled = lax.while_loop(bisect_cond, bisect,
                                                  (0, lo, hi, cnt0, jnp.max(jnp.where(cnt0 > k_top, 1.0, 0.0))))

    @pl.when(unsettled <= 0.0)
    def _():
        def bias_tile(j, carry):
            sc_ref[j] = jnp.where(sc_ref[j] >= thr, 0.0, NEG_BIG)
            return carry

        lax.fori_loop(0, n_real, bias_tile, 0)

    @pl.when(unsettled > 0.0)
    def _():
        def count_hi(j, cnt):
            return cnt + functools.reduce(jnp.add, _row_blocks(jnp.where(sc_ref[j] >= thr_hi, 1.0, 0.0), SUBLANES))

        above = jnp.sum(lax.fori_loop(0, n_real, count_hi, jnp.zeros((SUBLANES, QB), jnp.float32)),
                        axis=0, keepdims=True)
        top = jnp.where(above < k_top, thr_hi, jnp.inf)
        room = k_top - jnp.where(above < k_top, above, 0.0)

        def bias_tile(j, taken):
            s = sc_ref[j]
            tied = (s >= thr) & (s < top)
            before = jnp.dot(tri_ref[...], jnp.where(tied, 1.0, 0.0).astype(tri_ref.dtype),
                             preferred_element_type=jnp.float32)
            keep = (s >= top) | (tied & (taken + before <= room))
            sc_ref[j] = jnp.where(keep, 0.0, NEG_BIG)
            return taken + before[TM - 1:TM, :]

        lax.fori_loop(0, n_real, bias_tile, jnp.zeros((1, QB), jnp.float32))

    m_ref[...] = jnp.full(m_ref.shape, NEG_BIG, jnp.float32)
    l_ref[...] = jnp.zeros(l_ref.shape, jnp.float32)
    acc_ref[...] = jnp.zeros(acc_ref.shape, jnp.float32)

    def pair(h):
        return slice((h // 2) * LANES, (h // 2 + 1) * LANES)

    def logits_into(s_ref, mx_ref, j, heads):
        bias = sc_ref[j]
        for h in heads:
            s = jnp.dot(ka_ref[0, j, :, pair(h)], qmx_ref[h], preferred_element_type=jnp.float32) + bias
            s_ref[h] = s
            mx_ref[h] = jnp.max(s, axis=0, keepdims=True)

    ones_rows = jnp.ones((ONES_ROWS, TM), vat_ref.dtype)

    def absorb(s_ref, mx_ref, j, heads):
        for h in heads:
            m_old = m_ref[h]
            m_new = jnp.maximum(m_old, mx_ref[h])
            alpha = jnp.exp2(m_old - m_new)
            p = jnp.exp2(s_ref[h] - m_new).astype(vat_ref.dtype)
            pv = jnp.dot(jnp.concatenate([vat_ref[0, j, pair(h), :], ones_rows], axis=0), p,
                         preferred_element_type=jnp.float32)
            acc_ref[h] = acc_ref[h] * alpha + pv[0:LANES]
            l_ref[h] = l_ref[h] * alpha + pv[LANES:LANES + SUBLANES]
            m_ref[h] = m_new

    for h0 in range(0, n_heads, HEAD_GROUP):
        heads = range(h0, h0 + HEAD_GROUP)
        logits_into(sa_ref, mxa_ref, 0, heads)

        def tile_pair(t, heads=heads):
            logits_into(sb_ref, mxb_ref, t + 1, heads)
            absorb(sa_ref, mxa_ref, t, heads)
            logits_into(sa_ref, mxa_ref, jnp.minimum(t + 2, n_real - 1), heads)
            absorb(sb_ref, mxb_ref, t + 1, heads)

        def attn_step(g, carry):
            for u in range(0, ATT_TILES, 2):
                tile_pair(ATT_TILES * g + u)
            return carry

        lax.fori_loop(0, n_kt // ATT_TILES, attn_step, 0)

        def attn_tail(p, carry):
            tile_pair(n_kt // ATT_TILES * ATT_TILES + 2 * p)
            return carry

        lax.fori_loop(0, n_kt % ATT_TILES // 2, attn_tail, 0)

        @pl.when(odd_tile)
        def _(heads=heads):
            absorb(sa_ref, mxa_ref, n_real - 1, heads)

    for hp in range(n_heads // 2):
        even = acc_ref[2 * hp] / l_ref[2 * hp, 0:1, :]
        odd = acc_ref[2 * hp + 1] / l_ref[2 * hp + 1, 0:1, :]
        o_ref[:, hp * LANES:(hp + 1) * LANES] = jnp.where(row_q < HEAD_DIM, even, odd).T.astype(o_ref.dtype)


def _dsa(qat, qit, wit, kiw4, ka4, vat, seq):
    bsz, n_kt, d_a, _ = qat.shape
    n_heads = d_a // HEAD_DIM
    n_qb = seq // QB
    k_top = min(TOPK_MAX, seq // 4)
    qtile = lambda b, i: (b, i, 0, 0)
    per_b = lambda b, i: (b, 0, 0, 0)
    f32 = jnp.float32
    return pl.pallas_call(
        functools.partial(_dsa_kernel, k_top=float(k_top), n_heads=n_heads),
        grid=(bsz, n_qb),
        in_specs=[pl.BlockSpec((1, 1, d_a, QB), qtile),
                  pl.BlockSpec((1, 1, qit.shape[2], QB), qtile),
                  pl.BlockSpec((1, 1, N_IDX_HEADS, QB), qtile),
                  pl.BlockSpec((1, n_kt, TM, LANES), per_b, pipeline_mode=pl.Buffered(1)),
                  pl.BlockSpec((1, n_kt, TM, d_a), per_b, pipeline_mode=pl.Buffered(1)),
                  pl.BlockSpec((1, n_kt, d_a, TM), per_b, pipeline_mode=pl.Buffered(1)),
                  pl.BlockSpec((TM, TM), lambda b, i: (0, 0), pipeline_mode=pl.Buffered(1))],
        out_specs=pl.BlockSpec((QB, d_a), lambda b, i: (b * n_qb + i, 0)),
        out_shape=jax.ShapeDtypeStruct((bsz * seq, d_a), jnp.bfloat16),
        scratch_shapes=[pltpu.VMEM((N_IDX_HEADS, LANES, QB), jnp.bfloat16),
                        pltpu.VMEM((n_heads, LANES, QB), jnp.bfloat16),
                        pltpu.VMEM((n_kt, TM, QB), f32),
                        pltpu.VMEM((n_heads, 1, QB), f32),
                        pltpu.VMEM((n_heads, SUBLANES, QB), f32),
                        pltpu.VMEM((n_heads, LANES, QB), f32),
                        pltpu.VMEM((n_heads, TM, QB), f32),
                        pltpu.VMEM((n_heads, TM, QB), f32),
                        pltpu.VMEM((n_heads, 1, QB), f32),
                        pltpu.VMEM((n_heads, 1, QB), f32),
                        pltpu.VMEM((2, SUBLANES, QB), f32)],
        compiler_params=_cparams(("arbitrary", "arbitrary")),
        name="dsa",
    )(qat, qit, wit, kiw4, ka4, vat, jnp.asarray(np.tril(np.ones((TM, TM), np.float32)), jnp.bfloat16))


def _group_mean(y, avg):
    hi = y.astype(jnp.bfloat16)
    lo = (y - hi.astype(jnp.float32)).astype(jnp.bfloat16)
    return (jnp.dot(hi, avg, preferred_element_type=jnp.float32)
            + jnp.dot(lo, avg, preferred_element_type=jnp.float32))


def _ret_kernel(q_ref, k_ref, v_ref, g_ref, dec_ref, zt_ref, xi_ref, gc_ref, blk_ref, avg_ref, o_ref, st_ref,
                *, n_heads):
    @pl.when(pl.program_id(1) == 0)
    def _():
        st_ref[...] = jnp.zeros_like(st_ref)

    even = lax.broadcasted_iota(jnp.int32, (RET_C, LANES), 1) < HEAD_DIM
    avg = avg_ref[...]
    for p in range(n_heads // 2):
        sl = slice(p * LANES, (p + 1) * LANES)
        state = st_ref[p]
        for c in range(q_ref.shape[0] // RET_C):
            rows = slice(c * RET_C, (c + 1) * RET_C)
            qp, kp, vp = q_ref[rows, sl], k_ref[rows, sl], v_ref[rows, sl]
            kpt = kp.astype(jnp.float32).T
            kpt_b = kpt.astype(kp.dtype)
            inner = None
            for e in range(2):
                q_e = jnp.where(even if e == 0 else jnp.logical_not(even), qp, jnp.zeros_like(qp))
                s = jnp.dot(q_e, kpt_b, preferred_element_type=jnp.float32) * dec_ref[2 * p + e]
                t = jnp.dot(s.astype(vp.dtype), vp, preferred_element_type=jnp.float32)
                inner = t if e == 0 else jnp.where(even, inner, t)
            cross = jnp.dot(qp, state.astype(qp.dtype), preferred_element_type=jnp.float32) * xi_ref[p]
            y = inner + cross
            yc = y - _group_mean(y, avg)
            yn = yc * lax.rsqrt(_group_mean(yc * yc, avg) + EPS)
            g = g_ref[rows, sl]
            o_ref[rows, sl] = (g / (1.0 + jnp.exp(-g)) * yn).astype(o_ref.dtype)
            kz = (kpt * zt_ref[p]).astype(kp.dtype)
            kv = jnp.dot(kz, vp, preferred_element_type=jnp.float32)
            state = state * gc_ref[p] + kv * blk_ref[...]
        st_ref[p] = state


def _ret_consts(n_heads):
    log_gamma = jnp.log1p(-jnp.exp2(-5.0 - jnp.arange(n_heads, dtype=jnp.float32)))
    pos = jnp.arange(RET_C, dtype=jnp.float32)
    diff = pos[:, None] - pos[None, :]
    dec = jnp.where(diff[None] >= 0, jnp.exp(jnp.maximum(diff, 0.0)[None] * log_gamma[:, None, None]), 0.0)
    zeta = jnp.exp((RET_C - 1.0 - pos)[None, :] * log_gamma[:, None])
    xi = jnp.exp((pos + 1.0)[None, :] * log_gamma[:, None])
    gc = jnp.exp(RET_C * log_gamma)
    n_pairs = n_heads // 2
    lanes = lambda a: jnp.repeat(a.reshape(n_pairs, 2, -1), HEAD_DIM, axis=1)
    zt = lanes(zeta)
    xi_p = jnp.swapaxes(lanes(xi), 1, 2)
    gc_p = jnp.broadcast_to(lanes(gc[:, None]), (n_pairs, LANES, LANES))
    head_of = jnp.arange(LANES) // HEAD_DIM
    blk = (head_of[:, None] == head_of[None, :]).astype(jnp.float32)
    avg = (blk / HEAD_DIM).astype(jnp.bfloat16)
    f32 = lambda a: a.astype(jnp.float32)
    return dec, f32(zt), f32(xi_p), f32(gc_p), blk, avg


def _ret(qb, kb, vb, gb, seq):
    n, d_b = qb.shape
    bsz = n // seq
    n_heads = d_b // HEAD_DIM
    step = RET_C * RET_CHUNKS
    n_c = seq // step
    consts = _ret_consts(n_heads)
    row = lambda b, c: (b * n_c + c, 0)
    const_spec = lambda a: pl.BlockSpec(a.shape, lambda b, c: (0,) * a.ndim)
    return pl.pallas_call(
        functools.partial(_ret_kernel, n_heads=n_heads),
        grid=(bsz, n_c),
        in_specs=[pl.BlockSpec((step, d_b), row)] * 4 + [const_spec(a) for a in consts],
        out_specs=pl.BlockSpec((step, d_b), row),
        out_shape=jax.ShapeDtypeStruct((n, d_b), jnp.bfloat16),
        scratch_shapes=[pltpu.VMEM((n_heads // 2, LANES, LANES), jnp.float32)],
        compiler_params=_cparams(("arbitrary", "arbitrary")),
        name="ret",
    )(qb, kb, vb, gb, *consts)


def _tiles_shape(rows, d):
    return (rows // SUBLANES, d // LANES, SUBLANES, LANES)


def _to_tiles(ref, x):
    for s in range(ref.shape[1]):
        ref[:, s] = x[:, s * LANES:(s + 1) * LANES].reshape(ref.shape[0], SUBLANES, LANES)


def _from_tiles(ref):
    rows = ref.shape[0] * SUBLANES
    return jnp.concatenate([ref[:, s].reshape(rows, LANES) for s in range(ref.shape[1])], axis=1)


def _lane_first_eq(x, m, lane):
    return jnp.min(jnp.where(x == m, lane, float(LANES)), axis=1, keepdims=True)


def _mix_out_kernel(x_ref, ya_ref, yb_ref, woa_ref, wob_ref, gt_ref, sc_ref, sh_ref, g_ref, wr_ref, tri_ref,
                    x1_ref, h2_ref, route_ref, cnt_ref, carry_ref):
    @pl.when(pl.program_id(0) == 0)
    def _():
        carry_ref[...] = jnp.zeros_like(carry_ref)

    mix = (jnp.dot(ya_ref[...], woa_ref[...], preferred_element_type=jnp.float32)
           + jnp.dot(yb_ref[...], wob_ref[...], preferred_element_type=jnp.float32))
    x1 = x_ref[...] + gt_ref[0] * mix
    x1_ref[...] = x1
    h2 = _rmsnorm_mod(x1, g_ref[...], sc_ref[0], sh_ref[0])
    _to_tiles(h2_ref, h2)

    lg = jnp.dot(h2.astype(jnp.bfloat16), wr_ref[...], preferred_element_type=jnp.float32)
    lane = lax.broadcasted_iota(jnp.int32, lg.shape, 1).astype(jnp.float32)
    is_grp = (lane >= N_EXPERTS) & (lane < N_EXPERTS + N_GROUPS)
    gl = jnp.where(is_grp, lg, -jnp.inf)
    gmax = jnp.max(gl, axis=1, keepdims=True)
    grp = _lane_first_eq(gl, gmax, lane) - N_EXPERTS
    p_grp = 1.0 / jnp.sum(jnp.exp(gl - gmax), axis=1, keepdims=True)
    in_grp = jnp.floor(lane * (1.0 / EXPERTS_PER_GROUP)) == grp
    f = jnp.where(in_grp & (lane < N_EXPERTS), lg, -jnp.inf)
    f1 = jnp.max(f, axis=1, keepdims=True)
    e1 = _lane_first_eq(f, f1, lane)
    f = jnp.where(lane == e1, -jnp.inf, f)
    f2 = jnp.max(f, axis=1, keepdims=True)
    e2 = _lane_first_eq(f, f2, lane)
    a2 = jnp.exp(f2 - f1)
    w1 = p_grp / (1.0 + a2)
    w2 = p_grp * a2 / (1.0 + a2)

    oh1 = jnp.where(lane == e1, 1.0, 0.0)
    oh2 = jnp.where(lane == e2, 1.0, 0.0)
    both = oh1 + oh2
    before = jnp.dot(tri_ref[...], both.astype(jnp.bfloat16), preferred_element_type=jnp.float32) + carry_ref[...]
    r1 = jnp.sum(before * oh1, axis=1, keepdims=True)
    r2 = jnp.sum(before * oh2, axis=1, keepdims=True)
    carry = carry_ref[...] + jnp.sum(both, axis=0, keepdims=True)
    carry_ref[...] = carry
    cnt_ref[...] = carry

    out = jnp.zeros(lg.shape, jnp.float32)
    for col, val in enumerate((e1, e2, w1, w2, r1, r2)):
        out = jnp.where(lane == col, val, out)
    route_ref[...] = out


def _mix_out(x2, ya, yb, wo_a, wo_b, mod3, g_ffn, w_route, seq):
    n, d = x2.shape
    tm = MIX_TM
    tiles_per_seq = seq // tm
    tri = jnp.asarray(np.tril(np.ones((tm, tm), np.float32), -1), jnp.bfloat16)
    row = lambda i: (i, 0)
    const = lambda i: (0, 0)
    modk = lambda k: pl.BlockSpec((1, 1, d), lambda i: ((i // tiles_per_seq) * 6 + k, 0, 0))
    return pl.pallas_call(
        _mix_out_kernel,
        grid=(n // tm,),
        in_specs=[pl.BlockSpec((tm, d), row),
                  pl.BlockSpec((tm, ya.shape[1]), row),
                  pl.BlockSpec((tm, yb.shape[1]), row),
                  pl.BlockSpec(wo_a.shape, const),
                  pl.BlockSpec(wo_b.shape, const),
                  modk(2), modk(4), modk(3),
                  pl.BlockSpec((1, d), const),
                  pl.BlockSpec(w_route.shape, const),
                  pl.BlockSpec(tri.shape, const)],
        out_specs=(pl.BlockSpec((tm, d), row), pl.BlockSpec(_tiles_shape(tm, d), lambda i: (i, 0, 0, 0)),
                   pl.BlockSpec((tm, LANES), row), pl.BlockSpec((1, LANES), const)),
        out_shape=(jax.ShapeDtypeStruct((n, d), jnp.float32), jax.ShapeDtypeStruct(_tiles_shape(n, d), jnp.float32),
                   jax.ShapeDtypeStruct((n, LANES), jnp.float32), jax.ShapeDtypeStruct((1, LANES), jnp.float32)),
        scratch_shapes=[pltpu.VMEM((1, LANES), jnp.float32)],
        compiler_params=_cparams(("arbitrary",)),
        name="mix_out",
    )(x2, ya, yb, wo_a, wo_b, mod3, mod3, mod3, g_ffn, w_route, tri)


def _plan_kernel(route_ref, pst_ref, dest_ref):
    r = route_ref[...]
    lane = lax.broadcasted_iota(jnp.int32, r.shape, 1).astype(jnp.float32)
    pst = pst_ref[...]
    d1 = jnp.sum(jnp.where(lane == r[:, 0:1], pst, 0.0), axis=1, keepdims=True) + r[:, 4:5]
    d2 = jnp.sum(jnp.where(lane == r[:, 1:2], pst, 0.0), axis=1, keepdims=True) + r[:, 5:6]
    t1 = jnp.floor(d1 * (1.0 / SUBLANES))
    t2 = jnp.floor(d2 * (1.0 / SUBLANES))
    packed = jnp.zeros(r.shape, jnp.float32)
    for k, v in enumerate((t1, d1 - t1 * SUBLANES, t2, d2 - t2 * SUBLANES)):
        packed = jnp.where(lane == float(k), v, packed)
    dest_ref[0] = packed.T[0:8, :].astype(jnp.int32)


def _dest_block(i):
    return (i // (PLAN_TM // TM), 0, i % (PLAN_TM // TM))


def _plan(route, pst_row):
    n = route.shape[0]
    return pl.pallas_call(
        _plan_kernel,
        grid=(n // PLAN_TM,),
        in_specs=[pl.BlockSpec((PLAN_TM, LANES), lambda i: (i, 0)), pl.BlockSpec((1, LANES), lambda i: (0, 0))],
        out_specs=pl.BlockSpec((1, 8, PLAN_TM), lambda i: (i, 0, 0)),
        out_shape=jax.ShapeDtypeStruct((n // PLAN_TM, 8, PLAN_TM), jnp.int32),
        compiler_params=_cparams(("arbitrary",)),
        name="plan",
    )(route, pst_row)


def _dispatch_kernel(zrow_ref, n_used_ref, dest_ref, h2_ref, xs_hbm, zbuf, sem, zsem, *, n_blk):
    i = pl.program_id(0)

    @pl.when(i == 0)
    def _():
        zbuf[...] = jnp.zeros(zbuf.shape, zbuf.dtype)
        blk_tiles = BM // SUBLANES
        for e in range(N_EXPERTS):
            pltpu.make_async_copy(zbuf, xs_hbm.at[pl.ds(zrow_ref[e], blk_tiles)], zsem).start()
        for e in range(N_EXPERTS):
            pltpu.make_async_copy(zbuf, xs_hbm.at[pl.ds(0, blk_tiles)], zsem).wait()
        for b in range(N_EXPERTS + 1):
            @pl.when(n_used_ref[0] + b <= n_blk)
            def _():
                tail = pltpu.make_async_copy(zbuf, xs_hbm.at[pl.ds((n_used_ref[0] + b) * blk_tiles, blk_tiles)], zsem)
                tail.start()
                tail.wait()

    def issue(k, c):
        for u in range(SUBLANES):
            r = k * SUBLANES + u
            row = h2_ref.at[k, :, u, :]
            pltpu.make_async_copy(row, xs_hbm.at[dest_ref[0, 0, r], :, dest_ref[0, 1, r], :], sem).start()
            pltpu.make_async_copy(row, xs_hbm.at[dest_ref[0, 2, r], :, dest_ref[0, 3, r], :], sem).start()
        return c

    lax.fori_loop(0, TM // SUBLANES, issue, 0)
    for _ in range(2):
        pltpu.make_async_copy(h2_ref, xs_hbm.at[pl.ds(0, TM // SUBLANES)], sem).wait()


def _dispatch(h2, dest, zrow, n_used, n_blk):
    n_rows = (n_blk + 1) * BM
    d = h2.shape[1] * LANES
    grid_spec = pltpu.PrefetchScalarGridSpec(
        num_scalar_prefetch=2,
        grid=(h2.shape[0] * SUBLANES // TM,),
        in_specs=[pl.BlockSpec((1, 8, TM), lambda i, z, nu: _dest_block(i), memory_space=pltpu.SMEM),
                  pl.BlockSpec(_tiles_shape(TM, d), lambda i, z, nu: (i, 0, 0, 0))],
        out_specs=pl.BlockSpec(memory_space=pl.ANY),
        scratch_shapes=[pltpu.VMEM(_tiles_shape(BM, d), h2.dtype),
                        pltpu.SemaphoreType.DMA(()),
                        pltpu.SemaphoreType.DMA(())],
    )
    return pl.pallas_call(
        functools.partial(_dispatch_kernel, n_blk=n_blk),
        grid_spec=grid_spec,
        out_shape=jax.ShapeDtypeStruct(_tiles_shape(n_rows, d), h2.dtype),
        compiler_params=_cparams(("arbitrary",)),
        name="dispatch",
    )(zrow, n_used, dest, h2)


def _experts_kernel(blk_e_ref, n_used_ref, x_ref, wg_ref, wu_ref, wd_ref, y_ref, wg_bf, wu_bf, wd_bf):
    j = pl.program_id(0)

    @pl.when((j == 0) | (blk_e_ref[j] != blk_e_ref[jnp.maximum(j - 1, 0)]))
    def _():
        wg_bf[...] = wg_ref[0].astype(wg_bf.dtype)
        wu_bf[...] = wu_ref[0].astype(wu_bf.dtype)
        wd_bf[...] = wd_ref[0].astype(wd_bf.dtype)

    @pl.when(j < n_used_ref[0])
    def _():
        x = _from_tiles(x_ref).astype(wg_bf.dtype)
        a = jnp.dot(x, wg_bf[...], preferred_element_type=jnp.float32)
        b = jnp.dot(x, wu_bf[...], preferred_element_type=jnp.float32)
        hmid = (a / (1.0 + jnp.exp(-a)) * b).astype(x.dtype)
        _to_tiles(y_ref, jnp.dot(hmid, wd_bf[...], preferred_element_type=jnp.float32))

    @pl.when(j >= n_used_ref[0])
    def _():
        y_ref[...] = jnp.zeros(y_ref.shape, y_ref.dtype)


def _experts(xs, wg, wu, wd, blk_e, n_used):
    n_blk = blk_e.shape[0]
    d, d_e = wg.shape[1], wg.shape[2]
    blk = _tiles_shape(BM, d)
    grid_spec = pltpu.PrefetchScalarGridSpec(
        num_scalar_prefetch=2,
        grid=(n_blk,),
        in_specs=[pl.BlockSpec(blk, lambda j, be, nu: (jnp.minimum(j, nu[0] - 1), 0, 0, 0)),
                  pl.BlockSpec((1, d, d_e), lambda j, be, nu: (be[j], 0, 0)),
                  pl.BlockSpec((1, d, d_e), lambda j, be, nu: (be[j], 0, 0)),
                  pl.BlockSpec((1, d_e, d), lambda j, be, nu: (be[j], 0, 0))],
        out_specs=pl.BlockSpec(blk, lambda j, be, nu: (j, 0, 0, 0)),
        scratch_shapes=[pltpu.VMEM((d, d_e), jnp.bfloat16), pltpu.VMEM((d, d_e), jnp.bfloat16),
                        pltpu.VMEM((d_e, d), jnp.bfloat16)],
    )
    return pl.pallas_call(
        _experts_kernel,
        grid_spec=grid_spec,
        out_shape=jax.ShapeDtypeStruct(_tiles_shape(n_blk * BM, d), jnp.float32),
        compiler_params=_cparams(("arbitrary",)),
        name="experts",
    )(blk_e, n_used, xs, wg, wu, wd)


def _final_kernel(dest_ref, dest_next_ref, x1_ref, route_ref, gt_ref, g_ref, ys_hbm, o_ref, ybuf, sem):
    i = pl.program_id(0)
    slot = i % 2

    tiles = TM // SUBLANES

    def gather(d_ref, s):
        def issue(k, c):
            for u in range(SUBLANES):
                r = k * SUBLANES + u
                pltpu.make_async_copy(ys_hbm.at[d_ref[0, 0, r], :, d_ref[0, 1, r], :],
                                      ybuf.at[s, k, :, u, :], sem.at[s]).start()
                pltpu.make_async_copy(ys_hbm.at[d_ref[0, 2, r], :, d_ref[0, 3, r], :],
                                      ybuf.at[s, tiles + k, :, u, :], sem.at[s]).start()
            return c
        lax.fori_loop(0, tiles, issue, 0)

    @pl.when(i == 0)
    def _():
        gather(dest_ref, 0)

    @pl.when(i + 1 < pl.num_programs(0))
    def _():
        gather(dest_next_ref, 1 - slot)

    pltpu.make_async_copy(ys_hbm.at[pl.ds(0, 2 * tiles)], ybuf.at[slot], sem.at[slot]).wait()
    w1 = route_ref[:, 2:3]
    w2 = route_ref[:, 3:4]
    y0 = _from_tiles(ybuf.at[slot, 0:tiles])
    y1 = _from_tiles(ybuf.at[slot, tiles:2 * tiles])
    x2 = x1_ref[...] + gt_ref[0] * (w1 * y0 + w2 * y1)
    o_ref[...] = x2 * lax.rsqrt(jnp.mean(x2 * x2, axis=-1, keepdims=True) + EPS) * g_ref[...]


def _final(x1, ys, dest, route, mod3, g_final, seq):
    n, d = x1.shape
    n_tiles = n // TM
    tiles_per_seq = seq // TM
    row = lambda i: (i, 0)
    return pl.pallas_call(
        _final_kernel,
        grid=(n_tiles,),
        in_specs=[pl.BlockSpec((1, 8, TM), _dest_block, memory_space=pltpu.SMEM),
                  pl.BlockSpec((1, 8, TM), lambda i: _dest_block(jnp.minimum(i + 1, n_tiles - 1)),
                               memory_space=pltpu.SMEM),
                  pl.BlockSpec((TM, d), row),
                  pl.BlockSpec((TM, LANES), row),
                  pl.BlockSpec((1, 1, d), lambda i: ((i // tiles_per_seq) * 6 + 5, 0, 0)),
                  pl.BlockSpec((1, d), lambda i: (0, 0)),
                  pl.BlockSpec(memory_space=pl.ANY)],
        out_specs=pl.BlockSpec((TM, d), row),
        out_shape=jax.ShapeDtypeStruct((n, d), jnp.float32),
        scratch_shapes=[pltpu.VMEM((2,) + _tiles_shape(2 * TM, d), jnp.float32),
                        pltpu.SemaphoreType.DMA((2,))],
        compiler_params=_cparams(("arbitrary",)),
        name="final",
    )(dest, dest, x1, route, mod3, g_final, ys)


def _block_layout(counts, n):
    cnt = counts[0, :N_EXPERTS].astype(jnp.int32)
    blocks = (cnt + BM - 1) // BM
    bends = jnp.cumsum(blocks)
    pstarts = (bends - blocks) * BM
    n_blk = (2 * n) // BM + N_EXPERTS
    pst_row = jnp.zeros((1, LANES), jnp.float32).at[0, :N_EXPERTS].set(pstarts.astype(jnp.float32))
    zrow = ((pstarts + cnt) // SUBLANES).astype(jnp.int32)
    n_used = bends[-1:].astype(jnp.int32)
    blk_e = jnp.sum(bends[None, :] <= jnp.arange(n_blk, dtype=jnp.int32)[:, None], axis=1)
    blk_e = jnp.minimum(blk_e, N_EXPERTS - 1).astype(jnp.int32)
    return pst_row, zrow, n_used, blk_e


def _layer(x2, c, pos2, w_ada, b_ada, g_mix, w_in, w_o, g_ffn, w_rg, w_re, w_up, w_gate, w_down, seq):
    n, d = x2.shape
    bsz = n // seq
    bf = jnp.bfloat16
    d_a = d // 2
    d_i = N_IDX_HEADS * IDX_DIM
    mod3 = _ada(c, w_ada, b_ada).reshape(bsz * 6, 1, d)

    c0 = 3 * d_a + d_i
    c1 = c0 + IDX_DIM + N_IDX_HEADS
    w_cat = jnp.concatenate([w_in[:, :c0], w_in[:, c0:c1],
                             jnp.zeros((d, LANES - (c1 - c0)), w_in.dtype), w_in[:, c1:]], axis=1).astype(bf)
    qat, ka, vat, qit, kiw, wit, qb, kb, vb, gb = _proj(x2, pos2, mod3, g_mix.reshape(1, d), w_cat, seq)

    n_kt = seq // TM
    ya = _dsa(qat, qit, wit, kiw.reshape(bsz, n_kt, TM, LANES), ka.reshape(bsz, n_kt, TM, d_a), vat, seq)
    yb = _ret(qb, kb, vb, gb, seq)

    w_route = jnp.concatenate([jnp.transpose(w_re, (1, 0, 2)).reshape(d, N_EXPERTS), w_rg,
                               jnp.zeros((d, LANES - N_EXPERTS - N_GROUPS), w_rg.dtype)], axis=1).astype(bf)
    x1, h2, route, counts = _mix_out(x2, ya, yb, w_o[:d_a].astype(bf), w_o[d_a:].astype(bf), mod3,
                                     g_ffn.reshape(1, d), w_route, seq)

    pst_row, zrow, n_used, blk_e = _block_layout(counts, n)
    dest = _plan(route, pst_row)
    xs = _dispatch(h2, dest, zrow, n_used, blk_e.shape[0])
    ys = _experts(xs, w_gate, w_up, w_down, blk_e, n_used)
    return x1, ys, dest, route, mod3


def kernel(x, c, positions, w_ada, b_ada, g_norm_mix, w_in, w_o, g_norm_ffn, w_router_group, w_router_expert,
           w_up, w_gate, w_down, g_norm_final):
    bsz, seq, d = x.shape
    depth = w_ada.shape[0]
    assert depth == 1, "the final norm is fused into the last layer's combine kernel"
    assert seq % (2 * TM) == 0 and seq % (RET_C * RET_CHUNKS) == 0 and seq % MIX_TM == 0 and (2 * bsz * seq) % BM == 0
    assert (bsz * seq) % PLAN_TM == 0 and PLAN_TM % TM == 0
    x2 = x.reshape(bsz * seq, d)
    pos2 = positions.astype(jnp.float32).reshape(bsz * seq, 1)
    x1, ys, dest, route, mod3 = _layer(x2, c, pos2, w_ada[0], b_ada[0], g_norm_mix[0], w_in[0], w_o[0],
                                       g_norm_ffn[0], w_router_group[0], w_router_expert[0], w_up[0], w_gate[0],
                                       w_down[0], seq)
    out = _final(x1, ys, dest, route, mod3, g_norm_final.reshape(1, d), seq)
    return out.reshape(bsz, seq, d)
```

```python
import functools

import jax
import jax.numpy as jnp
import numpy as np
from jax import lax
from jax.experimental import pallas as pl
from jax.experimental.pallas import tpu as pltpu

CHUNK = 64
HEAD_DIM = 64
N_IDX_HEADS = 16
IDX_DIM = 64
TOPK_MAX = 256
ROPE_THETA = 500000.0
ROT_DIM = HEAD_DIM // 4
RET_THETA = 10000.0
N_GROUPS = 4
EXPERTS_PER_GROUP = 8
N_EXPERTS = N_GROUPS * EXPERTS_PER_GROUP
EPS = 1e-6

LANES = 128
SUBLANES = 8
VMEM_LIMIT = 56 * 1024 * 1024

TM = 256
QB = TM
RET_C = 256
RET_CHUNKS = 2
BM = 256
PLAN_TM = 1024
MIX_TM = 512
CNT_ROWS = 16
ONES_ROWS = 16
HEAD_GROUP = 4
IDX_TILES = 8
ATT_TILES = 8
BISECT_PER_CHECK = 2
MAX_BISECT = 40

NEG_BIG = -1e30
LOG2E = 1.4426950408889634


def _cparams(sem):
    return pltpu.CompilerParams(dimension_semantics=sem, vmem_limit_bytes=VMEM_LIMIT)


def _ada_kernel(c_ref, w_ref, b_ref, o_ref):
    o_ref[...] = jnp.dot(c_ref[...], w_ref[...], preferred_element_type=jnp.float32) + b_ref[...]


def _ada(c, w_ada, b_ada):
    bsz, d = c.shape
    n_out = w_ada.shape[1]
    return pl.pallas_call(
        _ada_kernel,
        grid=(n_out // d,),
        in_specs=[pl.BlockSpec((bsz, d), lambda j: (0, 0)),
                  pl.BlockSpec((d, d), lambda j: (0, j)),
                  pl.BlockSpec((1, d), lambda j: (0, j))],
        out_specs=pl.BlockSpec((bsz, d), lambda j: (0, j)),
        out_shape=jax.ShapeDtypeStruct((bsz, n_out), jnp.float32),
        compiler_params=_cparams(("arbitrary",)),
        name="ada",
    )(c, w_ada, b_ada.reshape(1, n_out))


def _rmsnorm_mod(x, g, sc, sh):
    xn = x * lax.rsqrt(jnp.mean(x * x, axis=-1, keepdims=True) + EPS)
    return xn * g * (1.0 + sc) + sh


def _rope_lanes(x, cos, sin_lo, sin_hi, half):
    cols = []
    for k in range(x.shape[1] // LANES):
        xb = x[:, k * LANES:(k + 1) * LANES]
        cols.append(xb * cos + pltpu.roll(xb, LANES - half, 1) * sin_lo + pltpu.roll(xb, half, 1) * sin_hi)
    return cols[0] if len(cols) == 1 else jnp.concatenate(cols, axis=1)


def _proj_kernel(x_ref, pos_ref, sc_ref, sh_ref, g_ref, w_ref, tab_ref,
                 qat_ref, ka_ref, vat_ref, qit_ref, kiw_ref, wit_ref, qb_ref, kb_ref, vb_ref, gb_ref,
                 *, d_a, d_i, d_b):
    h = _rmsnorm_mod(x_ref[...], g_ref[...], sc_ref[0], sh_ref[0]).astype(jnp.bfloat16)
    pos = pos_ref[...]
    ang_a = pos * tab_ref[0:1, :]
    cos_a, sin_a = jnp.cos(ang_a), jnp.sin(ang_a)
    sa_lo, sa_hi = sin_a * tab_ref[1:2, :], sin_a * tab_ref[2:3, :]
    ang_b = pos * tab_ref[3:4, :]
    cos_b, sin_b = jnp.cos(ang_b), jnp.sin(ang_b)
    sb_lo, sb_hi = sin_b * tab_ref[4:5, :], sin_b * tab_ref[5:6, :]
    half_a, half_b = ROT_DIM // 2, HEAD_DIM // 2

    def seg(lo, width):
        return jnp.dot(h, w_ref[:, lo:lo + width], preferred_element_type=jnp.float32)

    o = 0
    qa = seg(o, d_a); o += d_a
    qat_ref[0, 0] = (_rope_lanes(qa, cos_a, sa_lo, sa_hi, half_a) * (HEAD_DIM ** -0.5 * LOG2E)).T.astype(qat_ref.dtype)
    ka = seg(o, d_a); o += d_a
    ka_ref[...] = _rope_lanes(ka, cos_a, sa_lo, sa_hi, half_a).astype(ka_ref.dtype)
    vat_ref[0, 0] = seg(o, d_a).T.astype(vat_ref.dtype); o += d_a
    qi = seg(o, d_i); o += d_i
    qit_ref[0, 0] = (_rope_lanes(qi, cos_a, sa_lo, sa_hi, half_a) * (IDX_DIM ** -0.5)).T.astype(qit_ref.dtype)
    kw = seg(o, LANES); o += LANES
    kiw_ref[...] = _rope_lanes(kw, cos_a, sa_lo, sa_hi, half_a).astype(kiw_ref.dtype)
    wit_ref[0, 0] = kw.T[IDX_DIM:IDX_DIM + N_IDX_HEADS, :] * (N_IDX_HEADS ** -0.5)
    qb = seg(o, d_b); o += d_b
    qb_ref[...] = _rope_lanes(qb, cos_b, sb_lo, sb_hi, half_b).astype(qb_ref.dtype)
    kb = seg(o, d_b); o += d_b
    kb_ref[...] = (_rope_lanes(kb, cos_b, sb_lo, sb_hi, half_b) * (HEAD_DIM ** -0.5)).astype(kb_ref.dtype)
    vb_ref[...] = seg(o, d_b).astype(vb_ref.dtype); o += d_b
    gb_ref[...] = seg(o, d_b)


def _rope_tables():
    lane = jnp.arange(LANES) % HEAD_DIM
    rows = []
    for rot, theta in ((ROT_DIM, ROPE_THETA), (HEAD_DIM, RET_THETA)):
        half = rot // 2
        inv_freq = theta ** (-jnp.arange(half, dtype=jnp.float32) / half)
        rows.append(jnp.where(lane < rot, inv_freq[lane % half], 0.0))
        rows.append(jnp.where(lane < half, -1.0, 0.0))
        rows.append(jnp.where((lane >= half) & (lane < rot), 1.0, 0.0))
    rows += [jnp.zeros((LANES,), jnp.float32)] * 2
    return jnp.stack(rows).astype(jnp.float32)


def _proj(x2, pos2, mod3, g_mix, w_cat, seq):
    n, d = x2.shape
    bsz = n // seq
    d_a = d // 2
    d_b = d // 2
    d_i = N_IDX_HEADS * IDX_DIM
    tiles_per_seq = seq // TM
    tab = _rope_tables()
    row = lambda i: (i, 0)
    const = lambda i: (0, 0)
    tile4 = lambda i: (i // tiles_per_seq, i % tiles_per_seq, 0, 0)
    bf = jnp.bfloat16
    out_shape = (
        jax.ShapeDtypeStruct((bsz, tiles_per_seq, d_a, TM), bf),
        jax.ShapeDtypeStruct((n, d_a), bf),
        jax.ShapeDtypeStruct((bsz, tiles_per_seq, d_a, TM), bf),
        jax.ShapeDtypeStruct((bsz, tiles_per_seq, d_i, TM), bf),
        jax.ShapeDtypeStruct((n, LANES), bf),
        jax.ShapeDtypeStruct((bsz, tiles_per_seq, N_IDX_HEADS, TM), jnp.float32),
        jax.ShapeDtypeStruct((n, d_b), bf),
        jax.ShapeDtypeStruct((n, d_b), bf),
        jax.ShapeDtypeStruct((n, d_b), bf),
        jax.ShapeDtypeStruct((n, d_b), jnp.float32),
    )
    out_specs = (
        pl.BlockSpec((1, 1, d_a, TM), tile4),
        pl.BlockSpec((TM, d_a), row),
        pl.BlockSpec((1, 1, d_a, TM), tile4),
        pl.BlockSpec((1, 1, d_i, TM), tile4),
        pl.BlockSpec((TM, LANES), row),
        pl.BlockSpec((1, 1, N_IDX_HEADS, TM), tile4),
        pl.BlockSpec((TM, d_b), row),
        pl.BlockSpec((TM, d_b), row),
        pl.BlockSpec((TM, d_b), row),
        pl.BlockSpec((TM, d_b), row),
    )
    return pl.pallas_call(
        functools.partial(_proj_kernel, d_a=d_a, d_i=d_i, d_b=d_b),
        grid=(n // TM,),
        in_specs=[pl.BlockSpec((TM, d), row),
                  pl.BlockSpec((TM, 1), row),
                  pl.BlockSpec((1, 1, d), lambda i: ((i // tiles_per_seq) * 6 + 1, 0, 0)),
                  pl.BlockSpec((1, 1, d), lambda i: ((i // tiles_per_seq) * 6 + 0, 0, 0)),
                  pl.BlockSpec((1, d), const),
                  pl.BlockSpec(w_cat.shape, const),
                  pl.BlockSpec(tab.shape, const)],
        out_specs=out_specs,
        out_shape=out_shape,
        compiler_params=_cparams(("arbitrary",)),
        name="proj",
    )(x2, pos2, mod3, mod3, g_mix, w_cat, tab)


def _row_blocks(x, rows):
    return [x[r * rows:(r + 1) * rows] for r in range(x.shape[0] // rows)]


def _dsa_kernel(qat_ref, qit_ref, wit_ref, kiw_ref, ka_ref, vat_ref, tri_ref, o_ref,
                qix_ref, qmx_ref, sc_ref, m_ref, l_ref, acc_ref, sa_ref, sb_ref, mxa_ref, mxb_ref, lohi_ref,
                *, k_top, n_heads):
    i = pl.program_id(1)
    n_real = (i * QB + QB + TM - 1) // TM
    n_kt = n_real // 2 * 2
    odd_tile = n_real % 2 == 1

    zero_rows = jnp.zeros((LANES - IDX_DIM, QB), qix_ref.dtype)
    for h in range(N_IDX_HEADS):
        qix_ref[h] = jnp.concatenate([qit_ref[0, 0, h * IDX_DIM:(h + 1) * IDX_DIM, :], zero_rows], axis=0)
    row_q = lax.broadcasted_iota(jnp.int32, (LANES, QB), 0)
    for h in range(n_heads):
        pair = qat_ref[0, 0, (h // 2) * LANES:(h // 2 + 1) * LANES, :]
        own = (row_q < HEAD_DIM) if h % 2 == 0 else (row_q >= HEAD_DIM)
        qmx_ref[h] = jnp.where(own, pair, jnp.zeros_like(pair))

    q_chunk = (i * QB + lax.broadcasted_iota(jnp.int32, (1, QB), 1)) // CHUNK
    key_chunk_in_tile = lax.broadcasted_iota(jnp.int32, (TM, QB), 0) // CHUNK
    w_all = wit_ref[0, 0]

    def idx_tiles(tiles):
        lo, hi = lohi_ref[0], lohi_ref[1]
        for j in tiles:
            kt = kiw_ref[0, j]
            acc = None
            for h in range(N_IDX_HEADS):
                d = jnp.dot(kt, qix_ref[h], preferred_element_type=jnp.float32)
                t = w_all[h:h + 1, :] * jnp.maximum(d, 0.0)
                acc = t if acc is None else acc + t
            adm = key_chunk_in_tile <= q_chunk - j * (TM // CHUNK)
            s = jnp.where(adm, acc, -jnp.inf)
            sc_ref[j] = s
            lo = jnp.minimum(lo, functools.reduce(jnp.minimum, _row_blocks(jnp.where(adm, acc, jnp.inf), SUBLANES)))
            hi = jnp.maximum(hi, functools.reduce(jnp.maximum, _row_blocks(s, SUBLANES)))
        lohi_ref[0], lohi_ref[1] = lo, hi

    lohi_ref[0] = jnp.full((SUBLANES, QB), jnp.inf, jnp.float32)
    lohi_ref[1] = jnp.full((SUBLANES, QB), -jnp.inf, jnp.float32)

    def idx_step(g, carry):
        idx_tiles([IDX_TILES * g + u for u in range(IDX_TILES)])
        return carry

    lax.fori_loop(0, n_kt // IDX_TILES, idx_step, 0)

    def idx_tail(p, carry):
        first = n_kt // IDX_TILES * IDX_TILES + 2 * p
        idx_tiles([first, first + 1])
        return carry

    lax.fori_loop(0, n_kt % IDX_TILES // 2, idx_tail, 0)

    @pl.when(odd_tile)
    def _():
        idx_tiles([n_real - 1])

    lo = jnp.min(lohi_ref[0], axis=0, keepdims=True)
    hi = jnp.max(lohi_ref[1], axis=0, keepdims=True)
    n_adm = ((i * QB + lax.broadcasted_iota(jnp.int32, (1, QB), 1)) // CHUNK + 1) * CHUNK

    def bisect_cond(carry):
        it, _, _, _, unsettled = carry
        return (it < MAX_BISECT) & (unsettled > 0.0)

    def bisect(carry):
        it, lo, hi, cnt_lo, _ = carry
        for _ in range(BISECT_PER_CHECK):
            mid = lo + (hi - lo) * 0.5
            mid_b = jnp.broadcast_to(mid, (CNT_ROWS, QB))

            def count_tile(j, cnt, mid_b=mid_b):
                for blk in _row_blocks(sc_ref[j], CNT_ROWS):
                    cnt = cnt + jnp.where(blk >= mid_b, 1.0, 0.0)
                return cnt

            def count_pair(g, cnt):
                return count_tile(2 * g + 1, count_tile(2 * g, cnt))

            cnt = lax.fori_loop(0, n_kt // 2, count_pair, jnp.zeros((CNT_ROWS, QB), jnp.float32))
            cnt = lax.fori_loop(n_kt, n_real, count_tile, cnt)
            c = jnp.sum(cnt, axis=0, keepdims=True)
            ge = c >= k_top
            lo = jnp.where(ge, mid, lo)
            hi = jnp.where(ge, hi, mid)
            cnt_lo = jnp.where(ge, c, cnt_lo)
        return it + BISECT_PER_CHECK, lo, hi, cnt_lo, jnp.max(jnp.where(cnt_lo > k_top, 1.0, 0.0))

    cnt0 = n_adm.astype(jnp.float32)
    _, thr, thr_hi, _, unsettled = lax.while_loop(bisect_cond, bisect,
                                                  (0, lo, hi, cnt0, jnp.max(jnp.where(cnt0 > k_top, 1.0, 0.0))))

    @pl.when(unsettled <= 0.0)
    def _():
        def bias_tile(j, carry):
            sc_ref[j] = jnp.where(sc_ref[j] >= thr, 0.0, NEG_BIG)
            return carry

        lax.fori_loop(0, n_real, bias_tile, 0)

    @pl.when(unsettled > 0.0)
    def _():
        def count_hi(j, cnt):
            return cnt + functools.reduce(jnp.add, _row_blocks(jnp.where(sc_ref[j] >= thr_hi, 1.0, 0.0), SUBLANES))

        above = jnp.sum(lax.fori_loop(0, n_real, count_hi, jnp.zeros((SUBLANES, QB), jnp.float32)),
                        axis=0, keepdims=True)
        top = jnp.where(above < k_top, thr_hi, jnp.inf)
        room = k_top - jnp.where(above < k_top, above, 0.0)

        def bias_tile(j, taken):
            s = sc_ref[j]
            tied = (s >= thr) & (s < top)
            before = jnp.dot(tri_ref[...], jnp.where(tied, 1.0, 0.0).astype(tri_ref.dtype),
                             preferred_element_type=jnp.float32)
            keep = (s >= top) | (tied & (taken + before <= room))
            sc_ref[j] = jnp.where(keep, 0.0, NEG_BIG)
            return taken + before[TM - 1:TM, :]

        lax.fori_loop(0, n_real, bias_tile, jnp.zeros((1, QB), jnp.float32))

    m_ref[...] = jnp.full(m_ref.shape, NEG_BIG, jnp.float32)
    l_ref[...] = jnp.zeros(l_ref.shape, jnp.float32)
    acc_ref[...] = jnp.zeros(acc_ref.shape, jnp.float32)

    def pair(h):
        return slice((h // 2) * LANES, (h // 2 + 1) * LANES)

    def logits_into(s_ref, mx_ref, j, heads):
        bias = sc_ref[j]
        for h in heads:
            s = jnp.dot(ka_ref[0, j, :, pair(h)], qmx_ref[h], preferred_element_type=jnp.float32) + bias
            s_ref[h] = s
            mx_ref[h] = jnp.max(s, axis=0, keepdims=True)

    ones_rows = jnp.ones((ONES_ROWS, TM), vat_ref.dtype)

    def absorb(s_ref, mx_ref, j, heads):
        for h in heads:
            m_old = m_ref[h]
            m_new = jnp.maximum(m_old, mx_ref[h])
            alpha = jnp.exp2(m_old - m_new)
            p = jnp.exp2(s_ref[h] - m_new).astype(vat_ref.dtype)
            pv = jnp.dot(jnp.concatenate([vat_ref[0, j, pair(h), :], ones_rows], axis=0), p,
                         preferred_element_type=jnp.float32)
            acc_ref[h] = acc_ref[h] * alpha + pv[0:LANES]
            l_ref[h] = l_ref[h] * alpha + pv[LANES:LANES + SUBLANES]
            m_ref[h] = m_new

    for h0 in range(0, n_heads, HEAD_GROUP):
        heads = range(h0, h0 + HEAD_GROUP)
        logits_into(sa_ref, mxa_ref, 0, heads)

        def tile_pair(t, heads=heads):
            logits_into(sb_ref, mxb_ref, t + 1, heads)
            absorb(sa_ref, mxa_ref, t, heads)
            logits_into(sa_ref, mxa_ref, jnp.minimum(t + 2, n_real - 1), heads)
            absorb(sb_ref, mxb_ref, t + 1, heads)

        def attn_step(g, carry):
            for u in range(0, ATT_TILES, 2):
                tile_pair(ATT_TILES * g + u)
            return carry

        lax.fori_loop(0, n_kt // ATT_TILES, attn_step, 0)

        def attn_tail(p, carry):
            tile_pair(n_kt // ATT_TILES * ATT_TILES + 2 * p)
            return carry

        lax.fori_loop(0, n_kt % ATT_TILES // 2, attn_tail, 0)

        @pl.when(odd_tile)
        def _(heads=heads):
            absorb(sa_ref, mxa_ref, n_real - 1, heads)

    for hp in range(n_heads // 2):
        even = acc_ref[2 * hp] / l_ref[2 * hp, 0:1, :]
        odd = acc_ref[2 * hp + 1] / l_ref[2 * hp + 1, 0:1, :]
        o_ref[:, hp * LANES:(hp + 1) * LANES] = jnp.where(row_q < HEAD_DIM, even, odd).T.astype(o_ref.dtype)


def _dsa(qat, qit, wit, kiw4, ka4, vat, seq):
    bsz, n_kt, d_a, _ = qat.shape
    n_heads = d_a // HEAD_DIM
    n_qb = seq // QB
    k_top = min(TOPK_MAX, seq // 4)
    qtile = lambda b, i: (b, i, 0, 0)
    per_b = lambda b, i: (b, 0, 0, 0)
    f32 = jnp.float32
    return pl.pallas_call(
        functools.partial(_dsa_kernel, k_top=float(k_top), n_heads=n_heads),
        grid=(bsz, n_qb),
        in_specs=[pl.BlockSpec((1, 1, d_a, QB), qtile),
                  pl.BlockSpec((1, 1, qit.shape[2], QB), qtile),
                  pl.BlockSpec((1, 1, N_IDX_HEADS, QB), qtile),
                  pl.BlockSpec((1, n_kt, TM, LANES), per_b, pipeline_mode=pl.Buffered(1)),
                  pl.BlockSpec((1, n_kt, TM, d_a), per_b, pipeline_mode=pl.Buffered(1)),
                  pl.BlockSpec((1, n_kt, d_a, TM), per_b, pipeline_mode=pl.Buffered(1)),
                  pl.BlockSpec((TM, TM), lambda b, i: (0, 0), pipeline_mode=pl.Buffered(1))],
        out_specs=pl.BlockSpec((QB, d_a), lambda b, i: (b * n_qb + i, 0)),
        out_shape=jax.ShapeDtypeStruct((bsz * seq, d_a), jnp.bfloat16),
        scratch_shapes=[pltpu.VMEM((N_IDX_HEADS, LANES, QB), jnp.bfloat16),
                        pltpu.VMEM((n_heads, LANES, QB), jnp.bfloat16),
                        pltpu.VMEM((n_kt, TM, QB), f32),
                        pltpu.VMEM((n_heads, 1, QB), f32),
                        pltpu.VMEM((n_heads, SUBLANES, QB), f32),
                        pltpu.VMEM((n_heads, LANES, QB), f32),
                        pltpu.VMEM((n_heads, TM, QB), f32),
                        pltpu.VMEM((n_heads, TM, QB), f32),
                        pltpu.VMEM((n_heads, 1, QB), f32),
                        pltpu.VMEM((n_heads, 1, QB), f32),
                        pltpu.VMEM((2, SUBLANES, QB), f32)],
        compiler_params=_cparams(("arbitrary", "arbitrary")),
        name="dsa",
    )(qat, qit, wit, kiw4, ka4, vat, jnp.asarray(np.tril(np.ones((TM, TM), np.float32)), jnp.bfloat16))


def _group_mean(y, avg):
    hi = y.astype(jnp.bfloat16)
    lo = (y - hi.astype(jnp.float32)).astype(jnp.bfloat16)
    return (jnp.dot(hi, avg, preferred_element_type=jnp.float32)
            + jnp.dot(lo, avg, preferred_element_type=jnp.float32))


def _ret_kernel(q_ref, k_ref, v_ref, g_ref, dec_ref, zt_ref, xi_ref, gc_ref, blk_ref, avg_ref, o_ref, st_ref,
                *, n_heads):
    @pl.when(pl.program_id(1) == 0)
    def _():
        st_ref[...] = jnp.zeros_like(st_ref)

    even = lax.broadcasted_iota(jnp.int32, (RET_C, LANES), 1) < HEAD_DIM
    avg = avg_ref[...]
    for p in range(n_heads // 2):
        sl = slice(p * LANES, (p + 1) * LANES)
        state = st_ref[p]
        for c in range(q_ref.shape[0] // RET_C):
            rows = slice(c * RET_C, (c + 1) * RET_C)
            qp, kp, vp = q_ref[rows, sl], k_ref[rows, sl], v_ref[rows, sl]
            kpt = kp.astype(jnp.float32).T
            kpt_b = kpt.astype(kp.dtype)
            inner = None
            for e in range(2):
                q_e = jnp.where(even if e == 0 else jnp.logical_not(even), qp, jnp.zeros_like(qp))
                s = jnp.dot(q_e, kpt_b, preferred_element_type=jnp.float32) * dec_ref[2 * p + e]
                t = jnp.dot(s.astype(vp.dtype), vp, preferred_element_type=jnp.float32)
                inner = t if e == 0 else jnp.where(even, inner, t)
            cross = jnp.dot(qp, state.astype(qp.dtype), preferred_element_type=jnp.float32) * xi_ref[p]
            y = inner + cross
            yc = y - _group_mean(y, avg)
            yn = yc * lax.rsqrt(_group_mean(yc * yc, avg) + EPS)
            g = g_ref[rows, sl]
            o_ref[rows, sl] = (g / (1.0 + jnp.exp(-g)) * yn).astype(o_ref.dtype)
            kz = (kpt * zt_ref[p]).astype(kp.dtype)
            kv = jnp.dot(kz, vp, preferred_element_type=jnp.float32)
            state = state * gc_ref[p] + kv * blk_ref[...]
        st_ref[p] = state


def _ret_consts(n_heads):
    log_gamma = jnp.log1p(-jnp.exp2(-5.0 - jnp.arange(n_heads, dtype=jnp.float32)))
    pos = jnp.arange(RET_C, dtype=jnp.float32)
    diff = pos[:, None] - pos[None, :]
    dec = jnp.where(diff[None] >= 0, jnp.exp(jnp.maximum(diff, 0.0)[None] * log_gamma[:, None, None]), 0.0)
    zeta = jnp.exp((RET_C - 1.0 - pos)[None, :] * log_gamma[:, None])
    xi = jnp.exp((pos + 1.0)[None, :] * log_gamma[:, None])
    gc = jnp.exp(RET_C * log_gamma)
    n_pairs = n_heads // 2
    lanes = lambda a: jnp.repeat(a.reshape(n_pairs, 2, -1), HEAD_DIM, axis=1)
    zt = lanes(zeta)
    xi_p = jnp.swapaxes(lanes(xi), 1, 2)
    gc_p = jnp.broadcast_to(lanes(gc[:, None]), (n_pairs, LANES, LANES))
    head_of = jnp.arange(LANES) // HEAD_DIM
    blk = (head_of[:, None] == head_of[None, :]).astype(jnp.float32)
    avg = (blk / HEAD_DIM).astype(jnp.bfloat16)
    f32 = lambda a: a.astype(jnp.float32)
    return dec, f32(zt), f32(xi_p), f32(gc_p), blk, avg


def _ret(qb, kb, vb, gb, seq):
    n, d_b = qb.shape
    bsz = n // seq
    n_heads = d_b // HEAD_DIM
    step = RET_C * RET_CHUNKS
    n_c = seq // step
    consts = _ret_consts(n_heads)
    row = lambda b, c: (b * n_c + c, 0)
    const_spec = lambda a: pl.BlockSpec(a.shape, lambda b, c: (0,) * a.ndim)
    return pl.pallas_call(
        functools.partial(_ret_kernel, n_heads=n_heads),
        grid=(bsz, n_c),
        in_specs=[pl.BlockSpec((step, d_b), row)] * 4 + [const_spec(a) for a in consts],
        out_specs=pl.BlockSpec((step, d_b), row),
        out_shape=jax.ShapeDtypeStruct((n, d_b), jnp.bfloat16),
        scratch_shapes=[pltpu.VMEM((n_heads // 2, LANES, LANES), jnp.float32)],
        compiler_params=_cparams(("arbitrary", "arbitrary")),
        name="ret",
    )(qb, kb, vb, gb, *consts)


def _tiles_shape(rows, d):
    return (rows // SUBLANES, d // LANES, SUBLANES, LANES)


def _to_tiles(ref, x):
    for s in range(ref.shape[1]):
        ref[:, s] = x[:, s * LANES:(s + 1) * LANES].reshape(ref.shape[0], SUBLANES, LANES)


def _from_tiles(ref):
    rows = ref.shape[0] * SUBLANES
    return jnp.concatenate([ref[:, s].reshape(rows, LANES) for s in range(ref.shape[1])], axis=1)


def _lane_first_eq(x, m, lane):
    return jnp.min(jnp.where(x == m, lane, float(LANES)), axis=1, keepdims=True)


def _mix_out_kernel(x_ref, ya_ref, yb_ref, woa_ref, wob_ref, gt_ref, sc_ref, sh_ref, g_ref, wr_ref, tri_ref,
                    x1_ref, h2_ref, route_ref, cnt_ref, carry_ref):
    @pl.when(pl.program_id(0) == 0)
    def _():
        carry_ref[...] = jnp.zeros_like(carry_ref)

    mix = (jnp.dot(ya_ref[...], woa_ref[...], preferred_element_type=jnp.float32)
           + jnp.dot(yb_ref[...], wob_ref[...], preferred_element_type=jnp.float32))
    x1 = x_ref[...] + gt_ref[0] * mix
    x1_ref[...] = x1
    h2 = _rmsnorm_mod(x1, g_ref[...], sc_ref[0], sh_ref[0])
    _to_tiles(h2_ref, h2)

    lg = jnp.dot(h2.astype(jnp.bfloat16), wr_ref[...], preferred_element_type=jnp.float32)
    lane = lax.broadcasted_iota(jnp.int32, lg.shape, 1).astype(jnp.float32)
    is_grp = (lane >= N_EXPERTS) & (lane < N_EXPERTS + N_GROUPS)
    gl = jnp.where(is_grp, lg, -jnp.inf)
    gmax = jnp.max(gl, axis=1, keepdims=True)
    grp = _lane_first_eq(gl, gmax, lane) - N_EXPERTS
    p_grp = 1.0 / jnp.sum(jnp.exp(gl - gmax), axis=1, keepdims=True)
    in_grp = jnp.floor(lane * (1.0 / EXPERTS_PER_GROUP)) == grp
    f = jnp.where(in_grp & (lane < N_EXPERTS), lg, -jnp.inf)
    f1 = jnp.max(f, axis=1, keepdims=True)
    e1 = _lane_first_eq(f, f1, lane)
    f = jnp.where(lane == e1, -jnp.inf, f)
    f2 = jnp.max(f, axis=1, keepdims=True)
    e2 = _lane_first_eq(f, f2, lane)
    a2 = jnp.exp(f2 - f1)
    w1 = p_grp / (1.0 + a2)
    w2 = p_grp * a2 / (1.0 + a2)

    oh1 = jnp.where(lane == e1, 1.0, 0.0)
    oh2 = jnp.where(lane == e2, 1.0, 0.0)
    both = oh1 + oh2
    before = jnp.dot(tri_ref[...], both.astype(jnp.bfloat16), preferred_element_type=jnp.float32) + carry_ref[...]
    r1 = jnp.sum(before * oh1, axis=1, keepdims=True)
    r2 = jnp.sum(before * oh2, axis=1, keepdims=True)
    carry = carry_ref[...] + jnp.sum(both, axis=0, keepdims=True)
    carry_ref[...] = carry
    cnt_ref[...] = carry

    out = jnp.zeros(lg.shape, jnp.float32)
    for col, val in enumerate((e1, e2, w1, w2, r1, r2)):
        out = jnp.where(lane == col, val, out)
    route_ref[...] = out


def _mix_out(x2, ya, yb, wo_a, wo_b, mod3, g_ffn, w_route, seq):
    n, d = x2.shape
    tm = MIX_TM
    tiles_per_seq = seq // tm
    tri = jnp.asarray(np.tril(np.ones((tm, tm), np.float32), -1), jnp.bfloat16)
    row = lambda i: (i, 0)
    const = lambda i: (0, 0)
    modk = lambda k: pl.BlockSpec((1, 1, d), lambda i: ((i // tiles_per_seq) * 6 + k, 0, 0))
    return pl.pallas_call(
        _mix_out_kernel,
        grid=(n // tm,),
        in_specs=[pl.BlockSpec((tm, d), row),
                  pl.BlockSpec((tm, ya.shape[1]), row),
                  pl.BlockSpec((tm, yb.shape[1]), row),
                  pl.BlockSpec(wo_a.shape, const),
                  pl.BlockSpec(wo_b.shape, const),
                  modk(2), modk(4), modk(3),
                  pl.BlockSpec((1, d), const),
                  pl.BlockSpec(w_route.shape, const),
                  pl.BlockSpec(tri.shape, const)],
        out_specs=(pl.BlockSpec((tm, d), row), pl.BlockSpec(_tiles_shape(tm, d), lambda i: (i, 0, 0, 0)),
                   pl.BlockSpec((tm, LANES), row), pl.BlockSpec((1, LANES), const)),
        out_shape=(jax.ShapeDtypeStruct((n, d), jnp.float32), jax.ShapeDtypeStruct(_tiles_shape(n, d), jnp.float32),
                   jax.ShapeDtypeStruct((n, LANES), jnp.float32), jax.ShapeDtypeStruct((1, LANES), jnp.float32)),
        scratch_shapes=[pltpu.VMEM((1, LANES), jnp.float32)],
        compiler_params=_cparams(("arbitrary",)),
        name="mix_out",
    )(x2, ya, yb, wo_a, wo_b, mod3, mod3, mod3, g_ffn, w_route, tri)


def _plan_kernel(route_ref, pst_ref, dest_ref):
    r = route_ref[...]
    lane = lax.broadcasted_iota(jnp.int32, r.shape, 1).astype(jnp.float32)
    pst = pst_ref[...]
    d1 = jnp.sum(jnp.where(lane == r[:, 0:1], pst, 0.0), axis=1, keepdims=True) + r[:, 4:5]
    d2 = jnp.sum(jnp.where(lane == r[:, 1:2], pst, 0.0), axis=1, keepdims=True) + r[:, 5:6]
    t1 = jnp.floor(d1 * (1.0 / SUBLANES))
    t2 = jnp.floor(d2 * (1.0 / SUBLANES))
    packed = jnp.zeros(r.shape, jnp.float32)
    for k, v in enumerate((t1, d1 - t1 * SUBLANES, t2, d2 - t2 * SUBLANES)):
        packed = jnp.where(lane == float(k), v, packed)
    dest_ref[0] = packed.T[0:8, :].astype(jnp.int32)


def _dest_block(i):
    return (i // (PLAN_TM // TM), 0, i % (PLAN_TM // TM))


def _plan(route, pst_row):
    n = route.shape[0]
    return pl.pallas_call(
        _plan_kernel,
        grid=(n // PLAN_TM,),
        in_specs=[pl.BlockSpec((PLAN_TM, LANES), lambda i: (i, 0)), pl.BlockSpec((1, LANES), lambda i: (0, 0))],
        out_specs=pl.BlockSpec((1, 8, PLAN_TM), lambda i: (i, 0, 0)),
        out_shape=jax.ShapeDtypeStruct((n // PLAN_TM, 8, PLAN_TM), jnp.int32),
        compiler_params=_cparams(("arbitrary",)),
        name="plan",
    )(route, pst_row)


def _dispatch_kernel(zrow_ref, n_used_ref, dest_ref, h2_ref, xs_hbm, zbuf, sem, zsem, *, n_blk):
    i = pl.program_id(0)

    @pl.when(i == 0)
    def _():
        zbuf[...] = jnp.zeros(zbuf.shape, zbuf.dtype)
        blk_tiles = BM // SUBLANES
        for e in range(N_EXPERTS):
            pltpu.make_async_copy(zbuf, xs_hbm.at[pl.ds(zrow_ref[e], blk_tiles)], zsem).start()
        for e in range(N_EXPERTS):
            pltpu.make_async_copy(zbuf, xs_hbm.at[pl.ds(0, blk_tiles)], zsem).wait()
        for b in range(N_EXPERTS + 1):
            @pl.when(n_used_ref[0] + b <= n_blk)
            def _():
                tail = pltpu.make_async_copy(zbuf, xs_hbm.at[pl.ds((n_used_ref[0] + b) * blk_tiles, blk_tiles)], zsem)
                tail.start()
                tail.wait()

    def issue(k, c):
        for u in range(SUBLANES):
            r = k * SUBLANES + u
            row = h2_ref.at[k, :, u, :]
            pltpu.make_async_copy(row, xs_hbm.at[dest_ref[0, 0, r], :, dest_ref[0, 1, r], :], sem).start()
            pltpu.make_async_copy(row, xs_hbm.at[dest_ref[0, 2, r], :, dest_ref[0, 3, r], :], sem).start()
        return c

    lax.fori_loop(0, TM // SUBLANES, issue, 0)
    for _ in range(2):
        pltpu.make_async_copy(h2_ref, xs_hbm.at[pl.ds(0, TM // SUBLANES)], sem).wait()


def _dispatch(h2, dest, zrow, n_used, n_blk):
    n_rows = (n_blk + 1) * BM
    d = h2.shape[1] * LANES
    grid_spec = pltpu.PrefetchScalarGridSpec(
        num_scalar_prefetch=2,
        grid=(h2.shape[0] * SUBLANES // TM,),
        in_specs=[pl.BlockSpec((1, 8, TM), lambda i, z, nu: _dest_block(i), memory_space=pltpu.SMEM),
                  pl.BlockSpec(_tiles_shape(TM, d), lambda i, z, nu: (i, 0, 0, 0))],
        out_specs=pl.BlockSpec(memory_space=pl.ANY),
        scratch_shapes=[pltpu.VMEM(_tiles_shape(BM, d), h2.dtype),
                        pltpu.SemaphoreType.DMA(()),
                        pltpu.SemaphoreType.DMA(())],
    )
    return pl.pallas_call(
        functools.partial(_dispatch_kernel, n_blk=n_blk),
        grid_spec=grid_spec,
        out_shape=jax.ShapeDtypeStruct(_tiles_shape(n_rows, d), h2.dtype),
        compiler_params=_cparams(("arbitrary",)),
        name="dispatch",
    )(zrow, n_used, dest, h2)


def _experts_kernel(blk_e_ref, n_used_ref, x_ref, wg_ref, wu_ref, wd_ref, y_ref, wg_bf, wu_bf, wd_bf):
    j = pl.program_id(0)

    @pl.when((j == 0) | (blk_e_ref[j] != blk_e_ref[jnp.maximum(j - 1, 0)]))
    def _():
        wg_bf[...] = wg_ref[0].astype(wg_bf.dtype)
        wu_bf[...] = wu_ref[0].astype(wu_bf.dtype)
        wd_bf[...] = wd_ref[0].astype(wd_bf.dtype)

    @pl.when(j < n_used_ref[0])
    def _():
        x = _from_tiles(x_ref).astype(wg_bf.dtype)
        a = jnp.dot(x, wg_bf[...], preferred_element_type=jnp.float32)
        b = jnp.dot(x, wu_bf[...], preferred_element_type=jnp.float32)
        hmid = (a / (1.0 + jnp.exp(-a)) * b).astype(x.dtype)
        _to_tiles(y_ref, jnp.dot(hmid, wd_bf[...], preferred_element_type=jnp.float32))

    @pl.when(j >= n_used_ref[0])
    def _():
        y_ref[...] = jnp.zeros(y_ref.shape, y_ref.dtype)


def _experts(xs, wg, wu, wd, blk_e, n_used):
    n_blk = blk_e.shape[0]
    d, d_e = wg.shape[1], wg.shape[2]
    blk = _tiles_shape(BM, d)
    grid_spec = pltpu.PrefetchScalarGridSpec(
        num_scalar_prefetch=2,
        grid=(n_blk,),
        in_specs=[pl.BlockSpec(blk, lambda j, be, nu: (jnp.minimum(j, nu[0] - 1), 0, 0, 0)),
                  pl.BlockSpec((1, d, d_e), lambda j, be, nu: (be[j], 0, 0)),
                  pl.BlockSpec((1, d, d_e), lambda j, be, nu: (be[j], 0, 0)),
                  pl.BlockSpec((1, d_e, d), lambda j, be, nu: (be[j], 0, 0))],
        out_specs=pl.BlockSpec(blk, lambda j, be, nu: (j, 0, 0, 0)),
        scratch_shapes=[pltpu.VMEM((d, d_e), jnp.bfloat16), pltpu.VMEM((d, d_e), jnp.bfloat16),
                        pltpu.VMEM((d_e, d), jnp.bfloat16)],
    )
    return pl.pallas_call(
        _experts_kernel,
        grid_spec=grid_spec,
        out_shape=jax.ShapeDtypeStruct(_tiles_shape(n_blk * BM, d), jnp.float32),
        compiler_params=_cparams(("arbitrary",)),
        name="experts",
    )(blk_e, n_used, xs, wg, wu, wd)


def _final_kernel(dest_ref, dest_next_ref, x1_ref, route_ref, gt_ref, g_ref, ys_hbm, o_ref, ybuf, sem):
    i = pl.program_id(0)
    slot = i % 2

    tiles = TM // SUBLANES

    def gather(d_ref, s):
        def issue(k, c):
            for u in range(SUBLANES):
                r = k * SUBLANES + u
                pltpu.make_async_copy(ys_hbm.at[d_ref[0, 0, r], :, d_ref[0, 1, r], :],
                                      ybuf.at[s, k, :, u, :], sem.at[s]).start()
                pltpu.make_async_copy(ys_hbm.at[d_ref[0, 2, r], :, d_ref[0, 3, r], :],
                                      ybuf.at[s, tiles + k, :, u, :], sem.at[s]).start()
            return c
        lax.fori_loop(0, tiles, issue, 0)

    @pl.when(i == 0)
    def _():
        gather(dest_ref, 0)

    @pl.when(i + 1 < pl.num_programs(0))
    def _():
        gather(dest_next_ref, 1 - slot)

    pltpu.make_async_copy(ys_hbm.at[pl.ds(0, 2 * tiles)], ybuf.at[slot], sem.at[slot]).wait()
    w1 = route_ref[:, 2:3]
    w2 = route_ref[:, 3:4]
    y0 = _from_tiles(ybuf.at[slot, 0:tiles])
    y1 = _from_tiles(ybuf.at[slot, tiles:2 * tiles])
    x2 = x1_ref[...] + gt_ref[0] * (w1 * y0 + w2 * y1)
    o_ref[...] = x2 * lax.rsqrt(jnp.mean(x2 * x2, axis=-1, keepdims=True) + EPS) * g_ref[...]


def _final(x1, ys, dest, route, mod3, g_final, seq):
    n, d = x1.shape
    n_tiles = n // TM
    tiles_per_seq = seq // TM
    row = lambda i: (i, 0)
    return pl.pallas_call(
        _final_kernel,
        grid=(n_tiles,),
        in_specs=[pl.BlockSpec((1, 8, TM), _dest_block, memory_space=pltpu.SMEM),
                  pl.BlockSpec((1, 8, TM), lambda i: _dest_block(jnp.minimum(i + 1, n_tiles - 1)),
                               memory_space=pltpu.SMEM),
                  pl.BlockSpec((TM, d), row),
                  pl.BlockSpec((TM, LANES), row),
                  pl.BlockSpec((1, 1, d), lambda i: ((i // tiles_per_seq) * 6 + 5, 0, 0)),
                  pl.BlockSpec((1, d), lambda i: (0, 0)),
                  pl.BlockSpec(memory_space=pl.ANY)],
        out_specs=pl.BlockSpec((TM, d), row),
        out_shape=jax.ShapeDtypeStruct((n, d), jnp.float32),
        scratch_shapes=[pltpu.VMEM((2,) + _tiles_shape(2 * TM, d), jnp.float32),
                        pltpu.SemaphoreType.DMA((2,))],
        compiler_params=_cparams(("arbitrary",)),
        name="final",
    )(dest, dest, x1, route, mod3, g_final, ys)


def _block_layout(counts, n):
    cnt = counts[0, :N_EXPERTS].astype(jnp.int32)
    blocks = (cnt + BM - 1) // BM
    bends = jnp.cumsum(blocks)
    pstarts = (bends - blocks) * BM
    n_blk = (2 * n) // BM + N_EXPERTS
    pst_row = jnp.zeros((1, LANES), jnp.float32).at[0, :N_EXPERTS].set(pstarts.astype(jnp.float32))
    zrow = ((pstarts + cnt) // SUBLANES).astype(jnp.int32)
    n_used = bends[-1:].astype(jnp.int32)
    blk_e = jnp.sum(bends[None, :] <= jnp.arange(n_blk, dtype=jnp.int32)[:, None], axis=1)
    blk_e = jnp.minimum(blk_e, N_EXPERTS - 1).astype(jnp.int32)
    return pst_row, zrow, n_used, blk_e


def _layer(x2, c, pos2, w_ada, b_ada, g_mix, w_in, w_o, g_ffn, w_rg, w_re, w_up, w_gate, w_down, seq):
    n, d = x2.shape
    bsz = n // seq
    bf = jnp.bfloat16
    d_a = d // 2
    d_i = N_IDX_HEADS * IDX_DIM
    mod3 = _ada(c, w_ada, b_ada).reshape(bsz * 6, 1, d)

    c0 = 3 * d_a + d_i
    c1 = c0 + IDX_DIM + N_IDX_HEADS
    w_cat = jnp.concatenate([w_in[:, :c0], w_in[:, c0:c1],
                             jnp.zeros((d, LANES - (c1 - c0)), w_in.dtype), w_in[:, c1:]], axis=1).astype(bf)
    qat, ka, vat, qit, kiw, wit, qb, kb, vb, gb = _proj(x2, pos2, mod3, g_mix.reshape(1, d), w_cat, seq)

    n_kt = seq // TM
    ya = _dsa(qat, qit, wit, kiw.reshape(bsz, n_kt, TM, LANES), ka.reshape(bsz, n_kt, TM, d_a), vat, seq)
    yb = _ret(qb, kb, vb, gb, seq)

    w_route = jnp.concatenate([jnp.transpose(w_re, (1, 0, 2)).reshape(d, N_EXPERTS), w_rg,
                               jnp.zeros((d, LANES - N_EXPERTS - N_GROUPS), w_rg.dtype)], axis=1).astype(bf)
    x1, h2, route, counts = _mix_out(x2, ya, yb, w_o[:d_a].astype(bf), w_o[d_a:].astype(bf), mod3,
                                     g_ffn.reshape(1, d), w_route, seq)

    pst_row, zrow, n_used, blk_e = _block_layout(counts, n)
    dest = _plan(route, pst_row)
    xs = _dispatch(h2, dest, zrow, n_used, blk_e.shape[0])
    ys = _experts(xs, w_gate, w_up, w_down, blk_e, n_used)
    return x1, ys, dest, route, mod3


def kernel(x, c, positions, w_ada, b_ada, g_norm_mix, w_in, w_o, g_norm_ffn, w_router_group, w_router_expert,
           w_up, w_gate, w_down, g_norm_final):
    bsz, seq, d = x.shape
    depth = w_ada.shape[0]
    assert depth == 1, "the final norm is fused into the last layer's combine kernel"
    assert seq % (2 * TM) == 0 and seq % (RET_C * RET_CHUNKS) == 0 and seq % MIX_TM == 0 and (2 * bsz * seq) % BM == 0
    assert (bsz * seq) % PLAN_TM == 0 and PLAN_TM % TM == 0
    x2 = x.reshape(bsz * seq, d)
    pos2 = positions.astype(jnp.float32).reshape(bsz * seq, 1)
    x1, ys, dest, route, mod3 = _layer(x2, c, pos2, w_ada[0], b_ada[0], g_norm_mix[0], w_in[0], w_o[0],
                                       g_norm_ffn[0], w_router_group[0], w_router_expert[0], w_up[0], w_gate[0],
                                       w_down[0], seq)
    out = _final(x1, ys, dest, route, mod3, g_norm_final.reshape(1, d), seq)
    return out.reshape(bsz, seq, d)
```

```python
import functools

import jax
import jax.numpy as jnp
import numpy as np
from jax import lax
from jax.experimental import pallas as pl
from jax.experimental.pallas import tpu as pltpu

CHUNK = 64
HEAD_DIM = 64
N_IDX_HEADS = 16
IDX_DIM = 64
TOPK_MAX = 256
ROPE_THETA = 500000.0
ROT_DIM = HEAD_DIM // 4
RET_THETA = 10000.0
N_GROUPS = 4
EXPERTS_PER_GROUP = 8
N_EXPERTS = N_GROUPS * EXPERTS_PER_GROUP
EPS = 1e-6

LANES = 128
SUBLANES = 8
VMEM_LIMIT = 56 * 1024 * 1024

TM = 256
QB = TM
RET_C = 256
RET_CHUNKS = 2
BM = 256
PLAN_TM = 1024
MIX_TM = 512
CNT_ROWS = 16
ONES_ROWS = 16
HEAD_GROUP = 4
IDX_TILES = 8
ATT_TILES = 8
BISECT_PER_CHECK = 2
MAX_BISECT = 40

NEG_BIG = -1e30
LOG2E = 1.4426950408889634


def _cparams(sem):
    return pltpu.CompilerParams(dimension_semantics=sem, vmem_limit_bytes=VMEM_LIMIT)


def _ada_kernel(c_ref, w_ref, b_ref, o_ref):
    o_ref[...] = jnp.dot(c_ref[...], w_ref[...], preferred_element_type=jnp.float32) + b_ref[...]


def _ada(c, w_ada, b_ada):
    bsz, d = c.shape
    n_out = w_ada.shape[1]
    return pl.pallas_call(
        _ada_kernel,
        grid=(n_out // d,),
        in_specs=[pl.BlockSpec((bsz, d), lambda j: (0, 0)),
                  pl.BlockSpec((d, d), lambda j: (0, j)),
                  pl.BlockSpec((1, d), lambda j: (0, j))],
        out_specs=pl.BlockSpec((bsz, d), lambda j: (0, j)),
        out_shape=jax.ShapeDtypeStruct((bsz, n_out), jnp.float32),
        compiler_params=_cparams(("arbitrary",)),
        name="ada",
    )(c, w_ada, b_ada.reshape(1, n_out))


def _rmsnorm_mod(x, g, sc, sh):
    xn = x * lax.rsqrt(jnp.mean(x * x, axis=-1, keepdims=True) + EPS)
    return xn * g * (1.0 + sc) + sh


def _rope_lanes(x, cos, sin_lo, sin_hi, half):
    cols = []
    for k in range(x.shape[1] // LANES):
        xb = x[:, k * LANES:(k + 1) * LANES]
        cols.append(xb * cos + pltpu.roll(xb, LANES - half, 1) * sin_lo + pltpu.roll(xb, half, 1) * sin_hi)
    return cols[0] if len(cols) == 1 else jnp.concatenate(cols, axis=1)


def _proj_kernel(x_ref, pos_ref, sc_ref, sh_ref, g_ref, w_ref, tab_ref,
                 qat_ref, ka_ref, vat_ref, qit_ref, kiw_ref, wit_ref, qb_ref, kb_ref, vb_ref, gb_ref,
                 *, d_a, d_i, d_b):
    h = _rmsnorm_mod(x_ref[...], g_ref[...], sc_ref[0], sh_ref[0]).astype(jnp.bfloat16)
    pos = pos_ref[...]
    ang_a = pos * tab_ref[0:1, :]
    cos_a, sin_a = jnp.cos(ang_a), jnp.sin(ang_a)
    sa_lo, sa_hi = sin_a * tab_ref[1:2, :], sin_a * tab_ref[2:3, :]
    ang_b = pos * tab_ref[3:4, :]
    cos_b, sin_b = jnp.cos(ang_b), jnp.sin(ang_b)
    sb_lo, sb_hi = sin_b * tab_ref[4:5, :], sin_b * tab_ref[5:6, :]
    half_a, half_b = ROT_DIM // 2, HEAD_DIM // 2

    def seg(lo, width):
        return jnp.dot(h, w_ref[:, lo:lo + width], preferred_element_type=jnp.float32)

    o = 0
    qa = seg(o, d_a); o += d_a
    qat_ref[0, 0] = (_rope_lanes(qa, cos_a, sa_lo, sa_hi, half_a) * (HEAD_DIM ** -0.5 * LOG2E)).T.astype(qat_ref.dtype)
    ka = seg(o, d_a); o += d_a
    ka_ref[...] = _rope_lanes(ka, cos_a, sa_lo, sa_hi, half_a).astype(ka_ref.dtype)
    vat_ref[0, 0] = seg(o, d_a).T.astype(vat_ref.dtype); o += d_a
    qi = seg(o, d_i); o += d_i
    qit_ref[0, 0] = (_rope_lanes(qi, cos_a, sa_lo, sa_hi, half_a) * (IDX_DIM ** -0.5)).T.astype(qit_ref.dtype)
    kw = seg(o, LANES); o += LANES
    kiw_ref[...] = _rope_lanes(kw, cos_a, sa_lo, sa_hi, half_a).astype(kiw_ref.dtype)
    wit_ref[0, 0] = kw.T[IDX_DIM:IDX_DIM + N_IDX_HEADS, :] * (N_IDX_HEADS ** -0.5)
    qb = seg(o, d_b); o += d_b
    qb_ref[...] = _rope_lanes(qb, cos_b, sb_lo, sb_hi, half_b).astype(qb_ref.dtype)
    kb = seg(o, d_b); o += d_b
    kb_ref[...] = (_rope_lanes(kb, cos_b, sb_lo, sb_hi, half_b) * (HEAD_DIM ** -0.5)).astype(kb_ref.dtype)
    vb_ref[...] = seg(o, d_b).astype(vb_ref.dtype); o += d_b
    gb_ref[...] = seg(o, d_b)


def _rope_tables():
    lane = jnp.arange(LANES) % HEAD_DIM
    rows = []
    for rot, theta in ((ROT_DIM, ROPE_THETA), (HEAD_DIM, RET_THETA)):
        half = rot // 2
        inv_freq = theta ** (-jnp.arange(half, dtype=jnp.float32) / half)
        rows.append(jnp.where(lane < rot, inv_freq[lane % half], 0.0))
        rows.append(jnp.where(lane < half, -1.0, 0.0))
        rows.append(jnp.where((lane >= half) & (lane < rot), 1.0, 0.0))
    rows += [jnp.zeros((LANES,), jnp.float32)] * 2
    return jnp.stack(rows).astype(jnp.float32)


def _proj(x2, pos2, mod3, g_mix, w_cat, seq):
    n, d = x2.shape
    bsz = n // seq
    d_a = d // 2
    d_b = d // 2
    d_i = N_IDX_HEADS * IDX_DIM
    tiles_per_seq = seq // TM
    tab = _rope_tables()
    row = lambda i: (i, 0)
    const = lambda i: (0, 0)
    tile4 = lambda i: (i // tiles_per_seq, i % tiles_per_seq, 0, 0)
    bf = jnp.bfloat16
    out_shape = (
        jax.ShapeDtypeStruct((bsz, tiles_per_seq, d_a, TM), bf),
        jax.ShapeDtypeStruct((n, d_a), bf),
        jax.ShapeDtypeStruct((bsz, tiles_per_seq, d_a, TM), bf),
        jax.ShapeDtypeStruct((bsz, tiles_per_seq, d_i, TM), bf),
        jax.ShapeDtypeStruct((n, LANES), bf),
        jax.ShapeDtypeStruct((bsz, tiles_per_seq, N_IDX_HEADS, TM), jnp.float32),
        jax.ShapeDtypeStruct((n, d_b), bf),
        jax.ShapeDtypeStruct((n, d_b), bf),
        jax.ShapeDtypeStruct((n, d_b), bf),
        jax.ShapeDtypeStruct((n, d_b), jnp.float32),
    )
    out_specs = (
        pl.BlockSpec((1, 1, d_a, TM), tile4),
        pl.BlockSpec((TM, d_a), row),
        pl.BlockSpec((1, 1, d_a, TM), tile4),
        pl.BlockSpec((1, 1, d_i, TM), tile4),
        pl.BlockSpec((TM, LANES), row),
        pl.BlockSpec((1, 1, N_IDX_HEADS, TM), tile4),
        pl.BlockSpec((TM, d_b), row),
        pl.BlockSpec((TM, d_b), row),
        pl.BlockSpec((TM, d_b), row),
        pl.BlockSpec((TM, d_b), row),
    )
    return pl.pallas_call(
        functools.partial(_proj_kernel, d_a=d_a, d_i=d_i, d_b=d_b),
        grid=(n // TM,),
        in_specs=[pl.BlockSpec((TM, d), row),
                  pl.BlockSpec((TM, 1), row),
                  pl.BlockSpec((1, 1, d), lambda i: ((i // tiles_per_seq) * 6 + 1, 0, 0)),
                  pl.BlockSpec((1, 1, d), lambda i: ((i // tiles_per_seq) * 6 + 0, 0, 0)),
                  pl.BlockSpec((1, d), const),
                  pl.BlockSpec(w_cat.shape, const),
                  pl.BlockSpec(tab.shape, const)],
        out_specs=out_specs,
        out_shape=out_shape,
        compiler_params=_cparams(("arbitrary",)),
        name="proj",
    )(x2, pos2, mod3, mod3, g_mix, w_cat, tab)


def _row_blocks(x, rows):
    return [x[r * rows:(r + 1) * rows] for r in range(x.shape[0] // rows)]


def _dsa_kernel(qat_ref, qit_ref, wit_ref, kiw_ref, ka_ref, vat_ref, tri_ref, o_ref,
                qix_ref, qmx_ref, sc_ref, m_ref, l_ref, acc_ref, sa_ref, sb_ref, mxa_ref, mxb_ref, lohi_ref,
                *, k_top, n_heads):
    i = pl.program_id(1)
    n_real = (i * QB + QB + TM - 1) // TM
    n_kt = n_real // 2 * 2
    odd_tile = n_real % 2 == 1

    zero_rows = jnp.zeros((LANES - IDX_DIM, QB), qix_ref.dtype)
    for h in range(N_IDX_HEADS):
        qix_ref[h] = jnp.concatenate([qit_ref[0, 0, h * IDX_DIM:(h + 1) * IDX_DIM, :], zero_rows], axis=0)
    row_q = lax.broadcasted_iota(jnp.int32, (LANES, QB), 0)
    for h in range(n_heads):
        pair = qat_ref[0, 0, (h // 2) * LANES:(h // 2 + 1) * LANES, :]
        own = (row_q < HEAD_DIM) if h % 2 == 0 else (row_q >= HEAD_DIM)
        qmx_ref[h] = jnp.where(own, pair, jnp.zeros_like(pair))

    q_chunk = (i * QB + lax.broadcasted_iota(jnp.int32, (1, QB), 1)) // CHUNK
    key_chunk_in_tile = lax.broadcasted_iota(jnp.int32, (TM, QB), 0) // CHUNK
    w_all = wit_ref[0, 0]

    def idx_tiles(tiles):
        lo, hi = lohi_ref[0], lohi_ref[1]
        for j in tiles:
            kt = kiw_ref[0, j]
            acc = None
            for h in range(N_IDX_HEADS):
                d = jnp.dot(kt, qix_ref[h], preferred_element_type=jnp.float32)
                t = w_all[h:h + 1, :] * jnp.maximum(d, 0.0)
                acc = t if acc is None else acc + t
            adm = key_chunk_in_tile <= q_chunk - j * (TM // CHUNK)
            s = jnp.where(adm, acc, -jnp.inf)
            sc_ref[j] = s
            lo = jnp.minimum(lo, functools.reduce(jnp.minimum, _row_blocks(jnp.where(adm, acc, jnp.inf), SUBLANES)))
            hi = jnp.maximum(hi, functools.reduce(jnp.maximum, _row_blocks(s, SUBLANES)))
        lohi_ref[0], lohi_ref[1] = lo, hi

    lohi_ref[0] = jnp.full((SUBLANES, QB), jnp.inf, jnp.float32)
    lohi_ref[1] = jnp.full((SUBLANES, QB), -jnp.inf, jnp.float32)

    def idx_step(g, carry):
        idx_tiles([IDX_TILES * g + u for u in range(IDX_TILES)])
        return carry

    lax.fori_loop(0, n_kt // IDX_TILES, idx_step, 0)

    def idx_tail(p, carry):
        first = n_kt // IDX_TILES * IDX_TILES + 2 * p
        idx_tiles([first, first + 1])
        return carry

    lax.fori_loop(0, n_kt % IDX_TILES // 2, idx_tail, 0)

    @pl.when(odd_tile)
    def _():
        idx_tiles([n_real - 1])

    lo = jnp.min(lohi_ref[0], axis=0, keepdims=True)
    hi = jnp.max(lohi_ref[1], axis=0, keepdims=True)
    n_adm = ((i * QB + lax.broadcasted_iota(jnp.int32, (1, QB), 1)) // CHUNK + 1) * CHUNK

    def bisect_cond(carry):
        it, _, _, _, unsettled = carry
        return (it < MAX_BISECT) & (unsettled > 0.0)

    def bisect(carry):
        it, lo, hi, cnt_lo, _ = carry
        for _ in range(BISECT_PER_CHECK):
            mid = lo + (hi - lo) * 0.5
            mid_b = jnp.broadcast_to(mid, (CNT_ROWS, QB))

            def count_tile(j, cnt, mid_b=mid_b):
                for blk in _row_blocks(sc_ref[j], CNT_ROWS):
                    cnt = cnt + jnp.where(blk >= mid_b, 1.0, 0.0)
                return cnt

            def count_quad(g, cnt):
                for u in range(4):
                    cnt = count_tile(4 * g + u, cnt)
                return cnt

            cnt = lax.fori_loop(0, n_real // 4, count_quad, jnp.zeros((CNT_ROWS, QB), jnp.float32))
            cnt = lax.fori_loop(n_real // 4 * 4, n_real, count_tile, cnt)
            c = jnp.sum(cnt, axis=0, keepdims=True)
            ge = c >= k_top
            lo = jnp.where(ge, mid, lo)
            hi = jnp.where(ge, hi, mid)
            cnt_lo = jnp.where(ge, c, cnt_lo)
        return it + BISECT_PER_CHECK, lo, hi, cnt_lo, jnp.max(jnp.where(cnt_lo > k_top, 1.0, 0.0))

    cnt0 = n_adm.astype(jnp.float32)
    _, thr, thr_hi, _, unsettled = lax.while_loop(bisect_cond, bisect,
                                                  (0, lo, hi, cnt0, jnp.max(jnp.where(cnt0 > k_top, 1.0, 0.0))))

    @pl.when(unsettled <= 0.0)
    def _():
        def bias_tile(j, carry):
            sc_ref[j] = jnp.where(sc_ref[j] >= thr, 0.0, NEG_BIG)
            return carry

        lax.fori_loop(0, n_real, bias_tile, 0)

    @pl.when(unsettled > 0.0)
    def _():
        def count_hi(j, cnt):
            return cnt + functools.reduce(jnp.add, _row_blocks(jnp.where(sc_ref[j] >= thr_hi, 1.0, 0.0), SUBLANES))

        above = jnp.sum(lax.fori_loop(0, n_real, count_hi, jnp.zeros((SUBLANES, QB), jnp.float32)),
                        axis=0, keepdims=True)
        top = jnp.where(above < k_top, thr_hi, jnp.inf)
        room = k_top - jnp.where(above < k_top, above, 0.0)

        def bias_tile(j, taken):
            s = sc_ref[j]
            tied = (s >= thr) & (s < top)
            before = jnp.dot(tri_ref[...], jnp.where(tied, 1.0, 0.0).astype(tri_ref.dtype),
                             preferred_element_type=jnp.float32)
            keep = (s >= top) | (tied & (taken + before <= room))
            sc_ref[j] = jnp.where(keep, 0.0, NEG_BIG)
            return taken + before[TM - 1:TM, :]

        lax.fori_loop(0, n_real, bias_tile, jnp.zeros((1, QB), jnp.float32))

    m_ref[...] = jnp.full(m_ref.shape, NEG_BIG, jnp.float32)
    l_ref[...] = jnp.zeros(l_ref.shape, jnp.float32)
    acc_ref[...] = jnp.zeros(acc_ref.shape, jnp.float32)

    def pair(h):
        return slice((h // 2) * LANES, (h // 2 + 1) * LANES)

    def logits_into(s_ref, mx_ref, j, heads):
        bias = sc_ref[j]
        for h in heads:
            s = jnp.dot(ka_ref[0, j, :, pair(h)], qmx_ref[h], preferred_element_type=jnp.float32) + bias
            s_ref[h] = s
            mx_ref[h] = jnp.max(s, axis=0, keepdims=True)

    ones_rows = jnp.ones((ONES_ROWS, TM), vat_ref.dtype)

    def absorb(s_ref, mx_ref, j, heads):
        for h in heads:
            m_old = m_ref[h]
            m_new = jnp.maximum(m_old, mx_ref[h])
            alpha = jnp.exp2(m_old - m_new)
            p = jnp.exp2(s_ref[h] - m_new).astype(vat_ref.dtype)
            pv = jnp.dot(jnp.concatenate([vat_ref[0, j, pair(h), :], ones_rows], axis=0), p,
                         preferred_element_type=jnp.float32)
            acc_ref[h] = acc_ref[h] * alpha + pv[0:LANES]
            l_ref[h] = l_ref[h] * alpha + pv[LANES:LANES + SUBLANES]
            m_ref[h] = m_new

    for h0 in range(0, n_heads, HEAD_GROUP):
        heads = range(h0, h0 + HEAD_GROUP)
        logits_into(sa_ref, mxa_ref, 0, heads)

        def tile_pair(t, heads=heads):
            logits_into(sb_ref, mxb_ref, t + 1, heads)
            absorb(sa_ref, mxa_ref, t, heads)
            logits_into(sa_ref, mxa_ref, jnp.minimum(t + 2, n_real - 1), heads)
            absorb(sb_ref, mxb_ref, t + 1, heads)

        def attn_step(g, carry):
            for u in range(0, ATT_TILES, 2):
                tile_pair(ATT_TILES * g + u)
            return carry

        lax.fori_loop(0, n_kt // ATT_TILES, attn_step, 0)

        def attn_tail(p, carry):
            tile_pair(n_kt // ATT_TILES * ATT_TILES + 2 * p)
            return carry

        lax.fori_loop(0, n_kt % ATT_TILES // 2, attn_tail, 0)

        @pl.when(odd_tile)
        def _(heads=heads):
            absorb(sa_ref, mxa_ref, n_real - 1, heads)

    for hp in range(n_heads // 2):
        even = acc_ref[2 * hp] / l_ref[2 * hp, 0:1, :]
        odd = acc_ref[2 * hp + 1] / l_ref[2 * hp + 1, 0:1, :]
        o_ref[:, hp * LANES:(hp + 1) * LANES] = jnp.where(row_q < HEAD_DIM, even, odd).T.astype(o_ref.dtype)


def _dsa(qat, qit, wit, kiw4, ka4, vat, seq):
    bsz, n_kt, d_a, _ = qat.shape
    n_heads = d_a // HEAD_DIM
    n_qb = seq // QB
    k_top = min(TOPK_MAX, seq // 4)
    qtile = lambda b, i: (b, i, 0, 0)
    per_b = lambda b, i: (b, 0, 0, 0)
    f32 = jnp.float32
    return pl.pallas_call(
        functools.partial(_dsa_kernel, k_top=float(k_top), n_heads=n_heads),
        grid=(bsz, n_qb),
        in_specs=[pl.BlockSpec((1, 1, d_a, QB), qtile),
                  pl.BlockSpec((1, 1, qit.shape[2], QB), qtile),
                  pl.BlockSpec((1, 1, N_IDX_HEADS, QB), qtile),
                  pl.BlockSpec((1, n_kt, TM, LANES), per_b, pipeline_mode=pl.Buffered(1)),
                  pl.BlockSpec((1, n_kt, TM, d_a), per_b, pipeline_mode=pl.Buffered(1)),
                  pl.BlockSpec((1, n_kt, d_a, TM), per_b, pipeline_mode=pl.Buffered(1)),
                  pl.BlockSpec((TM, TM), lambda b, i: (0, 0), pipeline_mode=pl.Buffered(1))],
        out_specs=pl.BlockSpec((QB, d_a), lambda b, i: (b * n_qb + i, 0)),
        out_shape=jax.ShapeDtypeStruct((bsz * seq, d_a), jnp.bfloat16),
        scratch_shapes=[pltpu.VMEM((N_IDX_HEADS, LANES, QB), jnp.bfloat16),
                        pltpu.VMEM((n_heads, LANES, QB), jnp.bfloat16),
                        pltpu.VMEM((n_kt, TM, QB), f32),
                        pltpu.VMEM((n_heads, 1, QB), f32),
                        pltpu.VMEM((n_heads, SUBLANES, QB), f32),
                        pltpu.VMEM((n_heads, LANES, QB), f32),
                        pltpu.VMEM((n_heads, TM, QB), f32),
                        pltpu.VMEM((n_heads, TM, QB), f32),
                        pltpu.VMEM((n_heads, 1, QB), f32),
                        pltpu.VMEM((n_heads, 1, QB), f32),
                        pltpu.VMEM((2, SUBLANES, QB), f32)],
        compiler_params=_cparams(("arbitrary", "arbitrary")),
        name="dsa",
    )(qat, qit, wit, kiw4, ka4, vat, jnp.asarray(np.tril(np.ones((TM, TM), np.float32)), jnp.bfloat16))


def _group_mean(y, avg):
    hi = y.astype(jnp.bfloat16)
    lo = (y - hi.astype(jnp.float32)).astype(jnp.bfloat16)
    return (jnp.dot(hi, avg, preferred_element_type=jnp.float32)
            + jnp.dot(lo, avg, preferred_element_type=jnp.float32))


def _ret_kernel(q_ref, k_ref, v_ref, g_ref, dec_ref, zt_ref, xi_ref, gc_ref, blk_ref, avg_ref, o_ref, st_ref,
                *, n_heads):
    @pl.when(pl.program_id(1) == 0)
    def _():
        st_ref[...] = jnp.zeros_like(st_ref)

    even = lax.broadcasted_iota(jnp.int32, (RET_C, LANES), 1) < HEAD_DIM
    avg = avg_ref[...]
    for p in range(n_heads // 2):
        sl = slice(p * LANES, (p + 1) * LANES)
        state = st_ref[p]
        for c in range(q_ref.shape[0] // RET_C):
            rows = slice(c * RET_C, (c + 1) * RET_C)
            qp, kp, vp = q_ref[rows, sl], k_ref[rows, sl], v_ref[rows, sl]
            kpt = kp.astype(jnp.float32).T
            kpt_b = kpt.astype(kp.dtype)
            inner = None
            for e in range(2):
                q_e = jnp.where(even if e == 0 else jnp.logical_not(even), qp, jnp.zeros_like(qp))
                s = jnp.dot(q_e, kpt_b, preferred_element_type=jnp.float32) * dec_ref[2 * p + e]
                t = jnp.dot(s.astype(vp.dtype), vp, preferred_element_type=jnp.float32)
                inner = t if e == 0 else jnp.where(even, inner, t)
            cross = jnp.dot(qp, state.astype(qp.dtype), preferred_element_type=jnp.float32) * xi_ref[p]
            y = inner + cross
            yc = y - _group_mean(y, avg)
            yn = yc * lax.rsqrt(_group_mean(yc * yc, avg) + EPS)
            g = g_ref[rows, sl]
            o_ref[rows, sl] = (g / (1.0 + jnp.exp(-g)) * yn).astype(o_ref.dtype)
            kz = (kpt * zt_ref[p]).astype(kp.dtype)
            kv = jnp.dot(kz, vp, preferred_element_type=jnp.float32)
            state = state * gc_ref[p] + kv * blk_ref[...]
        st_ref[p] = state


def _ret_consts(n_heads):
    log_gamma = jnp.log1p(-jnp.exp2(-5.0 - jnp.arange(n_heads, dtype=jnp.float32)))
    pos = jnp.arange(RET_C, dtype=jnp.float32)
    diff = pos[:, None] - pos[None, :]
    dec = jnp.where(diff[None] >= 0, jnp.exp(jnp.maximum(diff, 0.0)[None] * log_gamma[:, None, None]), 0.0)
    zeta = jnp.exp((RET_C - 1.0 - pos)[None, :] * log_gamma[:, None])
    xi = jnp.exp((pos + 1.0)[None, :] * log_gamma[:, None])
    gc = jnp.exp(RET_C * log_gamma)
    n_pairs = n_heads // 2
    lanes = lambda a: jnp.repeat(a.reshape(n_pairs, 2, -1), HEAD_DIM, axis=1)
    zt = lanes(zeta)
    xi_p = jnp.swapaxes(lanes(xi), 1, 2)
    gc_p = jnp.broadcast_to(lanes(gc[:, None]), (n_pairs, LANES, LANES))
    head_of = jnp.arange(LANES) // HEAD_DIM
    blk = (head_of[:, None] == head_of[None, :]).astype(jnp.float32)
    avg = (blk / HEAD_DIM).astype(jnp.bfloat16)
    f32 = lambda a: a.astype(jnp.float32)
    return dec, f32(zt), f32(xi_p), f32(gc_p), blk, avg


def _ret(qb, kb, vb, gb, seq):
    n, d_b = qb.shape
    bsz = n // seq
    n_heads = d_b // HEAD_DIM
    step = RET_C * RET_CHUNKS
    n_c = seq // step
    consts = _ret_consts(n_heads)
    row = lambda b, c: (b * n_c + c, 0)
    const_spec = lambda a: pl.BlockSpec(a.shape, lambda b, c: (0,) * a.ndim)
    return pl.pallas_call(
        functools.partial(_ret_kernel, n_heads=n_heads),
        grid=(bsz, n_c),
        in_specs=[pl.BlockSpec((step, d_b), row)] * 4 + [const_spec(a) for a in consts],
        out_specs=pl.BlockSpec((step, d_b), row),
        out_shape=jax.ShapeDtypeStruct((n, d_b), jnp.bfloat16),
        scratch_shapes=[pltpu.VMEM((n_heads // 2, LANES, LANES), jnp.float32)],
        compiler_params=_cparams(("arbitrary", "arbitrary")),
        name="ret",
    )(qb, kb, vb, gb, *consts)


def _tiles_shape(rows, d):
    return (rows // SUBLANES, d // LANES, SUBLANES, LANES)


def _to_tiles(ref, x):
    for s in range(ref.shape[1]):
        ref[:, s] = x[:, s * LANES:(s + 1) * LANES].reshape(ref.shape[0], SUBLANES, LANES)


def _from_tiles(ref):
    rows = ref.shape[0] * SUBLANES
    return jnp.concatenate([ref[:, s].reshape(rows, LANES) for s in range(ref.shape[1])], axis=1)


def _lane_first_eq(x, m, lane):
    return jnp.min(jnp.where(x == m, lane, float(LANES)), axis=1, keepdims=True)


def _mix_out_kernel(x_ref, ya_ref, yb_ref, woa_ref, wob_ref, gt_ref, sc_ref, sh_ref, g_ref, wr_ref, tri_ref,
                    x1_ref, h2_ref, route_ref, cnt_ref, carry_ref):
    @pl.when(pl.program_id(0) == 0)
    def _():
        carry_ref[...] = jnp.zeros_like(carry_ref)

    mix = (jnp.dot(ya_ref[...], woa_ref[...], preferred_element_type=jnp.float32)
           + jnp.dot(yb_ref[...], wob_ref[...], preferred_element_type=jnp.float32))
    x1 = x_ref[...] + gt_ref[0] * mix
    x1_ref[...] = x1
    h2 = _rmsnorm_mod(x1, g_ref[...], sc_ref[0], sh_ref[0])
    _to_tiles(h2_ref, h2)

    lg = jnp.dot(h2.astype(jnp.bfloat16), wr_ref[...], preferred_element_type=jnp.float32)
    lane = lax.broadcasted_iota(jnp.int32, lg.shape, 1).astype(jnp.float32)
    is_grp = (lane >= N_EXPERTS) & (lane < N_EXPERTS + N_GROUPS)
    gl = jnp.where(is_grp, lg, -jnp.inf)
    gmax = jnp.max(gl, axis=1, keepdims=True)
    grp = _lane_first_eq(gl, gmax, lane) - N_EXPERTS
    p_grp = 1.0 / jnp.sum(jnp.exp(gl - gmax), axis=1, keepdims=True)
    in_grp = jnp.floor(lane * (1.0 / EXPERTS_PER_GROUP)) == grp
    f = jnp.where(in_grp & (lane < N_EXPERTS), lg, -jnp.inf)
    f1 = jnp.max(f, axis=1, keepdims=True)
    e1 = _lane_first_eq(f, f1, lane)
    f = jnp.where(lane == e1, -jnp.inf, f)
    f2 = jnp.max(f, axis=1, keepdims=True)
    e2 = _lane_first_eq(f, f2, lane)
    a2 = jnp.exp(f2 - f1)
    w1 = p_grp / (1.0 + a2)
    w2 = p_grp * a2 / (1.0 + a2)

    oh1 = jnp.where(lane == e1, 1.0, 0.0)
    oh2 = jnp.where(lane == e2, 1.0, 0.0)
    both = oh1 + oh2
    before = jnp.dot(tri_ref[...], both.astype(jnp.bfloat16), preferred_element_type=jnp.float32) + carry_ref[...]
    r1 = jnp.sum(before * oh1, axis=1, keepdims=True)
    r2 = jnp.sum(before * oh2, axis=1, keepdims=True)
    carry = carry_ref[...] + jnp.sum(both, axis=0, keepdims=True)
    carry_ref[...] = carry
    cnt_ref[...] = carry

    out = jnp.zeros(lg.shape, jnp.float32)
    for col, val in enumerate((e1, e2, w1, w2, r1, r2)):
        out = jnp.where(lane == col, val, out)
    route_ref[...] = out


def _mix_out(x2, ya, yb, wo_a, wo_b, mod3, g_ffn, w_route, seq):
    n, d = x2.shape
    tm = MIX_TM
    tiles_per_seq = seq // tm
    tri = jnp.asarray(np.tril(np.ones((tm, tm), np.float32), -1), jnp.bfloat16)
    row = lambda i: (i, 0)
    const = lambda i: (0, 0)
    modk = lambda k: pl.BlockSpec((1, 1, d), lambda i: ((i // tiles_per_seq) * 6 + k, 0, 0))
    return pl.pallas_call(
        _mix_out_kernel,
        grid=(n // tm,),
        in_specs=[pl.BlockSpec((tm, d), row),
                  pl.BlockSpec((tm, ya.shape[1]), row),
                  pl.BlockSpec((tm, yb.shape[1]), row),
                  pl.BlockSpec(wo_a.shape, const),
                  pl.BlockSpec(wo_b.shape, const),
                  modk(2), modk(4), modk(3),
                  pl.BlockSpec((1, d), const),
                  pl.BlockSpec(w_route.shape, const),
                  pl.BlockSpec(tri.shape, const)],
        out_specs=(pl.BlockSpec((tm, d), row), pl.BlockSpec(_tiles_shape(tm, d), lambda i: (i, 0, 0, 0)),
                   pl.BlockSpec((tm, LANES), row), pl.BlockSpec((1, LANES), const)),
        out_shape=(jax.ShapeDtypeStruct((n, d), jnp.float32), jax.ShapeDtypeStruct(_tiles_shape(n, d), jnp.float32),
                   jax.ShapeDtypeStruct((n, LANES), jnp.float32), jax.ShapeDtypeStruct((1, LANES), jnp.float32)),
        scratch_shapes=[pltpu.VMEM((1, LANES), jnp.float32)],
        compiler_params=_cparams(("arbitrary",)),
        name="mix_out",
    )(x2, ya, yb, wo_a, wo_b, mod3, mod3, mod3, g_ffn, w_route, tri)


def _plan_kernel(route_ref, pst_ref, dest_ref):
    r = route_ref[...]
    lane = lax.broadcasted_iota(jnp.int32, r.shape, 1).astype(jnp.float32)
    pst = pst_ref[...]
    d1 = jnp.sum(jnp.where(lane == r[:, 0:1], pst, 0.0), axis=1, keepdims=True) + r[:, 4:5]
    d2 = jnp.sum(jnp.where(lane == r[:, 1:2], pst, 0.0), axis=1, keepdims=True) + r[:, 5:6]
    t1 = jnp.floor(d1 * (1.0 / SUBLANES))
    t2 = jnp.floor(d2 * (1.0 / SUBLANES))
    packed = jnp.zeros(r.shape, jnp.float32)
    for k, v in enumerate((t1, d1 - t1 * SUBLANES, t2, d2 - t2 * SUBLANES)):
        packed = jnp.where(lane == float(k), v, packed)
    dest_ref[0] = packed.T[0:8, :].astype(jnp.int32)


def _dest_block(i):
    return (i // (PLAN_TM // TM), 0, i % (PLAN_TM // TM))


def _plan(route, pst_row):
    n = route.shape[0]
    return pl.pallas_call(
        _plan_kernel,
        grid=(n // PLAN_TM,),
        in_specs=[pl.BlockSpec((PLAN_TM, LANES), lambda i: (i, 0)), pl.BlockSpec((1, LANES), lambda i: (0, 0))],
        out_specs=pl.BlockSpec((1, 8, PLAN_TM), lambda i: (i, 0, 0)),
        out_shape=jax.ShapeDtypeStruct((n // PLAN_TM, 8, PLAN_TM), jnp.int32),
        compiler_params=_cparams(("arbitrary",)),
        name="plan",
    )(route, pst_row)


def _dispatch_kernel(zrow_ref, n_used_ref, dest_ref, h2_ref, xs_hbm, zbuf, sem, zsem, *, n_blk):
    i = pl.program_id(0)

    @pl.when(i == 0)
    def _():
        zbuf[...] = jnp.zeros(zbuf.shape, zbuf.dtype)
        blk_tiles = BM // SUBLANES
        for e in range(N_EXPERTS):
            pltpu.make_async_copy(zbuf, xs_hbm.at[pl.ds(zrow_ref[e], blk_tiles)], zsem).start()
        for e in range(N_EXPERTS):
            pltpu.make_async_copy(zbuf, xs_hbm.at[pl.ds(0, blk_tiles)], zsem).wait()
        for b in range(N_EXPERTS + 1):
            @pl.when(n_used_ref[0] + b <= n_blk)
            def _():
                tail = pltpu.make_async_copy(zbuf, xs_hbm.at[pl.ds((n_used_ref[0] + b) * blk_tiles, blk_tiles)], zsem)
                tail.start()
                tail.wait()

    def issue(k, c):
        for u in range(SUBLANES):
            r = k * SUBLANES + u
            row = h2_ref.at[k, :, u, :]
            pltpu.make_async_copy(row, xs_hbm.at[dest_ref[0, 0, r], :, dest_ref[0, 1, r], :], sem).start()
            pltpu.make_async_copy(row, xs_hbm.at[dest_ref[0, 2, r], :, dest_ref[0, 3, r], :], sem).start()
        return c

    lax.fori_loop(0, TM // SUBLANES, issue, 0)
    for _ in range(2):
        pltpu.make_async_copy(h2_ref, xs_hbm.at[pl.ds(0, TM // SUBLANES)], sem).wait()


def _dispatch(h2, dest, zrow, n_used, n_blk):
    n_rows = (n_blk + 1) * BM
    d = h2.shape[1] * LANES
    grid_spec = pltpu.PrefetchScalarGridSpec(
        num_scalar_prefetch=2,
        grid=(h2.shape[0] * SUBLANES // TM,),
        in_specs=[pl.BlockSpec((1, 8, TM), lambda i, z, nu: _dest_block(i), memory_space=pltpu.SMEM),
                  pl.BlockSpec(_tiles_shape(TM, d), lambda i, z, nu: (i, 0, 0, 0))],
        out_specs=pl.BlockSpec(memory_space=pl.ANY),
        scratch_shapes=[pltpu.VMEM(_tiles_shape(BM, d), h2.dtype),
                        pltpu.SemaphoreType.DMA(()),
                        pltpu.SemaphoreType.DMA(())],
    )
    return pl.pallas_call(
        functools.partial(_dispatch_kernel, n_blk=n_blk),
        grid_spec=grid_spec,
        out_shape=jax.ShapeDtypeStruct(_tiles_shape(n_rows, d), h2.dtype),
        compiler_params=_cparams(("arbitrary",)),
        name="dispatch",
    )(zrow, n_used, dest, h2)


def _experts_kernel(blk_e_ref, n_used_ref, x_ref, wg_ref, wu_ref, wd_ref, y_ref, wg_bf, wu_bf, wd_bf):
    j = pl.program_id(0)

    @pl.when((j == 0) | (blk_e_ref[j] != blk_e_ref[jnp.maximum(j - 1, 0)]))
    def _():
        wg_bf[...] = wg_ref[0].astype(wg_bf.dtype)
        wu_bf[...] = wu_ref[0].astype(wu_bf.dtype)
        wd_bf[...] = wd_ref[0].astype(wd_bf.dtype)

    @pl.when(j < n_used_ref[0])
    def _():
        x = _from_tiles(x_ref).astype(wg_bf.dtype)
        a = jnp.dot(x, wg_bf[...], preferred_element_type=jnp.float32)
        b = jnp.dot(x, wu_bf[...], preferred_element_type=jnp.float32)
        hmid = (a / (1.0 + jnp.exp(-a)) * b).astype(x.dtype)
        _to_tiles(y_ref, jnp.dot(hmid, wd_bf[...], preferred_element_type=jnp.float32))

    @pl.when(j >= n_used_ref[0])
    def _():
        y_ref[...] = jnp.zeros(y_ref.shape, y_ref.dtype)


def _experts(xs, wg, wu, wd, blk_e, n_used):
    n_blk = blk_e.shape[0]
    d, d_e = wg.shape[1], wg.shape[2]
    blk = _tiles_shape(BM, d)
    grid_spec = pltpu.PrefetchScalarGridSpec(
        num_scalar_prefetch=2,
        grid=(n_blk,),
        in_specs=[pl.BlockSpec(blk, lambda j, be, nu: (jnp.minimum(j, nu[0] - 1), 0, 0, 0)),
                  pl.BlockSpec((1, d, d_e), lambda j, be, nu: (be[j], 0, 0)),
                  pl.BlockSpec((1, d, d_e), lambda j, be, nu: (be[j], 0, 0)),
                  pl.BlockSpec((1, d_e, d), lambda j, be, nu: (be[j], 0, 0))],
        out_specs=pl.BlockSpec(blk, lambda j, be, nu: (j, 0, 0, 0)),
        scratch_shapes=[pltpu.VMEM((d, d_e), jnp.bfloat16), pltpu.VMEM((d, d_e), jnp.bfloat16),
                        pltpu.VMEM((d_e, d), jnp.bfloat16)],
    )
    return pl.pallas_call(
        _experts_kernel,
        grid_spec=grid_spec,
        out_shape=jax.ShapeDtypeStruct(_tiles_shape(n_blk * BM, d), jnp.float32),
        compiler_params=_cparams(("arbitrary",)),
        name="experts",
    )(blk_e, n_used, xs, wg, wu, wd)


def _final_kernel(dest_ref, dest_next_ref, x1_ref, route_ref, gt_ref, g_ref, ys_hbm, o_ref, ybuf, sem):
    i = pl.program_id(0)
    slot = i % 2

    tiles = TM // SUBLANES

    def gather(d_ref, s):
        def issue(k, c):
            for u in range(SUBLANES):
                r = k * SUBLANES + u
                pltpu.make_async_copy(ys_hbm.at[d_ref[0, 0, r], :, d_ref[0, 1, r], :],
                                      ybuf.at[s, k, :, u, :], sem.at[s]).start()
                pltpu.make_async_copy(ys_hbm.at[d_ref[0, 2, r], :, d_ref[0, 3, r], :],
                                      ybuf.at[s, tiles + k, :, u, :], sem.at[s]).start()
            return c
        lax.fori_loop(0, tiles, issue, 0)

    @pl.when(i == 0)
    def _():
        gather(dest_ref, 0)

    @pl.when(i + 1 < pl.num_programs(0))
    def _():
        gather(dest_next_ref, 1 - slot)

    pltpu.make_async_copy(ys_hbm.at[pl.ds(0, 2 * tiles)], ybuf.at[slot], sem.at[slot]).wait()
    w1 = route_ref[:, 2:3]
    w2 = route_ref[:, 3:4]
    y0 = _from_tiles(ybuf.at[slot, 0:tiles])
    y1 = _from_tiles(ybuf.at[slot, tiles:2 * tiles])
    x2 = x1_ref[...] + gt_ref[0] * (w1 * y0 + w2 * y1)
    o_ref[...] = x2 * lax.rsqrt(jnp.mean(x2 * x2, axis=-1, keepdims=True) + EPS) * g_ref[...]


def _final(x1, ys, dest, route, mod3, g_final, seq):
    n, d = x1.shape
    n_tiles = n // TM
    tiles_per_seq = seq // TM
    row = lambda i: (i, 0)
    return pl.pallas_call(
        _final_kernel,
        grid=(n_tiles,),
        in_specs=[pl.BlockSpec((1, 8, TM), _dest_block, memory_space=pltpu.SMEM),
                  pl.BlockSpec((1, 8, TM), lambda i: _dest_block(jnp.minimum(i + 1, n_tiles - 1)),
                               memory_space=pltpu.SMEM),
                  pl.BlockSpec((TM, d), row),
                  pl.BlockSpec((TM, LANES), row),
                  pl.BlockSpec((1, 1, d), lambda i: ((i // tiles_per_seq) * 6 + 5, 0, 0)),
                  pl.BlockSpec((1, d), lambda i: (0, 0)),
                  pl.BlockSpec(memory_space=pl.ANY)],
        out_specs=pl.BlockSpec((TM, d), row),
        out_shape=jax.ShapeDtypeStruct((n, d), jnp.float32),
        scratch_shapes=[pltpu.VMEM((2,) + _tiles_shape(2 * TM, d), jnp.float32),
                        pltpu.SemaphoreType.DMA((2,))],
        compiler_params=_cparams(("arbitrary",)),
        name="final",
    )(dest, dest, x1, route, mod3, g_final, ys)


def _block_layout(counts, n):
    cnt = counts[0, :N_EXPERTS].astype(jnp.int32)
    blocks = (cnt + BM - 1) // BM
    bends = jnp.cumsum(blocks)
    pstarts = (bends - blocks) * BM
    n_blk = (2 * n) // BM + N_EXPERTS
    pst_row = jnp.zeros((1, LANES), jnp.float32).at[0, :N_EXPERTS].set(pstarts.astype(jnp.float32))
    zrow = ((pstarts + cnt) // SUBLANES).astype(jnp.int32)
    n_used = bends[-1:].astype(jnp.int32)
    blk_e = jnp.sum(bends[None, :] <= jnp.arange(n_blk, dtype=jnp.int32)[:, None], axis=1)
    blk_e = jnp.minimum(blk_e, N_EXPERTS - 1).astype(jnp.int32)
    return pst_row, zrow, n_used, blk_e


def _layer(x2, c, pos2, w_ada, b_ada, g_mix, w_in, w_o, g_ffn, w_rg, w_re, w_up, w_gate, w_down, seq):
    n, d = x2.shape
    bsz = n // seq
    bf = jnp.bfloat16
    d_a = d // 2
    d_i = N_IDX_HEADS * IDX_DIM
    mod3 = _ada(c, w_ada, b_ada).reshape(bsz * 6, 1, d)

    c0 = 3 * d_a + d_i
    c1 = c0 + IDX_DIM + N_IDX_HEADS
    w_cat = jnp.concatenate([w_in[:, :c0], w_in[:, c0:c1],
                             jnp.zeros((d, LANES - (c1 - c0)), w_in.dtype), w_in[:, c1:]], axis=1).astype(bf)
    qat, ka, vat, qit, kiw, wit, qb, kb, vb, gb = _proj(x2, pos2, mod3, g_mix.reshape(1, d), w_cat, seq)

    n_kt = seq // TM
    ya = _dsa(qat, qit, wit, kiw.reshape(bsz, n_kt, TM, LANES), ka.reshape(bsz, n_kt, TM, d_a), vat, seq)
    yb = _ret(qb, kb, vb, gb, seq)

    w_route = jnp.concatenate([jnp.transpose(w_re, (1, 0, 2)).reshape(d, N_EXPERTS), w_rg,
                               jnp.zeros((d, LANES - N_EXPERTS - N_GROUPS), w_rg.dtype)], axis=1).astype(bf)
    x1, h2, route, counts = _mix_out(x2, ya, yb, w_o[:d_a].astype(bf), w_o[d_a:].astype(bf), mod3,
                                     g_ffn.reshape(1, d), w_route, seq)

    pst_row, zrow, n_used, blk_e = _block_layout(counts, n)
    dest = _plan(route, pst_row)
    xs = _dispatch(h2, dest, zrow, n_used, blk_e.shape[0])
    ys = _experts(xs, w_gate, w_up, w_down, blk_e, n_used)
    return x1, ys, dest, route, mod3


def kernel(x, c, positions, w_ada, b_ada, g_norm_mix, w_in, w_o, g_norm_ffn, w_router_group, w_router_expert,
           w_up, w_gate, w_down, g_norm_final):
    bsz, seq, d = x.shape
    depth = w_ada.shape[0]
    assert depth == 1, "the final norm is fused into the last layer's combine kernel"
    assert seq % (2 * TM) == 0 and seq % (RET_C * RET_CHUNKS) == 0 and seq % MIX_TM == 0 and (2 * bsz * seq) % BM == 0
    assert (bsz * seq) % PLAN_TM == 0 and PLAN_TM % TM == 0
    x2 = x.reshape(bsz * seq, d)
    pos2 = positions.astype(jnp.float32).reshape(bsz * seq, 1)
    x1, ys, dest, route, mod3 = _layer(x2, c, pos2, w_ada[0], b_ada[0], g_norm_mix[0], w_in[0], w_o[0],
                                       g_norm_ffn[0], w_router_group[0], w_router_expert[0], w_up[0], w_gate[0],
                                       w_down[0], seq)
    out = _final(x1, ys, dest, route, mod3, g_norm_final.reshape(1, d), seq)
    return out.reshape(bsz, seq, d)
```
